```python
import jax
import jax.numpy as jnp
from jax import lax
import numpy as np

D_MODEL = 2048
BATCH = 1
SEQ = 16384
DEPTH = 2

NSA_HEADS = 8
NSA_GROUPS = 2
NSA_HPG = NSA_HEADS // NSA_GROUPS
NSA_DH = 64
CMP_LEN = 32
CMP_STRIDE = 16
CMP_HIDDEN = 128
SEL_LEN = 64
SEL_TOPN = 16
WINDOW = 512
FORCE_SCORE = 1.0e4
POOL_GROUPS = 4
POOL_WINDOWS = (2, 4, 8, 16)
POOL_WIDTH = 512
POOL_GW = POOL_WIDTH // POOL_GROUPS
MLA_HEADS = 8
MLA_Q_RANK = 512
MLA_KV_RANK = 256
MLA_NOPE = 64
MLA_ROPE = 32
MLA_DV = 64
ROPE_BASE = 10000.0
CONV_CH = 512
CONV_K = 31
N_BRANCH = 4
X_HEADS = 4
X_DH = 128
MEM_LEN = 256
MOE_GROUPS = 4
MOE_EPG = 8
N_EXPERTS = MOE_GROUPS * MOE_EPG
MOE_TOPK = 2
EXPERT_FF = 512
MOE_BLOCK = 256
Q_BLK = 128
LN_EPS = 1e-5
RMS_EPS = 1e-6
DN_ALPHA = (2 * DEPTH) ** 0.25
DN_BETA = (8 * DEPTH) ** -0.25
IN_SIZES = (NSA_HEADS * NSA_DH, 6 * NSA_GROUPS * NSA_DH, 3 * NSA_HEADS, POOL_WIDTH,
            MLA_Q_RANK, MLA_KV_RANK, MLA_ROPE, 2 * CONV_CH, N_BRANCH * D_MODEL)
IN_WIDTH = sum(IN_SIZES)

kernel_name = 'hybrid_nsa_pool_mla_conv_hmoe_block'


def _split(u, sizes):
    out, o = [], 0
    for s in sizes:
        out.append(u[..., o:o + s])
        o += s
    return out


def _layer_norm(x, g, b):
    xf = x.astype(jnp.float32)
    mu = jnp.mean(xf, -1, keepdims=True)
    var = jnp.mean(jnp.square(xf - mu), -1, keepdims=True)
    return ((xf - mu) * lax.rsqrt(var + LN_EPS) * g + b).astype(x.dtype)


def _rms_norm(x, g):
    xf = x.astype(jnp.float32)
    return (xf * lax.rsqrt(jnp.mean(xf * xf, -1, keepdims=True) + RMS_EPS) * g).astype(x.dtype)


def _rope(x, positions):
    half = x.shape[-1] // 2
    inv = ROPE_BASE ** (-jnp.arange(half, dtype=jnp.float32) / half)
    ang = positions.astype(jnp.float32)[..., None] * inv
    ang = ang.reshape(ang.shape[:2] + (1,) * (x.ndim - 3) + (half,))
    cos, sin = jnp.cos(ang), jnp.sin(ang)
    xf = x.astype(jnp.float32)
    x1, x2 = xf[..., :half], xf[..., half:]
    return jnp.concatenate([x1 * cos - x2 * sin, x2 * cos + x1 * sin], -1).astype(x.dtype)


def _masked_softmax(s, mask):
    s = jnp.where(mask, s.astype(jnp.float32), -jnp.inf)
    m = jnp.max(s, -1, keepdims=True)
    m = jnp.where(jnp.isfinite(m), m, 0.0)
    p = jnp.exp(s - m)
    return p / jnp.maximum(jnp.sum(p, -1, keepdims=True), 1e-30)


def _nsa(q, kv, gates, cmp_pos, cmp_w1, cmp_w2):
    B, S = q.shape[0], q.shape[1]
    f32 = jnp.float32
    n_cmp = (S - CMP_LEN) // CMP_STRIDE + 1
    n_sel = S // SEL_LEN
    top_n = min(SEL_TOPN, n_sel)
    span = CMP_LEN // CMP_STRIDE
    ratio = SEL_LEN // CMP_STRIDE
    k_cmp, v_cmp, k_sel, v_sel, k_win, v_win = [kv[:, :, j] for j in range(6)]
    tok = CMP_STRIDE * np.arange(n_cmp)[:, None] + np.arange(CMP_LEN)[None, :]

    def compress(t, j):
        blocks = t[:, tok] + cmp_pos[j][:, None, :]
        w1 = cmp_w1[j].reshape(CMP_LEN, NSA_DH, CMP_HIDDEN)
        hid = jax.nn.gelu(jnp.einsum('bnlgd,ldh->bngh', blocks, w1))
        return jnp.einsum('bngh,hd->bngd', hid, cmp_w2[j])

    kc = compress(k_cmp, 0)
    vc = compress(v_cmp, 1)
    cmp_last = CMP_STRIDE * jnp.arange(n_cmp) + CMP_LEN - 1
    ks_blk = k_sel.reshape(B, n_sel, SEL_LEN, NSA_GROUPS, NSA_DH).transpose(0, 3, 1, 2, 4)
    vs_blk = v_sel.reshape(B, n_sel, SEL_LEN, NSA_GROUPS, NSA_DH).transpose(0, 3, 1, 2, 4)
    pad = ((0, 0), (WINDOW, 0), (0, 0), (0, 0))
    kw_pad = jnp.pad(k_win, pad)
    vw_pad = jnp.pad(v_win, pad)
    b_ix = jnp.arange(B)[:, None, None, None]
    g_ix = jnp.arange(NSA_GROUPS)[None, None, :, None]
    blk_id = jnp.arange(n_sel)
    q_blocks = (q.astype(f32) * NSA_DH ** -0.5).reshape(
        B, S // Q_BLK, Q_BLK, NSA_GROUPS, NSA_HPG, NSA_DH)

    def one_block(i):
        t0 = i * Q_BLK
        tq = t0 + jnp.arange(Q_BLK)
        qi = q_blocks[:, i]
        s = jnp.einsum('btghd,bngd->btghn', qi, kc)
        p_cmp = _masked_softmax(s, (cmp_last[None, :] <= tq[:, None])[None, :, None, None, :])
        o_cmp = jnp.einsum('btghn,bngd->btghd', p_cmp, vc)
        imp = jnp.sum(p_cmp, 3)
        imp = jnp.pad(imp, ((0, 0), (0, 0), (0, 0), (span - 1, max(ratio * n_sel - n_cmp, 0))))
        score = imp[..., 0:ratio * n_sel:ratio]
        for o in range(1, ratio + span - 1):
            score = score + imp[..., o:o + ratio * n_sel:ratio]
        cur = tq // SEL_LEN
        valid = blk_id[None, :] <= cur[:, None]
        forced = (blk_id[None, :] == 0) | (blk_id[None, :] == cur[:, None]) | (blk_id[None, :] == cur[:, None] - 1)
        score = jnp.where(forced[None, :, None, :], FORCE_SCORE, score)
        score = jnp.where(valid[None, :, None, :], score, -1.0)
        _, idx = lax.top_k(score, top_n)
        ks = ks_blk[b_ix, g_ix, idx]
        vs = vs_blk[b_ix, g_ix, idx]
        kpos = idx[..., None] * SEL_LEN + jnp.arange(SEL_LEN)
        s = jnp.einsum('btghd,btgnld->btghnl', qi, ks).reshape(B, Q_BLK, NSA_GROUPS, NSA_HPG, top_n * SEL_LEN)
        smask = (kpos <= tq[None, :, None, None, None]).reshape(B, Q_BLK, NSA_GROUPS, 1, top_n * SEL_LEN)
        p = _masked_softmax(s, smask)
        o_sel = jnp.einsum('btghm,btgmd->btghd', p, vs.reshape(B, Q_BLK, NSA_GROUPS, top_n * SEL_LEN, NSA_DH))
        kw = lax.dynamic_slice_in_dim(kw_pad, t0, WINDOW + Q_BLK, 1)
        vw = lax.dynamic_slice_in_dim(vw_pad, t0, WINDOW + Q_BLK, 1)
        kwpos = t0 - WINDOW + jnp.arange(WINDOW + Q_BLK)
        wmask = (kwpos[None, :] <= tq[:, None]) & (kwpos[None, :] > tq[:, None] - WINDOW) & (kwpos[None, :] >= 0)
        s = jnp.einsum('btghd,bsgd->btghs', qi, kw)
        p = _masked_softmax(s, wmask[None, :, None, None, :])
        o_win = jnp.einsum('btghs,bsgd->btghd', p, vw)
        return o_cmp, o_sel, o_win

    o_cmp, o_sel, o_win = lax.map(one_block, jnp.arange(S // Q_BLK))

    def unblock(o):
        return jnp.moveaxis(o, 0, 1).reshape(B, S, NSA_HEADS, NSA_DH)

    g = jax.nn.sigmoid(gates.astype(f32))
    o = g[..., 0:1] * unblock(o_cmp) + g[..., 1:2] * unblock(o_sel) + g[..., 2:3] * unblock(o_win)
    return o.reshape(B, S, NSA_HEADS * NSA_DH)


def _multiscale_pool(u, w_pool, pool_scale):
    B, S, _ = u.shape
    ug = u.astype(jnp.float32).reshape(B, S, POOL_GROUPS, POOL_GW)
    cs = jnp.cumsum(ug, axis=1)
    t = jnp.arange(S)
    outs = []
    for gi, w in enumerate(POOL_WINDOWS):
        c = cs[:, :, gi]
        prev = jnp.pad(c, ((0, 0), (w, 0), (0, 0)))[:, :S]
        mean = (c - prev) / jnp.minimum(t + 1, w).astype(jnp.float32)[None, :, None]
        outs.append(mean - ug[:, :, gi])
    pooled = jnp.stack(outs, axis=2)
    mixed = jnp.einsum('bsgc,gcd->bsgd', pooled, w_pool)
    return (mixed.reshape(B, S, POOL_WIDTH) * pool_scale).astype(u.dtype)


def _causal_attention(q, k, v):
    B, S, H, Dk = q.shape
    Dv = v.shape[-1]
    n_blk = S // Q_BLK
    f32 = jnp.float32
    qb = (q.astype(f32) * Dk ** -0.5).reshape(B, n_blk, Q_BLK, H, Dk)
    kb = k.reshape(B, n_blk, Q_BLK, H, Dk)
    vb = v.reshape(B, n_blk, Q_BLK, H, Dv)
    ar = jnp.arange(Q_BLK)

    def q_block(i):
        qi = qb[:, i]

        def kv_step(j, carry):
            m, l, acc = carry
            kj = kb[:, j].astype(f32)
            vj = vb[:, j].astype(f32)
            s = jnp.einsum('bqhd,bkhd->bhqk', qi, kj)
            causal = (j * Q_BLK + ar)[None, :] <= (i * Q_BLK + ar)[:, None]
            s = jnp.where(causal, s, -jnp.inf)
            m_new = jnp.maximum(m, jnp.max(s, -1))
            p = jnp.exp(s - m_new[..., None])
            corr = jnp.exp(m - m_new)
            return (m_new, l * corr + jnp.sum(p, -1),
                    acc * corr[..., None] + jnp.einsum('bhqk,bkhd->bhqd', p, vj))

        init = (jnp.full((B, H, Q_BLK), -jnp.inf, f32), jnp.zeros((B, H, Q_BLK), f32),
                jnp.zeros((B, H, Q_BLK, Dv), f32))
        m, l, acc = lax.fori_loop(0, i + 1, kv_step, init)
        return jnp.transpose(acc / l[..., None], (0, 2, 1, 3))

    out = lax.map(q_block, jnp.arange(n_blk))
    return jnp.moveaxis(out, 0, 1).reshape(B, S, H, Dv)


def _mla(u_cq, u_ckv, u_kr, positions, q_norm, w_uq, kv_norm, w_ukv):
    B, S, _ = u_cq.shape
    qh = (_rms_norm(u_cq, q_norm) @ w_uq).reshape(B, S, MLA_HEADS, MLA_NOPE + MLA_ROPE)
    kvh = (_rms_norm(u_ckv, kv_norm) @ w_ukv).reshape(B, S, MLA_HEADS, MLA_NOPE + MLA_DV)
    k_rope = _rope(u_kr, positions)
    q_full = jnp.concatenate([qh[..., :MLA_NOPE], _rope(qh[..., MLA_NOPE:], positions)], -1)
    k_full = jnp.concatenate([kvh[..., :MLA_NOPE],
                              jnp.broadcast_to(k_rope[:, :, None, :], (B, S, MLA_HEADS, MLA_ROPE)).astype(kvh.dtype)], -1)
    o = _causal_attention(q_full, k_full, kvh[..., MLA_NOPE:])
    return o.reshape(B, S, MLA_HEADS * MLA_DV)


def _conv_module(u, conv_w, conv_b, ln_g, ln_b):
    a, b = jnp.split(u, 2, axis=-1)
    h = a * jax.nn.sigmoid(b)
    h = lax.conv_general_dilated(h, conv_w[:, None, :].astype(h.dtype), (1,), [(CONV_K - 1, 0)],
                                 dimension_numbers=('NWC', 'WIO', 'NWC'),
                                 feature_group_count=CONV_CH) + conv_b
    return jax.nn.silu(_layer_norm(h, ln_g, ln_b))


def _hybrid_mixer(x, positions, w_in, b_in, nsa_cmp_pos, nsa_cmp_w1, nsa_cmp_w2, w_nsa_o,
                  w_pool, pool_scale, w_pool_o, mla_q_norm, w_mla_uq, mla_kv_norm, w_mla_ukv,
                  w_mla_o, conv_w, conv_b, conv_ln_g, conv_ln_b, w_conv_o, w_out):
    B, S, _ = x.shape
    u = x @ w_in + b_in
    u_q, u_kv, u_g, u_pool, u_cq, u_ckv, u_kr, u_conv, u_merge = _split(u, IN_SIZES)
    q = u_q.reshape(B, S, NSA_GROUPS, NSA_HPG, NSA_DH)
    kv = u_kv.reshape(B, S, 6, NSA_GROUPS, NSA_DH)
    br_nsa = _nsa(q, kv, u_g.reshape(B, S, NSA_HEADS, 3), nsa_cmp_pos, nsa_cmp_w1, nsa_cmp_w2) @ w_nsa_o
    br_pool = _multiscale_pool(u_pool, w_pool, pool_scale) @ w_pool_o
    br_mla = _mla(u_cq, u_ckv, u_kr, positions, mla_q_norm, w_mla_uq, mla_kv_norm, w_mla_ukv) @ w_mla_o
    br_conv = _conv_module(u_conv, conv_w, conv_b, conv_ln_g, conv_ln_b) @ w_conv_o
    g = jax.nn.sigmoid(u_merge.astype(jnp.float32)).reshape(B, S, N_BRANCH, D_MODEL)
    merged = g[:, :, 0] * br_nsa + g[:, :, 1] * br_pool + g[:, :, 2] * br_mla + g[:, :, 3] * br_conv
    return merged.astype(x.dtype) @ w_out


def _cross_attention(x, mem, w_q, w_k, w_v, w_o):
    B, S, _ = x.shape
    M = mem.shape[1]
    q = (x @ w_q).reshape(B, S, X_HEADS, X_DH)
    k = (mem @ w_k).reshape(B, M, X_HEADS, X_DH)
    v = (mem @ w_v).reshape(B, M, X_HEADS, X_DH)
    s = jnp.einsum('bshd,bmhd->bhsm', q, k).astype(jnp.float32) * X_DH ** -0.5
    p = jax.nn.softmax(s, axis=-1)
    o = jnp.einsum('bhsm,bmhd->bshd', p, v.astype(jnp.float32)).reshape(B, S, X_HEADS * X_DH)
    return o.astype(x.dtype) @ w_o


def _hier_moe(x, w_group, b_group, w_router, b_router, w_e_in, w_e_out):
    B, S, D = x.shape
    T = B * S
    f32 = jnp.float32
    h = x.reshape(T, D)
    g_logits = (h @ w_group).astype(f32) + b_group
    g_sel = jnp.argmax(g_logits, -1)
    g_w = jnp.take_along_axis(jax.nn.softmax(g_logits, -1), g_sel[:, None], 1)
    e_logits = ((h @ w_router).astype(f32) + b_router).reshape(T, MOE_GROUPS, MOE_EPG)
    e_logits = jnp.take_along_axis(e_logits, g_sel[:, None, None], 1)[:, 0]
    top_p, top_i = lax.top_k(jax.nn.softmax(e_logits, -1), MOE_TOPK)
    gate = g_w * top_p / jnp.sum(top_p, -1, keepdims=True)
    expert = g_sel[:, None] * MOE_EPG + top_i
    A = T * MOE_TOPK
    e_flat = expert.reshape(A)
    w_flat = gate.reshape(A)
    tok_flat = jnp.repeat(jnp.arange(T, dtype=jnp.int32), MOE_TOPK)
    order = jnp.argsort(e_flat)
    e_sorted = e_flat[order]
    counts = jnp.bincount(e_flat, length=N_EXPERTS)
    padded = (counts + MOE_BLOCK - 1) // MOE_BLOCK * MOE_BLOCK
    start = jnp.cumsum(counts) - counts
    pend = jnp.cumsum(padded)
    pstart = pend - padded
    dest = pstart[e_sorted] + jnp.arange(A) - start[e_sorted]
    n_blocks = -(-A // MOE_BLOCK) + N_EXPERTS
    rows = n_blocks * MOE_BLOCK
    row_tok = jnp.full((rows,), T, jnp.int32).at[dest].set(tok_flat[order])
    row_w = jnp.zeros((rows,), f32).at[dest].set(w_flat[order])
    block_expert = jnp.minimum(jnp.searchsorted(pend, jnp.arange(n_blocks) * MOE_BLOCK, side='right'), N_EXPERTS - 1)
    h_pad = jnp.concatenate([h, jnp.zeros((1, D), h.dtype)], 0)

    def expert_block(args):
        toks, wts, e = args
        xe = h_pad[toks]
        a, b = jnp.split(xe @ w_e_in[e], 2, axis=-1)
        return ((jax.nn.silu(a) * b) @ w_e_out[e]).astype(f32) * wts[:, None]

    yb = lax.map(expert_block, (row_tok.reshape(n_blocks, MOE_BLOCK), row_w.reshape(n_blocks, MOE_BLOCK), block_expert))
    y = jnp.zeros((T + 1, D), f32).at[row_tok].add(yb.reshape(rows, D))
    return y[:T].astype(x.dtype).reshape(B, S, D)


def setup_inputs(seed: int = 0) -> dict:
    key = jax.random.key(seed)
    ks = iter(jax.random.split(key, 48))
    f32 = jnp.float32
    L = DEPTH

    def w(shape, fan_in, gain=1.0):
        return jax.random.normal(next(ks), shape, f32) * (gain * fan_in ** -0.5)

    def gain(shape):
        return 1.0 + 0.02 * jax.random.normal(next(ks), shape, f32)

    def bias(shape, s=0.01):
        return s * jax.random.normal(next(ks), shape, f32)

    return {
        'x': jax.random.normal(next(ks), (BATCH, SEQ, D_MODEL), f32),
        'mem': jax.random.normal(next(ks), (BATCH, MEM_LEN, D_MODEL), f32),
        'positions': jnp.broadcast_to(jnp.arange(SEQ, dtype=jnp.int32), (BATCH, SEQ)),
        'w_in': w((L, D_MODEL, IN_WIDTH), D_MODEL),
        'b_in': bias((L, IN_WIDTH)),
        'nsa_cmp_pos': bias((L, 2, CMP_LEN, NSA_DH), 0.1),
        'nsa_cmp_w1': w((L, 2, CMP_LEN * NSA_DH, CMP_HIDDEN), CMP_LEN * NSA_DH),
        'nsa_cmp_w2': w((L, 2, CMP_HIDDEN, NSA_DH), CMP_HIDDEN),
        'w_nsa_o': w((L, NSA_HEADS * NSA_DH, D_MODEL), NSA_HEADS * NSA_DH),
        'w_pool': w((L, POOL_GROUPS, POOL_GW, POOL_GW), POOL_GW),
        'pool_scale': gain((L, POOL_WIDTH)),
        'w_pool_o': w((L, POOL_WIDTH, D_MODEL), POOL_WIDTH),
        'mla_q_norm': gain((L, MLA_Q_RANK)),
        'w_mla_uq': w((L, MLA_Q_RANK, MLA_HEADS * (MLA_NOPE + MLA_ROPE)), MLA_Q_RANK),
        'mla_kv_norm': gain((L, MLA_KV_RANK)),
        'w_mla_ukv': w((L, MLA_KV_RANK, MLA_HEADS * (MLA_NOPE + MLA_DV)), MLA_KV_RANK),
        'w_mla_o': w((L, MLA_HEADS * MLA_DV, D_MODEL), MLA_HEADS * MLA_DV),
        'conv_w': w((L, CONV_K, CONV_CH), CONV_K),
        'conv_b': bias((L, CONV_CH)),
        'conv_ln_g': gain((L, CONV_CH)),
        'conv_ln_b': bias((L, CONV_CH)),
        'w_conv_o': w((L, CONV_CH, D_MODEL), CONV_CH),
        'w_out': w((L, D_MODEL, D_MODEL), D_MODEL, DN_BETA),
        'ln_mix_g': gain((L, D_MODEL)),
        'ln_mix_b': bias((L, D_MODEL)),
        'w_xq': w((L, D_MODEL, X_HEADS * X_DH), D_MODEL),
        'w_xk': w((L, D_MODEL, X_HEADS * X_DH), D_MODEL),
        'w_xv': w((L, D_MODEL, X_HEADS * X_DH), D_MODEL, DN_BETA),
        'w_xo': w((L, X_HEADS * X_DH, D_MODEL), X_HEADS * X_DH, DN_BETA),
        'ln_x_g': gain((L, D_MODEL)),
        'ln_x_b': bias((L, D_MODEL)),
        'w_group': w((L, D_MODEL, MOE_GROUPS), D_MODEL),
        'b_group': bias((L, MOE_GROUPS)),
        'w_router': w((L, D_MODEL, N_EXPERTS), D_MODEL),
        'b_router': bias((L, N_EXPERTS)),
        'w_expert_in': w((L, N_EXPERTS, D_MODEL, 2 * EXPERT_FF), D_MODEL),
        'w_expert_out': w((L, N_EXPERTS, EXPERT_FF, D_MODEL), EXPERT_FF, DN_BETA),
        'ln_ffn_g': gain((L, D_MODEL)),
        'ln_ffn_b': bias((L, D_MODEL)),
    }


def reference(x, mem, positions, w_in, b_in, nsa_cmp_pos, nsa_cmp_w1, nsa_cmp_w2, w_nsa_o,
              w_pool, pool_scale, w_pool_o, mla_q_norm, w_mla_uq, mla_kv_norm, w_mla_ukv, w_mla_o,
              conv_w, conv_b, conv_ln_g, conv_ln_b, w_conv_o, w_out, ln_mix_g, ln_mix_b,
              w_xq, w_xk, w_xv, w_xo, ln_x_g, ln_x_b, w_group, b_group, w_router, b_router,
              w_expert_in, w_expert_out, ln_ffn_g, ln_ffn_b):
    for l in range(DEPTH):
        h = _hybrid_mixer(x, positions, w_in[l], b_in[l], nsa_cmp_pos[l], nsa_cmp_w1[l], nsa_cmp_w2[l],
                          w_nsa_o[l], w_pool[l], pool_scale[l], w_pool_o[l], mla_q_norm[l], w_mla_uq[l],
                          mla_kv_norm[l], w_mla_ukv[l], w_mla_o[l], conv_w[l], conv_b[l], conv_ln_g[l],
                          conv_ln_b[l], w_conv_o[l], w_out[l])
        x = _layer_norm(DN_ALPHA * x + h, ln_mix_g[l], ln_mix_b[l])
        h = _cross_attention(x, mem, w_xq[l], w_xk[l], w_xv[l], w_xo[l])
        x = _layer_norm(DN_ALPHA * x + h, ln_x_g[l], ln_x_b[l])
        h = _hier_moe(x, w_group[l], b_group[l], w_router[l], b_router[l], w_expert_in[l], w_expert_out[l])
        x = _layer_norm(DN_ALPHA * x + h, ln_ffn_g[l], ln_ffn_b[l])
    return x
```

```python
import functools

import numpy as np
import jax
import jax.numpy as jnp
from jax import lax
from jax.experimental import pallas as pl
from jax.experimental.pallas import tpu as pltpu

F32 = jnp.float32
BF16 = jnp.bfloat16
I32 = jnp.int32

D_MODEL = 2048
NSA_HEADS = 8
NSA_GROUPS = 2
NSA_HPG = NSA_HEADS // NSA_GROUPS
NSA_DH = 64
CMP_LEN = 32
CMP_STRIDE = 16
CMP_HIDDEN = 128
SEL_LEN = 64
SEL_TOPN = 16
WINDOW = 512
FORCE_SCORE = 1.0e4
POOL_GROUPS = 4
POOL_WINDOWS = (2, 4, 8, 16)
POOL_WIDTH = 512
POOL_GW = POOL_WIDTH // POOL_GROUPS
MLA_HEADS = 8
MLA_Q_RANK = 512
MLA_KV_RANK = 256
MLA_NOPE = 64
MLA_ROPE = 32
MLA_DV = 64
ROPE_BASE = 10000.0
CONV_CH = 512
CONV_K = 31
N_BRANCH = 4
X_HEADS = 4
X_DH = 128
MOE_GROUPS = 4
MOE_EPG = 8
N_EXPERTS = MOE_GROUPS * MOE_EPG
MOE_TOPK = 2
EXPERT_FF = 512
MOE_BLOCK = 256
LN_EPS = 1e-5
RMS_EPS = 1e-6
DEPTH = 2
DN_ALPHA = (2 * DEPTH) ** 0.25
IN_SIZES = (NSA_HEADS * NSA_DH, 6 * NSA_GROUPS * NSA_DH, 3 * NSA_HEADS, POOL_WIDTH,
            MLA_Q_RANK, MLA_KV_RANK, MLA_ROPE, 2 * CONV_CH, N_BRANCH * D_MODEL)

LANE = 128
VMEM_LIMIT = 56 * 1024 * 1024
NEG = -1.0e30
FLASH_T = 512

SLOT_Q = 0
SLOT_KV = 8
SLOT_POOL = 20
SLOT_CQ = 24
SLOT_CKV = 28
SLOT_KR = 30
SLOT_CONV = 32
SLOT_GATE = 40
N_SLOTS1 = 42
N1 = N_SLOTS1 * LANE


def _cparams(sem, vmem=VMEM_LIMIT):
    return pltpu.CompilerParams(dimension_semantics=sem, vmem_limit_bytes=vmem)


def _sigmoid(x):
    return 1.0 / (1.0 + jnp.exp(-x))


def _layer_norm(z, g, b):
    mu = jnp.mean(z, -1, keepdims=True)
    d = z - mu
    var = jnp.mean(d * d, -1, keepdims=True)
    return d * lax.rsqrt(var + LN_EPS) * g + b


def _dot(a, b):
    return jnp.dot(a, b, preferred_element_type=F32)


def _dot_nt(a, b):
    return lax.dot_general(a, b, (((1,), (1,)), ((), ())), preferred_element_type=F32)


def _dot3(a, b):
    hi = a.astype(BF16)
    r1 = a - hi.astype(F32)
    mid = r1.astype(BF16)
    lo = (r1 - mid.astype(F32)).astype(BF16)
    return _dot(hi, b) + _dot(mid, b) + _dot(lo, b)


def _mm_kernel(a_ref, b_ref, bias_ref, o_ref):
    o_ref[...] = (_dot(a_ref[...], b_ref[...]) + bias_ref[...]).astype(o_ref.dtype)


def _matmul(a, b, bias, tm, tn, out_dtype):
    m, k = a.shape
    n = b.shape[1]
    return pl.pallas_call(
        _mm_kernel,
        grid=(m // tm, n // tn),
        in_specs=[pl.BlockSpec((tm, k), lambda i, j: (i, 0)),
                  pl.BlockSpec((k, tn), lambda i, j: (0, j)),
                  pl.BlockSpec((1, tn), lambda i, j: (0, j))],
        out_specs=pl.BlockSpec((tm, tn), lambda i, j: (i, j)),
        out_shape=jax.ShapeDtypeStruct((m, n), out_dtype),
        compiler_params=_cparams(("parallel", "arbitrary")),
        name="matmul",
    )(a, b, bias)


def _mm_ln_kernel(a_ref, w_ref, x_ref, g_ref, b_ref, xo_ref, xb_ref):
    h = _dot(a_ref[...], w_ref[...])
    y = _layer_norm(DN_ALPHA * x_ref[...] + h, g_ref[...], b_ref[...])
    xo_ref[...] = y
    xb_ref[...] = y.astype(BF16)


def _matmul_res_ln(a, w, x, g, b, tm):
    m, k = a.shape
    d = w.shape[1]
    return pl.pallas_call(
        _mm_ln_kernel,
        grid=(m // tm,),
        in_specs=[pl.BlockSpec((tm, k), lambda i: (i, 0)),
                  pl.BlockSpec((k, d), lambda i: (0, 0)),
                  pl.BlockSpec((tm, d), lambda i: (i, 0)),
                  pl.BlockSpec((1, d), lambda i: (0, 0)),
                  pl.BlockSpec((1, d), lambda i: (0, 0))],
        out_specs=[pl.BlockSpec((tm, d), lambda i: (i, 0)),
                   pl.BlockSpec((tm, d), lambda i: (i, 0))],
        out_shape=[jax.ShapeDtypeStruct((m, d), F32), jax.ShapeDtypeStruct((m, d), BF16)],
        compiler_params=_cparams(("parallel",)),
        name="matmul_res_ln",
    )(a, w, x, g, b)


def _compress_kernel(a_ref, w1_ref, pos_ref, w2_ref, o_ref):
    nch = a_ref.shape[1]
    hh = _dot(a_ref[0], w1_ref[0])
    pp = _dot(pos_ref[0], w1_ref[0])
    pos_term = pp[0:1, :CMP_HIDDEN] + pp[1:2, CMP_HIDDEN:]
    h2_next = pltpu.roll(hh[:, CMP_HIDDEN:], nch - 1, 0)
    z = hh[:, :CMP_HIDDEN] + h2_next + pos_term
    hid = 0.5 * z * (1.0 + jnp.tanh(0.7978845608028654 * (z + 0.044715 * z * z * z)))
    o_ref[0] = _dot(hid.astype(BF16), w2_ref[0]).astype(o_ref.dtype)


def _nsa_compress(chunks, w1cat, pos2, w2pad):
    n4, nch, kdim = chunks.shape
    return pl.pallas_call(
        _compress_kernel,
        grid=(n4,),
        in_specs=[pl.BlockSpec((1, nch, kdim), lambda c: (c, 0, 0)),
                  pl.BlockSpec((1, kdim, 2 * CMP_HIDDEN), lambda c: (c // NSA_GROUPS, 0, 0)),
                  pl.BlockSpec((1, 8, kdim), lambda c: (c // NSA_GROUPS, 0, 0)),
                  pl.BlockSpec((1, CMP_HIDDEN, LANE), lambda c: (c // NSA_GROUPS, 0, 0))],
        out_specs=pl.BlockSpec((1, nch, LANE), lambda c: (c, 0, 0)),
        out_shape=jax.ShapeDtypeStruct((n4, nch, LANE), BF16),
        compiler_params=_cparams(("arbitrary",)),
        name="nsa_compress",
    )(chunks, w1cat, pos2, w2pad)


def _nsa_cmp_kernel(q_ref, kc_ref, vc_ref, mband_ref, ocmp_ref, selneg_ref, *, tq, n_cmp, n_sel, top_n):
    t0 = pl.program_id(0) * tq
    nch = kc_ref.shape[1]
    row = lax.broadcasted_iota(I32, (NSA_HPG * tq, 1), 0)
    t_row = t0 + (row & (tq - 1))
    col = lax.broadcasted_iota(I32, (1, nch), 1)
    vis = (col * CMP_STRIDE + (CMP_LEN - 1) <= t_row) & (col < n_cmp)

    blk = lax.broadcasted_iota(I32, (tq, n_sel), 1).astype(F32)
    cur = ((t0 + lax.broadcasted_iota(I32, (tq, 1), 0)) // SEL_LEN).astype(F32)
    valid = blk <= cur
    forced = (blk == 0.0) | (blk == cur) | (blk == cur - 1.0)

    for g in range(NSA_GROUPS):
        qs = jnp.concatenate(
            [q_ref[:, (g * NSA_HPG + h) * LANE:(g * NSA_HPG + h + 1) * LANE] for h in range(NSA_HPG)], axis=0)
        s = _dot_nt(qs, kc_ref[g])
        s = jnp.where(vis, s, NEG)
        m = jnp.max(s, -1, keepdims=True)
        p = jnp.where(vis, jnp.exp(s - m), 0.0)
        l = jnp.sum(p, -1, keepdims=True)
        p = p * (1.0 / jnp.maximum(l, 1e-30))
        o = _dot(p.astype(BF16), vc_ref[g])
        for h in range(NSA_HPG):
            hh = g * NSA_HPG + h
            ocmp_ref[:, hh * NSA_DH:(hh + 1) * NSA_DH] = o[h * tq:(h + 1) * tq, :NSA_DH]
        imp = p[0:tq]
        for h in range(1, NSA_HPG):
            imp = imp + p[h * tq:(h + 1) * tq]
        score = _dot3(imp, mband_ref[...])
        score = jnp.where(forced, FORCE_SCORE, score)
        score = jnp.where(valid, score, -1.0)

        def pick_one(_, carry):
            sc, selm = carry
            mx = jnp.max(sc, -1, keepdims=True)
            first = jnp.min(jnp.where(sc == mx, blk, float(n_sel)), -1, keepdims=True)
            pick = blk == first
            return jnp.where(pick, -2.0, sc), jnp.where(pick, 1.0, selm)

        _, selm = lax.fori_loop(0, top_n, pick_one, (score, jnp.zeros_like(score)))
        selneg = jnp.where(valid & (selm > 0.5), 0.0, -1.0)
        selneg_ref[:, g * n_sel:(g + 1) * n_sel] = selneg.astype(selneg_ref.dtype)


def _nsa_cmp(u1, kcvc, mband, tq):
    s_len = u1.shape[0]
    nch = kcvc.shape[1]
    n_sel = s_len // SEL_LEN
    n_cmp = (s_len - CMP_LEN) // CMP_STRIDE + 1
    kern = functools.partial(_nsa_cmp_kernel, tq=tq, n_cmp=n_cmp, n_sel=n_sel, top_n=min(SEL_TOPN, n_sel))
    return pl.pallas_call(
        kern,
        grid=(s_len // tq,),
        in_specs=[pl.BlockSpec((tq, NSA_HEADS * LANE), lambda i: (i, SLOT_Q // NSA_HEADS)),
                  pl.BlockSpec((NSA_GROUPS, nch, LANE), lambda i: (0, 0, 0)),
                  pl.BlockSpec((NSA_GROUPS, nch, LANE), lambda i: (1, 0, 0)),
                  pl.BlockSpec((nch, n_sel), lambda i: (0, 0))],
        out_specs=[pl.BlockSpec((tq, NSA_HEADS * NSA_DH), lambda i: (i, 0)),
                   pl.BlockSpec((tq, NSA_GROUPS * n_sel), lambda i: (i, 0))],
        out_shape=[jax.ShapeDtypeStruct((s_len, NSA_HEADS * NSA_DH), F32),
                   jax.ShapeDtypeStruct((s_len, NSA_GROUPS * n_sel), BF16)],
        compiler_params=_cparams(("parallel",)),
        name="nsa_cmp_topk",
    )(u1, kcvc, kcvc, mband)


def _flash_kernel(qi_ref, kj_ref, first_ref, last_ref, mode_ref, *refs, hpg, shared_kv, select, modes):
    if select:
        q_ref, k_ref, v_ref, sel_ref, scat_ref, eslot_ref, o_ref, m_sc, l_sc, acc_sc = refs
    else:
        q_ref, k_ref, v_ref, o_ref, m_sc, l_sc, acc_sc = refs
    p_idx = pl.program_id(1)
    t = q_ref.shape[0]
    dv = o_ref.shape[1] // hpg

    @pl.when(first_ref[p_idx] == 1)
    def _():
        m_sc[...] = jnp.full(m_sc.shape, NEG, F32)
        l_sc[...] = jnp.zeros(l_sc.shape, F32)
        acc_sc[...] = jnp.zeros(acc_sc.shape, F32)

    def step(mask_mode):
        k_all = k_ref[...]
        v_all = v_ref[...]
        if select:
            k_all = k_all + eslot_ref[...]
            bias = (_dot(sel_ref[...], scat_ref[kj_ref[p_idx]]) * (-NEG)).astype(BF16)
        if mask_mode:
            r = lax.broadcasted_iota(I32, (t, t), 0)
            c = lax.broadcasted_iota(I32, (t, t), 1)
            keep = (c <= r) if mask_mode == 1 else (c > r)
        for h in range(hpg):
            hk = 0 if shared_kv else h
            q = q_ref[:, h * LANE:(h + 1) * LANE]
            if select:
                q = q + bias
            s = _dot_nt(q, k_all[:, hk * LANE:(hk + 1) * LANE])
            if mask_mode:
                s = jnp.where(keep, s, NEG)
            m_prev = m_sc[h]
            m_new = jnp.maximum(m_prev, jnp.max(s, -1, keepdims=True))
            p = jnp.exp(s - m_new)
            corr = jnp.exp(m_prev - m_new)
            l_sc[h] = corr * l_sc[h] + jnp.sum(p, -1, keepdims=True)
            acc_sc[h] = corr * acc_sc[h] + _dot(p.astype(BF16), v_all[:, hk * LANE:(hk + 1) * LANE])
            m_sc[h] = m_new

    for mm in modes:
        pl.when(mode_ref[p_idx] == mm)(functools.partial(step, mm))

    @pl.when(last_ref[p_idx] == 1)
    def _():
        for h in range(hpg):
            o = acc_sc[h] * (1.0 / l_sc[h])
            o_ref[:, h * dv:(h + 1) * dv] = o[:, :dv].astype(o_ref.dtype)


def _pair_tables(n_tiles, window):
    rows = []
    for i in range(n_tiles):
        js = ([i - 1] if i > 0 else []) + [i] if window else list(range(i + 1))
        for j in js:
            mode = 1 if j == i else (2 if window else 0)
            rows.append((i, j, int(j == js[0]), int(j == i), mode))
    tab = np.asarray(rows, np.int32)
    return [jnp.asarray(tab[:, c]) for c in range(5)]


def _flash(q_arr, q_blk0, k_arr, k_blk0, v_arr, v_blk0, *, n_groups, hpg, shared_kv, window,
           out_dtype, dv, sel=None, scat=None, eslot=None):
    s_len = q_arr.shape[0]
    t = FLASH_T
    n_tiles = s_len // t
    nk = 1 if shared_kv else hpg
    select = sel is not None
    tables = _pair_tables(n_tiles, window)
    n_pairs = int(tables[0].shape[0])
    modes = (1, 2) if window else (0, 1)

    def qmap(g, p, qi, kj, fi, la, mo):
        return (qi[p], q_blk0 + g)

    def kmap(g, p, qi, kj, fi, la, mo):
        return (kj[p], k_blk0 + g)

    def vmap_(g, p, qi, kj, fi, la, mo):
        return (kj[p], v_blk0 + g)

    def omap(g, p, qi, kj, fi, la, mo):
        return (qi[p], g)

    in_specs = [pl.BlockSpec((t, hpg * LANE), qmap),
                pl.BlockSpec((t, nk * LANE), kmap),
                pl.BlockSpec((t, nk * LANE), vmap_)]
    args = [q_arr, k_arr, v_arr]
    if select:
        n_sel = sel.shape[1] // n_groups
        in_specs += [pl.BlockSpec((t, n_sel), lambda g, p, qi, kj, fi, la, mo: (qi[p], g)),
                     pl.BlockSpec(scat.shape, lambda g, p, qi, kj, fi, la, mo: (0, 0, 0)),
                     pl.BlockSpec(eslot.shape, lambda g, p, qi, kj, fi, la, mo: (0, 0))]
        args += [sel, scat, eslot]
    kern = functools.partial(_flash_kernel, hpg=hpg, shared_kv=shared_kv, select=select, modes=modes)
    grid_spec = pltpu.PrefetchScalarGridSpec(
        num_scalar_prefetch=5,
        grid=(n_groups, n_pairs),
        in_specs=in_specs,
        out_specs=pl.BlockSpec((t, hpg * dv), omap),
        scratch_shapes=[pltpu.VMEM((hpg, t, 1), F32), pltpu.VMEM((hpg, t, 1), F32),
                        pltpu.VMEM((hpg, t, LANE), F32)],
    )
    return pl.pallas_call(
        kern,
        grid_spec=grid_spec,
        out_shape=jax.ShapeDtypeStruct((s_len, n_groups * hpg * dv), out_dtype),
        compiler_params=_cparams(("parallel", "arbitrary")),
        name="flash_sel" if select else ("flash_win" if window else "flash_causal"),
    )(*tables, *args)


def _mla_prep_kernel(cq_ref, ckv_ref, kr_ref, pos_ref, inv_ref, qn_ref, kvn_ref, wuq_ref, wk_ref, wv_ref,
                     q_ref, k_ref, v_ref):
    def rms(x, g):
        return x * lax.rsqrt(jnp.mean(x * x, -1, keepdims=True) + RMS_EPS) * g

    ang = pos_ref[...] * inv_ref[...]
    cos, sin = jnp.cos(ang), jnp.sin(ang)
    hw = MLA_HEADS * LANE
    qh = _dot(rms(cq_ref[...].astype(F32), qn_ref[...]).astype(BF16), wuq_ref[...])
    scale = (MLA_NOPE + MLA_ROPE) ** -0.5
    cos_t = jnp.concatenate([cos] * MLA_HEADS, axis=1)
    sin_t = jnp.concatenate([sin] * MLA_HEADS, axis=1)
    q_ref[...] = ((qh[:, :hw] * cos_t + qh[:, hw:] * sin_t) * scale).astype(BF16)
    ckv = rms(ckv_ref[...].astype(F32), kvn_ref[...]).astype(BF16)
    kr = kr_ref[...].astype(F32)
    k_rope = kr[:, :LANE] * cos + kr[:, LANE:] * sin
    k_ref[...] = (_dot(ckv, wk_ref[...]) + jnp.concatenate([k_rope] * MLA_HEADS, axis=1)).astype(BF16)
    v_ref[...] = _dot(ckv, wv_ref[...]).astype(BF16)


def _mla_prep(u1, pos_col, inv_slot, qn, kvn, wuq, wk, wv, tm):
    s_len = u1.shape[0]
    hw = MLA_HEADS * LANE
    full = lambda a: pl.BlockSpec(a.shape, lambda i: (0,) * a.ndim)
    out = jax.ShapeDtypeStruct((s_len, hw), BF16)
    return pl.pallas_call(
        _mla_prep_kernel,
        grid=(s_len // tm,),
        in_specs=[pl.BlockSpec((tm, MLA_Q_RANK), lambda i: (i, SLOT_CQ * LANE // MLA_Q_RANK)),
                  pl.BlockSpec((tm, MLA_KV_RANK), lambda i: (i, SLOT_CKV * LANE // MLA_KV_RANK)),
                  pl.BlockSpec((tm, 2 * LANE), lambda i: (i, SLOT_KR // 2)),
                  pl.BlockSpec((tm, 1), lambda i: (i, 0)),
                  full(inv_slot), full(qn), full(kvn), full(wuq), full(wk), full(wv)],
        out_specs=[pl.BlockSpec((tm, hw), lambda i: (i, 0))] * 3,
        out_shape=[out, out, out],
        compiler_params=_cparams(("parallel",)),
        name="mla_prep",
    )(u1, u1, u1, pos_col, inv_slot, qn, kvn, wuq, wk, wv)


POOL_HALO = 16


def _pool_kernel(cur_ref, halo_ref, wp_ref, scale_ref, o_ref):
    i = pl.program_id(0)
    tm = cur_ref.shape[0]
    halo = jnp.where(i > 0, halo_ref[...].astype(F32), 0.0)
    x = jnp.concatenate([halo, cur_ref[...].astype(F32)], axis=0)
    sums = {1: x}
    w = 1
    while w < max(POOL_WINDOWS):
        a = sums[w]
        sums[2 * w] = a[w:] + a[:-w]
        w *= 2
    t = (i * tm + lax.broadcasted_iota(I32, (tm, 1), 0) + 1).astype(F32)
    outs = []
    for gi, w in enumerate(POOL_WINDOWS):
        lo = gi * POOL_GW
        start = POOL_HALO - (w - 1)
        win = sums[w][start:start + tm, lo:lo + POOL_GW]
        mean = win / jnp.minimum(t, float(w))
        pooled = mean - x[POOL_HALO:, lo:lo + POOL_GW]
        outs.append(_dot(pooled.astype(BF16), wp_ref[gi]))
    o_ref[...] = (jnp.concatenate(outs, axis=1) * scale_ref[...]).astype(o_ref.dtype)


def _pool(u1, w_pool, pool_scale, tm):
    s_len = u1.shape[0]
    blk = SLOT_POOL * LANE // POOL_WIDTH
    return pl.pallas_call(
        _pool_kernel,
        grid=(s_len // tm,),
        in_specs=[pl.BlockSpec((tm, POOL_WIDTH), lambda i: (i, blk)),
                  pl.BlockSpec((POOL_HALO, POOL_WIDTH),
                               lambda i: (jnp.maximum(i * (tm // POOL_HALO) - 1, 0), blk)),
                  pl.BlockSpec(w_pool.shape, lambda i: (0, 0, 0)),
                  pl.BlockSpec((1, POOL_WIDTH), lambda i: (0, 0))],
        out_specs=pl.BlockSpec((tm, POOL_WIDTH), lambda i: (i, 0)),
        out_shape=jax.ShapeDtypeStruct((s_len, POOL_WIDTH), BF16),
        compiler_params=_cparams(("parallel",)),
        name="pool",
    )(u1, u1, w_pool, pool_scale)


CONV_HALO = 32


def _conv_kernel(cur_ref, halo_ref, w_ref, b_ref, g_ref, beta_ref, o_ref, hbuf):
    i = pl.program_id(0)
    tm = cur_ref.shape[0]

    def glu(u):
        u = u.astype(F32)
        return u[:, :CONV_CH] * _sigmoid(u[:, CONV_CH:])

    hbuf[0:CONV_HALO, :] = jnp.where(i > 0, glu(halo_ref[...]), 0.0)
    hbuf[CONV_HALO:, :] = glu(cur_ref[...])
    acc = jnp.zeros((tm, CONV_CH), F32) + b_ref[...]
    for k in range(CONV_K):
        off = CONV_HALO - (CONV_K - 1) + k
        acc = acc + hbuf[off:off + tm, :] * w_ref[k:k + 1, :]
    y = _layer_norm(acc, g_ref[...], beta_ref[...])
    o_ref[...] = (y * _sigmoid(y)).astype(o_ref.dtype)


def _conv(u1, conv_w, conv_b, ln_g, ln_b, tm):
    s_len = u1.shape[0]
    blk = SLOT_CONV * LANE // (2 * CONV_CH)
    row = lambda a: pl.BlockSpec(a.shape, lambda i: (0, 0))
    return pl.pallas_call(
        _conv_kernel,
        grid=(s_len // tm,),
        in_specs=[pl.BlockSpec((tm, 2 * CONV_CH), lambda i: (i, blk)),
                  pl.BlockSpec((CONV_HALO, 2 * CONV_CH),
                               lambda i: (jnp.maximum(i * (tm // CONV_HALO) - 1, 0), blk)),
                  row(conv_w), row(conv_b), row(ln_g), row(ln_b)],
        out_specs=pl.BlockSpec((tm, CONV_CH), lambda i: (i, 0)),
        out_shape=jax.ShapeDtypeStruct((s_len, CONV_CH), BF16),
        scratch_shapes=[pltpu.VMEM((tm + CONV_HALO, CONV_CH), F32)],
        compiler_params=_cparams(("parallel",)),
        name="conv_module",
    )(u1, u1, conv_w, conv_b, ln_g, ln_b)


def _merge_kernel(ocmp_ref, osel_ref, owin_ref, gate_ref, gexp_ref, pool_ref, mla_ref, conv_ref, um_ref,
                  wn_ref, wp_ref, wm_ref, wc_ref, o_ref):
    sg = _sigmoid(gate_ref[...].astype(F32))
    nsa = (_dot3(sg, gexp_ref[0]) * ocmp_ref[...] + _dot3(sg, gexp_ref[1]) * osel_ref[...]
           + _dot3(sg, gexp_ref[2]) * owin_ref[...])
    branches = (_dot(nsa.astype(BF16), wn_ref[...]), _dot(pool_ref[...], wp_ref[...]),
                _dot(mla_ref[...], wm_ref[...]), _dot(conv_ref[...], wc_ref[...]))
    merged = None
    for j, br in enumerate(branches):
        term = _sigmoid(um_ref[:, j * D_MODEL:(j + 1) * D_MODEL].astype(F32)) * br
        merged = term if merged is None else merged + term
    o_ref[...] = merged.astype(o_ref.dtype)


def _merge(o_cmp, o_sel, o_win, u1, gexp, pooled, o_mla, conv, um, wn, wp, wm, wc, tm):
    s_len = u1.shape[0]
    tile = lambda w: pl.BlockSpec((tm, w), lambda i: (i, 0))
    full = lambda a: pl.BlockSpec(a.shape, lambda i: (0,) * a.ndim)
    return pl.pallas_call(
        _merge_kernel,
        grid=(s_len // tm,),
        in_specs=[tile(512), tile(512), tile(512),
                  pl.BlockSpec((tm, LANE), lambda i: (i, SLOT_GATE)),
                  full(gexp), tile(512), tile(512), tile(512), tile(N_BRANCH * D_MODEL),
                  full(wn), full(wp), full(wm), full(wc)],
        out_specs=tile(D_MODEL),
        out_shape=jax.ShapeDtypeStruct((s_len, D_MODEL), BF16),
        compiler_params=_cparams(("parallel",)),
        name="branch_merge",
    )(o_cmp, o_sel, o_win, u1, gexp, pooled, o_mla, conv, um, wn, wp, wm, wc)


def _xattn_kernel(xb_ref, x_ref, wq_ref, k_ref, v_ref, wo_ref, g_ref, b_ref, xo_ref, xob_ref):
    q = _dot(xb_ref[...], wq_ref[...]).astype(BF16)
    k = k_ref[...]
    v = v_ref[...]
    outs = []
    for h in range(X_HEADS):
        sl = slice(h * X_DH, (h + 1) * X_DH)
        s = _dot_nt(q[:, sl], k[:, sl]) * (X_DH ** -0.5)
        m = jnp.max(s, -1, keepdims=True)
        p = jnp.exp(s - m)
        p = p * (1.0 / jnp.sum(p, -1, keepdims=True))
        outs.append(_dot(p.astype(BF16), v[:, sl]))
    o = jnp.concatenate(outs, axis=1).astype(BF16)
    y = _layer_norm(DN_ALPHA * x_ref[...] + _dot(o, wo_ref[...]), g_ref[...], b_ref[...])
    xo_ref[...] = y
    xob_ref[...] = y.astype(BF16)


def _xattn(xb, x, wq, k, v, wo, g, b, tm):
    s_len = x.shape[0]
    tile = lambda: pl.BlockSpec((tm, D_MODEL), lambda i: (i, 0))
    full = lambda a: pl.BlockSpec(a.shape, lambda i: (0,) * a.ndim)
    return pl.pallas_call(
        _xattn_kernel,
        grid=(s_len // tm,),
        in_specs=[tile(), tile(), full(wq), full(k), full(v), full(wo), full(g), full(b)],
        out_specs=[tile(), tile()],
        out_shape=[jax.ShapeDtypeStruct((s_len, D_MODEL), F32), jax.ShapeDtypeStruct((s_len, D_MODEL), BF16)],
        compiler_params=_cparams(("parallel",)),
        name="cross_attention_ln",
    )(xb, x, wq, k, v, wo, g, b)


ROUTE_OFF = MOE_GROUPS


def _router_kernel(xb_ref, w_ref, b_ref, tri_ref, info_ref, cnt_ref, carry):
    @pl.when(pl.program_id(0) == 0)
    def _():
        carry[...] = jnp.zeros(carry.shape, F32)

    logits = _dot(xb_ref[...], w_ref[...]) + b_ref[...]
    lane = lax.broadcasted_iota(I32, logits.shape, 1).astype(F32)
    is_g = lane < float(MOE_GROUPS)
    neg_inf = -jnp.inf
    gl = jnp.where(is_g, logits, neg_inf)
    gmax = jnp.max(gl, -1, keepdims=True)
    g_sel = jnp.min(jnp.where(gl == gmax, lane, float(LANE)), -1, keepdims=True)
    g_w = 1.0 / jnp.sum(jnp.where(is_g, jnp.exp(gl - gmax), 0.0), -1, keepdims=True)
    lo = ROUTE_OFF + MOE_EPG * g_sel
    in_g = (lane >= lo) & (lane < lo + MOE_EPG)
    el = jnp.where(in_g, logits, neg_inf)
    emax = jnp.max(el, -1, keepdims=True)
    e = jnp.where(in_g, jnp.exp(el - emax), 0.0)
    p = e / jnp.sum(e, -1, keepdims=True)
    pm = jnp.where(in_g, p, -1.0)
    p1 = jnp.max(pm, -1, keepdims=True)
    i1 = jnp.min(jnp.where(pm == p1, lane, float(LANE)), -1, keepdims=True)
    pm2 = jnp.where(lane == i1, -1.0, pm)
    p2 = jnp.max(pm2, -1, keepdims=True)
    i2 = jnp.min(jnp.where(pm2 == p2, lane, float(LANE)), -1, keepdims=True)
    denom = p1 + p2
    gate1 = g_w * p1 / denom
    gate2 = g_w * p2 / denom

    oh1 = (lane == i1).astype(BF16)
    oh2 = (lane == i2).astype(BF16)
    incl1 = _dot(tri_ref[...], oh1)
    incl2 = _dot(tri_ref[...], oh2)
    tot1 = jnp.sum(oh1.astype(F32), 0, keepdims=True)
    tot2 = jnp.sum(oh2.astype(F32), 0, keepdims=True)
    base = carry[...]
    rank1 = jnp.sum(jnp.where(lane == i1, base + incl1, 0.0), -1, keepdims=True) - 1.0
    rank2 = jnp.sum(jnp.where(lane == i2, base + tot1 + incl2, 0.0), -1, keepdims=True) - 1.0
    carry[...] = base + tot1 + tot2
    cnt_ref[...] = carry[...]

    cols = (i1 - ROUTE_OFF, i2 - ROUTE_OFF, gate1, gate2, rank1, rank2)
    info = jnp.zeros(logits.shape, F32)
    for c, val in enumerate(cols):
        info = jnp.where(lane == float(c), val, info)
    info_ref[...] = info


def _router(xb, w_gr, b_gr, tri, tm):
    t_len = xb.shape[0]
    return pl.pallas_call(
        _router_kernel,
        grid=(t_len // tm,),
        in_specs=[pl.BlockSpec((tm, D_MODEL), lambda i: (i, 0)),
                  pl.BlockSpec(w_gr.shape, lambda i: (0, 0)),
                  pl.BlockSpec((1, LANE), lambda i: (0, 0)),
                  pl.BlockSpec((tm, tm), lambda i: (0, 0))],
        out_specs=[pl.BlockSpec((tm, LANE), lambda i: (i, 0)),
                   pl.BlockSpec((1, LANE), lambda i: (0, 0))],
        out_shape=[jax.ShapeDtypeStruct((t_len, LANE), F32), jax.ShapeDtypeStruct((1, LANE), F32)],
        scratch_shapes=[pltpu.VMEM((1, LANE), F32)],
        compiler_params=_cparams(("arbitrary",)),
        name="moe_router",
    )(xb, w_gr, b_gr, tri)


def _row_copy(src, src_row, dst, dst_row, sem):
    return pltpu.make_async_copy(src.at[pl.ds(src_row, 1)], dst.at[pl.ds(dst_row, 1)], sem)


def _dispatch_kernel(pos_ref, x_hbm, xe_in_hbm, xe_hbm, sem, *, td):
    del xe_in_hbm
    base = pl.program_id(0) * td

    def issue(t, c):
        for k in range(MOE_TOPK):
            _row_copy(x_hbm, base + t, xe_hbm, pos_ref[MOE_TOPK * (base + t) + k], sem).start()
        return c

    lax.fori_loop(0, td, issue, 0)

    def drain(t, c):
        for k in range(MOE_TOPK):
            _row_copy(x_hbm, 0, xe_hbm, 0, sem).wait()
        return c

    lax.fori_loop(0, td, drain, 0)


def _dispatch(pos_flat, x, rows, td):
    t_len, d = x.shape
    zeros = jnp.zeros((rows, d), x.dtype)
    grid_spec = pltpu.PrefetchScalarGridSpec(
        num_scalar_prefetch=1,
        grid=(t_len // td,),
        in_specs=[pl.BlockSpec(memory_space=pl.ANY), pl.BlockSpec(memory_space=pl.ANY)],
        out_specs=pl.BlockSpec(memory_space=pl.ANY),
        scratch_shapes=[pltpu.SemaphoreType.DMA(())],
    )
    return pl.pallas_call(
        functools.partial(_dispatch_kernel, td=td),
        grid_spec=grid_spec,
        out_shape=jax.ShapeDtypeStruct((rows, d), x.dtype),
        input_output_aliases={2: 0},
        compiler_params=pltpu.CompilerParams(dimension_semantics=("arbitrary",)),
        name="moe_dispatch",
    )(pos_flat, x, zeros)


def _expert_kernel(be_ref, nused_ref, xe_ref, win_ref, wout_ref, yb_ref):
    b = pl.program_id(0)

    @pl.when(b < nused_ref[0])
    def _():
        hcat = _dot(xe_ref[...].astype(BF16), win_ref[0])
        a = hcat[:, :EXPERT_FF]
        act = (a * _sigmoid(a) * hcat[:, EXPERT_FF:]).astype(BF16)
        yb_ref[...] = _dot(act, wout_ref[0])

    @pl.when(b >= nused_ref[0])
    def _():
        yb_ref[...] = jnp.zeros(yb_ref.shape, F32)


def _experts(block_expert, n_used, xe, w_e_in, w_e_out):
    rows, d = xe.shape
    n_blocks = rows // MOE_BLOCK
    grid_spec = pltpu.PrefetchScalarGridSpec(
        num_scalar_prefetch=2,
        grid=(n_blocks,),
        in_specs=[pl.BlockSpec((MOE_BLOCK, d), lambda b, be, nu: (b, 0)),
                  pl.BlockSpec((1, d, 2 * EXPERT_FF), lambda b, be, nu: (be[b], 0, 0)),
                  pl.BlockSpec((1, EXPERT_FF, d), lambda b, be, nu: (be[b], 0, 0))],
        out_specs=pl.BlockSpec((MOE_BLOCK, d), lambda b, be, nu: (b, 0)),
    )
    return pl.pallas_call(
        _expert_kernel,
        grid_spec=grid_spec,
        out_shape=jax.ShapeDtypeStruct((rows, d), F32),
        compiler_params=_cparams(("arbitrary",)),
        name="moe_experts",
    )(block_expert, n_used, xe, w_e_in, w_e_out)


def _combine_kernel(pos_ref, yb_hbm, x_ref, info_ref, g_ref, b_ref, xo_ref, xob_ref, buf, sem):
    tm = x_ref.shape[0]
    base = pl.program_id(0) * tm

    def issue(t, c):
        for k in range(MOE_TOPK):
            _row_copy(yb_hbm, pos_ref[MOE_TOPK * (base + t) + k], buf.at[k], t, sem).start()
        return c

    lax.fori_loop(0, tm, issue, 0)

    def drain(t, c):
        for k in range(MOE_TOPK):
            _row_copy(yb_hbm, 0, buf.at[k], 0, sem).wait()
        return c

    lax.fori_loop(0, tm, drain, 0)
    info = info_ref[...]
    y = info[:, 2:3] * buf[0] + info[:, 3:4] * buf[1]
    z = _layer_norm(DN_ALPHA * x_ref[...] + y, g_ref[...], b_ref[...])
    xo_ref[...] = z
    xob_ref[...] = z.astype(BF16)


def _combine(pos_flat, yb, x, info, g, b, tm):
    t_len, d = x.shape
    grid_spec = pltpu.PrefetchScalarGridSpec(
        num_scalar_prefetch=1,
        grid=(t_len // tm,),
        in_specs=[pl.BlockSpec(memory_space=pl.ANY),
                  pl.BlockSpec((tm, d), lambda i, pos: (i, 0)),
                  pl.BlockSpec((tm, LANE), lambda i, pos: (i, 0)),
                  pl.BlockSpec((1, d), lambda i, pos: (0, 0)),
                  pl.BlockSpec((1, d), lambda i, pos: (0, 0))],
        out_specs=[pl.BlockSpec((tm, d), lambda i, pos: (i, 0)),
                   pl.BlockSpec((tm, d), lambda i, pos: (i, 0))],
        scratch_shapes=[pltpu.VMEM((MOE_TOPK, tm, d), F32), pltpu.SemaphoreType.DMA(())],
    )
    return pl.pallas_call(
        _combine_kernel,
        grid_spec=grid_spec,
        out_shape=[jax.ShapeDtypeStruct((t_len, d), F32), jax.ShapeDtypeStruct((t_len, d), BF16)],
        compiler_params=_cparams(("arbitrary",)),
        name="moe_combine_ln",
    )(pos_flat, yb, x, info, g, b)


def _pad_cols(m, width):
    return jnp.pad(m, [(0, 0)] * (m.ndim - 1) + [(0, width - m.shape[-1])])


def _rot_half_cols(m):
    half = m.shape[-1] // 2
    return jnp.concatenate([-m[..., half:], m[..., :half]], -1)


def _layout_in1(m):
    offs = np.cumsum((0,) + IN_SIZES)
    o_q, o_kv, o_g, o_pool, o_cq, o_ckv, o_kr, o_conv = offs[:8]
    z = lambda n: jnp.zeros(m.shape[:-1] + (n,), m.dtype)
    parts = []
    for h in range(NSA_HEADS):
        parts += [m[..., o_q + h * NSA_DH:o_q + (h + 1) * NSA_DH] * (NSA_DH ** -0.5), z(LANE - NSA_DH)]
    for c in range(6 * NSA_GROUPS):
        parts += [m[..., o_kv + c * NSA_DH:o_kv + (c + 1) * NSA_DH], z(LANE - NSA_DH)]
    parts += [m[..., o_pool:o_pool + POOL_WIDTH], m[..., o_cq:o_cq + MLA_Q_RANK], m[..., o_ckv:o_ckv + MLA_KV_RANK]]
    kr = m[..., o_kr:o_kr + MLA_ROPE]
    parts += [z(MLA_NOPE), kr, z(LANE - MLA_NOPE - MLA_ROPE), z(MLA_NOPE), _rot_half_cols(kr),
              z(LANE - MLA_NOPE - MLA_ROPE)]
    parts += [m[..., o_conv:o_conv + 2 * CONV_CH], _pad_cols(m[..., o_g:o_g + 3 * NSA_HEADS], LANE), z(LANE)]
    return jnp.concatenate(parts, -1)


def _layout_mla_q(w):
    dq = MLA_NOPE + MLA_ROPE
    z = lambda n: jnp.zeros((w.shape[0], n), w.dtype)
    a, b = [], []
    for h in range(MLA_HEADS):
        rope = w[:, h * dq + MLA_NOPE:(h + 1) * dq]
        a += [w[:, h * dq:h * dq + MLA_NOPE], rope, z(LANE - dq)]
        b += [z(MLA_NOPE), _rot_half_cols(rope), z(LANE - dq)]
    return jnp.concatenate(a + b, -1)


def _layout_mla_kv(w):
    dkv = MLA_NOPE + MLA_DV
    z = jnp.zeros((w.shape[0], LANE - MLA_NOPE), w.dtype)
    k, v = [], []
    for h in range(MLA_HEADS):
        k += [w[:, h * dkv:h * dkv + MLA_NOPE], z]
        v += [w[:, h * dkv + MLA_NOPE:(h + 1) * dkv], z]
    return jnp.concatenate(k, -1), jnp.concatenate(v, -1)


def _static_tables(s_len):
    nch = s_len // CMP_STRIDE
    n_cmp = (s_len - CMP_LEN) // CMP_STRIDE + 1
    n_sel = s_len // SEL_LEN
    ratio = SEL_LEN // CMP_STRIDE
    c = np.arange(nch)[:, None]
    j = np.arange(n_sel)[None, :]
    mband = ((c >= ratio * j - 1) & (c <= ratio * j + ratio - 1) & (c < n_cmp)).astype(np.float32)
    per_tile = FLASH_T // SEL_LEN
    n_tiles = s_len // FLASH_T
    scat = np.zeros((n_tiles, n_sel, LANE), np.float32)
    for b in range(n_sel):
        scat[b // per_tile, b, NSA_DH + b % per_tile] = 1.0
    eslot = np.zeros((FLASH_T, LANE), np.float32)
    eslot[np.arange(FLASH_T), NSA_DH + np.arange(FLASH_T) // SEL_LEN] = 1.0
    gexp = np.zeros((3, LANE, NSA_HEADS * NSA_DH), np.float32)
    for h in range(NSA_HEADS):
        for jj in range(3):
            gexp[jj, h * 3 + jj, h * NSA_DH:(h + 1) * NSA_DH] = 1.0
    half = MLA_ROPE // 2
    inv = ROPE_BASE ** (-jnp.arange(half, dtype=F32) / half)
    inv_slot = jnp.concatenate([jnp.zeros((MLA_NOPE,), F32), inv, inv,
                                jnp.zeros((LANE - MLA_NOPE - MLA_ROPE,), F32)])[None, :]
    as_bf = lambda a: jnp.asarray(a, BF16)
    return as_bf(mband), as_bf(scat), as_bf(eslot), as_bf(gexp), inv_slot


def _hybrid_mixer(x, xb, pos_col, tabs, w_in, b_in, cmp_pos, cmp_w1, cmp_w2, w_nsa_o, w_pool, pool_scale, w_pool_o,
                  q_norm, w_uq, kv_norm, w_ukv, w_mla_o, conv_w, conv_b, conv_ln_g, conv_ln_b, w_conv_o, w_out,
                  ln_g, ln_b):
    s_len = x.shape[0]
    mband, scat, eslot, gexp, inv_slot = tabs
    row = lambda v: v[None, :]
    o_merge = int(sum(IN_SIZES[:8]))
    tm_proj = min(2048, s_len)
    u1 = _matmul(xb, _layout_in1(w_in[:, :o_merge]).astype(BF16), _layout_in1(row(b_in[:o_merge])),
                 tm_proj, 7 * LANE, BF16)
    um = _matmul(xb, w_in[:, o_merge:].astype(BF16), row(b_in[o_merge:]), tm_proj, 1024, BF16)

    nch = s_len // CMP_STRIDE
    kdim = CMP_STRIDE * LANE
    chunks = u1[:, SLOT_KV * LANE:(SLOT_KV + 4) * LANE].reshape(nch, CMP_STRIDE, 4, LANE)
    chunks = chunks.transpose(2, 0, 1, 3).reshape(4, nch, kdim)
    w1 = _pad_cols(cmp_w1.reshape(2, CMP_LEN, NSA_DH, CMP_HIDDEN).transpose(0, 1, 3, 2), LANE)
    w1 = w1.transpose(0, 1, 3, 2)
    w1cat = jnp.concatenate([w1[:, :CMP_STRIDE].reshape(2, kdim, CMP_HIDDEN),
                             w1[:, CMP_STRIDE:].reshape(2, kdim, CMP_HIDDEN)], -1).astype(BF16)
    posp = _pad_cols(cmp_pos, LANE)
    pos2 = jnp.stack([posp[:, :CMP_STRIDE].reshape(2, kdim), posp[:, CMP_STRIDE:].reshape(2, kdim)], 1)
    pos2 = jnp.pad(pos2, ((0, 0), (0, 6), (0, 0))).astype(BF16)
    kcvc = _nsa_compress(chunks, w1cat, pos2, _pad_cols(cmp_w2, LANE).astype(BF16))
    o_cmp, selneg = _nsa_cmp(u1, kcvc, mband, 128)
    o_sel = _flash(u1, SLOT_Q // NSA_HPG, u1, SLOT_KV + 4, u1, SLOT_KV + 6, n_groups=NSA_GROUPS, hpg=NSA_HPG,
                   shared_kv=True, window=False, out_dtype=F32, dv=NSA_DH, sel=selneg, scat=scat, eslot=eslot)
    o_win = _flash(u1, SLOT_Q // NSA_HPG, u1, SLOT_KV + 8, u1, SLOT_KV + 10, n_groups=NSA_GROUPS, hpg=NSA_HPG,
                   shared_kv=True, window=True, out_dtype=F32, dv=NSA_DH)

    wk, wv = _layout_mla_kv(w_ukv)
    q_m, k_m, v_m = _mla_prep(u1, pos_col, inv_slot, row(q_norm), row(kv_norm), _layout_mla_q(w_uq).astype(BF16),
                              wk.astype(BF16), wv.astype(BF16), 512)
    o_mla = _flash(q_m, 0, k_m, 0, v_m, 0, n_groups=MLA_HEADS // 2, hpg=2, shared_kv=False, window=False,
                   out_dtype=BF16, dv=MLA_DV)

    pooled = _pool(u1, w_pool.astype(BF16), row(pool_scale), 512)
    conv = _conv(u1, conv_w, row(conv_b), row(conv_ln_g), row(conv_ln_b), 512)
    merged = _merge(o_cmp, o_sel, o_win, u1, gexp, pooled, o_mla, conv, um, w_nsa_o.astype(BF16),
                    w_pool_o.astype(BF16), w_mla_o.astype(BF16), w_conv_o.astype(BF16), 256)
    return _matmul_res_ln(merged, w_out.astype(BF16), x, row(ln_g), row(ln_b), 512)


def _cross_attention(x, xb, mem_b, w_q, w_k, w_v, w_o, ln_g, ln_b):
    row = lambda v: v[None, :]
    kv = _matmul(mem_b, jnp.concatenate([w_k, w_v], 1).astype(BF16), jnp.zeros((1, 2 * X_HEADS * X_DH), F32),
                 mem_b.shape[0], 2 * X_HEADS * X_DH, BF16)
    hw = X_HEADS * X_DH
    return _xattn(xb, x, w_q.astype(BF16), kv[:, :hw], kv[:, hw:], w_o.astype(BF16), row(ln_g), row(ln_b), 256)


def _hier_moe(x, xb, w_group, b_group, w_router, b_router, w_e_in, w_e_out, ln_g, ln_b):
    t_len = x.shape[0]
    row = lambda v: v[None, :]
    tm_r = 512
    w_gr = _pad_cols(jnp.concatenate([w_group, w_router], 1), LANE).astype(BF16)
    b_gr = _pad_cols(row(jnp.concatenate([b_group, b_router])), LANE)
    tri = jnp.asarray(np.tril(np.ones((tm_r, tm_r), np.float32)), BF16)
    info, cnt = _router(xb, w_gr, b_gr, tri, tm_r)

    counts = cnt[0, ROUTE_OFF:ROUTE_OFF + N_EXPERTS].astype(I32)
    padded = (counts + MOE_BLOCK - 1) // MOE_BLOCK * MOE_BLOCK
    pend = jnp.cumsum(padded)
    pstart = pend - padded
    n_blocks = -(-(t_len * MOE_TOPK) // MOE_BLOCK) + N_EXPERTS
    n_used = pend[-1] // MOE_BLOCK
    blk_ids = jnp.minimum(jnp.arange(n_blocks), n_used - 1)
    block_expert = jnp.minimum(jnp.searchsorted(pend, blk_ids * MOE_BLOCK, side='right'), N_EXPERTS - 1).astype(I32)
    e_ids = info[:, 0:MOE_TOPK].astype(I32)
    pos = (pstart[e_ids] + info[:, 4:4 + MOE_TOPK].astype(I32)).reshape(-1)

    xe = _dispatch(pos, x, n_blocks * MOE_BLOCK, 512)
    yb = _experts(block_expert, n_used.reshape(1).astype(I32), xe, w_e_in.astype(BF16), w_e_out.astype(BF16))
    return _combine(pos, yb, x, info, row(ln_g), row(ln_b), 256)


def kernel(x, mem, positions, w_in, b_in, nsa_cmp_pos, nsa_cmp_w1, nsa_cmp_w2, w_nsa_o, w_pool, pool_scale, w_pool_o, mla_q_norm, w_mla_uq, mla_kv_norm, w_mla_ukv, w_mla_o, conv_w, conv_b, conv_ln_g, conv_ln_b, w_conv_o, w_out, ln_mix_g, ln_mix_b, w_xq, w_xk, w_xv, w_xo, ln_x_g, ln_x_b, w_group, b_group, w_router, b_router, w_expert_in, w_expert_out, ln_ffn_g, ln_ffn_b):
    batch, s_len, d = x.shape
    assert batch == 1 and d == D_MODEL and s_len % (2 * FLASH_T) == 0 and WINDOW == FLASH_T
    x = x[0]
    xb = x.astype(BF16)
    mem_b = mem[0].astype(BF16)
    pos_col = positions[0].astype(F32)[:, None]
    tabs = _static_tables(s_len)
    for l in range(w_in.shape[0]):
        x, xb = _hybrid_mixer(x, xb, pos_col, tabs, w_in[l], b_in[l], nsa_cmp_pos[l], nsa_cmp_w1[l], nsa_cmp_w2[l],
                              w_nsa_o[l], w_pool[l], pool_scale[l], w_pool_o[l], mla_q_norm[l], w_mla_uq[l],
                              mla_kv_norm[l], w_mla_ukv[l], w_mla_o[l], conv_w[l], conv_b[l], conv_ln_g[l],
                              conv_ln_b[l], w_conv_o[l], w_out[l], ln_mix_g[l], ln_mix_b[l])
        x, xb = _cross_attention(x, xb, mem_b, w_xq[l], w_xk[l], w_xv[l], w_xo[l], ln_x_g[l], ln_x_b[l])
        x, xb = _hier_moe(x, xb, w_group[l], b_group[l], w_router[l], b_router[l], w_expert_in[l],
                          w_expert_out[l], ln_ffn_g[l], ln_ffn_b[l])
    return x[None]
```

```python
import functools

import numpy as np
import jax
import jax.numpy as jnp
from jax import lax
from jax.experimental import pallas as pl
from jax.experimental.pallas import tpu as pltpu

F32 = jnp.float32
BF16 = jnp.bfloat16
I32 = jnp.int32

D_MODEL = 2048
NSA_HEADS = 8
NSA_GROUPS = 2
NSA_HPG = NSA_HEADS // NSA_GROUPS
NSA_DH = 64
CMP_LEN = 32
CMP_STRIDE = 16
CMP_HIDDEN = 128
SEL_LEN = 64
SEL_TOPN = 16
WINDOW = 512
FORCE_SCORE = 1.0e4
POOL_GROUPS = 4
POOL_WINDOWS = (2, 4, 8, 16)
POOL_WIDTH = 512
POOL_GW = POOL_WIDTH // POOL_GROUPS
MLA_HEADS = 8
MLA_Q_RANK = 512
MLA_KV_RANK = 256
MLA_NOPE = 64
MLA_ROPE = 32
MLA_DV = 64
ROPE_BASE = 10000.0
CONV_CH = 512
CONV_K = 31
N_BRANCH = 4
X_HEADS = 4
X_DH = 128
MOE_GROUPS = 4
MOE_EPG = 8
N_EXPERTS = MOE_GROUPS * MOE_EPG
MOE_TOPK = 2
EXPERT_FF = 512
MOE_BLOCK = 256
LN_EPS = 1e-5
RMS_EPS = 1e-6
DEPTH = 2
DN_ALPHA = (2 * DEPTH) ** 0.25
IN_SIZES = (NSA_HEADS * NSA_DH, 6 * NSA_GROUPS * NSA_DH, 3 * NSA_HEADS, POOL_WIDTH,
            MLA_Q_RANK, MLA_KV_RANK, MLA_ROPE, 2 * CONV_CH, N_BRANCH * D_MODEL)

LANE = 128
VMEM_LIMIT = 56 * 1024 * 1024
NEG = -1.0e30
FLASH_T = 512

SLOT_Q = 0
SLOT_KV = 8
SLOT_POOL = 20
SLOT_CQ = 24
SLOT_CKV = 28
SLOT_KR = 30
SLOT_CONV = 32
SLOT_GATE = 40
N_SLOTS1 = 42
N1 = N_SLOTS1 * LANE


def _cparams(sem, vmem=VMEM_LIMIT):
    return pltpu.CompilerParams(dimension_semantics=sem, vmem_limit_bytes=vmem)


def _sigmoid(x):
    return 1.0 / (1.0 + jnp.exp(-x))


def _layer_norm(z, g, b):
    mu = jnp.mean(z, -1, keepdims=True)
    d = z - mu
    var = jnp.mean(d * d, -1, keepdims=True)
    return d * lax.rsqrt(var + LN_EPS) * g + b


def _dot(a, b):
    return jnp.dot(a, b, preferred_element_type=F32)


def _dot_nt(a, b):
    return lax.dot_general(a, b, (((1,), (1,)), ((), ())), preferred_element_type=F32)


def _dot3(a, b):
    hi = a.astype(BF16)
    r1 = a - hi.astype(F32)
    mid = r1.astype(BF16)
    lo = (r1 - mid.astype(F32)).astype(BF16)
    return _dot(hi, b) + _dot(mid, b) + _dot(lo, b)


def _mm_kernel(a_ref, b_ref, bias_ref, o_ref):
    o_ref[...] = (_dot(a_ref[...], b_ref[...]) + bias_ref[...]).astype(o_ref.dtype)


def _matmul(a, b, bias, tm, tn, out_dtype):
    m, k = a.shape
    n = b.shape[1]
    return pl.pallas_call(
        _mm_kernel,
        grid=(m // tm, n // tn),
        in_specs=[pl.BlockSpec((tm, k), lambda i, j: (i, 0)),
                  pl.BlockSpec((k, tn), lambda i, j: (0, j)),
                  pl.BlockSpec((1, tn), lambda i, j: (0, j))],
        out_specs=pl.BlockSpec((tm, tn), lambda i, j: (i, j)),
        out_shape=jax.ShapeDtypeStruct((m, n), out_dtype),
        compiler_params=_cparams(("parallel", "arbitrary")),
        name="matmul",
    )(a, b, bias)


def _mm_ln_kernel(a_ref, w_ref, x_ref, g_ref, b_ref, xo_ref, xb_ref):
    h = _dot(a_ref[...], w_ref[...])
    y = _layer_norm(DN_ALPHA * x_ref[...] + h, g_ref[...], b_ref[...])
    xo_ref[...] = y
    xb_ref[...] = y.astype(BF16)


def _matmul_res_ln(a, w, x, g, b, tm):
    m, k = a.shape
    d = w.shape[1]
    return pl.pallas_call(
        _mm_ln_kernel,
        grid=(m // tm,),
        in_specs=[pl.BlockSpec((tm, k), lambda i: (i, 0)),
                  pl.BlockSpec((k, d), lambda i: (0, 0)),
                  pl.BlockSpec((tm, d), lambda i: (i, 0)),
                  pl.BlockSpec((1, d), lambda i: (0, 0)),
                  pl.BlockSpec((1, d), lambda i: (0, 0))],
        out_specs=[pl.BlockSpec((tm, d), lambda i: (i, 0)),
                   pl.BlockSpec((tm, d), lambda i: (i, 0))],
        out_shape=[jax.ShapeDtypeStruct((m, d), F32), jax.ShapeDtypeStruct((m, d), BF16)],
        compiler_params=_cparams(("parallel",)),
        name="matmul_res_ln",
    )(a, w, x, g, b)


def _compress_kernel(a_ref, w1_ref, pos_ref, w2_ref, o_ref):
    nch = a_ref.shape[1]
    hh = _dot(a_ref[0], w1_ref[0])
    pp = _dot(pos_ref[0], w1_ref[0])
    pos_term = pp[0:1, :CMP_HIDDEN] + pp[1:2, CMP_HIDDEN:]
    h2_next = pltpu.roll(hh[:, CMP_HIDDEN:], nch - 1, 0)
    z = hh[:, :CMP_HIDDEN] + h2_next + pos_term
    hid = 0.5 * z * (1.0 + jnp.tanh(0.7978845608028654 * (z + 0.044715 * z * z * z)))
    o_ref[0] = _dot(hid.astype(BF16), w2_ref[0]).astype(o_ref.dtype)


def _nsa_compress(chunks, w1cat, pos2, w2pad):
    n4, nch, kdim = chunks.shape
    return pl.pallas_call(
        _compress_kernel,
        grid=(n4,),
        in_specs=[pl.BlockSpec((1, nch, kdim), lambda c: (c, 0, 0)),
                  pl.BlockSpec((1, kdim, 2 * CMP_HIDDEN), lambda c: (c // NSA_GROUPS, 0, 0)),
                  pl.BlockSpec((1, 8, kdim), lambda c: (c // NSA_GROUPS, 0, 0)),
                  pl.BlockSpec((1, CMP_HIDDEN, LANE), lambda c: (c // NSA_GROUPS, 0, 0))],
        out_specs=pl.BlockSpec((1, nch, LANE), lambda c: (c, 0, 0)),
        out_shape=jax.ShapeDtypeStruct((n4, nch, LANE), BF16),
        compiler_params=_cparams(("arbitrary",)),
        name="nsa_compress",
    )(chunks, w1cat, pos2, w2pad)


def _nsa_cmp_kernel(q_ref, kc_ref, vc_ref, mband_ref, ocmp_ref, selneg_ref, *, tq, n_cmp, n_sel, top_n):
    t0 = pl.program_id(0) * tq
    nch = kc_ref.shape[1]
    row = lax.broadcasted_iota(I32, (NSA_HPG * tq, 1), 0)
    t_row = t0 + (row & (tq - 1))
    col = lax.broadcasted_iota(I32, (1, nch), 1)
    vis = (col * CMP_STRIDE + (CMP_LEN - 1) <= t_row) & (col < n_cmp)

    blk = lax.broadcasted_iota(I32, (NSA_GROUPS * tq, n_sel), 1).astype(F32)
    row2 = lax.broadcasted_iota(I32, (NSA_GROUPS * tq, 1), 0)
    cur = ((t0 + (row2 & (tq - 1))) // SEL_LEN).astype(F32)
    valid = blk <= cur
    forced = (blk == 0.0) | (blk == cur) | (blk == cur - 1.0)

    scores = []
    for g in range(NSA_GROUPS):
        qs = jnp.concatenate(
            [q_ref[:, (g * NSA_HPG + h) * LANE:(g * NSA_HPG + h + 1) * LANE] for h in range(NSA_HPG)], axis=0)
        s = _dot_nt(qs, kc_ref[g])
        s = jnp.where(vis, s, NEG)
        m = jnp.max(s, -1, keepdims=True)
        p = jnp.where(vis, jnp.exp(s - m), 0.0)
        l = jnp.sum(p, -1, keepdims=True)
        p = p * (1.0 / jnp.maximum(l, 1e-30))
        o = _dot(p.astype(BF16), vc_ref[g])
        for h in range(NSA_HPG):
            hh = g * NSA_HPG + h
            ocmp_ref[:, hh * NSA_DH:(hh + 1) * NSA_DH] = o[h * tq:(h + 1) * tq, :NSA_DH]
        imp = p[0:tq]
        for h in range(1, NSA_HPG):
            imp = imp + p[h * tq:(h + 1) * tq]
        scores.append(_dot3(imp, mband_ref[...]))

    score = jnp.concatenate(scores, axis=0)
    score = jnp.where(forced, FORCE_SCORE, score)
    score = jnp.where(valid, score, -1.0)

    def pick_one(_, sc):
        mx = jnp.max(sc, -1, keepdims=True)
        first = jnp.min(jnp.where(sc == mx, blk, float(n_sel)), -1, keepdims=True)
        return jnp.where(blk == first, -2.0, sc)

    sc = lax.fori_loop(0, top_n, pick_one, score)
    selneg = jnp.where(valid & (sc == -2.0), 0.0, -1.0).astype(selneg_ref.dtype)
    for g in range(NSA_GROUPS):
        selneg_ref[:, g * n_sel:(g + 1) * n_sel] = selneg[g * tq:(g + 1) * tq]


def _nsa_cmp(u1, kcvc, mband, tq):
    s_len = u1.shape[0]
    nch = kcvc.shape[1]
    n_sel = s_len // SEL_LEN
    n_cmp = (s_len - CMP_LEN) // CMP_STRIDE + 1
    kern = functools.partial(_nsa_cmp_kernel, tq=tq, n_cmp=n_cmp, n_sel=n_sel, top_n=min(SEL_TOPN, n_sel))
    return pl.pallas_call(
        kern,
        grid=(s_len // tq,),
        in_specs=[pl.BlockSpec((tq, NSA_HEADS * LANE), lambda i: (i, SLOT_Q // NSA_HEADS)),
                  pl.BlockSpec((NSA_GROUPS, nch, LANE), lambda i: (0, 0, 0)),
                  pl.BlockSpec((NSA_GROUPS, nch, LANE), lambda i: (1, 0, 0)),
                  pl.BlockSpec((nch, n_sel), lambda i: (0, 0))],
        out_specs=[pl.BlockSpec((tq, NSA_HEADS * NSA_DH), lambda i: (i, 0)),
                   pl.BlockSpec((tq, NSA_GROUPS * n_sel), lambda i: (i, 0))],
        out_shape=[jax.ShapeDtypeStruct((s_len, NSA_HEADS * NSA_DH), F32),
                   jax.ShapeDtypeStruct((s_len, NSA_GROUPS * n_sel), BF16)],
        compiler_params=_cparams(("parallel",)),
        name="nsa_cmp_topk",
    )(u1, kcvc, kcvc, mband)


def _flash_kernel(qi_ref, kj_ref, first_ref, last_ref, mode_ref, *refs, hpg, shared_kv, select, modes, dv):
    if select:
        q_ref, k_ref, v_ref, sel_ref, scat_ref, eslot_ref, o_ref, m_sc, acc_sc = refs
    else:
        q_ref, k_ref, v_ref, o_ref, m_sc, acc_sc = refs
    p_idx = pl.program_id(1)
    t = q_ref.shape[0]

    @pl.when(first_ref[p_idx] == 1)
    def _():
        m_sc[...] = jnp.full(m_sc.shape, NEG, F32)
        acc_sc[...] = jnp.zeros(acc_sc.shape, F32)

    def step(mask_mode):
        k_all = k_ref[...]
        lane = lax.broadcasted_iota(I32, (1, v_ref.shape[1]), 1)
        v_all = v_ref[...] + ((lane & (LANE - 1)) == dv).astype(BF16)
        if select:
            k_all = k_all + eslot_ref[...]
            bias = (_dot(sel_ref[...], scat_ref[kj_ref[p_idx]]) * (-NEG)).astype(BF16)
        if mask_mode:
            r = lax.broadcasted_iota(I32, (t, t), 0)
            c = lax.broadcasted_iota(I32, (t, t), 1)
            keep = (c <= r) if mask_mode == 1 else (c > r)
        for h in range(hpg):
            hk = 0 if shared_kv else h
            q = q_ref[:, h * LANE:(h + 1) * LANE]
            if select:
                q = q + bias
            s = _dot_nt(q, k_all[:, hk * LANE:(hk + 1) * LANE])
            if mask_mode:
                s = jnp.where(keep, s, NEG)
            m_prev = m_sc[h]
            m_new = jnp.maximum(m_prev, jnp.max(s, -1, keepdims=True))
            p = jnp.exp(s - jnp.concatenate([m_new] * (t // LANE), axis=1))
            acc_sc[h] = (jnp.exp(m_prev - m_new) * acc_sc[h]
                         + _dot(p.astype(BF16), v_all[:, hk * LANE:(hk + 1) * LANE]))
            m_sc[h] = m_new

    for mm in modes:
        pl.when(mode_ref[p_idx] == mm)(functools.partial(step, mm))

    @pl.when(last_ref[p_idx] == 1)
    def _():
        for h in range(hpg):
            acc = acc_sc[h]
            o = acc[:, :dv] * (1.0 / acc[:, dv:dv + 1])
            o_ref[:, h * dv:(h + 1) * dv] = o.astype(o_ref.dtype)


def _pair_tables(n_tiles, window):
    rows = []
    for i in range(n_tiles):
        js = ([i - 1] if i > 0 else []) + [i] if window else list(range(i + 1))
        for j in js:
            mode = 1 if j == i else (2 if window else 0)
            rows.append((i, j, int(j == js[0]), int(j == i), mode))
    tab = np.asarray(rows, np.int32)
    return [jnp.asarray(tab[:, c]) for c in range(5)]


def _flash(q_arr, q_blk0, k_arr, k_blk0, v_arr, v_blk0, *, n_groups, hpg, shared_kv, window,
           out_dtype, dv, sel=None, scat=None, eslot=None):
    s_len = q_arr.shape[0]
    t = FLASH_T
    n_tiles = s_len // t
    nk = 1 if shared_kv else hpg
    select = sel is not None
    tables = _pair_tables(n_tiles, window)
    n_pairs = int(tables[0].shape[0])
    modes = (1, 2) if window else (0, 1)

    def qmap(g, p, qi, kj, fi, la, mo):
        return (qi[p], q_blk0 + g)

    def kmap(g, p, qi, kj, fi, la, mo):
        return (kj[p], k_blk0 + g)

    def vmap_(g, p, qi, kj, fi, la, mo):
        return (kj[p], v_blk0 + g)

    def omap(g, p, qi, kj, fi, la, mo):
        return (qi[p], g)

    in_specs = [pl.BlockSpec((t, hpg * LANE), qmap),
                pl.BlockSpec((t, nk * LANE), kmap),
                pl.BlockSpec((t, nk * LANE), vmap_)]
    args = [q_arr, k_arr, v_arr]
    if select:
        n_sel = sel.shape[1] // n_groups
        in_specs += [pl.BlockSpec((t, n_sel), lambda g, p, qi, kj, fi, la, mo: (qi[p], g)),
                     pl.BlockSpec(scat.shape, lambda g, p, qi, kj, fi, la, mo: (0, 0, 0)),
                     pl.BlockSpec(eslot.shape, lambda g, p, qi, kj, fi, la, mo: (0, 0))]
        args += [sel, scat, eslot]
    kern = functools.partial(_flash_kernel, hpg=hpg, shared_kv=shared_kv, select=select, modes=modes, dv=dv)
    grid_spec = pltpu.PrefetchScalarGridSpec(
        num_scalar_prefetch=5,
        grid=(n_groups, n_pairs),
        in_specs=in_specs,
        out_specs=pl.BlockSpec((t, hpg * dv), omap),
        scratch_shapes=[pltpu.VMEM((hpg, t, LANE), F32), pltpu.VMEM((hpg, t, LANE), F32)],
    )
    return pl.pallas_call(
        kern,
        grid_spec=grid_spec,
        out_shape=jax.ShapeDtypeStruct((s_len, n_groups * hpg * dv), out_dtype),
        compiler_params=_cparams(("parallel", "arbitrary")),
        name="flash_sel" if select else ("flash_win" if window else "flash_causal"),
    )(*tables, *args)


def _mla_prep_kernel(cq_ref, ckv_ref, kr_ref, pos_ref, inv_ref, qn_ref, kvn_ref, wuq_ref, wk_ref, wv_ref,
                     q_ref, k_ref, v_ref):
    def rms(x, g):
        return x * lax.rsqrt(jnp.mean(x * x, -1, keepdims=True) + RMS_EPS) * g

    ang = pos_ref[...] * inv_ref[...]
    cos, sin = jnp.cos(ang), jnp.sin(ang)
    hw = MLA_HEADS * LANE
    qh = _dot(rms(cq_ref[...].astype(F32), qn_ref[...]).astype(BF16), wuq_ref[...])
    scale = (MLA_NOPE + MLA_ROPE) ** -0.5
    cos_t = jnp.concatenate([cos] * MLA_HEADS, axis=1)
    sin_t = jnp.concatenate([sin] * MLA_HEADS, axis=1)
    q_ref[...] = ((qh[:, :hw] * cos_t + qh[:, hw:] * sin_t) * scale).astype(BF16)
    ckv = rms(ckv_ref[...].astype(F32), kvn_ref[...]).astype(BF16)
    kr = kr_ref[...].astype(F32)
    k_rope = kr[:, :LANE] * cos + kr[:, LANE:] * sin
    k_ref[...] = (_dot(ckv, wk_ref[...]) + jnp.concatenate([k_rope] * MLA_HEADS, axis=1)).astype(BF16)
    v_ref[...] = _dot(ckv, wv_ref[...]).astype(BF16)


def _mla_prep(u1, pos_col, inv_slot, qn, kvn, wuq, wk, wv, tm):
    s_len = u1.shape[0]
    hw = MLA_HEADS * LANE
    full = lambda a: pl.BlockSpec(a.shape, lambda i: (0,) * a.ndim)
    out = jax.ShapeDtypeStruct((s_len, hw), BF16)
    return pl.pallas_call(
        _mla_prep_kernel,
        grid=(s_len // tm,),
        in_specs=[pl.BlockSpec((tm, MLA_Q_RANK), lambda i: (i, SLOT_CQ * LANE // MLA_Q_RANK)),
                  pl.BlockSpec((tm, MLA_KV_RANK), lambda i: (i, SLOT_CKV * LANE // MLA_KV_RANK)),
                  pl.BlockSpec((tm, 2 * LANE), lambda i: (i, SLOT_KR // 2)),
                  pl.BlockSpec((tm, 1), lambda i: (i, 0)),
                  full(inv_slot), full(qn), full(kvn), full(wuq), full(wk), full(wv)],
        out_specs=[pl.BlockSpec((tm, hw), lambda i: (i, 0))] * 3,
        out_shape=[out, out, out],
        compiler_params=_cparams(("parallel",)),
        name="mla_prep",
    )(u1, u1, u1, pos_col, inv_slot, qn, kvn, wuq, wk, wv)


POOL_HALO = 16


def _pool_kernel(cur_ref, halo_ref, wp_ref, scale_ref, o_ref):
    i = pl.program_id(0)
    tm = cur_ref.shape[0]
    halo = jnp.where(i > 0, halo_ref[...].astype(F32), 0.0)
    x = jnp.concatenate([halo, cur_ref[...].astype(F32)], axis=0)
    sums = {1: x}
    w = 1
    while w < max(POOL_WINDOWS):
        a = sums[w]
        sums[2 * w] = a[w:] + a[:-w]
        w *= 2
    t = (i * tm + lax.broadcasted_iota(I32, (tm, 1), 0) + 1).astype(F32)
    outs = []
    for gi, w in enumerate(POOL_WINDOWS):
        lo = gi * POOL_GW
        start = POOL_HALO - (w - 1)
        win = sums[w][start:start + tm, lo:lo + POOL_GW]
        mean = win / jnp.minimum(t, float(w))
        pooled = mean - x[POOL_HALO:, lo:lo + POOL_GW]
        outs.append(_dot(pooled.astype(BF16), wp_ref[gi]))
    o_ref[...] = (jnp.concatenate(outs, axis=1) * scale_ref[...]).astype(o_ref.dtype)


def _pool(u1, w_pool, pool_scale, tm):
    s_len = u1.shape[0]
    blk = SLOT_POOL * LANE // POOL_WIDTH
    return pl.pallas_call(
        _pool_kernel,
        grid=(s_len // tm,),
        in_specs=[pl.BlockSpec((tm, POOL_WIDTH), lambda i: (i, blk)),
                  pl.BlockSpec((POOL_HALO, POOL_WIDTH),
                               lambda i: (jnp.maximum(i * (tm // POOL_HALO) - 1, 0), blk)),
                  pl.BlockSpec(w_pool.shape, lambda i: (0, 0, 0)),
                  pl.BlockSpec((1, POOL_WIDTH), lambda i: (0, 0))],
        out_specs=pl.BlockSpec((tm, POOL_WIDTH), lambda i: (i, 0)),
        out_shape=jax.ShapeDtypeStruct((s_len, POOL_WIDTH), BF16),
        compiler_params=_cparams(("parallel",)),
        name="pool",
    )(u1, u1, w_pool, pool_scale)


CONV_HALO = 32


def _conv_kernel(cur_ref, halo_ref, w_ref, b_ref, g_ref, beta_ref, o_ref, hbuf):
    i = pl.program_id(0)
    tm = cur_ref.shape[0]

    def glu(u):
        u = u.astype(F32)
        return u[:, :CONV_CH] * _sigmoid(u[:, CONV_CH:])

    hbuf[0:CONV_HALO, :] = jnp.where(i > 0, glu(halo_ref[...]), 0.0)
    hbuf[CONV_HALO:, :] = glu(cur_ref[...])
    acc = jnp.zeros((tm, CONV_CH), F32) + b_ref[...]
    for k in range(CONV_K):
        off = CONV_HALO - (CONV_K - 1) + k
        acc = acc + hbuf[off:off + tm, :] * w_ref[k:k + 1, :]
    y = _layer_norm(acc, g_ref[...], beta_ref[...])
    o_ref[...] = (y * _sigmoid(y)).astype(o_ref.dtype)


def _conv(u1, conv_w, conv_b, ln_g, ln_b, tm):
    s_len = u1.shape[0]
    blk = SLOT_CONV * LANE // (2 * CONV_CH)
    row = lambda a: pl.BlockSpec(a.shape, lambda i: (0, 0))
    return pl.pallas_call(
        _conv_kernel,
        grid=(s_len // tm,),
        in_specs=[pl.BlockSpec((tm, 2 * CONV_CH), lambda i: (i, blk)),
                  pl.BlockSpec((CONV_HALO, 2 * CONV_CH),
                               lambda i: (jnp.maximum(i * (tm // CONV_HALO) - 1, 0), blk)),
                  row(conv_w), row(conv_b), row(ln_g), row(ln_b)],
        out_specs=pl.BlockSpec((tm, CONV_CH), lambda i: (i, 0)),
        out_shape=jax.ShapeDtypeStruct((s_len, CONV_CH), BF16),
        scratch_shapes=[pltpu.VMEM((tm + CONV_HALO, CONV_CH), F32)],
        compiler_params=_cparams(("parallel",)),
        name="conv_module",
    )(u1, u1, conv_w, conv_b, ln_g, ln_b)


def _merge_kernel(ocmp_ref, osel_ref, owin_ref, gate_ref, gexp_ref, pool_ref, mla_ref, conv_ref, um_ref,
                  wn_ref, wp_ref, wm_ref, wc_ref, o_ref):
    sg = _sigmoid(gate_ref[...].astype(F32))
    nsa = (_dot3(sg, gexp_ref[0]) * ocmp_ref[...] + _dot3(sg, gexp_ref[1]) * osel_ref[...]
           + _dot3(sg, gexp_ref[2]) * owin_ref[...])
    branches = (_dot(nsa.astype(BF16), wn_ref[...]), _dot(pool_ref[...], wp_ref[...]),
                _dot(mla_ref[...], wm_ref[...]), _dot(conv_ref[...], wc_ref[...]))
    merged = None
    for j, br in enumerate(branches):
        term = _sigmoid(um_ref[:, j * D_MODEL:(j + 1) * D_MODEL].astype(F32)) * br
        merged = term if merged is None else merged + term
    o_ref[...] = merged.astype(o_ref.dtype)


def _merge(o_cmp, o_sel, o_win, u1, gexp, pooled, o_mla, conv, um, wn, wp, wm, wc, tm):
    s_len = u1.shape[0]
    tile = lambda w: pl.BlockSpec((tm, w), lambda i: (i, 0))
    full = lambda a: pl.BlockSpec(a.shape, lambda i: (0,) * a.ndim)
    return pl.pallas_call(
        _merge_kernel,
        grid=(s_len // tm,),
        in_specs=[tile(512), tile(512), tile(512),
                  pl.BlockSpec((tm, LANE), lambda i: (i, SLOT_GATE)),
                  full(gexp), tile(512), tile(512), tile(512), tile(N_BRANCH * D_MODEL),
                  full(wn), full(wp), full(wm), full(wc)],
        out_specs=tile(D_MODEL),
        out_shape=jax.ShapeDtypeStruct((s_len, D_MODEL), BF16),
        compiler_params=_cparams(("parallel",)),
        name="branch_merge",
    )(o_cmp, o_sel, o_win, u1, gexp, pooled, o_mla, conv, um, wn, wp, wm, wc)


def _xattn_kernel(xb_ref, x_ref, wq_ref, k_ref, v_ref, wo_ref, g_ref, b_ref, xo_ref, xob_ref):
    q = _dot(xb_ref[...], wq_ref[...]).astype(BF16)
    k = k_ref[...]
    v = v_ref[...]
    outs = []
    for h in range(X_HEADS):
        sl = slice(h * X_DH, (h + 1) * X_DH)
        s = _dot_nt(q[:, sl], k[:, sl]) * (X_DH ** -0.5)
        m = jnp.max(s, -1, keepdims=True)
        p = jnp.exp(s - m)
        p = p * (1.0 / jnp.sum(p, -1, keepdims=True))
        outs.append(_dot(p.astype(BF16), v[:, sl]))
    o = jnp.concatenate(outs, axis=1).astype(BF16)
    y = _layer_norm(DN_ALPHA * x_ref[...] + _dot(o, wo_ref[...]), g_ref[...], b_ref[...])
    xo_ref[...] = y
    xob_ref[...] = y.astype(BF16)


def _xattn(xb, x, wq, k, v, wo, g, b, tm):
    s_len = x.shape[0]
    tile = lambda: pl.BlockSpec((tm, D_MODEL), lambda i: (i, 0))
    full = lambda a: pl.BlockSpec(a.shape, lambda i: (0,) * a.ndim)
    return pl.pallas_call(
        _xattn_kernel,
        grid=(s_len // tm,),
        in_specs=[tile(), tile(), full(wq), full(k), full(v), full(wo), full(g), full(b)],
        out_specs=[tile(), tile()],
        out_shape=[jax.ShapeDtypeStruct((s_len, D_MODEL), F32), jax.ShapeDtypeStruct((s_len, D_MODEL), BF16)],
        compiler_params=_cparams(("parallel",)),
        name="cross_attention_ln",
    )(xb, x, wq, k, v, wo, g, b)


ROUTE_OFF = MOE_GROUPS


def _router_kernel(xb_ref, w_ref, b_ref, tri_ref, info_ref, cnt_ref, carry):
    @pl.when(pl.program_id(0) == 0)
    def _():
        carry[...] = jnp.zeros(carry.shape, F32)

    logits = _dot(xb_ref[...], w_ref[...]) + b_ref[...]
    lane = lax.broadcasted_iota(I32, logits.shape, 1).astype(F32)
    is_g = lane < float(MOE_GROUPS)
    neg_inf = -jnp.inf
    gl = jnp.where(is_g, logits, neg_inf)
    gmax = jnp.max(gl, -1, keepdims=True)
    g_sel = jnp.min(jnp.where(gl == gmax, lane, float(LANE)), -1, keepdims=True)
    g_w = 1.0 / jnp.sum(jnp.where(is_g, jnp.exp(gl - gmax), 0.0), -1, keepdims=True)
    lo = ROUTE_OFF + MOE_EPG * g_sel
    in_g = (lane >= lo) & (lane < lo + MOE_EPG)
    el = jnp.where(in_g, logits, neg_inf)
    emax = jnp.max(el, -1, keepdims=True)
    e = jnp.where(in_g, jnp.exp(el - emax), 0.0)
    p = e / jnp.sum(e, -1, keepdims=True)
    pm = jnp.where(in_g, p, -1.0)
    p1 = jnp.max(pm, -1, keepdims=True)
    i1 = jnp.min(jnp.where(pm == p1, lane, float(LANE)), -1, keepdims=True)
    pm2 = jnp.where(lane == i1, -1.0, pm)
    p2 = jnp.max(pm2, -1, keepdims=True)
    i2 = jnp.min(jnp.where(pm2 == p2, lane, float(LANE)), -1, keepdims=True)
    denom = p1 + p2
    gate1 = g_w * p1 / denom
    gate2 = g_w * p2 / denom

    oh1 = (lane == i1).astype(BF16)
    oh2 = (lane == i2).astype(BF16)
    incl1 = _dot(tri_ref[...], oh1)
    incl2 = _dot(tri_ref[...], oh2)
    tot1 = jnp.sum(oh1.astype(F32), 0, keepdims=True)
    tot2 = jnp.sum(oh2.astype(F32), 0, keepdims=True)
    base = carry[...]
    rank1 = jnp.sum(jnp.where(lane == i1, base + incl1, 0.0), -1, keepdims=True) - 1.0
    rank2 = jnp.sum(jnp.where(lane == i2, base + tot1 + incl2, 0.0), -1, keepdims=True) - 1.0
    carry[...] = base + tot1 + tot2
    cnt_ref[...] = carry[...]

    cols = (i1 - ROUTE_OFF, i2 - ROUTE_OFF, gate1, gate2, rank1, rank2)
    info = jnp.zeros(logits.shape, F32)
    for c, val in enumerate(cols):
        info = jnp.where(lane == float(c), val, info)
    info_ref[...] = info


def _router(xb, w_gr, b_gr, tri, tm):
    t_len = xb.shape[0]
    return pl.pallas_call(
        _router_kernel,
        grid=(t_len // tm,),
        in_specs=[pl.BlockSpec((tm, D_MODEL), lambda i: (i, 0)),
                  pl.BlockSpec(w_gr.shape, lambda i: (0, 0)),
                  pl.BlockSpec((1, LANE), lambda i: (0, 0)),
                  pl.BlockSpec((tm, tm), lambda i: (0, 0))],
        out_specs=[pl.BlockSpec((tm, LANE), lambda i: (i, 0)),
                   pl.BlockSpec((1, LANE), lambda i: (0, 0))],
        out_shape=[jax.ShapeDtypeStruct((t_len, LANE), F32), jax.ShapeDtypeStruct((1, LANE), F32)],
        scratch_shapes=[pltpu.VMEM((1, LANE), F32)],
        compiler_params=_cparams(("arbitrary",)),
        name="moe_router",
    )(xb, w_gr, b_gr, tri)


def _row_copy(src, src_row, dst, dst_row, sem):
    return pltpu.make_async_copy(src.at[pl.ds(src_row, 1)], dst.at[pl.ds(dst_row, 1)], sem)


def _dispatch_kernel(pos_ref, x_ref, xe_in_hbm, xe_hbm, sem, *, td):
    del xe_in_hbm
    base = pl.program_id(0) * td

    def issue(t, c):
        for k in range(MOE_TOPK):
            _row_copy(x_ref, t, xe_hbm, pos_ref[MOE_TOPK * (base + t) + k], sem).start()
        return c

    lax.fori_loop(0, td, issue, 0)

    def drain(t, c):
        for k in range(MOE_TOPK):
            _row_copy(x_ref, 0, xe_hbm, 0, sem).wait()
        return c

    lax.fori_loop(0, td, drain, 0)


def _dispatch(pos_flat, x, rows, td):
    t_len, d = x.shape
    zeros = jnp.zeros((rows, d), x.dtype)
    grid_spec = pltpu.PrefetchScalarGridSpec(
        num_scalar_prefetch=1,
        grid=(t_len // td,),
        in_specs=[pl.BlockSpec((td, d), lambda i, pos: (i, 0)), pl.BlockSpec(memory_space=pl.ANY)],
        out_specs=pl.BlockSpec(memory_space=pl.ANY),
        scratch_shapes=[pltpu.SemaphoreType.DMA(())],
    )
    return pl.pallas_call(
        functools.partial(_dispatch_kernel, td=td),
        grid_spec=grid_spec,
        out_shape=jax.ShapeDtypeStruct((rows, d), x.dtype),
        input_output_aliases={2: 0},
        compiler_params=pltpu.CompilerParams(dimension_semantics=("arbitrary",)),
        name="moe_dispatch",
    )(pos_flat, x, zeros)


def _expert_kernel(be_ref, nused_ref, xe_ref, win_ref, wout_ref, yb_ref):
    b = pl.program_id(0)

    @pl.when(b < nused_ref[0])
    def _():
        hcat = _dot(xe_ref[...].astype(BF16), win_ref[0])
        a = hcat[:, :EXPERT_FF]
        act = (a * _sigmoid(a) * hcat[:, EXPERT_FF:]).astype(BF16)
        yb_ref[...] = _dot(act, wout_ref[0])

    @pl.when(b >= nused_ref[0])
    def _():
        yb_ref[...] = jnp.zeros(yb_ref.shape, F32)


def _experts(block_expert, n_used, xe, w_e_in, w_e_out):
    rows, d = xe.shape
    n_blocks = rows // MOE_BLOCK
    grid_spec = pltpu.PrefetchScalarGridSpec(
        num_scalar_prefetch=2,
        grid=(n_blocks,),
        in_specs=[pl.BlockSpec((MOE_BLOCK, d), lambda b, be, nu: (b, 0)),
                  pl.BlockSpec((1, d, 2 * EXPERT_FF), lambda b, be, nu: (be[b], 0, 0)),
                  pl.BlockSpec((1, EXPERT_FF, d), lambda b, be, nu: (be[b], 0, 0))],
        out_specs=pl.BlockSpec((MOE_BLOCK, d), lambda b, be, nu: (b, 0)),
    )
    return pl.pallas_call(
        _expert_kernel,
        grid_spec=grid_spec,
        out_shape=jax.ShapeDtypeStruct((rows, d), F32),
        compiler_params=_cparams(("arbitrary",)),
        name="moe_experts",
    )(block_expert, n_used, xe, w_e_in, w_e_out)


def _combine_kernel(pos_ref, yb_hbm, x_ref, info_ref, g_ref, b_ref, xo_ref, xob_ref, buf, sem):
    tm = x_ref.shape[0]
    base = pl.program_id(0) * tm

    def issue(t, c):
        for k in range(MOE_TOPK):
            _row_copy(yb_hbm, pos_ref[MOE_TOPK * (base + t) + k], buf.at[k], t, sem).start()
        return c

    lax.fori_loop(0, tm, issue, 0)

    def drain(t, c):
        for k in range(MOE_TOPK):
            _row_copy(yb_hbm, 0, buf.at[k], 0, sem).wait()
        return c

    lax.fori_loop(0, tm, drain, 0)
    info = info_ref[...]
    y = info[:, 2:3] * buf[0] + info[:, 3:4] * buf[1]
    z = _layer_norm(DN_ALPHA * x_ref[...] + y, g_ref[...], b_ref[...])
    xo_ref[...] = z
    xob_ref[...] = z.astype(BF16)


def _combine(pos_flat, yb, x, info, g, b, tm):
    t_len, d = x.shape
    grid_spec = pltpu.PrefetchScalarGridSpec(
        num_scalar_prefetch=1,
        grid=(t_len // tm,),
        in_specs=[pl.BlockSpec(memory_space=pl.ANY),
                  pl.BlockSpec((tm, d), lambda i, pos: (i, 0)),
                  pl.BlockSpec((tm, LANE), lambda i, pos: (i, 0)),
                  pl.BlockSpec((1, d), lambda i, pos: (0, 0)),
                  pl.BlockSpec((1, d), lambda i, pos: (0, 0))],
        out_specs=[pl.BlockSpec((tm, d), lambda i, pos: (i, 0)),
                   pl.BlockSpec((tm, d), lambda i, pos: (i, 0))],
        scratch_shapes=[pltpu.VMEM((MOE_TOPK, tm, d), F32), pltpu.SemaphoreType.DMA(())],
    )
    return pl.pallas_call(
        _combine_kernel,
        grid_spec=grid_spec,
        out_shape=[jax.ShapeDtypeStruct((t_len, d), F32), jax.ShapeDtypeStruct((t_len, d), BF16)],
        compiler_params=_cparams(("arbitrary",)),
        name="moe_combine_ln",
    )(pos_flat, yb, x, info, g, b)


def _pad_cols(m, width):
    return jnp.pad(m, [(0, 0)] * (m.ndim - 1) + [(0, width - m.shape[-1])])


def _rot_half_cols(m):
    half = m.shape[-1] // 2
    return jnp.concatenate([-m[..., half:], m[..., :half]], -1)


def _layout_in1(m):
    offs = np.cumsum((0,) + IN_SIZES)
    o_q, o_kv, o_g, o_pool, o_cq, o_ckv, o_kr, o_conv = offs[:8]
    z = lambda n: jnp.zeros(m.shape[:-1] + (n,), m.dtype)
    parts = []
    for h in range(NSA_HEADS):
        parts += [m[..., o_q + h * NSA_DH:o_q + (h + 1) * NSA_DH] * (NSA_DH ** -0.5), z(LANE - NSA_DH)]
    for c in range(6 * NSA_GROUPS):
        parts += [m[..., o_kv + c * NSA_DH:o_kv + (c + 1) * NSA_DH], z(LANE - NSA_DH)]
    parts += [m[..., o_pool:o_pool + POOL_WIDTH], m[..., o_cq:o_cq + MLA_Q_RANK], m[..., o_ckv:o_ckv + MLA_KV_RANK]]
    kr = m[..., o_kr:o_kr + MLA_ROPE]
    parts += [z(MLA_NOPE), kr, z(LANE - MLA_NOPE - MLA_ROPE), z(MLA_NOPE), _rot_half_cols(kr),
              z(LANE - MLA_NOPE - MLA_ROPE)]
    parts += [m[..., o_conv:o_conv + 2 * CONV_CH], _pad_cols(m[..., o_g:o_g + 3 * NSA_HEADS], LANE), z(LANE)]
    return jnp.concatenate(parts, -1)


def _layout_mla_q(w):
    dq = MLA_NOPE + MLA_ROPE
    z = lambda n: jnp.zeros((w.shape[0], n), w.dtype)
    a, b = [], []
    for h in range(MLA_HEADS):
        rope = w[:, h * dq + MLA_NOPE:(h + 1) * dq]
        a += [w[:, h * dq:h * dq + MLA_NOPE], rope, z(LANE - dq)]
        b += [z(MLA_NOPE), _rot_half_cols(rope), z(LANE - dq)]
    return jnp.concatenate(a + b, -1)


def _layout_mla_kv(w):
    dkv = MLA_NOPE + MLA_DV
    z = jnp.zeros((w.shape[0], LANE - MLA_NOPE), w.dtype)
    k, v = [], []
    for h in range(MLA_HEADS):
        k += [w[:, h * dkv:h * dkv + MLA_NOPE], z]
        v += [w[:, h * dkv + MLA_NOPE:(h + 1) * dkv], z]
    return jnp.concatenate(k, -1), jnp.concatenate(v, -1)


def _static_tables(s_len):
    nch = s_len // CMP_STRIDE
    n_cmp = (s_len - CMP_LEN) // CMP_STRIDE + 1
    n_sel = s_len // SEL_LEN
    ratio = SEL_LEN // CMP_STRIDE
    c = np.arange(nch)[:, None]
    j = np.arange(n_sel)[None, :]
    mband = ((c >= ratio * j - 1) & (c <= ratio * j + ratio - 1) & (c < n_cmp)).astype(np.float32)
    per_tile = FLASH_T // SEL_LEN
    n_tiles = s_len // FLASH_T
    scat = np.zeros((n_tiles, n_sel, LANE), np.float32)
    for b in range(n_sel):
        scat[b // per_tile, b, NSA_DH + b % per_tile] = 1.0
    eslot = np.zeros((FLASH_T, LANE), np.float32)
    eslot[np.arange(FLASH_T), NSA_DH + np.arange(FLASH_T) // SEL_LEN] = 1.0
    gexp = np.zeros((3, LANE, NSA_HEADS * NSA_DH), np.float32)
    for h in range(NSA_HEADS):
        for jj in range(3):
            gexp[jj, h * 3 + jj, h * NSA_DH:(h + 1) * NSA_DH] = 1.0
    half = MLA_ROPE // 2
    inv = ROPE_BASE ** (-jnp.arange(half, dtype=F32) / half)
    inv_slot = jnp.concatenate([jnp.zeros((MLA_NOPE,), F32), inv, inv,
                                jnp.zeros((LANE - MLA_NOPE - MLA_ROPE,), F32)])[None, :]
    as_bf = lambda a: jnp.asarray(a, BF16)
    return as_bf(mband), as_bf(scat), as_bf(eslot), as_bf(gexp), inv_slot


def _hybrid_mixer(x, xb, pos_col, tabs, w_in, b_in, cmp_pos, cmp_w1, cmp_w2, w_nsa_o, w_pool, pool_scale, w_pool_o,
                  q_norm, w_uq, kv_norm, w_ukv, w_mla_o, conv_w, conv_b, conv_ln_g, conv_ln_b, w_conv_o, w_out,
                  ln_g, ln_b):
    s_len = x.shape[0]
    mband, scat, eslot, gexp, inv_slot = tabs
    row = lambda v: v[None, :]
    o_merge = int(sum(IN_SIZES[:8]))
    tm_proj = min(2048, s_len)
    u1 = _matmul(xb, _layout_in1(w_in[:, :o_merge]).astype(BF16), _layout_in1(row(b_in[:o_merge])),
                 tm_proj, 7 * LANE, BF16)
    um = _matmul(xb, w_in[:, o_merge:].astype(BF16), row(b_in[o_merge:]), tm_proj, 1024, BF16)

    nch = s_len // CMP_STRIDE
    kdim = CMP_STRIDE * LANE
    chunks = u1[:, SLOT_KV * LANE:(SLOT_KV + 4) * LANE].reshape(nch, CMP_STRIDE, 4, LANE)
    chunks = chunks.transpose(2, 0, 1, 3).reshape(4, nch, kdim)
    w1 = _pad_cols(cmp_w1.reshape(2, CMP_LEN, NSA_DH, CMP_HIDDEN).transpose(0, 1, 3, 2), LANE)
    w1 = w1.transpose(0, 1, 3, 2)
    w1cat = jnp.concatenate([w1[:, :CMP_STRIDE].reshape(2, kdim, CMP_HIDDEN),
                             w1[:, CMP_STRIDE:].reshape(2, kdim, CMP_HIDDEN)], -1).astype(BF16)
    posp = _pad_cols(cmp_pos, LANE)
    pos2 = jnp.stack([posp[:, :CMP_STRIDE].reshape(2, kdim), posp[:, CMP_STRIDE:].reshape(2, kdim)], 1)
    pos2 = jnp.pad(pos2, ((0, 0), (0, 6), (0, 0))).astype(BF16)
    kcvc = _nsa_compress(chunks, w1cat, pos2, _pad_cols(cmp_w2, LANE).astype(BF16))
    o_cmp, selneg = _nsa_cmp(u1, kcvc, mband, 256)
    o_sel = _flash(u1, SLOT_Q // NSA_HPG, u1, SLOT_KV + 4, u1, SLOT_KV + 6, n_groups=NSA_GROUPS, hpg=NSA_HPG,
                   shared_kv=True, window=False, out_dtype=F32, dv=NSA_DH, sel=selneg, scat=scat, eslot=eslot)
    o_win = _flash(u1, SLOT_Q // NSA_HPG, u1, SLOT_KV + 8, u1, SLOT_KV + 10, n_groups=NSA_GROUPS, hpg=NSA_HPG,
                   shared_kv=True, window=True, out_dtype=F32, dv=NSA_DH)

    wk, wv = _layout_mla_kv(w_ukv)
    q_m, k_m, v_m = _mla_prep(u1, pos_col, inv_slot, row(q_norm), row(kv_norm), _layout_mla_q(w_uq).astype(BF16),
                              wk.astype(BF16), wv.astype(BF16), 512)
    o_mla = _flash(q_m, 0, k_m, 0, v_m, 0, n_groups=MLA_HEADS // 4, hpg=4, shared_kv=False, window=False,
                   out_dtype=BF16, dv=MLA_DV)

    pooled = _pool(u1, w_pool.astype(BF16), row(pool_scale), 512)
    conv = _conv(u1, conv_w, row(conv_b), row(conv_ln_g), row(conv_ln_b), 512)
    merged = _merge(o_cmp, o_sel, o_win, u1, gexp, pooled, o_mla, conv, um, w_nsa_o.astype(BF16),
                    w_pool_o.astype(BF16), w_mla_o.astype(BF16), w_conv_o.astype(BF16), 256)
    return _matmul_res_ln(merged, w_out.astype(BF16), x, row(ln_g), row(ln_b), 512)


def _cross_attention(x, xb, mem_b, w_q, w_k, w_v, w_o, ln_g, ln_b):
    row = lambda v: v[None, :]
    kv = _matmul(mem_b, jnp.concatenate([w_k, w_v], 1).astype(BF16), jnp.zeros((1, 2 * X_HEADS * X_DH), F32),
                 mem_b.shape[0], 2 * X_HEADS * X_DH, BF16)
    hw = X_HEADS * X_DH
    return _xattn(xb, x, w_q.astype(BF16), kv[:, :hw], kv[:, hw:], w_o.astype(BF16), row(ln_g), row(ln_b), 256)


def _hier_moe(x, xb, w_group, b_group, w_router, b_router, w_e_in, w_e_out, ln_g, ln_b):
    t_len = x.shape[0]
    row = lambda v: v[None, :]
    tm_r = 512
    w_gr = _pad_cols(jnp.concatenate([w_group, w_router], 1), LANE).astype(BF16)
    b_gr = _pad_cols(row(jnp.concatenate([b_group, b_router])), LANE)
    tri = jnp.asarray(np.tril(np.ones((tm_r, tm_r), np.float32)), BF16)
    info, cnt = _router(xb, w_gr, b_gr, tri, tm_r)

    counts = cnt[0, ROUTE_OFF:ROUTE_OFF + N_EXPERTS].astype(I32)
    padded = (counts + MOE_BLOCK - 1) // MOE_BLOCK * MOE_BLOCK
    pend = jnp.cumsum(padded)
    pstart = pend - padded
    n_blocks = -(-(t_len * MOE_TOPK) // MOE_BLOCK) + N_EXPERTS
    n_used = pend[-1] // MOE_BLOCK
    blk_ids = jnp.minimum(jnp.arange(n_blocks), n_used - 1)
    owner = jnp.sum((pend[None, :] <= (blk_ids * MOE_BLOCK)[:, None]).astype(I32), axis=1)
    block_expert = jnp.minimum(owner, N_EXPERTS - 1).astype(I32)
    e_ids = info[:, 0:MOE_TOPK].astype(I32)
    pos = (pstart[e_ids] + info[:, 4:4 + MOE_TOPK].astype(I32)).reshape(-1)

    xe = _dispatch(pos, x, n_blocks * MOE_BLOCK, 512)
    yb = _experts(block_expert, n_used.reshape(1).astype(I32), xe, w_e_in.astype(BF16), w_e_out.astype(BF16))
    return _combine(pos, yb, x, info, row(ln_g), row(ln_b), 256)


def kernel(x, mem, positions, w_in, b_in, nsa_cmp_pos, nsa_cmp_w1, nsa_cmp_w2, w_nsa_o, w_pool, pool_scale, w_pool_o, mla_q_norm, w_mla_uq, mla_kv_norm, w_mla_ukv, w_mla_o, conv_w, conv_b, conv_ln_g, conv_ln_b, w_conv_o, w_out, ln_mix_g, ln_mix_b, w_xq, w_xk, w_xv, w_xo, ln_x_g, ln_x_b, w_group, b_group, w_router, b_router, w_expert_in, w_expert_out, ln_ffn_g, ln_ffn_b):
    batch, s_len, d = x.shape
    assert batch == 1 and d == D_MODEL and s_len % (2 * FLASH_T) == 0 and WINDOW == FLASH_T
    x = x[0]
    xb = x.astype(BF16)
    mem_b = mem[0].astype(BF16)
    pos_col = positions[0].astype(F32)[:, None]
    tabs = _static_tables(s_len)
    for l in range(w_in.shape[0]):
        x, xb = _hybrid_mixer(x, xb, pos_col, tabs, w_in[l], b_in[l], nsa_cmp_pos[l], nsa_cmp_w1[l], nsa_cmp_w2[l],
                              w_nsa_o[l], w_pool[l], pool_scale[l], w_pool_o[l], mla_q_norm[l], w_mla_uq[l],
                              mla_kv_norm[l], w_mla_ukv[l], w_mla_o[l], conv_w[l], conv_b[l], conv_ln_g[l],
                              conv_ln_b[l], w_conv_o[l], w_out[l], ln_mix_g[l], ln_mix_b[l])
        x, xb = _cross_attention(x, xb, mem_b, w_xq[l], w_xk[l], w_xv[l], w_xo[l], ln_x_g[l], ln_x_b[l])
        x, xb = _hier_moe(x, xb, w_group[l], b_group[l], w_router[l], b_router[l], w_expert_in[l],
                          w_expert_out[l], ln_ffn_g[l], ln_ffn_b[l])
    return x[None]
```

```python
import functools

import numpy as np
import jax
import jax.numpy as jnp
from jax import lax
from jax.experimental import pallas as pl
from jax.experimental.pallas import tpu as pltpu

F32 = jnp.float32
BF16 = jnp.bfloat16
I32 = jnp.int32

D_MODEL = 2048
NSA_HEADS = 8
NSA_GROUPS = 2
NSA_HPG = NSA_HEADS // NSA_GROUPS
NSA_DH = 64
CMP_LEN = 32
CMP_STRIDE = 16
CMP_HIDDEN = 128
SEL_LEN = 64
SEL_TOPN = 16
WINDOW = 512
FORCE_SCORE = 1.0e4
POOL_GROUPS = 4
POOL_WINDOWS = (2, 4, 8, 16)
POOL_WIDTH = 512
POOL_GW = POOL_WIDTH // POOL_GROUPS
MLA_HEADS = 8
MLA_Q_RANK = 512
MLA_KV_RANK = 256
MLA_NOPE = 64
MLA_ROPE = 32
MLA_DV = 64
ROPE_BASE = 10000.0
CONV_CH = 512
CONV_K = 31
N_BRANCH = 4
X_HEADS = 4
X_DH = 128
MOE_GROUPS = 4
MOE_EPG = 8
N_EXPERTS = MOE_GROUPS * MOE_EPG
MOE_TOPK = 2
EXPERT_FF = 512
MOE_BLOCK = 256
LN_EPS = 1e-5
RMS_EPS = 1e-6
DEPTH = 2
DN_ALPHA = (2 * DEPTH) ** 0.25
IN_SIZES = (NSA_HEADS * NSA_DH, 6 * NSA_GROUPS * NSA_DH, 3 * NSA_HEADS, POOL_WIDTH,
            MLA_Q_RANK, MLA_KV_RANK, MLA_ROPE, 2 * CONV_CH, N_BRANCH * D_MODEL)

LANE = 128
VMEM_LIMIT = 56 * 1024 * 1024
NEG = -1.0e30
FLASH_TQ = 512
FLASH_TK = 1024
LOG2E = 1.4426950408889634

SLOT_Q = 0
SLOT_KV = 8
SLOT_POOL = 20
SLOT_CQ = 24
SLOT_CKV = 28
SLOT_KR = 30
SLOT_CONV = 32
SLOT_GATE = 40
N_SLOTS1 = 42
N1 = N_SLOTS1 * LANE


def _cparams(sem, vmem=VMEM_LIMIT):
    return pltpu.CompilerParams(dimension_semantics=sem, vmem_limit_bytes=vmem)


def _sigmoid(x):
    return 1.0 / (1.0 + jnp.exp(-x))


def _layer_norm(z, g, b):
    mu = jnp.mean(z, -1, keepdims=True)
    d = z - mu
    var = jnp.mean(d * d, -1, keepdims=True)
    return d * lax.rsqrt(var + LN_EPS) * g + b


def _dot(a, b):
    return jnp.dot(a, b, preferred_element_type=F32)


def _dot_nt(a, b):
    return lax.dot_general(a, b, (((1,), (1,)), ((), ())), preferred_element_type=F32)


def _dot3(a, b):
    hi = a.astype(BF16)
    r1 = a - hi.astype(F32)
    mid = r1.astype(BF16)
    lo = (r1 - mid.astype(F32)).astype(BF16)
    return _dot(hi, b) + _dot(mid, b) + _dot(lo, b)


def _mm_kernel(a_ref, b_ref, bias_ref, o_ref):
    o_ref[...] = (_dot(a_ref[...], b_ref[...]) + bias_ref[...]).astype(o_ref.dtype)


def _matmul(a, b, bias, tm, tn, out_dtype):
    m, k = a.shape
    n = b.shape[1]
    return pl.pallas_call(
        _mm_kernel,
        grid=(m // tm, n // tn),
        in_specs=[pl.BlockSpec((tm, k), lambda i, j: (i, 0)),
                  pl.BlockSpec((k, tn), lambda i, j: (0, j)),
                  pl.BlockSpec((1, tn), lambda i, j: (0, j))],
        out_specs=pl.BlockSpec((tm, tn), lambda i, j: (i, j)),
        out_shape=jax.ShapeDtypeStruct((m, n), out_dtype),
        compiler_params=_cparams(("parallel", "arbitrary")),
        name="matmul",
    )(a, b, bias)


def _mm_ln_kernel(a_ref, w_ref, x_ref, g_ref, b_ref, xo_ref, xb_ref):
    h = _dot(a_ref[...], w_ref[...])
    y = _layer_norm(DN_ALPHA * x_ref[...] + h, g_ref[...], b_ref[...])
    xo_ref[...] = y
    xb_ref[...] = y.astype(BF16)


def _matmul_res_ln(a, w, x, g, b, tm):
    m, k = a.shape
    d = w.shape[1]
    return pl.pallas_call(
        _mm_ln_kernel,
        grid=(m // tm,),
        in_specs=[pl.BlockSpec((tm, k), lambda i: (i, 0)),
                  pl.BlockSpec((k, d), lambda i: (0, 0)),
                  pl.BlockSpec((tm, d), lambda i: (i, 0)),
                  pl.BlockSpec((1, d), lambda i: (0, 0)),
                  pl.BlockSpec((1, d), lambda i: (0, 0))],
        out_specs=[pl.BlockSpec((tm, d), lambda i: (i, 0)),
                   pl.BlockSpec((tm, d), lambda i: (i, 0))],
        out_shape=[jax.ShapeDtypeStruct((m, d), F32), jax.ShapeDtypeStruct((m, d), BF16)],
        compiler_params=_cparams(("parallel",)),
        name="matmul_res_ln",
    )(a, w, x, g, b)


def _compress_kernel(a_ref, w1_ref, pos_ref, w2_ref, o_ref):
    nch = a_ref.shape[1]
    hh = _dot(a_ref[0], w1_ref[0])
    pp = _dot(pos_ref[0], w1_ref[0])
    pos_term = pp[0:1, :CMP_HIDDEN] + pp[1:2, CMP_HIDDEN:]
    h2_next = pltpu.roll(hh[:, CMP_HIDDEN:], nch - 1, 0)
    z = hh[:, :CMP_HIDDEN] + h2_next + pos_term
    hid = 0.5 * z * (1.0 + jnp.tanh(0.7978845608028654 * (z + 0.044715 * z * z * z)))
    o_ref[0] = _dot(hid.astype(BF16), w2_ref[0]).astype(o_ref.dtype)


def _nsa_compress(chunks, w1cat, pos2, w2pad):
    n4, nch, kdim = chunks.shape
    return pl.pallas_call(
        _compress_kernel,
        grid=(n4,),
        in_specs=[pl.BlockSpec((1, nch, kdim), lambda c: (c, 0, 0)),
                  pl.BlockSpec((1, kdim, 2 * CMP_HIDDEN), lambda c: (c // NSA_GROUPS, 0, 0)),
                  pl.BlockSpec((1, 8, kdim), lambda c: (c // NSA_GROUPS, 0, 0)),
                  pl.BlockSpec((1, CMP_HIDDEN, LANE), lambda c: (c // NSA_GROUPS, 0, 0))],
        out_specs=pl.BlockSpec((1, nch, LANE), lambda c: (c, 0, 0)),
        out_shape=jax.ShapeDtypeStruct((n4, nch, LANE), BF16),
        compiler_params=_cparams(("arbitrary",)),
        name="nsa_compress",
    )(chunks, w1cat, pos2, w2pad)


CMP_WIDTH_STEPS = 4


def _nsa_cmp_kernel(q_ref, kc_ref, vc_ref, mband_ref, ocmp_ref, selneg_ref, score_sc, *, tq, n_cmp, n_sel, top_n):
    t0 = pl.program_id(0) * tq
    nch = kc_ref.shape[1]
    row = lax.broadcasted_iota(I32, (NSA_HPG * tq, 1), 0)
    t_row = t0 + (row & (tq - 1))

    def softmax_part(width):
        col = lax.broadcasted_iota(I32, (1, width), 1)
        vis = (col * CMP_STRIDE + (CMP_LEN - 1) <= t_row) & (col < n_cmp)
        for g in range(NSA_GROUPS):
            qs = jnp.concatenate(
                [q_ref[:, (g * NSA_HPG + h) * LANE:(g * NSA_HPG + h + 1) * LANE] for h in range(NSA_HPG)], axis=0)
            s = _dot_nt(qs, kc_ref[g, 0:width, :])
            s = jnp.where(vis, s, NEG)
            m = jnp.max(s, -1, keepdims=True)
            p = jnp.where(vis, jnp.exp2(s - m), 0.0)
            l = jnp.sum(p, -1, keepdims=True)
            p = p * (1.0 / jnp.maximum(l, 1e-30))
            o = _dot(p.astype(BF16), vc_ref[g, 0:width, :])
            for h in range(NSA_HPG):
                hh = g * NSA_HPG + h
                ocmp_ref[:, hh * NSA_DH:(hh + 1) * NSA_DH] = o[h * tq:(h + 1) * tq, :NSA_DH]
            imp = p[0:tq]
            for h in range(1, NSA_HPG):
                imp = imp + p[h * tq:(h + 1) * tq]
            score_sc[g * tq:(g + 1) * tq, :] = _dot3(imp, mband_ref[0:width, :])

    last_vis = (t0 + tq - CMP_LEN) // CMP_STRIDE
    step_w = nch // CMP_WIDTH_STEPS
    variant = jnp.minimum(last_vis // step_w, CMP_WIDTH_STEPS - 1)
    for v in range(CMP_WIDTH_STEPS):
        pl.when(variant == v)(functools.partial(softmax_part, (v + 1) * step_w))

    blk = lax.broadcasted_iota(I32, (NSA_GROUPS * tq, n_sel), 1).astype(F32)
    row2 = lax.broadcasted_iota(I32, (NSA_GROUPS * tq, 1), 0)
    cur = ((t0 + (row2 & (tq - 1))) // SEL_LEN).astype(F32)
    valid = blk <= cur
    forced = (blk == 0.0) | (blk == cur) | (blk == cur - 1.0)
    score = jnp.where(forced, FORCE_SCORE, score_sc[...])
    score = jnp.where(valid, score, -1.0)

    def pick_one(_, sc):
        mx = jnp.max(sc, -1, keepdims=True)
        first = jnp.min(jnp.where(sc == mx, blk, float(n_sel)), -1, keepdims=True)
        return jnp.where(blk == first, -2.0, sc)

    sc = lax.fori_loop(0, top_n, pick_one, score)
    selneg = jnp.where(valid & (sc == -2.0), 0.0, -1.0).astype(selneg_ref.dtype)
    for g in range(NSA_GROUPS):
        selneg_ref[:, g * n_sel:(g + 1) * n_sel] = selneg[g * tq:(g + 1) * tq]


def _nsa_cmp(u1, kcvc, mband, tq):
    s_len = u1.shape[0]
    nch = kcvc.shape[1]
    n_sel = s_len // SEL_LEN
    n_cmp = (s_len - CMP_LEN) // CMP_STRIDE + 1
    kern = functools.partial(_nsa_cmp_kernel, tq=tq, n_cmp=n_cmp, n_sel=n_sel, top_n=min(SEL_TOPN, n_sel))
    return pl.pallas_call(
        kern,
        grid=(s_len // tq,),
        in_specs=[pl.BlockSpec((tq, NSA_HEADS * LANE), lambda i: (i, SLOT_Q // NSA_HEADS)),
                  pl.BlockSpec((NSA_GROUPS, nch, LANE), lambda i: (0, 0, 0)),
                  pl.BlockSpec((NSA_GROUPS, nch, LANE), lambda i: (1, 0, 0)),
                  pl.BlockSpec((nch, n_sel), lambda i: (0, 0))],
        out_specs=[pl.BlockSpec((tq, NSA_HEADS * NSA_DH), lambda i: (i, 0)),
                   pl.BlockSpec((tq, NSA_GROUPS * n_sel), lambda i: (i, 0))],
        out_shape=[jax.ShapeDtypeStruct((s_len, NSA_HEADS * NSA_DH), F32),
                   jax.ShapeDtypeStruct((s_len, NSA_GROUPS * n_sel), BF16)],
        scratch_shapes=[pltpu.VMEM((NSA_GROUPS * tq, n_sel), F32)],
        compiler_params=_cparams(("parallel",)),
        name="nsa_cmp_topk",
    )(u1, kcvc, kcvc, mband)


def _flash_kernel(qi_ref, kj_ref, first_ref, last_ref, mode_ref, off_ref, *refs, hpg, shared_kv, select, modes,
                  dv):
    if select:
        q_ref, k_ref, v_ref, sel_ref, scat_ref, eslot_ref, o_ref, m_sc, acc_sc = refs
    else:
        q_ref, k_ref, v_ref, o_ref, m_sc, acc_sc = refs
    p_idx = pl.program_id(1)
    tq = q_ref.shape[0]
    tk = k_ref.shape[0]

    @pl.when(first_ref[p_idx] == 1)
    def _():
        m_sc[...] = jnp.full(m_sc.shape, NEG, F32)
        acc_sc[...] = jnp.zeros(acc_sc.shape, F32)

    def step(mask_mode):
        k_all = k_ref[...]
        lane = lax.broadcasted_iota(I32, (1, v_ref.shape[1]), 1)
        v_all = v_ref[...] + ((lane & (LANE - 1)) == dv).astype(BF16)
        if select:
            k_all = k_all + eslot_ref[...]
            bias = (_dot(sel_ref[...], scat_ref[kj_ref[p_idx]]) * (-NEG)).astype(BF16)
        if mask_mode:
            r = lax.broadcasted_iota(I32, (tq, tk), 0)
            c = lax.broadcasted_iota(I32, (tq, tk), 1)
            keep = (c <= r + off_ref[p_idx]) if mask_mode == 1 else (c > r)
        for h in range(hpg):
            hk = 0 if shared_kv else h
            q = q_ref[:, h * LANE:(h + 1) * LANE]
            if select:
                q = q + bias
            s = _dot_nt(q, k_all[:, hk * LANE:(hk + 1) * LANE])
            if mask_mode:
                s = jnp.where(keep, s, NEG)
            m_prev = m_sc[h]
            m_new = jnp.maximum(m_prev, jnp.max(s, -1, keepdims=True))
            p = jnp.exp2(s - jnp.concatenate([m_new] * (tk // LANE), axis=1))
            acc_sc[h] = (jnp.exp2(m_prev - m_new) * acc_sc[h]
                         + _dot(p.astype(BF16), v_all[:, hk * LANE:(hk + 1) * LANE]))
            m_sc[h] = m_new

    for mm in modes:
        pl.when(mode_ref[p_idx] == mm)(functools.partial(step, mm))

    @pl.when(last_ref[p_idx] == 1)
    def _():
        for h in range(hpg):
            acc = acc_sc[h]
            o = acc[:, :dv] * (1.0 / acc[:, dv:dv + 1])
            o_ref[:, h * dv:(h + 1) * dv] = o.astype(o_ref.dtype)


def _pair_tables(s_len, tq, tk, window):
    rows = []
    for i in range(s_len // tq):
        if window:
            js = ([i - 1] if i > 0 else []) + [i]
        else:
            js = list(range((i * tq + tq - 1) // tk + 1))
        for j in js:
            if window:
                mode = 1 if j == i else 2
            else:
                mode = 0 if (j + 1) * tk - 1 <= i * tq else 1
            rows.append((i, j, int(j == js[0]), int(j == js[-1]), mode, i * tq - j * tk))
    tab = np.asarray(rows, np.int32)
    return [jnp.asarray(tab[:, c]) for c in range(tab.shape[1])]


def _flash(q_arr, q_blk0, k_arr, k_blk0, v_arr, v_blk0, *, n_groups, hpg, shared_kv, window,
           out_dtype, dv, tk, sel=None, scat=None, eslot=None):
    s_len = q_arr.shape[0]
    tq = FLASH_TQ
    assert not window or tk == tq == WINDOW
    nk = 1 if shared_kv else hpg
    select = sel is not None
    tables = _pair_tables(s_len, tq, tk, window)
    n_pairs = int(tables[0].shape[0])
    modes = (1, 2) if window else (0, 1)

    def qmap(g, p, qi, kj, *_):
        return (qi[p], q_blk0 + g)

    def kmap(g, p, qi, kj, *_):
        return (kj[p], k_blk0 + g)

    def vmap_(g, p, qi, kj, *_):
        return (kj[p], v_blk0 + g)

    def omap(g, p, qi, kj, *_):
        return (qi[p], g)

    in_specs = [pl.BlockSpec((tq, hpg * LANE), qmap),
                pl.BlockSpec((tk, nk * LANE), kmap),
                pl.BlockSpec((tk, nk * LANE), vmap_)]
    args = [q_arr, k_arr, v_arr]
    if select:
        n_sel = sel.shape[1] // n_groups
        in_specs += [pl.BlockSpec((tq, n_sel), omap),
                     pl.BlockSpec(scat.shape, lambda g, p, *_: (0, 0, 0)),
                     pl.BlockSpec(eslot.shape, lambda g, p, *_: (0, 0))]
        args += [sel, scat, eslot]
    kern = functools.partial(_flash_kernel, hpg=hpg, shared_kv=shared_kv, select=select, modes=modes, dv=dv)
    grid_spec = pltpu.PrefetchScalarGridSpec(
        num_scalar_prefetch=len(tables),
        grid=(n_groups, n_pairs),
        in_specs=in_specs,
        out_specs=pl.BlockSpec((tq, hpg * dv), omap),
        scratch_shapes=[pltpu.VMEM((hpg, tq, LANE), F32), pltpu.VMEM((hpg, tq, LANE), F32)],
    )
    return pl.pallas_call(
        kern,
        grid_spec=grid_spec,
        out_shape=jax.ShapeDtypeStruct((s_len, n_groups * hpg * dv), out_dtype),
        compiler_params=_cparams(("parallel", "arbitrary")),
        name="flash_sel" if select else ("flash_win" if window else "flash_causal"),
    )(*tables, *args)


def _mla_prep_kernel(cq_ref, ckv_ref, kr_ref, pos_ref, inv_ref, qn_ref, kvn_ref, wuq_ref, wk_ref, wv_ref,
                     q_ref, k_ref, v_ref):
    def rms(x, g):
        return x * lax.rsqrt(jnp.mean(x * x, -1, keepdims=True) + RMS_EPS) * g

    ang = pos_ref[...] * inv_ref[...]
    cos, sin = jnp.cos(ang), jnp.sin(ang)
    hw = MLA_HEADS * LANE
    qh = _dot(rms(cq_ref[...].astype(F32), qn_ref[...]).astype(BF16), wuq_ref[...])
    scale = (MLA_NOPE + MLA_ROPE) ** -0.5 * LOG2E
    cos_t = jnp.concatenate([cos] * MLA_HEADS, axis=1)
    sin_t = jnp.concatenate([sin] * MLA_HEADS, axis=1)
    q_ref[...] = ((qh[:, :hw] * cos_t + qh[:, hw:] * sin_t) * scale).astype(BF16)
    ckv = rms(ckv_ref[...].astype(F32), kvn_ref[...]).astype(BF16)
    kr = kr_ref[...].astype(F32)
    k_rope = kr[:, :LANE] * cos + kr[:, LANE:] * sin
    k_ref[...] = (_dot(ckv, wk_ref[...]) + jnp.concatenate([k_rope] * MLA_HEADS, axis=1)).astype(BF16)
    v_ref[...] = _dot(ckv, wv_ref[...]).astype(BF16)


def _mla_prep(u1, pos_col, inv_slot, qn, kvn, wuq, wk, wv, tm):
    s_len = u1.shape[0]
    hw = MLA_HEADS * LANE
    full = lambda a: pl.BlockSpec(a.shape, lambda i: (0,) * a.ndim)
    out = jax.ShapeDtypeStruct((s_len, hw), BF16)
    return pl.pallas_call(
        _mla_prep_kernel,
        grid=(s_len // tm,),
        in_specs=[pl.BlockSpec((tm, MLA_Q_RANK), lambda i: (i, SLOT_CQ * LANE // MLA_Q_RANK)),
                  pl.BlockSpec((tm, MLA_KV_RANK), lambda i: (i, SLOT_CKV * LANE // MLA_KV_RANK)),
                  pl.BlockSpec((tm, 2 * LANE), lambda i: (i, SLOT_KR // 2)),
                  pl.BlockSpec((tm, 1), lambda i: (i, 0)),
                  full(inv_slot), full(qn), full(kvn), full(wuq), full(wk), full(wv)],
        out_specs=[pl.BlockSpec((tm, hw), lambda i: (i, 0))] * 3,
        out_shape=[out, out, out],
        compiler_params=_cparams(("parallel",)),
        name="mla_prep",
    )(u1, u1, u1, pos_col, inv_slot, qn, kvn, wuq, wk, wv)


POOL_HALO = 16


def _pool_kernel(cur_ref, halo_ref, wp_ref, scale_ref, o_ref):
    i = pl.program_id(0)
    tm = cur_ref.shape[0]
    halo = jnp.where(i > 0, halo_ref[...].astype(F32), 0.0)
    x = jnp.concatenate([halo, cur_ref[...].astype(F32)], axis=0)
    sums = {1: x}
    w = 1
    while w < max(POOL_WINDOWS):
        a = sums[w]
        sums[2 * w] = a[w:] + a[:-w]
        w *= 2
    t = (i * tm + lax.broadcasted_iota(I32, (tm, 1), 0) + 1).astype(F32)
    outs = []
    for gi, w in enumerate(POOL_WINDOWS):
        lo = gi * POOL_GW
        start = POOL_HALO - (w - 1)
        win = sums[w][start:start + tm, lo:lo + POOL_GW]
        mean = win / jnp.minimum(t, float(w))
        pooled = mean - x[POOL_HALO:, lo:lo + POOL_GW]
        outs.append(_dot(pooled.astype(BF16), wp_ref[gi]))
    o_ref[...] = (jnp.concatenate(outs, axis=1) * scale_ref[...]).astype(o_ref.dtype)


def _pool(u1, w_pool, pool_scale, tm):
    s_len = u1.shape[0]
    blk = SLOT_POOL * LANE // POOL_WIDTH
    return pl.pallas_call(
        _pool_kernel,
        grid=(s_len // tm,),
        in_specs=[pl.BlockSpec((tm, POOL_WIDTH), lambda i: (i, blk)),
                  pl.BlockSpec((POOL_HALO, POOL_WIDTH),
                               lambda i: (jnp.maximum(i * (tm // POOL_HALO) - 1, 0), blk)),
                  pl.BlockSpec(w_pool.shape, lambda i: (0, 0, 0)),
                  pl.BlockSpec((1, POOL_WIDTH), lambda i: (0, 0))],
        out_specs=pl.BlockSpec((tm, POOL_WIDTH), lambda i: (i, 0)),
        out_shape=jax.ShapeDtypeStruct((s_len, POOL_WIDTH), BF16),
        compiler_params=_cparams(("parallel",)),
        name="pool",
    )(u1, u1, w_pool, pool_scale)


CONV_HALO = 32


def _conv_kernel(cur_ref, halo_ref, w_ref, b_ref, g_ref, beta_ref, o_ref, hbuf):
    i = pl.program_id(0)
    tm = cur_ref.shape[0]

    def glu(u):
        u = u.astype(F32)
        return u[:, :CONV_CH] * _sigmoid(u[:, CONV_CH:])

    hbuf[0:CONV_HALO, :] = jnp.where(i > 0, glu(halo_ref[...]), 0.0)
    hbuf[CONV_HALO:, :] = glu(cur_ref[...])
    acc = jnp.zeros((tm, CONV_CH), F32) + b_ref[...]
    for k in range(CONV_K):
        off = CONV_HALO - (CONV_K - 1) + k
        acc = acc + hbuf[off:off + tm, :] * w_ref[k:k + 1, :]
    y = _layer_norm(acc, g_ref[...], beta_ref[...])
    o_ref[...] = (y * _sigmoid(y)).astype(o_ref.dtype)


def _conv(u1, conv_w, conv_b, ln_g, ln_b, tm):
    s_len = u1.shape[0]
    blk = SLOT_CONV * LANE // (2 * CONV_CH)
    row = lambda a: pl.BlockSpec(a.shape, lambda i: (0, 0))
    return pl.pallas_call(
        _conv_kernel,
        grid=(s_len // tm,),
        in_specs=[pl.BlockSpec((tm, 2 * CONV_CH), lambda i: (i, blk)),
                  pl.BlockSpec((CONV_HALO, 2 * CONV_CH),
                               lambda i: (jnp.maximum(i * (tm // CONV_HALO) - 1, 0), blk)),
                  row(conv_w), row(conv_b), row(ln_g), row(ln_b)],
        out_specs=pl.BlockSpec((tm, CONV_CH), lambda i: (i, 0)),
        out_shape=jax.ShapeDtypeStruct((s_len, CONV_CH), BF16),
        scratch_shapes=[pltpu.VMEM((tm + CONV_HALO, CONV_CH), F32)],
        compiler_params=_cparams(("parallel",)),
        name="conv_module",
    )(u1, u1, conv_w, conv_b, ln_g, ln_b)


def _merge_kernel(ocmp_ref, osel_ref, owin_ref, gate_ref, gexp_ref, pool_ref, mla_ref, conv_ref, um_ref,
                  wn_ref, wp_ref, wm_ref, wc_ref, o_ref):
    sg = _sigmoid(gate_ref[...].astype(F32))
    nsa = (_dot3(sg, gexp_ref[0]) * ocmp_ref[...] + _dot3(sg, gexp_ref[1]) * osel_ref[...]
           + _dot3(sg, gexp_ref[2]) * owin_ref[...])
    branches = (_dot(nsa.astype(BF16), wn_ref[...]), _dot(pool_ref[...], wp_ref[...]),
                _dot(mla_ref[...], wm_ref[...]), _dot(conv_ref[...], wc_ref[...]))
    merged = None
    for j, br in enumerate(branches):
        term = _sigmoid(um_ref[:, j * D_MODEL:(j + 1) * D_MODEL].astype(F32)) * br
        merged = term if merged is None else merged + term
    o_ref[...] = merged.astype(o_ref.dtype)


def _merge(o_cmp, o_sel, o_win, u1, gexp, pooled, o_mla, conv, um, wn, wp, wm, wc, tm):
    s_len = u1.shape[0]
    tile = lambda w: pl.BlockSpec((tm, w), lambda i: (i, 0))
    full = lambda a: pl.BlockSpec(a.shape, lambda i: (0,) * a.ndim)
    return pl.pallas_call(
        _merge_kernel,
        grid=(s_len // tm,),
        in_specs=[tile(512), tile(512), tile(512),
                  pl.BlockSpec((tm, LANE), lambda i: (i, SLOT_GATE)),
                  full(gexp), tile(512), tile(512), tile(512), tile(N_BRANCH * D_MODEL),
                  full(wn), full(wp), full(wm), full(wc)],
        out_specs=tile(D_MODEL),
        out_shape=jax.ShapeDtypeStruct((s_len, D_MODEL), BF16),
        compiler_params=_cparams(("parallel",)),
        name="branch_merge",
    )(o_cmp, o_sel, o_win, u1, gexp, pooled, o_mla, conv, um, wn, wp, wm, wc)


def _xattn_kernel(xb_ref, x_ref, wq_ref, k_ref, v_ref, wo_ref, g_ref, b_ref, xo_ref, xob_ref):
    q = _dot(xb_ref[...], wq_ref[...]).astype(BF16)
    k = k_ref[...]
    v = v_ref[...]
    outs = []
    for h in range(X_HEADS):
        sl = slice(h * X_DH, (h + 1) * X_DH)
        s = _dot_nt(q[:, sl], k[:, sl]) * (X_DH ** -0.5)
        m = jnp.max(s, -1, keepdims=True)
        p = jnp.exp(s - m)
        p = p * (1.0 / jnp.sum(p, -1, keepdims=True))
        outs.append(_dot(p.astype(BF16), v[:, sl]))
    o = jnp.concatenate(outs, axis=1).astype(BF16)
    y = _layer_norm(DN_ALPHA * x_ref[...] + _dot(o, wo_ref[...]), g_ref[...], b_ref[...])
    xo_ref[...] = y
    xob_ref[...] = y.astype(BF16)


def _xattn(xb, x, wq, k, v, wo, g, b, tm):
    s_len = x.shape[0]
    tile = lambda: pl.BlockSpec((tm, D_MODEL), lambda i: (i, 0))
    full = lambda a: pl.BlockSpec(a.shape, lambda i: (0,) * a.ndim)
    return pl.pallas_call(
        _xattn_kernel,
        grid=(s_len // tm,),
        in_specs=[tile(), tile(), full(wq), full(k), full(v), full(wo), full(g), full(b)],
        out_specs=[tile(), tile()],
        out_shape=[jax.ShapeDtypeStruct((s_len, D_MODEL), F32), jax.ShapeDtypeStruct((s_len, D_MODEL), BF16)],
        compiler_params=_cparams(("parallel",)),
        name="cross_attention_ln",
    )(xb, x, wq, k, v, wo, g, b)


ROUTE_OFF = MOE_GROUPS


def _router_kernel(xb_ref, w_ref, b_ref, tri_ref, info_ref, cnt_ref, carry):
    @pl.when(pl.program_id(0) == 0)
    def _():
        carry[...] = jnp.zeros(carry.shape, F32)

    logits = _dot(xb_ref[...], w_ref[...]) + b_ref[...]
    lane = lax.broadcasted_iota(I32, logits.shape, 1).astype(F32)
    is_g = lane < float(MOE_GROUPS)
    neg_inf = -jnp.inf
    gl = jnp.where(is_g, logits, neg_inf)
    gmax = jnp.max(gl, -1, keepdims=True)
    g_sel = jnp.min(jnp.where(gl == gmax, lane, float(LANE)), -1, keepdims=True)
    g_w = 1.0 / jnp.sum(jnp.where(is_g, jnp.exp(gl - gmax), 0.0), -1, keepdims=True)
    lo = ROUTE_OFF + MOE_EPG * g_sel
    in_g = (lane >= lo) & (lane < lo + MOE_EPG)
    el = jnp.where(in_g, logits, neg_inf)
    emax = jnp.max(el, -1, keepdims=True)
    e = jnp.where(in_g, jnp.exp(el - emax), 0.0)
    p = e / jnp.sum(e, -1, keepdims=True)
    pm = jnp.where(in_g, p, -1.0)
    p1 = jnp.max(pm, -1, keepdims=True)
    i1 = jnp.min(jnp.where(pm == p1, lane, float(LANE)), -1, keepdims=True)
    pm2 = jnp.where(lane == i1, -1.0, pm)
    p2 = jnp.max(pm2, -1, keepdims=True)
    i2 = jnp.min(jnp.where(pm2 == p2, lane, float(LANE)), -1, keepdims=True)
    denom = p1 + p2
    gate1 = g_w * p1 / denom
    gate2 = g_w * p2 / denom

    oh1 = (lane == i1).astype(BF16)
    oh2 = (lane == i2).astype(BF16)
    incl1 = _dot(tri_ref[...], oh1)
    incl2 = _dot(tri_ref[...], oh2)
    tot1 = jnp.sum(oh1.astype(F32), 0, keepdims=True)
    tot2 = jnp.sum(oh2.astype(F32), 0, keepdims=True)
    base = carry[...]
    rank1 = jnp.sum(jnp.where(lane == i1, base + incl1, 0.0), -1, keepdims=True) - 1.0
    rank2 = jnp.sum(jnp.where(lane == i2, base + tot1 + incl2, 0.0), -1, keepdims=True) - 1.0
    carry[...] = base + tot1 + tot2
    cnt_ref[...] = carry[...]

    cols = (i1 - ROUTE_OFF, i2 - ROUTE_OFF, gate1, gate2, rank1, rank2)
    info = jnp.zeros(logits.shape, F32)
    for c, val in enumerate(cols):
        info = jnp.where(lane == float(c), val, info)
    info_ref[...] = info


def _router(xb, w_gr, b_gr, tri, tm):
    t_len = xb.shape[0]
    return pl.pallas_call(
        _router_kernel,
        grid=(t_len // tm,),
        in_specs=[pl.BlockSpec((tm, D_MODEL), lambda i: (i, 0)),
                  pl.BlockSpec(w_gr.shape, lambda i: (0, 0)),
                  pl.BlockSpec((1, LANE), lambda i: (0, 0)),
                  pl.BlockSpec((tm, tm), lambda i: (0, 0))],
        out_specs=[pl.BlockSpec((tm, LANE), lambda i: (i, 0)),
                   pl.BlockSpec((1, LANE), lambda i: (0, 0))],
        out_shape=[jax.ShapeDtypeStruct((t_len, LANE), F32), jax.ShapeDtypeStruct((1, LANE), F32)],
        scratch_shapes=[pltpu.VMEM((1, LANE), F32)],
        compiler_params=_cparams(("arbitrary",)),
        name="moe_router",
    )(xb, w_gr, b_gr, tri)


def _row_copy(src, src_row, dst, dst_row, sem):
    return pltpu.make_async_copy(src.at[pl.ds(src_row, 1)], dst.at[pl.ds(dst_row, 1)], sem)


def _dispatch_kernel(pos_ref, x_ref, xe_in_hbm, xe_hbm, sem, *, td):
    del xe_in_hbm
    base = pl.program_id(0) * td

    def issue(t, c):
        for k in range(MOE_TOPK):
            _row_copy(x_ref, t, xe_hbm, pos_ref[MOE_TOPK * (base + t) + k], sem).start()
        return c

    lax.fori_loop(0, td, issue, 0)

    def drain(t, c):
        for k in range(MOE_TOPK):
            _row_copy(x_ref, 0, xe_hbm, 0, sem).wait()
        return c

    lax.fori_loop(0, td, drain, 0)


def _dispatch(pos_flat, x, rows, td):
    t_len, d = x.shape
    zeros = jnp.zeros((rows, d), x.dtype)
    grid_spec = pltpu.PrefetchScalarGridSpec(
        num_scalar_prefetch=1,
        grid=(t_len // td,),
        in_specs=[pl.BlockSpec((td, d), lambda i, pos: (i, 0)), pl.BlockSpec(memory_space=pl.ANY)],
        out_specs=pl.BlockSpec(memory_space=pl.ANY),
        scratch_shapes=[pltpu.SemaphoreType.DMA(())],
    )
    return pl.pallas_call(
        functools.partial(_dispatch_kernel, td=td),
        grid_spec=grid_spec,
        out_shape=jax.ShapeDtypeStruct((rows, d), x.dtype),
        input_output_aliases={2: 0},
        compiler_params=pltpu.CompilerParams(dimension_semantics=("arbitrary",)),
        name="moe_dispatch",
    )(pos_flat, x, zeros)


def _expert_kernel(be_ref, nused_ref, xe_ref, win_ref, wout_ref, yb_ref):
    b = pl.program_id(0)

    @pl.when(b < nused_ref[0])
    def _():
        hcat = _dot(xe_ref[...].astype(BF16), win_ref[0].astype(BF16))
        a = hcat[:, :EXPERT_FF]
        act = (a * _sigmoid(a) * hcat[:, EXPERT_FF:]).astype(BF16)
        yb_ref[...] = _dot(act, wout_ref[0].astype(BF16))

    @pl.when(b >= nused_ref[0])
    def _():
        yb_ref[...] = jnp.zeros(yb_ref.shape, F32)


def _experts(block_expert, n_used, xe, w_e_in, w_e_out):
    rows, d = xe.shape
    n_blocks = rows // MOE_BLOCK
    grid_spec = pltpu.PrefetchScalarGridSpec(
        num_scalar_prefetch=2,
        grid=(n_blocks,),
        in_specs=[pl.BlockSpec((MOE_BLOCK, d), lambda b, be, nu: (b, 0)),
                  pl.BlockSpec((1, d, 2 * EXPERT_FF), lambda b, be, nu: (be[b], 0, 0)),
                  pl.BlockSpec((1, EXPERT_FF, d), lambda b, be, nu: (be[b], 0, 0))],
        out_specs=pl.BlockSpec((MOE_BLOCK, d), lambda b, be, nu: (b, 0)),
    )
    return pl.pallas_call(
        _expert_kernel,
        grid_spec=grid_spec,
        out_shape=jax.ShapeDtypeStruct((rows, d), F32),
        compiler_params=_cparams(("arbitrary",)),
        name="moe_experts",
    )(block_expert, n_used, xe, w_e_in, w_e_out)


def _combine_kernel(pos_ref, yb_hbm, x_ref, info_ref, g_ref, b_ref, xo_ref, xob_ref, buf, sem):
    tm = x_ref.shape[0]
    base = pl.program_id(0) * tm

    def issue(t, c):
        for k in range(MOE_TOPK):
            _row_copy(yb_hbm, pos_ref[MOE_TOPK * (base + t) + k], buf.at[k], t, sem).start()
        return c

    lax.fori_loop(0, tm, issue, 0)

    def drain(t, c):
        for k in range(MOE_TOPK):
            _row_copy(yb_hbm, 0, buf.at[k], 0, sem).wait()
        return c

    lax.fori_loop(0, tm, drain, 0)
    info = info_ref[...]
    y = info[:, 2:3] * buf[0] + info[:, 3:4] * buf[1]
    z = _layer_norm(DN_ALPHA * x_ref[...] + y, g_ref[...], b_ref[...])
    xo_ref[...] = z
    xob_ref[...] = z.astype(BF16)


def _combine(pos_flat, yb, x, info, g, b, tm):
    t_len, d = x.shape
    grid_spec = pltpu.PrefetchScalarGridSpec(
        num_scalar_prefetch=1,
        grid=(t_len // tm,),
        in_specs=[pl.BlockSpec(memory_space=pl.ANY),
                  pl.BlockSpec((tm, d), lambda i, pos: (i, 0)),
                  pl.BlockSpec((tm, LANE), lambda i, pos: (i, 0)),
                  pl.BlockSpec((1, d), lambda i, pos: (0, 0)),
                  pl.BlockSpec((1, d), lambda i, pos: (0, 0))],
        out_specs=[pl.BlockSpec((tm, d), lambda i, pos: (i, 0)),
                   pl.BlockSpec((tm, d), lambda i, pos: (i, 0))],
        scratch_shapes=[pltpu.VMEM((MOE_TOPK, tm, d), F32), pltpu.SemaphoreType.DMA(())],
    )
    return pl.pallas_call(
        _combine_kernel,
        grid_spec=grid_spec,
        out_shape=[jax.ShapeDtypeStruct((t_len, d), F32), jax.ShapeDtypeStruct((t_len, d), BF16)],
        compiler_params=_cparams(("arbitrary",)),
        name="moe_combine_ln",
    )(pos_flat, yb, x, info, g, b)


def _pad_cols(m, width):
    return jnp.pad(m, [(0, 0)] * (m.ndim - 1) + [(0, width - m.shape[-1])])


def _rot_half_cols(m):
    half = m.shape[-1] // 2
    return jnp.concatenate([-m[..., half:], m[..., :half]], -1)


def _layout_in1(m):
    offs = np.cumsum((0,) + IN_SIZES)
    o_q, o_kv, o_g, o_pool, o_cq, o_ckv, o_kr, o_conv = offs[:8]
    z = lambda n: jnp.zeros(m.shape[:-1] + (n,), m.dtype)
    parts = []
    for h in range(NSA_HEADS):
        parts += [m[..., o_q + h * NSA_DH:o_q + (h + 1) * NSA_DH] * (NSA_DH ** -0.5 * LOG2E), z(LANE - NSA_DH)]
    for c in range(6 * NSA_GROUPS):
        parts += [m[..., o_kv + c * NSA_DH:o_kv + (c + 1) * NSA_DH], z(LANE - NSA_DH)]
    parts += [m[..., o_pool:o_pool + POOL_WIDTH], m[..., o_cq:o_cq + MLA_Q_RANK], m[..., o_ckv:o_ckv + MLA_KV_RANK]]
    kr = m[..., o_kr:o_kr + MLA_ROPE]
    parts += [z(MLA_NOPE), kr, z(LANE - MLA_NOPE - MLA_ROPE), z(MLA_NOPE), _rot_half_cols(kr),
              z(LANE - MLA_NOPE - MLA_ROPE)]
    parts += [m[..., o_conv:o_conv + 2 * CONV_CH], _pad_cols(m[..., o_g:o_g + 3 * NSA_HEADS], LANE), z(LANE)]
    return jnp.concatenate(parts, -1)


def _layout_mla_q(w):
    dq = MLA_NOPE + MLA_ROPE
    z = lambda n: jnp.zeros((w.shape[0], n), w.dtype)
    a, b = [], []
    for h in range(MLA_HEADS):
        rope = w[:, h * dq + MLA_NOPE:(h + 1) * dq]
        a += [w[:, h * dq:h * dq + MLA_NOPE], rope, z(LANE - dq)]
        b += [z(MLA_NOPE), _rot_half_cols(rope), z(LANE - dq)]
    return jnp.concatenate(a + b, -1)


def _layout_mla_kv(w):
    dkv = MLA_NOPE + MLA_DV
    z = jnp.zeros((w.shape[0], LANE - MLA_NOPE), w.dtype)
    k, v = [], []
    for h in range(MLA_HEADS):
        k += [w[:, h * dkv:h * dkv + MLA_NOPE], z]
        v += [w[:, h * dkv + MLA_NOPE:(h + 1) * dkv], z]
    return jnp.concatenate(k, -1), jnp.concatenate(v, -1)


def _static_tables(s_len):
    nch = s_len // CMP_STRIDE
    n_cmp = (s_len - CMP_LEN) // CMP_STRIDE + 1
    n_sel = s_len // SEL_LEN
    ratio = SEL_LEN // CMP_STRIDE
    c = np.arange(nch)[:, None]
    j = np.arange(n_sel)[None, :]
    mband = ((c >= ratio * j - 1) & (c <= ratio * j + ratio - 1) & (c < n_cmp)).astype(np.float32)
    per_tile = FLASH_TK // SEL_LEN
    n_tiles = s_len // FLASH_TK
    scat = np.zeros((n_tiles, n_sel, LANE), np.float32)
    for b in range(n_sel):
        scat[b // per_tile, b, NSA_DH + b % per_tile] = 1.0
    eslot = np.zeros((FLASH_TK, LANE), np.float32)
    eslot[np.arange(FLASH_TK), NSA_DH + np.arange(FLASH_TK) // SEL_LEN] = 1.0
    gexp = np.zeros((3, LANE, NSA_HEADS * NSA_DH), np.float32)
    for h in range(NSA_HEADS):
        for jj in range(3):
            gexp[jj, h * 3 + jj, h * NSA_DH:(h + 1) * NSA_DH] = 1.0
    half = MLA_ROPE // 2
    inv = ROPE_BASE ** (-jnp.arange(half, dtype=F32) / half)
    inv_slot = jnp.concatenate([jnp.zeros((MLA_NOPE,), F32), inv, inv,
                                jnp.zeros((LANE - MLA_NOPE - MLA_ROPE,), F32)])[None, :]
    as_bf = lambda a: jnp.asarray(a, BF16)
    return as_bf(mband), as_bf(scat), as_bf(eslot), as_bf(gexp), inv_slot


def _hybrid_mixer(x, xb, pos_col, tabs, w_in, b_in, cmp_pos, cmp_w1, cmp_w2, w_nsa_o, w_pool, pool_scale, w_pool_o,
                  q_norm, w_uq, kv_norm, w_ukv, w_mla_o, conv_w, conv_b, conv_ln_g, conv_ln_b, w_conv_o, w_out,
                  ln_g, ln_b):
    s_len = x.shape[0]
    mband, scat, eslot, gexp, inv_slot = tabs
    row = lambda v: v[None, :]
    o_merge = int(sum(IN_SIZES[:8]))
    tm_proj = min(2048, s_len)
    u1 = _matmul(xb, _layout_in1(w_in[:, :o_merge]).astype(BF16), _layout_in1(row(b_in[:o_merge])),
                 tm_proj, 7 * LANE, BF16)
    um = _matmul(xb, w_in[:, o_merge:].astype(BF16), row(b_in[o_merge:]), tm_proj, 1024, BF16)

    nch = s_len // CMP_STRIDE
    kdim = CMP_STRIDE * LANE
    chunks = u1[:, SLOT_KV * LANE:(SLOT_KV + 4) * LANE].reshape(nch, CMP_STRIDE, 4, LANE)
    chunks = chunks.transpose(2, 0, 1, 3).reshape(4, nch, kdim)
    w1 = _pad_cols(cmp_w1.reshape(2, CMP_LEN, NSA_DH, CMP_HIDDEN).transpose(0, 1, 3, 2), LANE)
    w1 = w1.transpose(0, 1, 3, 2)
    w1cat = jnp.concatenate([w1[:, :CMP_STRIDE].reshape(2, kdim, CMP_HIDDEN),
                             w1[:, CMP_STRIDE:].reshape(2, kdim, CMP_HIDDEN)], -1).astype(BF16)
    posp = _pad_cols(cmp_pos, LANE)
    pos2 = jnp.stack([posp[:, :CMP_STRIDE].reshape(2, kdim), posp[:, CMP_STRIDE:].reshape(2, kdim)], 1)
    pos2 = jnp.pad(pos2, ((0, 0), (0, 6), (0, 0))).astype(BF16)
    kcvc = _nsa_compress(chunks, w1cat, pos2, _pad_cols(cmp_w2, LANE).astype(BF16))
    o_cmp, selneg = _nsa_cmp(u1, kcvc, mband, 256)
    o_sel = _flash(u1, SLOT_Q // NSA_HPG, u1, SLOT_KV + 4, u1, SLOT_KV + 6, n_groups=NSA_GROUPS, hpg=NSA_HPG,
                   shared_kv=True, window=False, out_dtype=F32, dv=NSA_DH, tk=FLASH_TK, sel=selneg, scat=scat, eslot=eslot)
    o_win = _flash(u1, SLOT_Q // NSA_HPG, u1, SLOT_KV + 8, u1, SLOT_KV + 10, n_groups=NSA_GROUPS, hpg=NSA_HPG,
                   shared_kv=True, window=True, out_dtype=F32, dv=NSA_DH, tk=FLASH_TQ)

    wk, wv = _layout_mla_kv(w_ukv)
    q_m, k_m, v_m = _mla_prep(u1, pos_col, inv_slot, row(q_norm), row(kv_norm), _layout_mla_q(w_uq).astype(BF16),
                              wk.astype(BF16), wv.astype(BF16), 512)
    o_mla = _flash(q_m, 0, k_m, 0, v_m, 0, n_groups=MLA_HEADS // 4, hpg=4, shared_kv=False, window=False,
                   out_dtype=BF16, dv=MLA_DV, tk=FLASH_TK)

    pooled = _pool(u1, w_pool.astype(BF16), row(pool_scale), 512)
    conv = _conv(u1, conv_w, row(conv_b), row(conv_ln_g), row(conv_ln_b), 512)
    merged = _merge(o_cmp, o_sel, o_win, u1, gexp, pooled, o_mla, conv, um, w_nsa_o.astype(BF16),
                    w_pool_o.astype(BF16), w_mla_o.astype(BF16), w_conv_o.astype(BF16), 256)
    return _matmul_res_ln(merged, w_out.astype(BF16), x, row(ln_g), row(ln_b), 512)


def _cross_attention(x, xb, mem_b, w_q, w_k, w_v, w_o, ln_g, ln_b):
    row = lambda v: v[None, :]
    kv = _matmul(mem_b, jnp.concatenate([w_k, w_v], 1).astype(BF16), jnp.zeros((1, 2 * X_HEADS * X_DH), F32),
                 mem_b.shape[0], 2 * X_HEADS * X_DH, BF16)
    hw = X_HEADS * X_DH
    return _xattn(xb, x, w_q.astype(BF16), kv[:, :hw], kv[:, hw:], w_o.astype(BF16), row(ln_g), row(ln_b), 256)


def _hier_moe(x, xb, w_group, b_group, w_router, b_router, w_e_in, w_e_out, ln_g, ln_b):
    t_len = x.shape[0]
    row = lambda v: v[None, :]
    tm_r = 512
    w_gr = _pad_cols(jnp.concatenate([w_group, w_router], 1), LANE).astype(BF16)
    b_gr = _pad_cols(row(jnp.concatenate([b_group, b_router])), LANE)
    tri = jnp.asarray(np.tril(np.ones((tm_r, tm_r), np.float32)), BF16)
    info, cnt = _router(xb, w_gr, b_gr, tri, tm_r)

    counts = cnt[0, ROUTE_OFF:ROUTE_OFF + N_EXPERTS].astype(I32)
    padded = (counts + MOE_BLOCK - 1) // MOE_BLOCK * MOE_BLOCK
    pend = jnp.cumsum(padded)
    pstart = pend - padded
    n_blocks = -(-(t_len * MOE_TOPK) // MOE_BLOCK) + N_EXPERTS
    n_used = pend[-1] // MOE_BLOCK
    blk_ids = jnp.minimum(jnp.arange(n_blocks), n_used - 1)
    owner = jnp.sum((pend[None, :] <= (blk_ids * MOE_BLOCK)[:, None]).astype(I32), axis=1)
    block_expert = jnp.minimum(owner, N_EXPERTS - 1).astype(I32)
    e_ids = info[:, 0:MOE_TOPK].astype(I32)
    pos = (pstart[e_ids] + info[:, 4:4 + MOE_TOPK].astype(I32)).reshape(-1)

    xe = _dispatch(pos, x, n_blocks * MOE_BLOCK, 512)
    yb = _experts(block_expert, n_used.reshape(1).astype(I32), xe, w_e_in, w_e_out)
    return _combine(pos, yb, x, info, row(ln_g), row(ln_b), 256)


def kernel(x, mem, positions, w_in, b_in, nsa_cmp_pos, nsa_cmp_w1, nsa_cmp_w2, w_nsa_o, w_pool, pool_scale, w_pool_o, mla_q_norm, w_mla_uq, mla_kv_norm, w_mla_ukv, w_mla_o, conv_w, conv_b, conv_ln_g, conv_ln_b, w_conv_o, w_out, ln_mix_g, ln_mix_b, w_xq, w_xk, w_xv, w_xo, ln_x_g, ln_x_b, w_group, b_group, w_router, b_router, w_expert_in, w_expert_out, ln_ffn_g, ln_ffn_b):
    batch, s_len, d = x.shape
    assert batch == 1 and d == D_MODEL and s_len % (2 * FLASH_TK) == 0 and WINDOW == FLASH_TQ
    x = x[0]
    xb = x.astype(BF16)
    mem_b = mem[0].astype(BF16)
    pos_col = positions[0].astype(F32)[:, None]
    tabs = _static_tables(s_len)
    for l in range(w_in.shape[0]):
        x, xb = _hybrid_mixer(x, xb, pos_col, tabs, w_in[l], b_in[l], nsa_cmp_pos[l], nsa_cmp_w1[l], nsa_cmp_w2[l],
                              w_nsa_o[l], w_pool[l], pool_scale[l], w_pool_o[l], mla_q_norm[l], w_mla_uq[l],
                              mla_kv_norm[l], w_mla_ukv[l], w_mla_o[l], conv_w[l], conv_b[l], conv_ln_g[l],
                              conv_ln_b[l], w_conv_o[l], w_out[l], ln_mix_g[l], ln_mix_b[l])
        x, xb = _cross_attention(x, xb, mem_b, w_xq[l], w_xk[l], w_xv[l], w_xo[l], ln_x_g[l], ln_x_b[l])
        x, xb = _hier_moe(x, xb, w_group[l], b_group[l], w_router[l], b_router[l], w_expert_in[l],
                          w_expert_out[l], ln_ffn_g[l], ln_ffn_b[l])
    return x[None]
```

```python
import functools

import numpy as np
import jax
import jax.numpy as jnp
from jax import lax
from jax.experimental import pallas as pl
from jax.experimental.pallas import tpu as pltpu

F32 = jnp.float32
BF16 = jnp.bfloat16
I32 = jnp.int32

D_MODEL = 2048
NSA_HEADS = 8
NSA_GROUPS = 2
NSA_HPG = NSA_HEADS // NSA_GROUPS
NSA_DH = 64
CMP_LEN = 32
CMP_STRIDE = 16
CMP_HIDDEN = 128
SEL_LEN = 64
SEL_TOPN = 16
WINDOW = 512
FORCE_SCORE = 1.0e4
POOL_GROUPS = 4
POOL_WINDOWS = (2, 4, 8, 16)
POOL_WIDTH = 512
POOL_GW = POOL_WIDTH // POOL_GROUPS
MLA_HEADS = 8
MLA_Q_RANK = 512
MLA_KV_RANK = 256
MLA_NOPE = 64
MLA_ROPE = 32
MLA_DV = 64
ROPE_BASE = 10000.0
CONV_CH = 512
CONV_K = 31
N_BRANCH = 4
X_HEADS = 4
X_DH = 128
MOE_GROUPS = 4
MOE_EPG = 8
N_EXPERTS = MOE_GROUPS * MOE_EPG
MOE_TOPK = 2
EXPERT_FF = 512
MOE_BLOCK = 256
LN_EPS = 1e-5
RMS_EPS = 1e-6
DEPTH = 2
DN_ALPHA = (2 * DEPTH) ** 0.25
IN_SIZES = (NSA_HEADS * NSA_DH, 6 * NSA_GROUPS * NSA_DH, 3 * NSA_HEADS, POOL_WIDTH,
            MLA_Q_RANK, MLA_KV_RANK, MLA_ROPE, 2 * CONV_CH, N_BRANCH * D_MODEL)

LANE = 128
VMEM_LIMIT = 56 * 1024 * 1024
NEG = -1.0e30
FLASH_TQ = 1024
FLASH_TK = 1024
LOG2E = 1.4426950408889634

SLOT_Q = 0
SLOT_KV = 8
SLOT_POOL = 20
SLOT_CQ = 24
SLOT_CKV = 28
SLOT_KR = 30
SLOT_CONV = 32
SLOT_GATE = 40
N_SLOTS1 = 42
N1 = N_SLOTS1 * LANE


def _cparams(sem, vmem=VMEM_LIMIT):
    return pltpu.CompilerParams(dimension_semantics=sem, vmem_limit_bytes=vmem)


def _sigmoid(x):
    return 1.0 / (1.0 + jnp.exp(-x))


def _layer_norm(z, g, b):
    mu = jnp.mean(z, -1, keepdims=True)
    d = z - mu
    var = jnp.mean(d * d, -1, keepdims=True)
    return d * lax.rsqrt(var + LN_EPS) * g + b


def _dot(a, b):
    return jnp.dot(a, b, preferred_element_type=F32)


def _dot_nt(a, b):
    return lax.dot_general(a, b, (((1,), (1,)), ((), ())), preferred_element_type=F32)


def _dot3(a, b):
    hi = a.astype(BF16)
    r1 = a - hi.astype(F32)
    mid = r1.astype(BF16)
    lo = (r1 - mid.astype(F32)).astype(BF16)
    return _dot(hi, b) + _dot(mid, b) + _dot(lo, b)


def _mm_kernel(a_ref, b_ref, bias_ref, o_ref):
    o_ref[...] = (_dot(a_ref[...], b_ref[...]) + bias_ref[...]).astype(o_ref.dtype)


def _matmul(a, b, bias, tm, tn, out_dtype):
    m, k = a.shape
    n = b.shape[1]
    return pl.pallas_call(
        _mm_kernel,
        grid=(m // tm, n // tn),
        in_specs=[pl.BlockSpec((tm, k), lambda i, j: (i, 0)),
                  pl.BlockSpec((k, tn), lambda i, j: (0, j)),
                  pl.BlockSpec((1, tn), lambda i, j: (0, j))],
        out_specs=pl.BlockSpec((tm, tn), lambda i, j: (i, j)),
        out_shape=jax.ShapeDtypeStruct((m, n), out_dtype),
        compiler_params=_cparams(("parallel", "arbitrary")),
        name="matmul",
    )(a, b, bias)


def _mm_ln_kernel(a_ref, w_ref, x_ref, g_ref, b_ref, xo_ref, xb_ref):
    h = _dot(a_ref[...], w_ref[...])
    y = _layer_norm(DN_ALPHA * x_ref[...] + h, g_ref[...], b_ref[...])
    xo_ref[...] = y
    xb_ref[...] = y.astype(BF16)


def _matmul_res_ln(a, w, x, g, b, tm):
    m, k = a.shape
    d = w.shape[1]
    return pl.pallas_call(
        _mm_ln_kernel,
        grid=(m // tm,),
        in_specs=[pl.BlockSpec((tm, k), lambda i: (i, 0)),
                  pl.BlockSpec((k, d), lambda i: (0, 0)),
                  pl.BlockSpec((tm, d), lambda i: (i, 0)),
                  pl.BlockSpec((1, d), lambda i: (0, 0)),
                  pl.BlockSpec((1, d), lambda i: (0, 0))],
        out_specs=[pl.BlockSpec((tm, d), lambda i: (i, 0)),
                   pl.BlockSpec((tm, d), lambda i: (i, 0))],
        out_shape=[jax.ShapeDtypeStruct((m, d), F32), jax.ShapeDtypeStruct((m, d), BF16)],
        compiler_params=_cparams(("parallel",)),
        name="matmul_res_ln",
    )(a, w, x, g, b)


def _compress_kernel(a_ref, w1_ref, pos_ref, w2_ref, o_ref):
    nch = a_ref.shape[1]
    hh = _dot(a_ref[0], w1_ref[0])
    pp = _dot(pos_ref[0], w1_ref[0])
    pos_term = pp[0:1, :CMP_HIDDEN] + pp[1:2, CMP_HIDDEN:]
    h2_next = pltpu.roll(hh[:, CMP_HIDDEN:], nch - 1, 0)
    z = hh[:, :CMP_HIDDEN] + h2_next + pos_term
    hid = 0.5 * z * (1.0 + jnp.tanh(0.7978845608028654 * (z + 0.044715 * z * z * z)))
    o_ref[0] = _dot(hid.astype(BF16), w2_ref[0]).astype(o_ref.dtype)


def _nsa_compress(chunks, w1cat, pos2, w2pad):
    n4, nch, kdim = chunks.shape
    return pl.pallas_call(
        _compress_kernel,
        grid=(n4,),
        in_specs=[pl.BlockSpec((1, nch, kdim), lambda c: (c, 0, 0)),
                  pl.BlockSpec((1, kdim, 2 * CMP_HIDDEN), lambda c: (c // NSA_GROUPS, 0, 0)),
                  pl.BlockSpec((1, 8, kdim), lambda c: (c // NSA_GROUPS, 0, 0)),
                  pl.BlockSpec((1, CMP_HIDDEN, LANE), lambda c: (c // NSA_GROUPS, 0, 0))],
        out_specs=pl.BlockSpec((1, nch, LANE), lambda c: (c, 0, 0)),
        out_shape=jax.ShapeDtypeStruct((n4, nch, LANE), BF16),
        compiler_params=_cparams(("arbitrary",)),
        name="nsa_compress",
    )(chunks, w1cat, pos2, w2pad)


CMP_WIDTH_STEPS = 4


def _nsa_cmp_kernel(q_ref, kc_ref, vc_ref, mband_ref, ocmp_ref, selneg_ref, score_sc, *, tq, n_cmp, n_sel, top_n):
    t0 = pl.program_id(0) * tq
    nch = kc_ref.shape[1]
    row = lax.broadcasted_iota(I32, (NSA_HPG * tq, 1), 0)
    t_row = t0 + (row & (tq - 1))

    def softmax_part(width):
        col = lax.broadcasted_iota(I32, (1, width), 1)
        vis = (col * CMP_STRIDE + (CMP_LEN - 1) <= t_row) & (col < n_cmp)
        for g in range(NSA_GROUPS):
            qs = jnp.concatenate(
                [q_ref[:, (g * NSA_HPG + h) * LANE:(g * NSA_HPG + h + 1) * LANE] for h in range(NSA_HPG)], axis=0)
            s = _dot_nt(qs, kc_ref[g, 0:width, :])
            s = jnp.where(vis, s, NEG)
            m = jnp.max(s, -1, keepdims=True)
            p = jnp.where(vis, jnp.exp2(s - m), 0.0)
            l = jnp.sum(p, -1, keepdims=True)
            p = p * (1.0 / jnp.maximum(l, 1e-30))
            o = _dot(p.astype(BF16), vc_ref[g, 0:width, :])
            for h in range(NSA_HPG):
                hh = g * NSA_HPG + h
                ocmp_ref[:, hh * NSA_DH:(hh + 1) * NSA_DH] = o[h * tq:(h + 1) * tq, :NSA_DH]
            imp = p[0:tq]
            for h in range(1, NSA_HPG):
                imp = imp + p[h * tq:(h + 1) * tq]
            score_sc[g * tq:(g + 1) * tq, :] = _dot3(imp, mband_ref[0:width, :])

    last_vis = (t0 + tq - CMP_LEN) // CMP_STRIDE
    step_w = nch // CMP_WIDTH_STEPS
    variant = jnp.minimum(last_vis // step_w, CMP_WIDTH_STEPS - 1)
    for v in range(CMP_WIDTH_STEPS):
        pl.when(variant == v)(functools.partial(softmax_part, (v + 1) * step_w))

    blk = lax.broadcasted_iota(I32, (NSA_GROUPS * tq, n_sel), 1).astype(F32)
    row2 = lax.broadcasted_iota(I32, (NSA_GROUPS * tq, 1), 0)
    cur = ((t0 + (row2 & (tq - 1))) // SEL_LEN).astype(F32)
    valid = blk <= cur
    forced = (blk == 0.0) | (blk == cur) | (blk == cur - 1.0)
    score = jnp.where(forced, FORCE_SCORE, score_sc[...])
    score = jnp.where(valid, score, -1.0)

    def pick_one(_, sc):
        mx = jnp.max(sc, -1, keepdims=True)
        first = jnp.min(jnp.where(sc == mx, blk, float(n_sel)), -1, keepdims=True)
        return jnp.where(blk == first, -2.0, sc)

    sc = lax.fori_loop(0, top_n, pick_one, score)
    selneg = jnp.where(valid & (sc == -2.0), 0.0, -1.0).astype(selneg_ref.dtype)
    for g in range(NSA_GROUPS):
        selneg_ref[:, g * n_sel:(g + 1) * n_sel] = selneg[g * tq:(g + 1) * tq]


def _nsa_cmp(u1, kcvc, mband, tq):
    s_len = u1.shape[0]
    nch = kcvc.shape[1]
    n_sel = s_len // SEL_LEN
    n_cmp = (s_len - CMP_LEN) // CMP_STRIDE + 1
    kern = functools.partial(_nsa_cmp_kernel, tq=tq, n_cmp=n_cmp, n_sel=n_sel, top_n=min(SEL_TOPN, n_sel))
    return pl.pallas_call(
        kern,
        grid=(s_len // tq,),
        in_specs=[pl.BlockSpec((tq, NSA_HEADS * LANE), lambda i: (i, SLOT_Q // NSA_HEADS)),
                  pl.BlockSpec((NSA_GROUPS, nch, LANE), lambda i: (0, 0, 0)),
                  pl.BlockSpec((NSA_GROUPS, nch, LANE), lambda i: (1, 0, 0)),
                  pl.BlockSpec((nch, n_sel), lambda i: (0, 0))],
        out_specs=[pl.BlockSpec((tq, NSA_HEADS * NSA_DH), lambda i: (i, 0)),
                   pl.BlockSpec((tq, NSA_GROUPS * n_sel), lambda i: (i, 0))],
        out_shape=[jax.ShapeDtypeStruct((s_len, NSA_HEADS * NSA_DH), F32),
                   jax.ShapeDtypeStruct((s_len, NSA_GROUPS * n_sel), BF16)],
        scratch_shapes=[pltpu.VMEM((NSA_GROUPS * tq, n_sel), F32)],
        compiler_params=_cparams(("parallel",)),
        name="nsa_cmp_topk",
    )(u1, kcvc, kcvc, mband)


def _flash_kernel(qi_ref, kj_ref, first_ref, last_ref, mode_ref, off_ref, *refs, hpg, shared_kv, select, modes,
                  dv):
    if select:
        q_ref, k_ref, v_ref, sel_ref, scat_ref, eslot_ref, o_ref, m_sc, acc_sc = refs
    else:
        q_ref, k_ref, v_ref, o_ref, m_sc, acc_sc = refs
    p_idx = pl.program_id(1)
    tq = q_ref.shape[0]
    tk = k_ref.shape[0]

    @pl.when(first_ref[p_idx] == 1)
    def _():
        m_sc[...] = jnp.full(m_sc.shape, NEG, F32)
        acc_sc[...] = jnp.zeros(acc_sc.shape, F32)

    def step(mask_mode):
        k_all = k_ref[...]
        lane = lax.broadcasted_iota(I32, (1, v_ref.shape[1]), 1)
        v_all = v_ref[...] + ((lane & (LANE - 1)) == dv).astype(BF16)
        if select:
            k_all = k_all + eslot_ref[...]
            bias = (_dot(sel_ref[...], scat_ref[kj_ref[p_idx]]) * (-NEG)).astype(BF16)
        if mask_mode:
            r = lax.broadcasted_iota(I32, (tq, tk), 0)
            c = lax.broadcasted_iota(I32, (tq, tk), 1)
            keep = (c <= r + off_ref[p_idx]) if mask_mode == 1 else (c > r)
        for h in range(hpg):
            hk = 0 if shared_kv else h
            q = q_ref[:, h * LANE:(h + 1) * LANE]
            if select:
                q = q + bias
            s = _dot_nt(q, k_all[:, hk * LANE:(hk + 1) * LANE])
            if mask_mode:
                s = jnp.where(keep, s, NEG)
            m_prev = m_sc[h]
            m_new = jnp.maximum(m_prev, jnp.max(s, -1, keepdims=True))
            p = jnp.exp2(s - jnp.concatenate([m_new] * (tk // LANE), axis=1))
            acc_sc[h] = (jnp.exp2(m_prev - m_new) * acc_sc[h]
                         + _dot(p.astype(BF16), v_all[:, hk * LANE:(hk + 1) * LANE]))
            m_sc[h] = m_new

    for mm in modes:
        pl.when(mode_ref[p_idx] == mm)(functools.partial(step, mm))

    @pl.when(last_ref[p_idx] == 1)
    def _():
        for h in range(hpg):
            acc = acc_sc[h]
            o = acc[:, :dv] * (1.0 / acc[:, dv:dv + 1])
            o_ref[:, h * dv:(h + 1) * dv] = o.astype(o_ref.dtype)


def _pair_tables(s_len, tq, tk, window):
    rows = []
    for i in range(s_len // tq):
        if window:
            js = ([i - 1] if i > 0 else []) + [i]
        else:
            js = list(range((i * tq + tq - 1) // tk + 1))
        for j in js:
            if window:
                mode = 1 if j == i else 2
            else:
                mode = 0 if (j + 1) * tk - 1 <= i * tq else 1
            rows.append((i, j, int(j == js[0]), int(j == js[-1]), mode, i * tq - j * tk))
    tab = np.asarray(rows, np.int32)
    return [jnp.asarray(tab[:, c]) for c in range(tab.shape[1])]


def _flash(q_arr, q_blk0, k_arr, k_blk0, v_arr, v_blk0, *, n_groups, hpg, shared_kv, window,
           out_dtype, dv, tq, tk, sel=None, scat=None, eslot=None):
    s_len = q_arr.shape[0]
    assert not window or tk == tq == WINDOW
    nk = 1 if shared_kv else hpg
    select = sel is not None
    tables = _pair_tables(s_len, tq, tk, window)
    n_pairs = int(tables[0].shape[0])
    modes = (1, 2) if window else (0, 1)

    def qmap(g, p, qi, kj, *_):
        return (qi[p], q_blk0 + g)

    def kmap(g, p, qi, kj, *_):
        return (kj[p], k_blk0 + g)

    def vmap_(g, p, qi, kj, *_):
        return (kj[p], v_blk0 + g)

    def omap(g, p, qi, kj, *_):
        return (qi[p], g)

    in_specs = [pl.BlockSpec((tq, hpg * LANE), qmap),
                pl.BlockSpec((tk, nk * LANE), kmap),
                pl.BlockSpec((tk, nk * LANE), vmap_)]
    args = [q_arr, k_arr, v_arr]
    if select:
        n_sel = sel.shape[1] // n_groups
        in_specs += [pl.BlockSpec((tq, n_sel), omap),
                     pl.BlockSpec(scat.shape, lambda g, p, *_: (0, 0, 0)),
                     pl.BlockSpec(eslot.shape, lambda g, p, *_: (0, 0))]
        args += [sel, scat, eslot]
    kern = functools.partial(_flash_kernel, hpg=hpg, shared_kv=shared_kv, select=select, modes=modes, dv=dv)
    grid_spec = pltpu.PrefetchScalarGridSpec(
        num_scalar_prefetch=len(tables),
        grid=(n_groups, n_pairs),
        in_specs=in_specs,
        out_specs=pl.BlockSpec((tq, hpg * dv), omap),
        scratch_shapes=[pltpu.VMEM((hpg, tq, LANE), F32), pltpu.VMEM((hpg, tq, LANE), F32)],
    )
    return pl.pallas_call(
        kern,
        grid_spec=grid_spec,
        out_shape=jax.ShapeDtypeStruct((s_len, n_groups * hpg * dv), out_dtype),
        compiler_params=_cparams(("parallel", "arbitrary")),
        name="flash_sel" if select else ("flash_win" if window else "flash_causal"),
    )(*tables, *args)


def _mla_prep_kernel(cq_ref, ckv_ref, kr_ref, pos_ref, inv_ref, qn_ref, kvn_ref, wuq_ref, wk_ref, wv_ref,
                     q_ref, k_ref, v_ref):
    def rms(x, g):
        return x * lax.rsqrt(jnp.mean(x * x, -1, keepdims=True) + RMS_EPS) * g

    ang = pos_ref[...] * inv_ref[...]
    cos, sin = jnp.cos(ang), jnp.sin(ang)
    hw = MLA_HEADS * LANE
    qh = _dot(rms(cq_ref[...].astype(F32), qn_ref[...]).astype(BF16), wuq_ref[...])
    scale = (MLA_NOPE + MLA_ROPE) ** -0.5 * LOG2E
    cos_t = jnp.concatenate([cos] * MLA_HEADS, axis=1)
    sin_t = jnp.concatenate([sin] * MLA_HEADS, axis=1)
    q_ref[...] = ((qh[:, :hw] * cos_t + qh[:, hw:] * sin_t) * scale).astype(BF16)
    ckv = rms(ckv_ref[...].astype(F32), kvn_ref[...]).astype(BF16)
    kr = kr_ref[...].astype(F32)
    k_rope = kr[:, :LANE] * cos + kr[:, LANE:] * sin
    k_ref[...] = (_dot(ckv, wk_ref[...]) + jnp.concatenate([k_rope] * MLA_HEADS, axis=1)).astype(BF16)
    v_ref[...] = _dot(ckv, wv_ref[...]).astype(BF16)


def _mla_prep(u1, pos_col, inv_slot, qn, kvn, wuq, wk, wv, tm):
    s_len = u1.shape[0]
    hw = MLA_HEADS * LANE
    full = lambda a: pl.BlockSpec(a.shape, lambda i: (0,) * a.ndim)
    out = jax.ShapeDtypeStruct((s_len, hw), BF16)
    return pl.pallas_call(
        _mla_prep_kernel,
        grid=(s_len // tm,),
        in_specs=[pl.BlockSpec((tm, MLA_Q_RANK), lambda i: (i, SLOT_CQ * LANE // MLA_Q_RANK)),
                  pl.BlockSpec((tm, MLA_KV_RANK), lambda i: (i, SLOT_CKV * LANE // MLA_KV_RANK)),
                  pl.BlockSpec((tm, 2 * LANE), lambda i: (i, SLOT_KR // 2)),
                  pl.BlockSpec((tm, 1), lambda i: (i, 0)),
                  full(inv_slot), full(qn), full(kvn), full(wuq), full(wk), full(wv)],
        out_specs=[pl.BlockSpec((tm, hw), lambda i: (i, 0))] * 3,
        out_shape=[out, out, out],
        compiler_params=_cparams(("parallel",)),
        name="mla_prep",
    )(u1, u1, u1, pos_col, inv_slot, qn, kvn, wuq, wk, wv)


POOL_HALO = 16


def _pool_kernel(cur_ref, halo_ref, wp_ref, scale_ref, o_ref):
    i = pl.program_id(0)
    tm = cur_ref.shape[0]
    halo = jnp.where(i > 0, halo_ref[...].astype(F32), 0.0)
    x = jnp.concatenate([halo, cur_ref[...].astype(F32)], axis=0)
    sums = {1: x}
    w = 1
    while w < max(POOL_WINDOWS):
        a = sums[w]
        sums[2 * w] = a[w:] + a[:-w]
        w *= 2
    t = (i * tm + lax.broadcasted_iota(I32, (tm, 1), 0) + 1).astype(F32)
    outs = []
    for gi, w in enumerate(POOL_WINDOWS):
        lo = gi * POOL_GW
        start = POOL_HALO - (w - 1)
        win = sums[w][start:start + tm, lo:lo + POOL_GW]
        mean = win / jnp.minimum(t, float(w))
        pooled = mean - x[POOL_HALO:, lo:lo + POOL_GW]
        outs.append(_dot(pooled.astype(BF16), wp_ref[gi]))
    o_ref[...] = (jnp.concatenate(outs, axis=1) * scale_ref[...]).astype(o_ref.dtype)


def _pool(u1, w_pool, pool_scale, tm):
    s_len = u1.shape[0]
    blk = SLOT_POOL * LANE // POOL_WIDTH
    return pl.pallas_call(
        _pool_kernel,
        grid=(s_len // tm,),
        in_specs=[pl.BlockSpec((tm, POOL_WIDTH), lambda i: (i, blk)),
                  pl.BlockSpec((POOL_HALO, POOL_WIDTH),
                               lambda i: (jnp.maximum(i * (tm // POOL_HALO) - 1, 0), blk)),
                  pl.BlockSpec(w_pool.shape, lambda i: (0, 0, 0)),
                  pl.BlockSpec((1, POOL_WIDTH), lambda i: (0, 0))],
        out_specs=pl.BlockSpec((tm, POOL_WIDTH), lambda i: (i, 0)),
        out_shape=jax.ShapeDtypeStruct((s_len, POOL_WIDTH), BF16),
        compiler_params=_cparams(("parallel",)),
        name="pool",
    )(u1, u1, w_pool, pool_scale)


CONV_HALO = 32


def _conv_kernel(cur_ref, halo_ref, w_ref, b_ref, g_ref, beta_ref, o_ref, hbuf):
    i = pl.program_id(0)
    tm = cur_ref.shape[0]

    def glu(u):
        u = u.astype(F32)
        return u[:, :CONV_CH] * _sigmoid(u[:, CONV_CH:])

    hbuf[0:CONV_HALO, :] = jnp.where(i > 0, glu(halo_ref[...]), 0.0)
    hbuf[CONV_HALO:, :] = glu(cur_ref[...])
    acc = jnp.zeros((tm, CONV_CH), F32) + b_ref[...]
    for k in range(CONV_K):
        off = CONV_HALO - (CONV_K - 1) + k
        acc = acc + hbuf[off:off + tm, :] * w_ref[k:k + 1, :]
    y = _layer_norm(acc, g_ref[...], beta_ref[...])
    o_ref[...] = (y * _sigmoid(y)).astype(o_ref.dtype)


def _conv(u1, conv_w, conv_b, ln_g, ln_b, tm):
    s_len = u1.shape[0]
    blk = SLOT_CONV * LANE // (2 * CONV_CH)
    row = lambda a: pl.BlockSpec(a.shape, lambda i: (0, 0))
    return pl.pallas_call(
        _conv_kernel,
        grid=(s_len // tm,),
        in_specs=[pl.BlockSpec((tm, 2 * CONV_CH), lambda i: (i, blk)),
                  pl.BlockSpec((CONV_HALO, 2 * CONV_CH),
                               lambda i: (jnp.maximum(i * (tm // CONV_HALO) - 1, 0), blk)),
                  row(conv_w), row(conv_b), row(ln_g), row(ln_b)],
        out_specs=pl.BlockSpec((tm, CONV_CH), lambda i: (i, 0)),
        out_shape=jax.ShapeDtypeStruct((s_len, CONV_CH), BF16),
        scratch_shapes=[pltpu.VMEM((tm + CONV_HALO, CONV_CH), F32)],
        compiler_params=_cparams(("parallel",)),
        name="conv_module",
    )(u1, u1, conv_w, conv_b, ln_g, ln_b)


def _merge_kernel(ocmp_ref, osel_ref, owin_ref, gate_ref, gexp_ref, pool_ref, mla_ref, conv_ref, um_ref,
                  wn_ref, wp_ref, wm_ref, wc_ref, o_ref):
    sg = _sigmoid(gate_ref[...].astype(F32))
    nsa = (_dot3(sg, gexp_ref[0]) * ocmp_ref[...] + _dot3(sg, gexp_ref[1]) * osel_ref[...]
           + _dot3(sg, gexp_ref[2]) * owin_ref[...])
    branches = (_dot(nsa.astype(BF16), wn_ref[...]), _dot(pool_ref[...], wp_ref[...]),
                _dot(mla_ref[...], wm_ref[...]), _dot(conv_ref[...], wc_ref[...]))
    merged = None
    for j, br in enumerate(branches):
        term = _sigmoid(um_ref[:, j * D_MODEL:(j + 1) * D_MODEL].astype(F32)) * br
        merged = term if merged is None else merged + term
    o_ref[...] = merged.astype(o_ref.dtype)


def _merge(o_cmp, o_sel, o_win, u1, gexp, pooled, o_mla, conv, um, wn, wp, wm, wc, tm):
    s_len = u1.shape[0]
    tile = lambda w: pl.BlockSpec((tm, w), lambda i: (i, 0))
    full = lambda a: pl.BlockSpec(a.shape, lambda i: (0,) * a.ndim)
    return pl.pallas_call(
        _merge_kernel,
        grid=(s_len // tm,),
        in_specs=[tile(512), tile(512), tile(512),
                  pl.BlockSpec((tm, LANE), lambda i: (i, SLOT_GATE)),
                  full(gexp), tile(512), tile(512), tile(512), tile(N_BRANCH * D_MODEL),
                  full(wn), full(wp), full(wm), full(wc)],
        out_specs=tile(D_MODEL),
        out_shape=jax.ShapeDtypeStruct((s_len, D_MODEL), BF16),
        compiler_params=_cparams(("parallel",)),
        name="branch_merge",
    )(o_cmp, o_sel, o_win, u1, gexp, pooled, o_mla, conv, um, wn, wp, wm, wc)


def _xattn_kernel(xb_ref, x_ref, wq_ref, k_ref, v_ref, wo_ref, g_ref, b_ref, xo_ref, xob_ref):
    q = _dot(xb_ref[...], wq_ref[...]).astype(BF16)
    k = k_ref[...]
    v = v_ref[...]
    outs = []
    for h in range(X_HEADS):
        sl = slice(h * X_DH, (h + 1) * X_DH)
        s = _dot_nt(q[:, sl], k[:, sl]) * (X_DH ** -0.5)
        m = jnp.max(s, -1, keepdims=True)
        p = jnp.exp(s - m)
        p = p * (1.0 / jnp.sum(p, -1, keepdims=True))
        outs.append(_dot(p.astype(BF16), v[:, sl]))
    o = jnp.concatenate(outs, axis=1).astype(BF16)
    y = _layer_norm(DN_ALPHA * x_ref[...] + _dot(o, wo_ref[...]), g_ref[...], b_ref[...])
    xo_ref[...] = y
    xob_ref[...] = y.astype(BF16)


def _xattn(xb, x, wq, k, v, wo, g, b, tm):
    s_len = x.shape[0]
    tile = lambda: pl.BlockSpec((tm, D_MODEL), lambda i: (i, 0))
    full = lambda a: pl.BlockSpec(a.shape, lambda i: (0,) * a.ndim)
    return pl.pallas_call(
        _xattn_kernel,
        grid=(s_len // tm,),
        in_specs=[tile(), tile(), full(wq), full(k), full(v), full(wo), full(g), full(b)],
        out_specs=[tile(), tile()],
        out_shape=[jax.ShapeDtypeStruct((s_len, D_MODEL), F32), jax.ShapeDtypeStruct((s_len, D_MODEL), BF16)],
        compiler_params=_cparams(("parallel",)),
        name="cross_attention_ln",
    )(xb, x, wq, k, v, wo, g, b)


ROUTE_OFF = MOE_GROUPS


def _router_kernel(xb_ref, w_ref, b_ref, tri_ref, info_ref, cnt_ref, carry):
    @pl.when(pl.program_id(0) == 0)
    def _():
        carry[...] = jnp.zeros(carry.shape, F32)

    logits = _dot(xb_ref[...], w_ref[...]) + b_ref[...]
    lane = lax.broadcasted_iota(I32, logits.shape, 1).astype(F32)
    is_g = lane < float(MOE_GROUPS)
    neg_inf = -jnp.inf
    gl = jnp.where(is_g, logits, neg_inf)
    gmax = jnp.max(gl, -1, keepdims=True)
    g_sel = jnp.min(jnp.where(gl == gmax, lane, float(LANE)), -1, keepdims=True)
    g_w = 1.0 / jnp.sum(jnp.where(is_g, jnp.exp(gl - gmax), 0.0), -1, keepdims=True)
    lo = ROUTE_OFF + MOE_EPG * g_sel
    in_g = (lane >= lo) & (lane < lo + MOE_EPG)
    el = jnp.where(in_g, logits, neg_inf)
    emax = jnp.max(el, -1, keepdims=True)
    e = jnp.where(in_g, jnp.exp(el - emax), 0.0)
    p = e / jnp.sum(e, -1, keepdims=True)
    pm = jnp.where(in_g, p, -1.0)
    p1 = jnp.max(pm, -1, keepdims=True)
    i1 = jnp.min(jnp.where(pm == p1, lane, float(LANE)), -1, keepdims=True)
    pm2 = jnp.where(lane == i1, -1.0, pm)
    p2 = jnp.max(pm2, -1, keepdims=True)
    i2 = jnp.min(jnp.where(pm2 == p2, lane, float(LANE)), -1, keepdims=True)
    denom = p1 + p2
    gate1 = g_w * p1 / denom
    gate2 = g_w * p2 / denom

    oh1 = (lane == i1).astype(BF16)
    oh2 = (lane == i2).astype(BF16)
    incl1 = _dot(tri_ref[...], oh1)
    incl2 = _dot(tri_ref[...], oh2)
    tot1 = jnp.sum(oh1.astype(F32), 0, keepdims=True)
    tot2 = jnp.sum(oh2.astype(F32), 0, keepdims=True)
    base = carry[...]
    rank1 = jnp.sum(jnp.where(lane == i1, base + incl1, 0.0), -1, keepdims=True) - 1.0
    rank2 = jnp.sum(jnp.where(lane == i2, base + tot1 + incl2, 0.0), -1, keepdims=True) - 1.0
    carry[...] = base + tot1 + tot2
    cnt_ref[...] = carry[...]

    cols = (i1 - ROUTE_OFF, i2 - ROUTE_OFF, gate1, gate2, rank1, rank2)
    info = jnp.zeros(logits.shape, F32)
    for c, val in enumerate(cols):
        info = jnp.where(lane == float(c), val, info)
    info_ref[...] = info


def _router(xb, w_gr, b_gr, tri, tm):
    t_len = xb.shape[0]
    return pl.pallas_call(
        _router_kernel,
        grid=(t_len // tm,),
        in_specs=[pl.BlockSpec((tm, D_MODEL), lambda i: (i, 0)),
                  pl.BlockSpec(w_gr.shape, lambda i: (0, 0)),
                  pl.BlockSpec((1, LANE), lambda i: (0, 0)),
                  pl.BlockSpec((tm, tm), lambda i: (0, 0))],
        out_specs=[pl.BlockSpec((tm, LANE), lambda i: (i, 0)),
                   pl.BlockSpec((1, LANE), lambda i: (0, 0))],
        out_shape=[jax.ShapeDtypeStruct((t_len, LANE), F32), jax.ShapeDtypeStruct((1, LANE), F32)],
        scratch_shapes=[pltpu.VMEM((1, LANE), F32)],
        compiler_params=_cparams(("arbitrary",)),
        name="moe_router",
    )(xb, w_gr, b_gr, tri)


def _row_copy(src, src_row, dst, dst_row, sem):
    return pltpu.make_async_copy(src.at[pl.ds(src_row, 1)], dst.at[pl.ds(dst_row, 1)], sem)


def _dispatch_kernel(pos_ref, x_ref, xe_in_hbm, xe_hbm, sem, *, td):
    del xe_in_hbm
    base = pl.program_id(0) * td

    def issue(t, c):
        for k in range(MOE_TOPK):
            _row_copy(x_ref, t, xe_hbm, pos_ref[MOE_TOPK * (base + t) + k], sem).start()
        return c

    lax.fori_loop(0, td, issue, 0)

    def drain(t, c):
        for k in range(MOE_TOPK):
            _row_copy(x_ref, 0, xe_hbm, 0, sem).wait()
        return c

    lax.fori_loop(0, td, drain, 0)


def _dispatch(pos_flat, x, rows, td):
    t_len, d = x.shape
    zeros = jnp.zeros((rows, d), x.dtype)
    grid_spec = pltpu.PrefetchScalarGridSpec(
        num_scalar_prefetch=1,
        grid=(t_len // td,),
        in_specs=[pl.BlockSpec((td, d), lambda i, pos: (i, 0)), pl.BlockSpec(memory_space=pl.ANY)],
        out_specs=pl.BlockSpec(memory_space=pl.ANY),
        scratch_shapes=[pltpu.SemaphoreType.DMA(())],
    )
    return pl.pallas_call(
        functools.partial(_dispatch_kernel, td=td),
        grid_spec=grid_spec,
        out_shape=jax.ShapeDtypeStruct((rows, d), x.dtype),
        input_output_aliases={2: 0},
        compiler_params=pltpu.CompilerParams(dimension_semantics=("arbitrary",)),
        name="moe_dispatch",
    )(pos_flat, x, zeros)


def _expert_kernel(be_ref, nused_ref, xe_ref, win_ref, wout_ref, yb_ref):
    b = pl.program_id(0)

    @pl.when(b < nused_ref[0])
    def _():
        hcat = _dot(xe_ref[...].astype(BF16), win_ref[0, 0].astype(BF16))
        a = hcat[:, :EXPERT_FF]
        act = (a * _sigmoid(a) * hcat[:, EXPERT_FF:]).astype(BF16)
        yb_ref[...] = _dot(act, wout_ref[0, 0].astype(BF16))

    @pl.when(b >= nused_ref[0])
    def _():
        yb_ref[...] = jnp.zeros(yb_ref.shape, F32)


def _experts(block_expert, n_used, xe, w_e_in, w_e_out, layer):
    rows, d = xe.shape
    n_blocks = rows // MOE_BLOCK
    grid_spec = pltpu.PrefetchScalarGridSpec(
        num_scalar_prefetch=2,
        grid=(n_blocks,),
        in_specs=[pl.BlockSpec((MOE_BLOCK, d), lambda b, be, nu: (b, 0)),
                  pl.BlockSpec((1, 1, d, 2 * EXPERT_FF), lambda b, be, nu: (layer, be[b], 0, 0)),
                  pl.BlockSpec((1, 1, EXPERT_FF, d), lambda b, be, nu: (layer, be[b], 0, 0))],
        out_specs=pl.BlockSpec((MOE_BLOCK, d), lambda b, be, nu: (b, 0)),
    )
    return pl.pallas_call(
        _expert_kernel,
        grid_spec=grid_spec,
        out_shape=jax.ShapeDtypeStruct((rows, d), F32),
        compiler_params=_cparams(("arbitrary",)),
        name="moe_experts",
    )(block_expert, n_used, xe, w_e_in, w_e_out)


def _combine_kernel(pos_ref, yb_hbm, x_ref, info_ref, g_ref, b_ref, xo_ref, xob_ref, buf, sem):
    tm = x_ref.shape[0]
    i = pl.program_id(0)
    slot = i & 1

    def gather(tile, slot_):
        def issue(t, c):
            for k in range(MOE_TOPK):
                _row_copy(yb_hbm, pos_ref[MOE_TOPK * (tile * tm + t) + k], buf.at[slot_, k], t,
                          sem.at[slot_]).start()
            return c
        lax.fori_loop(0, tm, issue, 0)

    @pl.when(i == 0)
    def _():
        gather(0, 0)

    @pl.when(i + 1 < pl.num_programs(0))
    def _():
        gather(i + 1, 1 - slot)

    def drain(t, c):
        for k in range(MOE_TOPK):
            _row_copy(yb_hbm, 0, buf.at[slot, k], 0, sem.at[slot]).wait()
        return c

    lax.fori_loop(0, tm, drain, 0)
    info = info_ref[...]
    y = info[:, 2:3] * buf[slot, 0] + info[:, 3:4] * buf[slot, 1]
    z = _layer_norm(DN_ALPHA * x_ref[...] + y, g_ref[...], b_ref[...])
    xo_ref[...] = z
    xob_ref[...] = z.astype(BF16)


def _combine(pos_flat, yb, x, info, g, b, tm):
    t_len, d = x.shape
    grid_spec = pltpu.PrefetchScalarGridSpec(
        num_scalar_prefetch=1,
        grid=(t_len // tm,),
        in_specs=[pl.BlockSpec(memory_space=pl.ANY),
                  pl.BlockSpec((tm, d), lambda i, pos: (i, 0)),
                  pl.BlockSpec((tm, LANE), lambda i, pos: (i, 0)),
                  pl.BlockSpec((1, d), lambda i, pos: (0, 0)),
                  pl.BlockSpec((1, d), lambda i, pos: (0, 0))],
        out_specs=[pl.BlockSpec((tm, d), lambda i, pos: (i, 0)),
                   pl.BlockSpec((tm, d), lambda i, pos: (i, 0))],
        scratch_shapes=[pltpu.VMEM((2, MOE_TOPK, tm, d), F32), pltpu.SemaphoreType.DMA((2,))],
    )
    return pl.pallas_call(
        _combine_kernel,
        grid_spec=grid_spec,
        out_shape=[jax.ShapeDtypeStruct((t_len, d), F32), jax.ShapeDtypeStruct((t_len, d), BF16)],
        compiler_params=_cparams(("arbitrary",)),
        name="moe_combine_ln",
    )(pos_flat, yb, x, info, g, b)


def _pad_cols(m, width):
    return jnp.pad(m, [(0, 0)] * (m.ndim - 1) + [(0, width - m.shape[-1])])


def _rot_half_cols(m):
    half = m.shape[-1] // 2
    return jnp.concatenate([-m[..., half:], m[..., :half]], -1)


def _layout_in1(m):
    offs = np.cumsum((0,) + IN_SIZES)
    o_q, o_kv, o_g, o_pool, o_cq, o_ckv, o_kr, o_conv = offs[:8]
    z = lambda n: jnp.zeros(m.shape[:-1] + (n,), m.dtype)
    parts = []
    for h in range(NSA_HEADS):
        parts += [m[..., o_q + h * NSA_DH:o_q + (h + 1) * NSA_DH] * (NSA_DH ** -0.5 * LOG2E), z(LANE - NSA_DH)]
    for c in range(6 * NSA_GROUPS):
        parts += [m[..., o_kv + c * NSA_DH:o_kv + (c + 1) * NSA_DH], z(LANE - NSA_DH)]
    parts += [m[..., o_pool:o_pool + POOL_WIDTH], m[..., o_cq:o_cq + MLA_Q_RANK], m[..., o_ckv:o_ckv + MLA_KV_RANK]]
    kr = m[..., o_kr:o_kr + MLA_ROPE]
    parts += [z(MLA_NOPE), kr, z(LANE - MLA_NOPE - MLA_ROPE), z(MLA_NOPE), _rot_half_cols(kr),
              z(LANE - MLA_NOPE - MLA_ROPE)]
    parts += [m[..., o_conv:o_conv + 2 * CONV_CH], _pad_cols(m[..., o_g:o_g + 3 * NSA_HEADS], LANE), z(LANE)]
    return jnp.concatenate(parts, -1)


def _layout_mla_q(w):
    dq = MLA_NOPE + MLA_ROPE
    z = lambda n: jnp.zeros((w.shape[0], n), w.dtype)
    a, b = [], []
    for h in range(MLA_HEADS):
        rope = w[:, h * dq + MLA_NOPE:(h + 1) * dq]
        a += [w[:, h * dq:h * dq + MLA_NOPE], rope, z(LANE - dq)]
        b += [z(MLA_NOPE), _rot_half_cols(rope), z(LANE - dq)]
    return jnp.concatenate(a + b, -1)


def _layout_mla_kv(w):
    dkv = MLA_NOPE + MLA_DV
    z = jnp.zeros((w.shape[0], LANE - MLA_NOPE), w.dtype)
    k, v = [], []
    for h in range(MLA_HEADS):
        k += [w[:, h * dkv:h * dkv + MLA_NOPE], z]
        v += [w[:, h * dkv + MLA_NOPE:(h + 1) * dkv], z]
    return jnp.concatenate(k, -1), jnp.concatenate(v, -1)


def _static_tables(s_len):
    nch = s_len // CMP_STRIDE
    n_cmp = (s_len - CMP_LEN) // CMP_STRIDE + 1
    n_sel = s_len // SEL_LEN
    ratio = SEL_LEN // CMP_STRIDE
    c = np.arange(nch)[:, None]
    j = np.arange(n_sel)[None, :]
    mband = ((c >= ratio * j - 1) & (c <= ratio * j + ratio - 1) & (c < n_cmp)).astype(np.float32)
    per_tile = FLASH_TK // SEL_LEN
    n_tiles = s_len // FLASH_TK
    scat = np.zeros((n_tiles, n_sel, LANE), np.float32)
    for b in range(n_sel):
        scat[b // per_tile, b, NSA_DH + b % per_tile] = 1.0
    eslot = np.zeros((FLASH_TK, LANE), np.float32)
    eslot[np.arange(FLASH_TK), NSA_DH + np.arange(FLASH_TK) // SEL_LEN] = 1.0
    gexp = np.zeros((3, LANE, NSA_HEADS * NSA_DH), np.float32)
    for h in range(NSA_HEADS):
        for jj in range(3):
            gexp[jj, h * 3 + jj, h * NSA_DH:(h + 1) * NSA_DH] = 1.0
    half = MLA_ROPE // 2
    inv = ROPE_BASE ** (-jnp.arange(half, dtype=F32) / half)
    inv_slot = jnp.concatenate([jnp.zeros((MLA_NOPE,), F32), inv, inv,
                                jnp.zeros((LANE - MLA_NOPE - MLA_ROPE,), F32)])[None, :]
    as_bf = lambda a: jnp.asarray(a, BF16)
    return as_bf(mband), as_bf(scat), as_bf(eslot), as_bf(gexp), inv_slot


def _hybrid_mixer(x, xb, pos_col, tabs, w_in, b_in, cmp_pos, cmp_w1, cmp_w2, w_nsa_o, w_pool, pool_scale, w_pool_o,
                  q_norm, w_uq, kv_norm, w_ukv, w_mla_o, conv_w, conv_b, conv_ln_g, conv_ln_b, w_conv_o, w_out,
                  ln_g, ln_b):
    s_len = x.shape[0]
    mband, scat, eslot, gexp, inv_slot = tabs
    row = lambda v: v[None, :]
    o_merge = int(sum(IN_SIZES[:8]))
    tm_proj = min(2048, s_len)
    u1 = _matmul(xb, _layout_in1(w_in[:, :o_merge]).astype(BF16), _layout_in1(row(b_in[:o_merge])),
                 tm_proj, 7 * LANE, BF16)
    um = _matmul(xb, w_in[:, o_merge:].astype(BF16), row(b_in[o_merge:]), tm_proj, 1024, BF16)

    nch = s_len // CMP_STRIDE
    kdim = CMP_STRIDE * LANE
    chunks = u1[:, SLOT_KV * LANE:(SLOT_KV + 4) * LANE].reshape(nch, CMP_STRIDE, 4, LANE)
    chunks = chunks.transpose(2, 0, 1, 3).reshape(4, nch, kdim)
    w1 = _pad_cols(cmp_w1.reshape(2, CMP_LEN, NSA_DH, CMP_HIDDEN).transpose(0, 1, 3, 2), LANE)
    w1 = w1.transpose(0, 1, 3, 2)
    w1cat = jnp.concatenate([w1[:, :CMP_STRIDE].reshape(2, kdim, CMP_HIDDEN),
                             w1[:, CMP_STRIDE:].reshape(2, kdim, CMP_HIDDEN)], -1).astype(BF16)
    posp = _pad_cols(cmp_pos, LANE)
    pos2 = jnp.stack([posp[:, :CMP_STRIDE].reshape(2, kdim), posp[:, CMP_STRIDE:].reshape(2, kdim)], 1)
    pos2 = jnp.pad(pos2, ((0, 0), (0, 6), (0, 0))).astype(BF16)
    kcvc = _nsa_compress(chunks, w1cat, pos2, _pad_cols(cmp_w2, LANE).astype(BF16))
    o_cmp, selneg = _nsa_cmp(u1, kcvc, mband, 256)
    o_sel = _flash(u1, SLOT_Q // NSA_HPG, u1, SLOT_KV + 4, u1, SLOT_KV + 6, n_groups=NSA_GROUPS, hpg=NSA_HPG,
                   shared_kv=True, window=False, out_dtype=F32, dv=NSA_DH, tq=FLASH_TQ, tk=FLASH_TK, sel=selneg,
                   scat=scat, eslot=eslot)
    o_win = _flash(u1, SLOT_Q // NSA_HPG, u1, SLOT_KV + 8, u1, SLOT_KV + 10, n_groups=NSA_GROUPS, hpg=NSA_HPG,
                   shared_kv=True, window=True, out_dtype=F32, dv=NSA_DH, tq=WINDOW, tk=WINDOW)

    wk, wv = _layout_mla_kv(w_ukv)
    q_m, k_m, v_m = _mla_prep(u1, pos_col, inv_slot, row(q_norm), row(kv_norm), _layout_mla_q(w_uq).astype(BF16),
                              wk.astype(BF16), wv.astype(BF16), 512)
    o_mla = _flash(q_m, 0, k_m, 0, v_m, 0, n_groups=MLA_HEADS // 4, hpg=4, shared_kv=False, window=False,
                   out_dtype=BF16, dv=MLA_DV, tq=FLASH_TQ, tk=FLASH_TK)

    pooled = _pool(u1, w_pool.astype(BF16), row(pool_scale), 512)
    conv = _conv(u1, conv_w, row(conv_b), row(conv_ln_g), row(conv_ln_b), 512)
    merged = _merge(o_cmp, o_sel, o_win, u1, gexp, pooled, o_mla, conv, um, w_nsa_o.astype(BF16),
                    w_pool_o.astype(BF16), w_mla_o.astype(BF16), w_conv_o.astype(BF16), 256)
    return _matmul_res_ln(merged, w_out.astype(BF16), x, row(ln_g), row(ln_b), 512)


def _cross_attention(x, xb, mem_b, w_q, w_k, w_v, w_o, ln_g, ln_b):
    row = lambda v: v[None, :]
    kv = _matmul(mem_b, jnp.concatenate([w_k, w_v], 1).astype(BF16), jnp.zeros((1, 2 * X_HEADS * X_DH), F32),
                 mem_b.shape[0], 2 * X_HEADS * X_DH, BF16)
    hw = X_HEADS * X_DH
    return _xattn(xb, x, w_q.astype(BF16), kv[:, :hw], kv[:, hw:], w_o.astype(BF16), row(ln_g), row(ln_b), 256)


def _hier_moe(x, xb, w_group, b_group, w_router, b_router, w_e_in, w_e_out, layer, ln_g, ln_b):
    t_len = x.shape[0]
    row = lambda v: v[None, :]
    tm_r = 512
    w_gr = _pad_cols(jnp.concatenate([w_group, w_router], 1), LANE).astype(BF16)
    b_gr = _pad_cols(row(jnp.concatenate([b_group, b_router])), LANE)
    tri = jnp.asarray(np.tril(np.ones((tm_r, tm_r), np.float32)), BF16)
    info, cnt = _router(xb, w_gr, b_gr, tri, tm_r)

    counts = cnt[0, ROUTE_OFF:ROUTE_OFF + N_EXPERTS].astype(I32)
    padded = (counts + MOE_BLOCK - 1) // MOE_BLOCK * MOE_BLOCK
    pend = jnp.cumsum(padded)
    pstart = pend - padded
    n_blocks = -(-(t_len * MOE_TOPK) // MOE_BLOCK) + N_EXPERTS
    n_used = pend[-1] // MOE_BLOCK
    blk_ids = jnp.minimum(jnp.arange(n_blocks), n_used - 1)
    owner = jnp.sum((pend[None, :] <= (blk_ids * MOE_BLOCK)[:, None]).astype(I32), axis=1)
    block_expert = jnp.minimum(owner, N_EXPERTS - 1).astype(I32)
    e_ids = info[:, 0:MOE_TOPK].astype(I32)
    pos = (pstart[e_ids] + info[:, 4:4 + MOE_TOPK].astype(I32)).reshape(-1)

    xe = _dispatch(pos, x, n_blocks * MOE_BLOCK, 512)
    yb = _experts(block_expert, n_used.reshape(1).astype(I32), xe, w_e_in, w_e_out, layer)
    return _combine(pos, yb, x, info, row(ln_g), row(ln_b), 256)


def kernel(x, mem, positions, w_in, b_in, nsa_cmp_pos, nsa_cmp_w1, nsa_cmp_w2, w_nsa_o, w_pool, pool_scale, w_pool_o, mla_q_norm, w_mla_uq, mla_kv_norm, w_mla_ukv, w_mla_o, conv_w, conv_b, conv_ln_g, conv_ln_b, w_conv_o, w_out, ln_mix_g, ln_mix_b, w_xq, w_xk, w_xv, w_xo, ln_x_g, ln_x_b, w_group, b_group, w_router, b_router, w_expert_in, w_expert_out, ln_ffn_g, ln_ffn_b):
    batch, s_len, d = x.shape
    assert batch == 1 and d == D_MODEL and s_len % (2 * FLASH_TK) == 0 and s_len % FLASH_TQ == 0
    x = x[0]
    xb = x.astype(BF16)
    mem_b = mem[0].astype(BF16)
    pos_col = positions[0].astype(F32)[:, None]
    tabs = _static_tables(s_len)
    for l in range(w_in.shape[0]):
        x, xb = _hybrid_mixer(x, xb, pos_col, tabs, w_in[l], b_in[l], nsa_cmp_pos[l], nsa_cmp_w1[l], nsa_cmp_w2[l],
                              w_nsa_o[l], w_pool[l], pool_scale[l], w_pool_o[l], mla_q_norm[l], w_mla_uq[l],
                              mla_kv_norm[l], w_mla_ukv[l], w_mla_o[l], conv_w[l], conv_b[l], conv_ln_g[l],
                              conv_ln_b[l], w_conv_o[l], w_out[l], ln_mix_g[l], ln_mix_b[l])
        x, xb = _cross_attention(x, xb, mem_b, w_xq[l], w_xk[l], w_xv[l], w_xo[l], ln_x_g[l], ln_x_b[l])
        x, xb = _hier_moe(x, xb, w_group[l], b_group[l], w_router[l], b_router[l], w_expert_in,
                          w_expert_out, l, ln_ffn_g[l], ln_ffn_b[l])
    return x[None]
```

```python
import functools

import numpy as np
import jax
import jax.numpy as jnp
from jax import lax
from jax.experimental import pallas as pl
from jax.experimental.pallas import tpu as pltpu

F32 = jnp.float32
BF16 = jnp.bfloat16
I32 = jnp.int32

D_MODEL = 2048
NSA_HEADS = 8
NSA_GROUPS = 2
NSA_HPG = NSA_HEADS // NSA_GROUPS
NSA_DH = 64
CMP_LEN = 32
CMP_STRIDE = 16
CMP_HIDDEN = 128
SEL_LEN = 64
SEL_TOPN = 16
WINDOW = 512
FORCE_SCORE = 1.0e4
POOL_GROUPS = 4
POOL_WINDOWS = (2, 4, 8, 16)
POOL_WIDTH = 512
POOL_GW = POOL_WIDTH // POOL_GROUPS
MLA_HEADS = 8
MLA_Q_RANK = 512
MLA_KV_RANK = 256
MLA_NOPE = 64
MLA_ROPE = 32
MLA_DV = 64
ROPE_BASE = 10000.0
CONV_CH = 512
CONV_K = 31
N_BRANCH = 4
X_HEADS = 4
X_DH = 128
MOE_GROUPS = 4
MOE_EPG = 8
N_EXPERTS = MOE_GROUPS * MOE_EPG
MOE_TOPK = 2
EXPERT_FF = 512
MOE_BLOCK = 256
LN_EPS = 1e-5
RMS_EPS = 1e-6
DEPTH = 2
DN_ALPHA = (2 * DEPTH) ** 0.25
IN_SIZES = (NSA_HEADS * NSA_DH, 6 * NSA_GROUPS * NSA_DH, 3 * NSA_HEADS, POOL_WIDTH,
            MLA_Q_RANK, MLA_KV_RANK, MLA_ROPE, 2 * CONV_CH, N_BRANCH * D_MODEL)

LANE = 128
VMEM_LIMIT = 56 * 1024 * 1024
NEG = -1.0e30
FLASH_TQ = 1024
FLASH_TK = 1024
LOG2E = 1.4426950408889634

SLOT_Q = 0
SLOT_KV = 8
SLOT_POOL = 20
SLOT_CQ = 24
SLOT_CKV = 28
SLOT_KR = 30
SLOT_CONV = 32
SLOT_GATE = 40
N_SLOTS1 = 42
N1 = N_SLOTS1 * LANE


def _cparams(sem, vmem=VMEM_LIMIT):
    return pltpu.CompilerParams(dimension_semantics=sem, vmem_limit_bytes=vmem)


def _sigmoid(x):
    return 1.0 / (1.0 + jnp.exp(-x))


def _layer_norm(z, g, b):
    mu = jnp.mean(z, -1, keepdims=True)
    d = z - mu
    var = jnp.mean(d * d, -1, keepdims=True)
    return d * lax.rsqrt(var + LN_EPS) * g + b


def _dot(a, b):
    return jnp.dot(a, b, preferred_element_type=F32)


def _dot_nt(a, b):
    return lax.dot_general(a, b, (((1,), (1,)), ((), ())), preferred_element_type=F32)


def _dot3(a, b):
    hi = a.astype(BF16)
    r1 = a - hi.astype(F32)
    mid = r1.astype(BF16)
    lo = (r1 - mid.astype(F32)).astype(BF16)
    return _dot(hi, b) + _dot(mid, b) + _dot(lo, b)


def _mm_kernel(a_ref, b_ref, bias_ref, o_ref):
    o_ref[...] = (_dot(a_ref[...], b_ref[...]) + bias_ref[...]).astype(o_ref.dtype)


def _matmul(a, b, bias, tm, tn, out_dtype):
    m, k = a.shape
    n = b.shape[1]
    return pl.pallas_call(
        _mm_kernel,
        grid=(m // tm, n // tn),
        in_specs=[pl.BlockSpec((tm, k), lambda i, j: (i, 0)),
                  pl.BlockSpec((k, tn), lambda i, j: (0, j)),
                  pl.BlockSpec((1, tn), lambda i, j: (0, j))],
        out_specs=pl.BlockSpec((tm, tn), lambda i, j: (i, j)),
        out_shape=jax.ShapeDtypeStruct((m, n), out_dtype),
        compiler_params=_cparams(("parallel", "arbitrary")),
        name="matmul",
    )(a, b, bias)


def _mm_ln_kernel(a_ref, w_ref, x_ref, g_ref, b_ref, xo_ref, xb_ref):
    h = _dot(a_ref[...], w_ref[...])
    y = _layer_norm(DN_ALPHA * x_ref[...] + h, g_ref[...], b_ref[...])
    xo_ref[...] = y
    xb_ref[...] = y.astype(BF16)


def _matmul_res_ln(a, w, x, g, b, tm):
    m, k = a.shape
    d = w.shape[1]
    return pl.pallas_call(
        _mm_ln_kernel,
        grid=(m // tm,),
        in_specs=[pl.BlockSpec((tm, k), lambda i: (i, 0)),
                  pl.BlockSpec((k, d), lambda i: (0, 0)),
                  pl.BlockSpec((tm, d), lambda i: (i, 0)),
                  pl.BlockSpec((1, d), lambda i: (0, 0)),
                  pl.BlockSpec((1, d), lambda i: (0, 0))],
        out_specs=[pl.BlockSpec((tm, d), lambda i: (i, 0)),
                   pl.BlockSpec((tm, d), lambda i: (i, 0))],
        out_shape=[jax.ShapeDtypeStruct((m, d), F32), jax.ShapeDtypeStruct((m, d), BF16)],
        compiler_params=_cparams(("parallel",)),
        name="matmul_res_ln",
    )(a, w, x, g, b)


def _compress_kernel(a_ref, w1_ref, pos_ref, w2_ref, o_ref):
    nch = a_ref.shape[1]
    hh = _dot(a_ref[0], w1_ref[0])
    pp = _dot(pos_ref[0], w1_ref[0])
    pos_term = pp[0:1, :CMP_HIDDEN] + pp[1:2, CMP_HIDDEN:]
    h2_next = pltpu.roll(hh[:, CMP_HIDDEN:], nch - 1, 0)
    z = hh[:, :CMP_HIDDEN] + h2_next + pos_term
    hid = 0.5 * z * (1.0 + jnp.tanh(0.7978845608028654 * (z + 0.044715 * z * z * z)))
    o_ref[0] = _dot(hid.astype(BF16), w2_ref[0]).astype(o_ref.dtype)


def _nsa_compress(chunks, w1cat, pos2, w2pad):
    n4, nch, kdim = chunks.shape
    return pl.pallas_call(
        _compress_kernel,
        grid=(n4,),
        in_specs=[pl.BlockSpec((1, nch, kdim), lambda c: (c, 0, 0)),
                  pl.BlockSpec((1, kdim, 2 * CMP_HIDDEN), lambda c: (c // NSA_GROUPS, 0, 0)),
                  pl.BlockSpec((1, 8, kdim), lambda c: (c // NSA_GROUPS, 0, 0)),
                  pl.BlockSpec((1, CMP_HIDDEN, LANE), lambda c: (c // NSA_GROUPS, 0, 0))],
        out_specs=pl.BlockSpec((1, nch, LANE), lambda c: (c, 0, 0)),
        out_shape=jax.ShapeDtypeStruct((n4, nch, LANE), BF16),
        compiler_params=_cparams(("arbitrary",)),
        name="nsa_compress",
    )(chunks, w1cat, pos2, w2pad)


CMP_WIDTH_STEPS = 4


def _nsa_cmp_kernel(q_ref, kc_ref, vc_ref, mband_ref, ocmp_ref, selneg_ref, score_sc, *, tq, n_cmp, n_sel, top_n):
    t0 = pl.program_id(0) * tq
    nch = kc_ref.shape[1]
    row = lax.broadcasted_iota(I32, (NSA_HPG * tq, 1), 0)
    t_row = t0 + (row & (tq - 1))

    def softmax_part(width):
        col = lax.broadcasted_iota(I32, (1, width), 1)
        vis = (col * CMP_STRIDE + (CMP_LEN - 1) <= t_row) & (col < n_cmp)
        for g in range(NSA_GROUPS):
            qs = jnp.concatenate(
                [q_ref[:, (g * NSA_HPG + h) * LANE:(g * NSA_HPG + h + 1) * LANE] for h in range(NSA_HPG)], axis=0)
            s = _dot_nt(qs, kc_ref[g, 0:width, :])
            s = jnp.where(vis, s, NEG)
            m = jnp.max(s, -1, keepdims=True)
            p = jnp.where(vis, jnp.exp2(s - m), 0.0)
            l = jnp.sum(p, -1, keepdims=True)
            p = p * (1.0 / jnp.maximum(l, 1e-30))
            o = _dot(p.astype(BF16), vc_ref[g, 0:width, :])
            for h in range(NSA_HPG):
                hh = g * NSA_HPG + h
                ocmp_ref[:, hh * NSA_DH:(hh + 1) * NSA_DH] = o[h * tq:(h + 1) * tq, :NSA_DH]
            imp = p[0:tq]
            for h in range(1, NSA_HPG):
                imp = imp + p[h * tq:(h + 1) * tq]
            score_sc[g * tq:(g + 1) * tq, :] = _dot3(imp, mband_ref[0:width, :])

    last_vis = (t0 + tq - CMP_LEN) // CMP_STRIDE
    step_w = nch // CMP_WIDTH_STEPS
    variant = jnp.minimum(last_vis // step_w, CMP_WIDTH_STEPS - 1)
    for v in range(CMP_WIDTH_STEPS):
        pl.when(variant == v)(functools.partial(softmax_part, (v + 1) * step_w))

    blk = lax.broadcasted_iota(I32, (NSA_GROUPS * tq, n_sel), 1).astype(F32)
    row2 = lax.broadcasted_iota(I32, (NSA_GROUPS * tq, 1), 0)
    cur = ((t0 + (row2 & (tq - 1))) // SEL_LEN).astype(F32)
    valid = blk <= cur
    forced = (blk == 0.0) | (blk == cur) | (blk == cur - 1.0)
    assert FORCE_SCORE > NSA_HPG * (SEL_LEN // CMP_STRIDE + CMP_LEN // CMP_STRIDE - 1) and top_n > 3
    score = jnp.where(forced, -2.0, score_sc[...])
    score = jnp.where(valid, score, -1.0)

    def pick_one(_, sc):
        mx = jnp.max(sc, -1, keepdims=True)
        first = jnp.min(jnp.where(sc == mx, blk, float(n_sel)), -1, keepdims=True)
        return jnp.where(blk == first, -2.0, sc)

    sc = lax.fori_loop(0, top_n - 3, pick_one, score)
    selneg = jnp.where(valid & (sc == -2.0), 0.0, -1.0).astype(selneg_ref.dtype)
    for g in range(NSA_GROUPS):
        selneg_ref[:, g * n_sel:(g + 1) * n_sel] = selneg[g * tq:(g + 1) * tq]


def _nsa_cmp(u1, kcvc, mband, tq):
    s_len = u1.shape[0]
    nch = kcvc.shape[1]
    n_sel = s_len // SEL_LEN
    n_cmp = (s_len - CMP_LEN) // CMP_STRIDE + 1
    kern = functools.partial(_nsa_cmp_kernel, tq=tq, n_cmp=n_cmp, n_sel=n_sel, top_n=min(SEL_TOPN, n_sel))
    return pl.pallas_call(
        kern,
        grid=(s_len // tq,),
        in_specs=[pl.BlockSpec((tq, NSA_HEADS * LANE), lambda i: (i, SLOT_Q // NSA_HEADS)),
                  pl.BlockSpec((NSA_GROUPS, nch, LANE), lambda i: (0, 0, 0)),
                  pl.BlockSpec((NSA_GROUPS, nch, LANE), lambda i: (1, 0, 0)),
                  pl.BlockSpec((nch, n_sel), lambda i: (0, 0))],
        out_specs=[pl.BlockSpec((tq, NSA_HEADS * NSA_DH), lambda i: (i, 0)),
                   pl.BlockSpec((tq, NSA_GROUPS * n_sel), lambda i: (i, 0))],
        out_shape=[jax.ShapeDtypeStruct((s_len, NSA_HEADS * NSA_DH), F32),
                   jax.ShapeDtypeStruct((s_len, NSA_GROUPS * n_sel), BF16)],
        scratch_shapes=[pltpu.VMEM((NSA_GROUPS * tq, n_sel), F32)],
        compiler_params=_cparams(("parallel",)),
        name="nsa_cmp_topk",
    )(u1, kcvc, kcvc, mband)


def _flash_kernel(qi_ref, kj_ref, first_ref, last_ref, mode_ref, off_ref, *refs, hpg, shared_kv, select, modes,
                  dv):
    if select:
        q_ref, k_ref, v_ref, sel_ref, scat_ref, eslot_ref, o_ref, m_sc, acc_sc = refs
    else:
        q_ref, k_ref, v_ref, o_ref, m_sc, acc_sc = refs
    p_idx = pl.program_id(1)
    tq = q_ref.shape[0]
    tk = k_ref.shape[0]

    @pl.when(first_ref[p_idx] == 1)
    def _():
        m_sc[...] = jnp.full(m_sc.shape, NEG, F32)
        acc_sc[...] = jnp.zeros(acc_sc.shape, F32)

    def step(mask_mode):
        k_all = k_ref[...]
        lane = lax.broadcasted_iota(I32, (1, v_ref.shape[1]), 1)
        v_all = v_ref[...] + ((lane & (LANE - 1)) == dv).astype(BF16)
        if select:
            k_all = k_all + eslot_ref[...]
            bias = (_dot(sel_ref[...], scat_ref[kj_ref[p_idx]]) * (-NEG)).astype(BF16)
        if mask_mode:
            r = lax.broadcasted_iota(I32, (tq, tk), 0)
            c = lax.broadcasted_iota(I32, (tq, tk), 1)
            keep = (c <= r + off_ref[p_idx]) if mask_mode == 1 else (c > r)
        for h in range(hpg):
            hk = 0 if shared_kv else h
            q = q_ref[:, h * LANE:(h + 1) * LANE]
            if select:
                q = q + bias
            s = _dot_nt(q, k_all[:, hk * LANE:(hk + 1) * LANE])
            if mask_mode:
                s = jnp.where(keep, s, NEG)
            m_prev = m_sc[h]
            m_new = jnp.maximum(m_prev, jnp.max(s, -1, keepdims=True))
            p = jnp.exp2(s - jnp.concatenate([m_new] * (tk // LANE), axis=1))
            acc_sc[h] = (jnp.exp2(m_prev - m_new) * acc_sc[h]
                         + _dot(p.astype(BF16), v_all[:, hk * LANE:(hk + 1) * LANE]))
            m_sc[h] = m_new

    for mm in modes:
        pl.when(mode_ref[p_idx] == mm)(functools.partial(step, mm))

    @pl.when(last_ref[p_idx] == 1)
    def _():
        for h in range(hpg):
            acc = acc_sc[h]
            o = acc[:, :dv] * (1.0 / acc[:, dv:dv + 1])
            o_ref[:, h * dv:(h + 1) * dv] = o.astype(o_ref.dtype)


def _pair_tables(s_len, tq, tk, window):
    rows = []
    for i in range(s_len // tq):
        if window:
            js = ([i - 1] if i > 0 else []) + [i]
        else:
            js = list(range((i * tq + tq - 1) // tk + 1))
        for j in js:
            if window:
                mode = 1 if j == i else 2
            else:
                mode = 0 if (j + 1) * tk - 1 <= i * tq else 1
            rows.append((i, j, int(j == js[0]), int(j == js[-1]), mode, i * tq - j * tk))
    tab = np.asarray(rows, np.int32)
    return [jnp.asarray(tab[:, c]) for c in range(tab.shape[1])]


def _flash(q_arr, q_blk0, k_arr, k_blk0, v_arr, v_blk0, *, n_groups, hpg, shared_kv, window,
           out_dtype, dv, tq, tk, sel=None, scat=None, eslot=None):
    s_len = q_arr.shape[0]
    assert not window or tk == tq == WINDOW
    nk = 1 if shared_kv else hpg
    select = sel is not None
    tables = _pair_tables(s_len, tq, tk, window)
    n_pairs = int(tables[0].shape[0])
    modes = (1, 2) if window else (0, 1)

    def qmap(g, p, qi, kj, *_):
        return (qi[p], q_blk0 + g)

    def kmap(g, p, qi, kj, *_):
        return (kj[p], k_blk0 + g)

    def vmap_(g, p, qi, kj, *_):
        return (kj[p], v_blk0 + g)

    def omap(g, p, qi, kj, *_):
        return (qi[p], g)

    in_specs = [pl.BlockSpec((tq, hpg * LANE), qmap),
                pl.BlockSpec((tk, nk * LANE), kmap),
                pl.BlockSpec((tk, nk * LANE), vmap_)]
    args = [q_arr, k_arr, v_arr]
    if select:
        n_sel = sel.shape[1] // n_groups
        in_specs += [pl.BlockSpec((tq, n_sel), omap),
                     pl.BlockSpec(scat.shape, lambda g, p, *_: (0, 0, 0)),
                     pl.BlockSpec(eslot.shape, lambda g, p, *_: (0, 0))]
        args += [sel, scat, eslot]
    kern = functools.partial(_flash_kernel, hpg=hpg, shared_kv=shared_kv, select=select, modes=modes, dv=dv)
    grid_spec = pltpu.PrefetchScalarGridSpec(
        num_scalar_prefetch=len(tables),
        grid=(n_groups, n_pairs),
        in_specs=in_specs,
        out_specs=pl.BlockSpec((tq, hpg * dv), omap),
        scratch_shapes=[pltpu.VMEM((hpg, tq, LANE), F32), pltpu.VMEM((hpg, tq, LANE), F32)],
    )
    return pl.pallas_call(
        kern,
        grid_spec=grid_spec,
        out_shape=jax.ShapeDtypeStruct((s_len, n_groups * hpg * dv), out_dtype),
        compiler_params=_cparams(("parallel", "arbitrary")),
        name="flash_sel" if select else ("flash_win" if window else "flash_causal"),
    )(*tables, *args)


def _mla_prep_kernel(cq_ref, ckv_ref, kr_ref, pos_ref, inv_ref, qn_ref, kvn_ref, wuq_ref, wk_ref, wv_ref,
                     q_ref, k_ref, v_ref):
    def rms(x, g):
        return x * lax.rsqrt(jnp.mean(x * x, -1, keepdims=True) + RMS_EPS) * g

    ang = pos_ref[...] * inv_ref[...]
    cos, sin = jnp.cos(ang), jnp.sin(ang)
    hw = MLA_HEADS * LANE
    qh = _dot(rms(cq_ref[...].astype(F32), qn_ref[...]).astype(BF16), wuq_ref[...])
    scale = (MLA_NOPE + MLA_ROPE) ** -0.5 * LOG2E
    cos_t = jnp.concatenate([cos] * MLA_HEADS, axis=1)
    sin_t = jnp.concatenate([sin] * MLA_HEADS, axis=1)
    q_ref[...] = ((qh[:, :hw] * cos_t + qh[:, hw:] * sin_t) * scale).astype(BF16)
    ckv = rms(ckv_ref[...].astype(F32), kvn_ref[...]).astype(BF16)
    kr = kr_ref[...].astype(F32)
    k_rope = kr[:, :LANE] * cos + kr[:, LANE:] * sin
    k_ref[...] = (_dot(ckv, wk_ref[...]) + jnp.concatenate([k_rope] * MLA_HEADS, axis=1)).astype(BF16)
    v_ref[...] = _dot(ckv, wv_ref[...]).astype(BF16)


def _mla_prep(u1, pos_col, inv_slot, qn, kvn, wuq, wk, wv, tm):
    s_len = u1.shape[0]
    hw = MLA_HEADS * LANE
    full = lambda a: pl.BlockSpec(a.shape, lambda i: (0,) * a.ndim)
    out = jax.ShapeDtypeStruct((s_len, hw), BF16)
    return pl.pallas_call(
        _mla_prep_kernel,
        grid=(s_len // tm,),
        in_specs=[pl.BlockSpec((tm, MLA_Q_RANK), lambda i: (i, SLOT_CQ * LANE // MLA_Q_RANK)),
                  pl.BlockSpec((tm, MLA_KV_RANK), lambda i: (i, SLOT_CKV * LANE // MLA_KV_RANK)),
                  pl.BlockSpec((tm, 2 * LANE), lambda i: (i, SLOT_KR // 2)),
                  pl.BlockSpec((tm, 1), lambda i: (i, 0)),
                  full(inv_slot), full(qn), full(kvn), full(wuq), full(wk), full(wv)],
        out_specs=[pl.BlockSpec((tm, hw), lambda i: (i, 0))] * 3,
        out_shape=[out, out, out],
        compiler_params=_cparams(("parallel",)),
        name="mla_prep",
    )(u1, u1, u1, pos_col, inv_slot, qn, kvn, wuq, wk, wv)


POOL_HALO = 16


def _pool_kernel(cur_ref, halo_ref, wp_ref, scale_ref, o_ref):
    i = pl.program_id(0)
    tm = cur_ref.shape[0]
    halo = jnp.where(i > 0, halo_ref[...].astype(F32), 0.0)
    x = jnp.concatenate([halo, cur_ref[...].astype(F32)], axis=0)
    sums = {1: x}
    w = 1
    while w < max(POOL_WINDOWS):
        a = sums[w]
        sums[2 * w] = a[w:] + a[:-w]
        w *= 2
    t = (i * tm + lax.broadcasted_iota(I32, (tm, 1), 0) + 1).astype(F32)
    outs = []
    for gi, w in enumerate(POOL_WINDOWS):
        lo = gi * POOL_GW
        start = POOL_HALO - (w - 1)
        win = sums[w][start:start + tm, lo:lo + POOL_GW]
        mean = win / jnp.minimum(t, float(w))
        pooled = mean - x[POOL_HALO:, lo:lo + POOL_GW]
        outs.append(_dot(pooled.astype(BF16), wp_ref[gi]))
    o_ref[...] = (jnp.concatenate(outs, axis=1) * scale_ref[...]).astype(o_ref.dtype)


def _pool(u1, w_pool, pool_scale, tm):
    s_len = u1.shape[0]
    blk = SLOT_POOL * LANE // POOL_WIDTH
    return pl.pallas_call(
        _pool_kernel,
        grid=(s_len // tm,),
        in_specs=[pl.BlockSpec((tm, POOL_WIDTH), lambda i: (i, blk)),
                  pl.BlockSpec((POOL_HALO, POOL_WIDTH),
                               lambda i: (jnp.maximum(i * (tm // POOL_HALO) - 1, 0), blk)),
                  pl.BlockSpec(w_pool.shape, lambda i: (0, 0, 0)),
                  pl.BlockSpec((1, POOL_WIDTH), lambda i: (0, 0))],
        out_specs=pl.BlockSpec((tm, POOL_WIDTH), lambda i: (i, 0)),
        out_shape=jax.ShapeDtypeStruct((s_len, POOL_WIDTH), BF16),
        compiler_params=_cparams(("parallel",)),
        name="pool",
    )(u1, u1, w_pool, pool_scale)


CONV_HALO = 32


def _conv_kernel(cur_ref, halo_ref, w_ref, b_ref, g_ref, beta_ref, o_ref, hbuf):
    i = pl.program_id(0)
    tm = cur_ref.shape[0]

    def glu(u):
        u = u.astype(F32)
        return u[:, :CONV_CH] * _sigmoid(u[:, CONV_CH:])

    hbuf[0:CONV_HALO, :] = jnp.where(i > 0, glu(halo_ref[...]), 0.0)
    hbuf[CONV_HALO:, :] = glu(cur_ref[...])
    acc = jnp.zeros((tm, CONV_CH), F32) + b_ref[...]
    for k in range(CONV_K):
        off = CONV_HALO - (CONV_K - 1) + k
        acc = acc + hbuf[off:off + tm, :] * w_ref[k:k + 1, :]
    y = _layer_norm(acc, g_ref[...], beta_ref[...])
    o_ref[...] = (y * _sigmoid(y)).astype(o_ref.dtype)


def _conv(u1, conv_w, conv_b, ln_g, ln_b, tm):
    s_len = u1.shape[0]
    blk = SLOT_CONV * LANE // (2 * CONV_CH)
    row = lambda a: pl.BlockSpec(a.shape, lambda i: (0, 0))
    return pl.pallas_call(
        _conv_kernel,
        grid=(s_len // tm,),
        in_specs=[pl.BlockSpec((tm, 2 * CONV_CH), lambda i: (i, blk)),
                  pl.BlockSpec((CONV_HALO, 2 * CONV_CH),
                               lambda i: (jnp.maximum(i * (tm // CONV_HALO) - 1, 0), blk)),
                  row(conv_w), row(conv_b), row(ln_g), row(ln_b)],
        out_specs=pl.BlockSpec((tm, CONV_CH), lambda i: (i, 0)),
        out_shape=jax.ShapeDtypeStruct((s_len, CONV_CH), BF16),
        scratch_shapes=[pltpu.VMEM((tm + CONV_HALO, CONV_CH), F32)],
        compiler_params=_cparams(("parallel",)),
        name="conv_module",
    )(u1, u1, conv_w, conv_b, ln_g, ln_b)


def _merge_kernel(ocmp_ref, osel_ref, owin_ref, gate_ref, gexp_ref, pool_ref, mla_ref, conv_ref, um_ref,
                  wn_ref, wp_ref, wm_ref, wc_ref, o_ref):
    sg = _sigmoid(gate_ref[...].astype(F32))
    nsa = (_dot3(sg, gexp_ref[0]) * ocmp_ref[...] + _dot3(sg, gexp_ref[1]) * osel_ref[...]
           + _dot3(sg, gexp_ref[2]) * owin_ref[...])
    branches = (_dot(nsa.astype(BF16), wn_ref[...]), _dot(pool_ref[...], wp_ref[...]),
                _dot(mla_ref[...], wm_ref[...]), _dot(conv_ref[...], wc_ref[...]))
    merged = None
    for j, br in enumerate(branches):
        term = _sigmoid(um_ref[:, j * D_MODEL:(j + 1) * D_MODEL].astype(F32)) * br
        merged = term if merged is None else merged + term
    o_ref[...] = merged.astype(o_ref.dtype)


def _merge(o_cmp, o_sel, o_win, u1, gexp, pooled, o_mla, conv, um, wn, wp, wm, wc, tm):
    s_len = u1.shape[0]
    tile = lambda w: pl.BlockSpec((tm, w), lambda i: (i, 0))
    full = lambda a: pl.BlockSpec(a.shape, lambda i: (0,) * a.ndim)
    return pl.pallas_call(
        _merge_kernel,
        grid=(s_len // tm,),
        in_specs=[tile(512), tile(512), tile(512),
                  pl.BlockSpec((tm, LANE), lambda i: (i, SLOT_GATE)),
                  full(gexp), tile(512), tile(512), tile(512), tile(N_BRANCH * D_MODEL),
                  full(wn), full(wp), full(wm), full(wc)],
        out_specs=tile(D_MODEL),
        out_shape=jax.ShapeDtypeStruct((s_len, D_MODEL), BF16),
        compiler_params=_cparams(("parallel",)),
        name="branch_merge",
    )(o_cmp, o_sel, o_win, u1, gexp, pooled, o_mla, conv, um, wn, wp, wm, wc)


def _xattn_kernel(xb_ref, x_ref, wq_ref, k_ref, v_ref, wo_ref, g_ref, b_ref, xo_ref, xob_ref):
    q = _dot(xb_ref[...], wq_ref[...]).astype(BF16)
    k = k_ref[...]
    v = v_ref[...]
    outs = []
    for h in range(X_HEADS):
        sl = slice(h * X_DH, (h + 1) * X_DH)
        s = _dot_nt(q[:, sl], k[:, sl]) * (X_DH ** -0.5)
        m = jnp.max(s, -1, keepdims=True)
        p = jnp.exp(s - m)
        p = p * (1.0 / jnp.sum(p, -1, keepdims=True))
        outs.append(_dot(p.astype(BF16), v[:, sl]))
    o = jnp.concatenate(outs, axis=1).astype(BF16)
    y = _layer_norm(DN_ALPHA * x_ref[...] + _dot(o, wo_ref[...]), g_ref[...], b_ref[...])
    xo_ref[...] = y
    xob_ref[...] = y.astype(BF16)


def _xattn(xb, x, wq, k, v, wo, g, b, tm):
    s_len = x.shape[0]
    tile = lambda: pl.BlockSpec((tm, D_MODEL), lambda i: (i, 0))
    full = lambda a: pl.BlockSpec(a.shape, lambda i: (0,) * a.ndim)
    return pl.pallas_call(
        _xattn_kernel,
        grid=(s_len // tm,),
        in_specs=[tile(), tile(), full(wq), full(k), full(v), full(wo), full(g), full(b)],
        out_specs=[tile(), tile()],
        out_shape=[jax.ShapeDtypeStruct((s_len, D_MODEL), F32), jax.ShapeDtypeStruct((s_len, D_MODEL), BF16)],
        compiler_params=_cparams(("parallel",)),
        name="cross_attention_ln",
    )(xb, x, wq, k, v, wo, g, b)


ROUTE_OFF = MOE_GROUPS


def _router_kernel(xb_ref, w_ref, b_ref, tri_ref, info_ref, cnt_ref, carry):
    @pl.when(pl.program_id(0) == 0)
    def _():
        carry[...] = jnp.zeros(carry.shape, F32)

    logits = _dot(xb_ref[...], w_ref[...]) + b_ref[...]
    lane = lax.broadcasted_iota(I32, logits.shape, 1).astype(F32)
    is_g = lane < float(MOE_GROUPS)
    neg_inf = -jnp.inf
    gl = jnp.where(is_g, logits, neg_inf)
    gmax = jnp.max(gl, -1, keepdims=True)
    g_sel = jnp.min(jnp.where(gl == gmax, lane, float(LANE)), -1, keepdims=True)
    g_w = 1.0 / jnp.sum(jnp.where(is_g, jnp.exp(gl - gmax), 0.0), -1, keepdims=True)
    lo = ROUTE_OFF + MOE_EPG * g_sel
    in_g = (lane >= lo) & (lane < lo + MOE_EPG)
    el = jnp.where(in_g, logits, neg_inf)
    emax = jnp.max(el, -1, keepdims=True)
    e = jnp.where(in_g, jnp.exp(el - emax), 0.0)
    p = e / jnp.sum(e, -1, keepdims=True)
    pm = jnp.where(in_g, p, -1.0)
    p1 = jnp.max(pm, -1, keepdims=True)
    i1 = jnp.min(jnp.where(pm == p1, lane, float(LANE)), -1, keepdims=True)
    pm2 = jnp.where(lane == i1, -1.0, pm)
    p2 = jnp.max(pm2, -1, keepdims=True)
    i2 = jnp.min(jnp.where(pm2 == p2, lane, float(LANE)), -1, keepdims=True)
    denom = p1 + p2
    gate1 = g_w * p1 / denom
    gate2 = g_w * p2 / denom

    oh1 = (lane == i1).astype(BF16)
    oh2 = (lane == i2).astype(BF16)
    incl1 = _dot(tri_ref[...], oh1)
    incl2 = _dot(tri_ref[...], oh2)
    tot1 = jnp.sum(oh1.astype(F32), 0, keepdims=True)
    tot2 = jnp.sum(oh2.astype(F32), 0, keepdims=True)
    base = carry[...]
    rank1 = jnp.sum(jnp.where(lane == i1, base + incl1, 0.0), -1, keepdims=True) - 1.0
    rank2 = jnp.sum(jnp.where(lane == i2, base + tot1 + incl2, 0.0), -1, keepdims=True) - 1.0
    carry[...] = base + tot1 + tot2
    cnt_ref[...] = carry[...]

    cols = (i1 - ROUTE_OFF, i2 - ROUTE_OFF, gate1, gate2, rank1, rank2)
    info = jnp.zeros(logits.shape, F32)
    for c, val in enumerate(cols):
        info = jnp.where(lane == float(c), val, info)
    info_ref[...] = info


def _router(xb, w_gr, b_gr, tri, tm):
    t_len = xb.shape[0]
    return pl.pallas_call(
        _router_kernel,
        grid=(t_len // tm,),
        in_specs=[pl.BlockSpec((tm, D_MODEL), lambda i: (i, 0)),
                  pl.BlockSpec(w_gr.shape, lambda i: (0, 0)),
                  pl.BlockSpec((1, LANE), lambda i: (0, 0)),
                  pl.BlockSpec((tm, tm), lambda i: (0, 0))],
        out_specs=[pl.BlockSpec((tm, LANE), lambda i: (i, 0)),
                   pl.BlockSpec((1, LANE), lambda i: (0, 0))],
        out_shape=[jax.ShapeDtypeStruct((t_len, LANE), F32), jax.ShapeDtypeStruct((1, LANE), F32)],
        scratch_shapes=[pltpu.VMEM((1, LANE), F32)],
        compiler_params=_cparams(("arbitrary",)),
        name="moe_router",
    )(xb, w_gr, b_gr, tri)


def _row_copy(src, src_row, dst, dst_row, sem):
    return pltpu.make_async_copy(src.at[pl.ds(src_row, 1)], dst.at[pl.ds(dst_row, 1)], sem)


def _dispatch_kernel(pos_ref, x_ref, xe_in_hbm, xe_hbm, sem, *, td):
    del xe_in_hbm
    base = pl.program_id(0) * td

    def issue(t, c):
        for k in range(MOE_TOPK):
            _row_copy(x_ref, t, xe_hbm, pos_ref[MOE_TOPK * (base + t) + k], sem).start()
        return c

    lax.fori_loop(0, td, issue, 0, unroll=8)
    all_rows = xe_hbm.at[pl.ds(0, MOE_TOPK * td)]
    pltpu.make_async_copy(all_rows, all_rows, sem).wait()


def _dispatch(pos_flat, x, rows, td):
    t_len, d = x.shape
    zeros = jnp.zeros((rows, d), x.dtype)
    grid_spec = pltpu.PrefetchScalarGridSpec(
        num_scalar_prefetch=1,
        grid=(t_len // td,),
        in_specs=[pl.BlockSpec((td, d), lambda i, pos: (i, 0)), pl.BlockSpec(memory_space=pl.ANY)],
        out_specs=pl.BlockSpec(memory_space=pl.ANY),
        scratch_shapes=[pltpu.SemaphoreType.DMA(())],
    )
    return pl.pallas_call(
        functools.partial(_dispatch_kernel, td=td),
        grid_spec=grid_spec,
        out_shape=jax.ShapeDtypeStruct((rows, d), x.dtype),
        input_output_aliases={2: 0},
        compiler_params=pltpu.CompilerParams(dimension_semantics=("arbitrary",)),
        name="moe_dispatch",
    )(pos_flat, x, zeros)


def _expert_kernel(be_ref, nused_ref, xe_ref, win_ref, wout_ref, yb_ref, win_bf, wout_bf):
    b = pl.program_id(0)
    used = b < nused_ref[0]
    new_expert = (b == 0) | (be_ref[b] != be_ref[jnp.maximum(b - 1, 0)])

    @pl.when(used & new_expert)
    def _():
        win_bf[...] = win_ref[0, 0].astype(BF16)
        wout_bf[...] = wout_ref[0, 0].astype(BF16)

    @pl.when(used)
    def _():
        hcat = _dot(xe_ref[...].astype(BF16), win_bf[...])
        a = hcat[:, :EXPERT_FF]
        act = (a * _sigmoid(a) * hcat[:, EXPERT_FF:]).astype(BF16)
        yb_ref[...] = _dot(act, wout_bf[...])

    @pl.when(b >= nused_ref[0])
    def _():
        yb_ref[...] = jnp.zeros(yb_ref.shape, F32)


def _experts(block_expert, n_used, xe, w_e_in, w_e_out, layer):
    rows, d = xe.shape
    n_blocks = rows // MOE_BLOCK
    grid_spec = pltpu.PrefetchScalarGridSpec(
        num_scalar_prefetch=2,
        grid=(n_blocks,),
        in_specs=[pl.BlockSpec((MOE_BLOCK, d), lambda b, be, nu: (b, 0)),
                  pl.BlockSpec((1, 1, d, 2 * EXPERT_FF), lambda b, be, nu: (layer, be[b], 0, 0)),
                  pl.BlockSpec((1, 1, EXPERT_FF, d), lambda b, be, nu: (layer, be[b], 0, 0))],
        out_specs=pl.BlockSpec((MOE_BLOCK, d), lambda b, be, nu: (b, 0)),
        scratch_shapes=[pltpu.VMEM((d, 2 * EXPERT_FF), BF16), pltpu.VMEM((EXPERT_FF, d), BF16)],
    )
    return pl.pallas_call(
        _expert_kernel,
        grid_spec=grid_spec,
        out_shape=jax.ShapeDtypeStruct((rows, d), F32),
        compiler_params=_cparams(("arbitrary",)),
        name="moe_experts",
    )(block_expert, n_used, xe, w_e_in, w_e_out)


def _combine_kernel(pos_ref, yb_hbm, x_ref, info_ref, g_ref, b_ref, xo_ref, xob_ref, buf, sem):
    tm = x_ref.shape[0]
    i = pl.program_id(0)
    slot = i & 1

    def gather(tile, slot_):
        def issue(t, c):
            for k in range(MOE_TOPK):
                _row_copy(yb_hbm, pos_ref[MOE_TOPK * (tile * tm + t) + k], buf.at[slot_, k], t,
                          sem.at[slot_]).start()
            return c
        lax.fori_loop(0, tm, issue, 0, unroll=8)

    @pl.when(i == 0)
    def _():
        gather(0, 0)

    @pl.when(i + 1 < pl.num_programs(0))
    def _():
        gather(i + 1, 1 - slot)

    pltpu.make_async_copy(buf.at[slot], buf.at[slot], sem.at[slot]).wait()
    info = info_ref[...]
    y = info[:, 2:3] * buf[slot, 0] + info[:, 3:4] * buf[slot, 1]
    z = _layer_norm(DN_ALPHA * x_ref[...] + y, g_ref[...], b_ref[...])
    xo_ref[...] = z
    xob_ref[...] = z.astype(BF16)


def _combine(pos_flat, yb, x, info, g, b, tm):
    t_len, d = x.shape
    grid_spec = pltpu.PrefetchScalarGridSpec(
        num_scalar_prefetch=1,
        grid=(t_len // tm,),
        in_specs=[pl.BlockSpec(memory_space=pl.ANY),
                  pl.BlockSpec((tm, d), lambda i, pos: (i, 0)),
                  pl.BlockSpec((tm, LANE), lambda i, pos: (i, 0)),
                  pl.BlockSpec((1, d), lambda i, pos: (0, 0)),
                  pl.BlockSpec((1, d), lambda i, pos: (0, 0))],
        out_specs=[pl.BlockSpec((tm, d), lambda i, pos: (i, 0)),
                   pl.BlockSpec((tm, d), lambda i, pos: (i, 0))],
        scratch_shapes=[pltpu.VMEM((2, MOE_TOPK, tm, d), F32), pltpu.SemaphoreType.DMA((2,))],
    )
    return pl.pallas_call(
        _combine_kernel,
        grid_spec=grid_spec,
        out_shape=[jax.ShapeDtypeStruct((t_len, d), F32), jax.ShapeDtypeStruct((t_len, d), BF16)],
        compiler_params=_cparams(("arbitrary",)),
        name="moe_combine_ln",
    )(pos_flat, yb, x, info, g, b)


def _pad_cols(m, width):
    return jnp.pad(m, [(0, 0)] * (m.ndim - 1) + [(0, width - m.shape[-1])])


def _rot_half_cols(m):
    half = m.shape[-1] // 2
    return jnp.concatenate([-m[..., half:], m[..., :half]], -1)


def _layout_in1(m):
    offs = np.cumsum((0,) + IN_SIZES)
    o_q, o_kv, o_g, o_pool, o_cq, o_ckv, o_kr, o_conv = offs[:8]
    z = lambda n: jnp.zeros(m.shape[:-1] + (n,), m.dtype)
    parts = []
    for h in range(NSA_HEADS):
        parts += [m[..., o_q + h * NSA_DH:o_q + (h + 1) * NSA_DH] * (NSA_DH ** -0.5 * LOG2E), z(LANE - NSA_DH)]
    for c in range(6 * NSA_GROUPS):
        parts += [m[..., o_kv + c * NSA_DH:o_kv + (c + 1) * NSA_DH], z(LANE - NSA_DH)]
    parts += [m[..., o_pool:o_pool + POOL_WIDTH], m[..., o_cq:o_cq + MLA_Q_RANK], m[..., o_ckv:o_ckv + MLA_KV_RANK]]
    kr = m[..., o_kr:o_kr + MLA_ROPE]
    parts += [z(MLA_NOPE), kr, z(LANE - MLA_NOPE - MLA_ROPE), z(MLA_NOPE), _rot_half_cols(kr),
              z(LANE - MLA_NOPE - MLA_ROPE)]
    parts += [m[..., o_conv:o_conv + 2 * CONV_CH], _pad_cols(m[..., o_g:o_g + 3 * NSA_HEADS], LANE), z(LANE)]
    return jnp.concatenate(parts, -1)


def _layout_mla_q(w):
    dq = MLA_NOPE + MLA_ROPE
    z = lambda n: jnp.zeros((w.shape[0], n), w.dtype)
    a, b = [], []
    for h in range(MLA_HEADS):
        rope = w[:, h * dq + MLA_NOPE:(h + 1) * dq]
        a += [w[:, h * dq:h * dq + MLA_NOPE], rope, z(LANE - dq)]
        b += [z(MLA_NOPE), _rot_half_cols(rope), z(LANE - dq)]
    return jnp.concatenate(a + b, -1)


def _layout_mla_kv(w):
    dkv = MLA_NOPE + MLA_DV
    z = jnp.zeros((w.shape[0], LANE - MLA_NOPE), w.dtype)
    k, v = [], []
    for h in range(MLA_HEADS):
        k += [w[:, h * dkv:h * dkv + MLA_NOPE], z]
        v += [w[:, h * dkv + MLA_NOPE:(h + 1) * dkv], z]
    return jnp.concatenate(k, -1), jnp.concatenate(v, -1)


def _static_tables(s_len):
    nch = s_len // CMP_STRIDE
    n_cmp = (s_len - CMP_LEN) // CMP_STRIDE + 1
    n_sel = s_len // SEL_LEN
    ratio = SEL_LEN // CMP_STRIDE
    c = np.arange(nch)[:, None]
    j = np.arange(n_sel)[None, :]
    mband = ((c >= ratio * j - 1) & (c <= ratio * j + ratio - 1) & (c < n_cmp)).astype(np.float32)
    per_tile = FLASH_TK // SEL_LEN
    n_tiles = s_len // FLASH_TK
    scat = np.zeros((n_tiles, n_sel, LANE), np.float32)
    for b in range(n_sel):
        scat[b // per_tile, b, NSA_DH + b % per_tile] = 1.0
    eslot = np.zeros((FLASH_TK, LANE), np.float32)
    eslot[np.arange(FLASH_TK), NSA_DH + np.arange(FLASH_TK) // SEL_LEN] = 1.0
    gexp = np.zeros((3, LANE, NSA_HEADS * NSA_DH), np.float32)
    for h in range(NSA_HEADS):
        for jj in range(3):
            gexp[jj, h * 3 + jj, h * NSA_DH:(h + 1) * NSA_DH] = 1.0
    half = MLA_ROPE // 2
    inv = ROPE_BASE ** (-jnp.arange(half, dtype=F32) / half)
    inv_slot = jnp.concatenate([jnp.zeros((MLA_NOPE,), F32), inv, inv,
                                jnp.zeros((LANE - MLA_NOPE - MLA_ROPE,), F32)])[None, :]
    as_bf = lambda a: jnp.asarray(a, BF16)
    return as_bf(mband), as_bf(scat), as_bf(eslot), as_bf(gexp), inv_slot


def _hybrid_mixer(x, xb, pos_col, tabs, w_in, b_in, cmp_pos, cmp_w1, cmp_w2, w_nsa_o, w_pool, pool_scale, w_pool_o,
                  q_norm, w_uq, kv_norm, w_ukv, w_mla_o, conv_w, conv_b, conv_ln_g, conv_ln_b, w_conv_o, w_out,
                  ln_g, ln_b):
    s_len = x.shape[0]
    mband, scat, eslot, gexp, inv_slot = tabs
    row = lambda v: v[None, :]
    o_merge = int(sum(IN_SIZES[:8]))
    tm_proj = min(2048, s_len)
    u1 = _matmul(xb, _layout_in1(w_in[:, :o_merge]).astype(BF16), _layout_in1(row(b_in[:o_merge])),
                 tm_proj, 7 * LANE, BF16)
    um = _matmul(xb, w_in[:, o_merge:].astype(BF16), row(b_in[o_merge:]), tm_proj, 1024, BF16)

    nch = s_len // CMP_STRIDE
    kdim = CMP_STRIDE * LANE
    chunks = u1[:, SLOT_KV * LANE:(SLOT_KV + 4) * LANE].reshape(nch, CMP_STRIDE, 4, LANE)
    chunks = chunks.transpose(2, 0, 1, 3).reshape(4, nch, kdim)
    w1 = _pad_cols(cmp_w1.reshape(2, CMP_LEN, NSA_DH, CMP_HIDDEN).transpose(0, 1, 3, 2), LANE)
    w1 = w1.transpose(0, 1, 3, 2)
    w1cat = jnp.concatenate([w1[:, :CMP_STRIDE].reshape(2, kdim, CMP_HIDDEN),
                             w1[:, CMP_STRIDE:].reshape(2, kdim, CMP_HIDDEN)], -1).astype(BF16)
    posp = _pad_cols(cmp_pos, LANE)
    pos2 = jnp.stack([posp[:, :CMP_STRIDE].reshape(2, kdim), posp[:, CMP_STRIDE:].reshape(2, kdim)], 1)
    pos2 = jnp.pad(pos2, ((0, 0), (0, 6), (0, 0))).astype(BF16)
    kcvc = _nsa_compress(chunks, w1cat, pos2, _pad_cols(cmp_w2, LANE).astype(BF16))
    o_cmp, selneg = _nsa_cmp(u1, kcvc, mband, 256)
    o_sel = _flash(u1, SLOT_Q // NSA_HPG, u1, SLOT_KV + 4, u1, SLOT_KV + 6, n_groups=NSA_GROUPS, hpg=NSA_HPG,
                   shared_kv=True, window=False, out_dtype=F32, dv=NSA_DH, tq=FLASH_TQ, tk=FLASH_TK, sel=selneg,
                   scat=scat, eslot=eslot)
    o_win = _flash(u1, SLOT_Q // NSA_HPG, u1, SLOT_KV + 8, u1, SLOT_KV + 10, n_groups=NSA_GROUPS, hpg=NSA_HPG,
                   shared_kv=True, window=True, out_dtype=F32, dv=NSA_DH, tq=WINDOW, tk=WINDOW)

    wk, wv = _layout_mla_kv(w_ukv)
    q_m, k_m, v_m = _mla_prep(u1, pos_col, inv_slot, row(q_norm), row(kv_norm), _layout_mla_q(w_uq).astype(BF16),
                              wk.astype(BF16), wv.astype(BF16), 512)
    o_mla = _flash(q_m, 0, k_m, 0, v_m, 0, n_groups=MLA_HEADS // 4, hpg=4, shared_kv=False, window=False,
                   out_dtype=BF16, dv=MLA_DV, tq=FLASH_TQ, tk=FLASH_TK)

    pooled = _pool(u1, w_pool.astype(BF16), row(pool_scale), 512)
    conv = _conv(u1, conv_w, row(conv_b), row(conv_ln_g), row(conv_ln_b), 512)
    merged = _merge(o_cmp, o_sel, o_win, u1, gexp, pooled, o_mla, conv, um, w_nsa_o.astype(BF16),
                    w_pool_o.astype(BF16), w_mla_o.astype(BF16), w_conv_o.astype(BF16), 256)
    return _matmul_res_ln(merged, w_out.astype(BF16), x, row(ln_g), row(ln_b), 512)


def _cross_attention(x, xb, mem_b, w_q, w_k, w_v, w_o, ln_g, ln_b):
    row = lambda v: v[None, :]
    kv = _matmul(mem_b, jnp.concatenate([w_k, w_v], 1).astype(BF16), jnp.zeros((1, 2 * X_HEADS * X_DH), F32),
                 mem_b.shape[0], 2 * X_HEADS * X_DH, BF16)
    hw = X_HEADS * X_DH
    return _xattn(xb, x, w_q.astype(BF16), kv[:, :hw], kv[:, hw:], w_o.astype(BF16), row(ln_g), row(ln_b), 256)


def _hier_moe(x, xb, w_group, b_group, w_router, b_router, w_e_in, w_e_out, layer, ln_g, ln_b):
    t_len = x.shape[0]
    row = lambda v: v[None, :]
    tm_r = 512
    w_gr = _pad_cols(jnp.concatenate([w_group, w_router], 1), LANE).astype(BF16)
    b_gr = _pad_cols(row(jnp.concatenate([b_group, b_router])), LANE)
    tri = jnp.asarray(np.tril(np.ones((tm_r, tm_r), np.float32)), BF16)
    info, cnt = _router(xb, w_gr, b_gr, tri, tm_r)

    counts = cnt[0, ROUTE_OFF:ROUTE_OFF + N_EXPERTS].astype(I32)
    padded = (counts + MOE_BLOCK - 1) // MOE_BLOCK * MOE_BLOCK
    pend = jnp.cumsum(padded)
    pstart = pend - padded
    n_blocks = -(-(t_len * MOE_TOPK) // MOE_BLOCK) + N_EXPERTS
    n_used = pend[-1] // MOE_BLOCK
    blk_ids = jnp.minimum(jnp.arange(n_blocks), n_used - 1)
    owner = jnp.sum((pend[None, :] <= (blk_ids * MOE_BLOCK)[:, None]).astype(I32), axis=1)
    block_expert = jnp.minimum(owner, N_EXPERTS - 1).astype(I32)
    e_ids = info[:, 0:MOE_TOPK].astype(I32)
    pos = (pstart[e_ids] + info[:, 4:4 + MOE_TOPK].astype(I32)).reshape(-1)

    xe = _dispatch(pos, x, n_blocks * MOE_BLOCK, 512)
    yb = _experts(block_expert, n_used.reshape(1).astype(I32), xe, w_e_in, w_e_out, layer)
    return _combine(pos, yb, x, info, row(ln_g), row(ln_b), 256)


def kernel(x, mem, positions, w_in, b_in, nsa_cmp_pos, nsa_cmp_w1, nsa_cmp_w2, w_nsa_o, w_pool, pool_scale, w_pool_o, mla_q_norm, w_mla_uq, mla_kv_norm, w_mla_ukv, w_mla_o, conv_w, conv_b, conv_ln_g, conv_ln_b, w_conv_o, w_out, ln_mix_g, ln_mix_b, w_xq, w_xk, w_xv, w_xo, ln_x_g, ln_x_b, w_group, b_group, w_router, b_router, w_expert_in, w_expert_out, ln_ffn_g, ln_ffn_b):
    batch, s_len, d = x.shape
    assert batch == 1 and d == D_MODEL and s_len % (2 * FLASH_TK) == 0 and s_len % FLASH_TQ == 0
    x = x[0]
    xb = x.astype(BF16)
    mem_b = mem[0].astype(BF16)
    pos_col = positions[0].astype(F32)[:, None]
    tabs = _static_tables(s_len)
    for l in range(w_in.shape[0]):
        x, xb = _hybrid_mixer(x, xb, pos_col, tabs, w_in[l], b_in[l], nsa_cmp_pos[l], nsa_cmp_w1[l], nsa_cmp_w2[l],
                              w_nsa_o[l], w_pool[l], pool_scale[l], w_pool_o[l], mla_q_norm[l], w_mla_uq[l],
                              mla_kv_norm[l], w_mla_ukv[l], w_mla_o[l], conv_w[l], conv_b[l], conv_ln_g[l],
                              conv_ln_b[l], w_conv_o[l], w_out[l], ln_mix_g[l], ln_mix_b[l])
        x, xb = _cross_attention(x, xb, mem_b, w_xq[l], w_xk[l], w_xv[l], w_xo[l], ln_x_g[l], ln_x_b[l])
        x, xb = _hier_moe(x, xb, w_group[l], b_group[l], w_router[l], b_router[l], w_expert_in,
                          w_expert_out, l, ln_ffn_g[l], ln_ffn_b[l])
    return x[None]
```

```python
import functools

import numpy as np
import jax
import jax.numpy as jnp
from jax import lax
from jax.experimental import pallas as pl
from jax.experimental.pallas import tpu as pltpu

F32 = jnp.float32
BF16 = jnp.bfloat16
I32 = jnp.int32

D_MODEL = 2048
NSA_HEADS = 8
NSA_GROUPS = 2
NSA_HPG = NSA_HEADS // NSA_GROUPS
NSA_DH = 64
CMP_LEN = 32
CMP_STRIDE = 16
CMP_HIDDEN = 128
SEL_LEN = 64
SEL_TOPN = 16
WINDOW = 512
FORCE_SCORE = 1.0e4
POOL_GROUPS = 4
POOL_WINDOWS = (2, 4, 8, 16)
POOL_WIDTH = 512
POOL_GW = POOL_WIDTH // POOL_GROUPS
MLA_HEADS = 8
MLA_Q_RANK = 512
MLA_KV_RANK = 256
MLA_NOPE = 64
MLA_ROPE = 32
MLA_DV = 64
ROPE_BASE = 10000.0
CONV_CH = 512
CONV_K = 31
N_BRANCH = 4
X_HEADS = 4
X_DH = 128
MOE_GROUPS = 4
MOE_EPG = 8
N_EXPERTS = MOE_GROUPS * MOE_EPG
MOE_TOPK = 2
EXPERT_FF = 512
MOE_BLOCK = 256
LN_EPS = 1e-5
RMS_EPS = 1e-6
DEPTH = 2
DN_ALPHA = (2 * DEPTH) ** 0.25
IN_SIZES = (NSA_HEADS * NSA_DH, 6 * NSA_GROUPS * NSA_DH, 3 * NSA_HEADS, POOL_WIDTH,
            MLA_Q_RANK, MLA_KV_RANK, MLA_ROPE, 2 * CONV_CH, N_BRANCH * D_MODEL)

LANE = 128
VMEM_LIMIT = 56 * 1024 * 1024
NEG = -1.0e30
FLASH_TQ = 1024
FLASH_TK = 1024
LOG2E = 1.4426950408889634

SLOT_Q = 0
SLOT_KV = 8
SLOT_POOL = 20
SLOT_CQ = 24
SLOT_CKV = 28
SLOT_KR = 30
SLOT_CONV = 32
SLOT_GATE = 40
N_SLOTS1 = 42
N1 = N_SLOTS1 * LANE


def _cparams(sem, vmem=VMEM_LIMIT):
    return pltpu.CompilerParams(dimension_semantics=sem, vmem_limit_bytes=vmem)


def _sigmoid(x):
    return 0.5 * jnp.tanh(0.5 * x) + 0.5


def _layer_norm(z, g, b):
    mu = jnp.mean(z, -1, keepdims=True)
    d = z - mu
    var = jnp.mean(d * d, -1, keepdims=True)
    return d * lax.rsqrt(var + LN_EPS) * g + b


def _dot(a, b):
    return jnp.dot(a, b, preferred_element_type=F32)


def _dot_nt(a, b):
    return lax.dot_general(a, b, (((1,), (1,)), ((), ())), preferred_element_type=F32)


def _dot3(a, b):
    hi = a.astype(BF16)
    r1 = a - hi.astype(F32)
    mid = r1.astype(BF16)
    lo = (r1 - mid.astype(F32)).astype(BF16)
    return _dot(hi, b) + _dot(mid, b) + _dot(lo, b)


def _mm_kernel(a_ref, b_ref, bias_ref, o_ref):
    o_ref[...] = (_dot(a_ref[...], b_ref[...]) + bias_ref[...]).astype(o_ref.dtype)


def _matmul(a, b, bias, tm, tn, out_dtype):
    m, k = a.shape
    n = b.shape[1]
    return pl.pallas_call(
        _mm_kernel,
        grid=(m // tm, n // tn),
        in_specs=[pl.BlockSpec((tm, k), lambda i, j: (i, 0)),
                  pl.BlockSpec((k, tn), lambda i, j: (0, j)),
                  pl.BlockSpec((1, tn), lambda i, j: (0, j))],
        out_specs=pl.BlockSpec((tm, tn), lambda i, j: (i, j)),
        out_shape=jax.ShapeDtypeStruct((m, n), out_dtype),
        compiler_params=_cparams(("parallel", "arbitrary")),
        name="matmul",
    )(a, b, bias)


def _mm_ln_kernel(a_ref, w_ref, x_ref, g_ref, b_ref, xo_ref, xb_ref):
    h = _dot(a_ref[...], w_ref[...])
    y = _layer_norm(DN_ALPHA * x_ref[...] + h, g_ref[...], b_ref[...])
    xo_ref[...] = y
    xb_ref[...] = y.astype(BF16)


def _matmul_res_ln(a, w, x, g, b, tm):
    m, k = a.shape
    d = w.shape[1]
    return pl.pallas_call(
        _mm_ln_kernel,
        grid=(m // tm,),
        in_specs=[pl.BlockSpec((tm, k), lambda i: (i, 0)),
                  pl.BlockSpec((k, d), lambda i: (0, 0)),
                  pl.BlockSpec((tm, d), lambda i: (i, 0)),
                  pl.BlockSpec((1, d), lambda i: (0, 0)),
                  pl.BlockSpec((1, d), lambda i: (0, 0))],
        out_specs=[pl.BlockSpec((tm, d), lambda i: (i, 0)),
                   pl.BlockSpec((tm, d), lambda i: (i, 0))],
        out_shape=[jax.ShapeDtypeStruct((m, d), F32), jax.ShapeDtypeStruct((m, d), BF16)],
        compiler_params=_cparams(("parallel",)),
        name="matmul_res_ln",
    )(a, w, x, g, b)


def _compress_kernel(a_ref, w1_ref, pos_ref, w2_ref, o_ref):
    nch = a_ref.shape[1]
    hh = _dot(a_ref[0], w1_ref[0])
    pp = _dot(pos_ref[0], w1_ref[0])
    pos_term = pp[0:1, :CMP_HIDDEN] + pp[1:2, CMP_HIDDEN:]
    h2_next = pltpu.roll(hh[:, CMP_HIDDEN:], nch - 1, 0)
    z = hh[:, :CMP_HIDDEN] + h2_next + pos_term
    hid = 0.5 * z * (1.0 + jnp.tanh(0.7978845608028654 * (z + 0.044715 * z * z * z)))
    o_ref[0] = _dot(hid.astype(BF16), w2_ref[0]).astype(o_ref.dtype)


def _nsa_compress(chunks, w1cat, pos2, w2pad):
    n4, nch, kdim = chunks.shape
    return pl.pallas_call(
        _compress_kernel,
        grid=(n4,),
        in_specs=[pl.BlockSpec((1, nch, kdim), lambda c: (c, 0, 0)),
                  pl.BlockSpec((1, kdim, 2 * CMP_HIDDEN), lambda c: (c // NSA_GROUPS, 0, 0)),
                  pl.BlockSpec((1, 8, kdim), lambda c: (c // NSA_GROUPS, 0, 0)),
                  pl.BlockSpec((1, CMP_HIDDEN, LANE), lambda c: (c // NSA_GROUPS, 0, 0))],
        out_specs=pl.BlockSpec((1, nch, LANE), lambda c: (c, 0, 0)),
        out_shape=jax.ShapeDtypeStruct((n4, nch, LANE), BF16),
        compiler_params=_cparams(("arbitrary",)),
        name="nsa_compress",
    )(chunks, w1cat, pos2, w2pad)


CMP_WIDTH_STEPS = 4


def _nsa_cmp_kernel(q_ref, kc_ref, vc_ref, mband_ref, ocmp_ref, selneg_ref, score_sc, *, tq, n_cmp, n_sel, top_n):
    t0 = pl.program_id(0) * tq
    nch = kc_ref.shape[1]
    row = lax.broadcasted_iota(I32, (NSA_HPG * tq, 1), 0)
    t_row = t0 + (row & (tq - 1))

    def softmax_part(width):
        col = lax.broadcasted_iota(I32, (1, width), 1)
        vis = (col * CMP_STRIDE + (CMP_LEN - 1) <= t_row) & (col < n_cmp)
        for g in range(NSA_GROUPS):
            qs = jnp.concatenate(
                [q_ref[:, (g * NSA_HPG + h) * LANE:(g * NSA_HPG + h + 1) * LANE] for h in range(NSA_HPG)], axis=0)
            s = _dot_nt(qs, kc_ref[g, 0:width, :])
            s = jnp.where(vis, s, NEG)
            m = jnp.max(s, -1, keepdims=True)
            p = jnp.where(vis, jnp.exp2(s - m), 0.0)
            l = jnp.sum(p, -1, keepdims=True)
            p = p * (1.0 / jnp.maximum(l, 1e-30))
            o = _dot(p.astype(BF16), vc_ref[g, 0:width, :])
            for h in range(NSA_HPG):
                hh = g * NSA_HPG + h
                ocmp_ref[:, hh * NSA_DH:(hh + 1) * NSA_DH] = o[h * tq:(h + 1) * tq, :NSA_DH]
            imp = p[0:tq]
            for h in range(1, NSA_HPG):
                imp = imp + p[h * tq:(h + 1) * tq]
            score_sc[g * tq:(g + 1) * tq, :] = _dot3(imp, mband_ref[0:width, :])

    last_vis = (t0 + tq - CMP_LEN) // CMP_STRIDE
    step_w = nch // CMP_WIDTH_STEPS
    variant = jnp.minimum(last_vis // step_w, CMP_WIDTH_STEPS - 1)
    for v in range(CMP_WIDTH_STEPS):
        pl.when(variant == v)(functools.partial(softmax_part, (v + 1) * step_w))

    blk = lax.broadcasted_iota(I32, (NSA_GROUPS * tq, n_sel), 1).astype(F32)
    row2 = lax.broadcasted_iota(I32, (NSA_GROUPS * tq, 1), 0)
    cur = ((t0 + (row2 & (tq - 1))) // SEL_LEN).astype(F32)
    valid = blk <= cur
    forced = (blk == 0.0) | (blk == cur) | (blk == cur - 1.0)
    assert FORCE_SCORE > NSA_HPG * (SEL_LEN // CMP_STRIDE + CMP_LEN // CMP_STRIDE - 1) and top_n > 3
    score = jnp.where(forced, -2.0, score_sc[...])
    score = jnp.where(valid, score, -1.0)

    def pick_one(_, sc):
        mx = jnp.max(sc, -1, keepdims=True)
        first = jnp.min(jnp.where(sc == mx, blk, float(n_sel)), -1, keepdims=True)
        return jnp.where(blk == first, -2.0, sc)

    sc = lax.fori_loop(0, top_n - 3, pick_one, score)
    selneg = jnp.where(valid & (sc == -2.0), 0.0, -1.0).astype(selneg_ref.dtype)
    for g in range(NSA_GROUPS):
        selneg_ref[:, g * n_sel:(g + 1) * n_sel] = selneg[g * tq:(g + 1) * tq]


def _nsa_cmp(u1, kcvc, mband, tq):
    s_len = u1.shape[0]
    nch = kcvc.shape[1]
    n_sel = s_len // SEL_LEN
    n_cmp = (s_len - CMP_LEN) // CMP_STRIDE + 1
    kern = functools.partial(_nsa_cmp_kernel, tq=tq, n_cmp=n_cmp, n_sel=n_sel, top_n=min(SEL_TOPN, n_sel))
    return pl.pallas_call(
        kern,
        grid=(s_len // tq,),
        in_specs=[pl.BlockSpec((tq, NSA_HEADS * LANE), lambda i: (i, SLOT_Q // NSA_HEADS)),
                  pl.BlockSpec((NSA_GROUPS, nch, LANE), lambda i: (0, 0, 0)),
                  pl.BlockSpec((NSA_GROUPS, nch, LANE), lambda i: (1, 0, 0)),
                  pl.BlockSpec((nch, n_sel), lambda i: (0, 0))],
        out_specs=[pl.BlockSpec((tq, NSA_HEADS * NSA_DH), lambda i: (i, 0)),
                   pl.BlockSpec((tq, NSA_GROUPS * n_sel), lambda i: (i, 0))],
        out_shape=[jax.ShapeDtypeStruct((s_len, NSA_HEADS * NSA_DH), F32),
                   jax.ShapeDtypeStruct((s_len, NSA_GROUPS * n_sel), BF16)],
        scratch_shapes=[pltpu.VMEM((NSA_GROUPS * tq, n_sel), F32)],
        compiler_params=_cparams(("parallel",)),
        name="nsa_cmp_topk",
    )(u1, kcvc, kcvc, mband)


def _flash_kernel(qi_ref, kj_ref, first_ref, last_ref, mode_ref, off_ref, *refs, hpg, shared_kv, select, modes,
                  dv):
    if select:
        q_ref, k_ref, v_ref, sel_ref, scat_ref, eslot_ref, o_ref, m_sc, acc_sc = refs
    else:
        q_ref, k_ref, v_ref, o_ref, m_sc, acc_sc = refs
    p_idx = pl.program_id(1)
    tq = q_ref.shape[0]
    tk = k_ref.shape[0]

    @pl.when(first_ref[p_idx] == 1)
    def _():
        m_sc[...] = jnp.full(m_sc.shape, NEG, F32)
        acc_sc[...] = jnp.zeros(acc_sc.shape, F32)

    def step(mask_mode):
        k_all = k_ref[...]
        lane = lax.broadcasted_iota(I32, (1, v_ref.shape[1]), 1)
        v_all = v_ref[...] + ((lane & (LANE - 1)) == dv).astype(BF16)
        if select:
            k_all = k_all + eslot_ref[...]
            bias = (_dot(sel_ref[...], scat_ref[kj_ref[p_idx]]) * (-NEG)).astype(BF16)
        if mask_mode:
            r = lax.broadcasted_iota(I32, (tq, tk), 0)
            c = lax.broadcasted_iota(I32, (tq, tk), 1)
            keep = (c <= r + off_ref[p_idx]) if mask_mode == 1 else (c > r)
        for h in range(hpg):
            hk = 0 if shared_kv else h
            q = q_ref[:, h * LANE:(h + 1) * LANE]
            if select:
                q = q + bias
            s = _dot_nt(q, k_all[:, hk * LANE:(hk + 1) * LANE])
            if mask_mode:
                s = jnp.where(keep, s, NEG)
            m_prev = m_sc[h]
            m_new = jnp.maximum(m_prev, jnp.max(s, -1, keepdims=True))
            p = jnp.exp2(s - jnp.concatenate([m_new] * (tk // LANE), axis=1))
            acc_sc[h] = (jnp.exp2(m_prev - m_new) * acc_sc[h]
                         + _dot(p.astype(BF16), v_all[:, hk * LANE:(hk + 1) * LANE]))
            m_sc[h] = m_new

    for mm in modes:
        pl.when(mode_ref[p_idx] == mm)(functools.partial(step, mm))

    @pl.when(last_ref[p_idx] == 1)
    def _():
        for h in range(hpg):
            acc = acc_sc[h]
            o = acc[:, :dv] * (1.0 / acc[:, dv:dv + 1])
            o_ref[:, h * dv:(h + 1) * dv] = o.astype(o_ref.dtype)


def _pair_tables(s_len, tq, tk, window):
    rows = []
    for i in range(s_len // tq):
        if window:
            js = ([i - 1] if i > 0 else []) + [i]
        else:
            js = list(range((i * tq + tq - 1) // tk + 1))
        for j in js:
            if window:
                mode = 1 if j == i else 2
            else:
                mode = 0 if (j + 1) * tk - 1 <= i * tq else 1
            rows.append((i, j, int(j == js[0]), int(j == js[-1]), mode, i * tq - j * tk))
    tab = np.asarray(rows, np.int32)
    return [jnp.asarray(tab[:, c]) for c in range(tab.shape[1])]


def _flash(q_arr, q_blk0, k_arr, k_blk0, v_arr, v_blk0, *, n_groups, hpg, shared_kv, window,
           out_dtype, dv, tq, tk, sel=None, scat=None, eslot=None):
    s_len = q_arr.shape[0]
    assert not window or tk == tq == WINDOW
    nk = 1 if shared_kv else hpg
    select = sel is not None
    tables = _pair_tables(s_len, tq, tk, window)
    n_pairs = int(tables[0].shape[0])
    modes = (1, 2) if window else (0, 1)

    def qmap(g, p, qi, kj, *_):
        return (qi[p], q_blk0 + g)

    def kmap(g, p, qi, kj, *_):
        return (kj[p], k_blk0 + g)

    def vmap_(g, p, qi, kj, *_):
        return (kj[p], v_blk0 + g)

    def omap(g, p, qi, kj, *_):
        return (qi[p], g)

    in_specs = [pl.BlockSpec((tq, hpg * LANE), qmap),
                pl.BlockSpec((tk, nk * LANE), kmap),
                pl.BlockSpec((tk, nk * LANE), vmap_)]
    args = [q_arr, k_arr, v_arr]
    if select:
        n_sel = sel.shape[1] // n_groups
        in_specs += [pl.BlockSpec((tq, n_sel), omap),
                     pl.BlockSpec(scat.shape, lambda g, p, *_: (0, 0, 0)),
                     pl.BlockSpec(eslot.shape, lambda g, p, *_: (0, 0))]
        args += [sel, scat, eslot]
    kern = functools.partial(_flash_kernel, hpg=hpg, shared_kv=shared_kv, select=select, modes=modes, dv=dv)
    grid_spec = pltpu.PrefetchScalarGridSpec(
        num_scalar_prefetch=len(tables),
        grid=(n_groups, n_pairs),
        in_specs=in_specs,
        out_specs=pl.BlockSpec((tq, hpg * dv), omap),
        scratch_shapes=[pltpu.VMEM((hpg, tq, LANE), F32), pltpu.VMEM((hpg, tq, LANE), F32)],
    )
    return pl.pallas_call(
        kern,
        grid_spec=grid_spec,
        out_shape=jax.ShapeDtypeStruct((s_len, n_groups * hpg * dv), out_dtype),
        compiler_params=_cparams(("parallel", "arbitrary")),
        name="flash_sel" if select else ("flash_win" if window else "flash_causal"),
    )(*tables, *args)


def _mla_prep_kernel(cq_ref, ckv_ref, kr_ref, pos_ref, inv_ref, qn_ref, kvn_ref, wuq_ref, wk_ref, wv_ref,
                     q_ref, k_ref, v_ref):
    def rms(x, g):
        return x * lax.rsqrt(jnp.mean(x * x, -1, keepdims=True) + RMS_EPS) * g

    ang = pos_ref[...] * inv_ref[...]
    cos, sin = jnp.cos(ang), jnp.sin(ang)
    hw = MLA_HEADS * LANE
    qh = _dot(rms(cq_ref[...].astype(F32), qn_ref[...]).astype(BF16), wuq_ref[...])
    scale = (MLA_NOPE + MLA_ROPE) ** -0.5 * LOG2E
    cos_t = jnp.concatenate([cos] * MLA_HEADS, axis=1)
    sin_t = jnp.concatenate([sin] * MLA_HEADS, axis=1)
    q_ref[...] = ((qh[:, :hw] * cos_t + qh[:, hw:] * sin_t) * scale).astype(BF16)
    ckv = rms(ckv_ref[...].astype(F32), kvn_ref[...]).astype(BF16)
    kr = kr_ref[...].astype(F32)
    k_rope = kr[:, :LANE] * cos + kr[:, LANE:] * sin
    k_ref[...] = (_dot(ckv, wk_ref[...]) + jnp.concatenate([k_rope] * MLA_HEADS, axis=1)).astype(BF16)
    v_ref[...] = _dot(ckv, wv_ref[...]).astype(BF16)


def _mla_prep(u1, pos_col, inv_slot, qn, kvn, wuq, wk, wv, tm):
    s_len = u1.shape[0]
    hw = MLA_HEADS * LANE
    full = lambda a: pl.BlockSpec(a.shape, lambda i: (0,) * a.ndim)
    out = jax.ShapeDtypeStruct((s_len, hw), BF16)
    return pl.pallas_call(
        _mla_prep_kernel,
        grid=(s_len // tm,),
        in_specs=[pl.BlockSpec((tm, MLA_Q_RANK), lambda i: (i, SLOT_CQ * LANE // MLA_Q_RANK)),
                  pl.BlockSpec((tm, MLA_KV_RANK), lambda i: (i, SLOT_CKV * LANE // MLA_KV_RANK)),
                  pl.BlockSpec((tm, 2 * LANE), lambda i: (i, SLOT_KR // 2)),
                  pl.BlockSpec((tm, 1), lambda i: (i, 0)),
                  full(inv_slot), full(qn), full(kvn), full(wuq), full(wk), full(wv)],
        out_specs=[pl.BlockSpec((tm, hw), lambda i: (i, 0))] * 3,
        out_shape=[out, out, out],
        compiler_params=_cparams(("parallel",)),
        name="mla_prep",
    )(u1, u1, u1, pos_col, inv_slot, qn, kvn, wuq, wk, wv)


POOL_HALO = 16


def _pool_kernel(cur_ref, halo_ref, wp_ref, scale_ref, o_ref):
    i = pl.program_id(0)
    tm = cur_ref.shape[0]
    halo = jnp.where(i > 0, halo_ref[...].astype(F32), 0.0)
    x = jnp.concatenate([halo, cur_ref[...].astype(F32)], axis=0)
    sums = {1: x}
    w = 1
    while w < max(POOL_WINDOWS):
        a = sums[w]
        sums[2 * w] = a[w:] + a[:-w]
        w *= 2
    t = (i * tm + lax.broadcasted_iota(I32, (tm, 1), 0) + 1).astype(F32)
    outs = []
    for gi, w in enumerate(POOL_WINDOWS):
        lo = gi * POOL_GW
        start = POOL_HALO - (w - 1)
        win = sums[w][start:start + tm, lo:lo + POOL_GW]
        mean = win / jnp.minimum(t, float(w))
        pooled = mean - x[POOL_HALO:, lo:lo + POOL_GW]
        outs.append(_dot(pooled.astype(BF16), wp_ref[gi]))
    o_ref[...] = (jnp.concatenate(outs, axis=1) * scale_ref[...]).astype(o_ref.dtype)


def _pool(u1, w_pool, pool_scale, tm):
    s_len = u1.shape[0]
    blk = SLOT_POOL * LANE // POOL_WIDTH
    return pl.pallas_call(
        _pool_kernel,
        grid=(s_len // tm,),
        in_specs=[pl.BlockSpec((tm, POOL_WIDTH), lambda i: (i, blk)),
                  pl.BlockSpec((POOL_HALO, POOL_WIDTH),
                               lambda i: (jnp.maximum(i * (tm // POOL_HALO) - 1, 0), blk)),
                  pl.BlockSpec(w_pool.shape, lambda i: (0, 0, 0)),
                  pl.BlockSpec((1, POOL_WIDTH), lambda i: (0, 0))],
        out_specs=pl.BlockSpec((tm, POOL_WIDTH), lambda i: (i, 0)),
        out_shape=jax.ShapeDtypeStruct((s_len, POOL_WIDTH), BF16),
        compiler_params=_cparams(("parallel",)),
        name="pool",
    )(u1, u1, w_pool, pool_scale)


CONV_HALO = 32


def _conv_kernel(cur_ref, halo_ref, w_ref, b_ref, g_ref, beta_ref, o_ref, hbuf):
    i = pl.program_id(0)
    tm = cur_ref.shape[0]

    def glu(u):
        u = u.astype(F32)
        return u[:, :CONV_CH] * _sigmoid(u[:, CONV_CH:])

    hbuf[0:CONV_HALO, :] = jnp.where(i > 0, glu(halo_ref[...]), 0.0)
    hbuf[CONV_HALO:, :] = glu(cur_ref[...])
    acc = jnp.zeros((tm, CONV_CH), F32) + b_ref[...]
    for k in range(CONV_K):
        off = CONV_HALO - (CONV_K - 1) + k
        acc = acc + hbuf[off:off + tm, :] * w_ref[k:k + 1, :]
    y = _layer_norm(acc, g_ref[...], beta_ref[...])
    o_ref[...] = (y * _sigmoid(y)).astype(o_ref.dtype)


def _conv(u1, conv_w, conv_b, ln_g, ln_b, tm):
    s_len = u1.shape[0]
    blk = SLOT_CONV * LANE // (2 * CONV_CH)
    row = lambda a: pl.BlockSpec(a.shape, lambda i: (0, 0))
    return pl.pallas_call(
        _conv_kernel,
        grid=(s_len // tm,),
        in_specs=[pl.BlockSpec((tm, 2 * CONV_CH), lambda i: (i, blk)),
                  pl.BlockSpec((CONV_HALO, 2 * CONV_CH),
                               lambda i: (jnp.maximum(i * (tm // CONV_HALO) - 1, 0), blk)),
                  row(conv_w), row(conv_b), row(ln_g), row(ln_b)],
        out_specs=pl.BlockSpec((tm, CONV_CH), lambda i: (i, 0)),
        out_shape=jax.ShapeDtypeStruct((s_len, CONV_CH), BF16),
        scratch_shapes=[pltpu.VMEM((tm + CONV_HALO, CONV_CH), F32)],
        compiler_params=_cparams(("parallel",)),
        name="conv_module",
    )(u1, u1, conv_w, conv_b, ln_g, ln_b)


def _merge_kernel(ocmp_ref, osel_ref, owin_ref, gate_ref, gexp_ref, pool_ref, mla_ref, conv_ref, um_ref,
                  wn_ref, wp_ref, wm_ref, wc_ref, o_ref):
    sg = _sigmoid(gate_ref[...].astype(F32))
    nsa = (_dot3(sg, gexp_ref[0]) * ocmp_ref[...] + _dot3(sg, gexp_ref[1]) * osel_ref[...]
           + _dot3(sg, gexp_ref[2]) * owin_ref[...])
    branches = (_dot(nsa.astype(BF16), wn_ref[...]), _dot(pool_ref[...], wp_ref[...]),
                _dot(mla_ref[...], wm_ref[...]), _dot(conv_ref[...], wc_ref[...]))
    merged = None
    for j, br in enumerate(branches):
        term = _sigmoid(um_ref[:, j * D_MODEL:(j + 1) * D_MODEL].astype(F32)) * br
        merged = term if merged is None else merged + term
    o_ref[...] = merged.astype(o_ref.dtype)


def _merge(o_cmp, o_sel, o_win, u1, gexp, pooled, o_mla, conv, um, wn, wp, wm, wc, tm):
    s_len = u1.shape[0]
    tile = lambda w: pl.BlockSpec((tm, w), lambda i: (i, 0))
    full = lambda a: pl.BlockSpec(a.shape, lambda i: (0,) * a.ndim)
    return pl.pallas_call(
        _merge_kernel,
        grid=(s_len // tm,),
        in_specs=[tile(512), tile(512), tile(512),
                  pl.BlockSpec((tm, LANE), lambda i: (i, SLOT_GATE)),
                  full(gexp), tile(512), tile(512), tile(512), tile(N_BRANCH * D_MODEL),
                  full(wn), full(wp), full(wm), full(wc)],
        out_specs=tile(D_MODEL),
        out_shape=jax.ShapeDtypeStruct((s_len, D_MODEL), BF16),
        compiler_params=_cparams(("parallel",)),
        name="branch_merge",
    )(o_cmp, o_sel, o_win, u1, gexp, pooled, o_mla, conv, um, wn, wp, wm, wc)


def _xattn_kernel(xb_ref, x_ref, wq_ref, k_ref, v_ref, wo_ref, g_ref, b_ref, xo_ref, xob_ref):
    q = _dot(xb_ref[...], wq_ref[...]).astype(BF16)
    k = k_ref[...]
    v = v_ref[...]
    outs = []
    for h in range(X_HEADS):
        sl = slice(h * X_DH, (h + 1) * X_DH)
        s = _dot_nt(q[:, sl], k[:, sl]) * (X_DH ** -0.5)
        m = jnp.max(s, -1, keepdims=True)
        p = jnp.exp(s - m)
        p = p * (1.0 / jnp.sum(p, -1, keepdims=True))
        outs.append(_dot(p.astype(BF16), v[:, sl]))
    o = jnp.concatenate(outs, axis=1).astype(BF16)
    y = _layer_norm(DN_ALPHA * x_ref[...] + _dot(o, wo_ref[...]), g_ref[...], b_ref[...])
    xo_ref[...] = y
    xob_ref[...] = y.astype(BF16)


def _xattn(xb, x, wq, k, v, wo, g, b, tm):
    s_len = x.shape[0]
    tile = lambda: pl.BlockSpec((tm, D_MODEL), lambda i: (i, 0))
    full = lambda a: pl.BlockSpec(a.shape, lambda i: (0,) * a.ndim)
    return pl.pallas_call(
        _xattn_kernel,
        grid=(s_len // tm,),
        in_specs=[tile(), tile(), full(wq), full(k), full(v), full(wo), full(g), full(b)],
        out_specs=[tile(), tile()],
        out_shape=[jax.ShapeDtypeStruct((s_len, D_MODEL), F32), jax.ShapeDtypeStruct((s_len, D_MODEL), BF16)],
        compiler_params=_cparams(("parallel",)),
        name="cross_attention_ln",
    )(xb, x, wq, k, v, wo, g, b)


ROUTE_OFF = MOE_GROUPS


def _router_kernel(xb_ref, w_ref, b_ref, tri_ref, info_ref, cnt_ref, carry):
    @pl.when(pl.program_id(0) == 0)
    def _():
        carry[...] = jnp.zeros(carry.shape, F32)

    logits = _dot(xb_ref[...], w_ref[...]) + b_ref[...]
    lane = lax.broadcasted_iota(I32, logits.shape, 1).astype(F32)
    is_g = lane < float(MOE_GROUPS)
    neg_inf = -jnp.inf
    gl = jnp.where(is_g, logits, neg_inf)
    gmax = jnp.max(gl, -1, keepdims=True)
    g_sel = jnp.min(jnp.where(gl == gmax, lane, float(LANE)), -1, keepdims=True)
    g_w = 1.0 / jnp.sum(jnp.where(is_g, jnp.exp(gl - gmax), 0.0), -1, keepdims=True)
    lo = ROUTE_OFF + MOE_EPG * g_sel
    in_g = (lane >= lo) & (lane < lo + MOE_EPG)
    el = jnp.where(in_g, logits, neg_inf)
    emax = jnp.max(el, -1, keepdims=True)
    e = jnp.where(in_g, jnp.exp(el - emax), 0.0)
    p = e / jnp.sum(e, -1, keepdims=True)
    pm = jnp.where(in_g, p, -1.0)
    p1 = jnp.max(pm, -1, keepdims=True)
    i1 = jnp.min(jnp.where(pm == p1, lane, float(LANE)), -1, keepdims=True)
    pm2 = jnp.where(lane == i1, -1.0, pm)
    p2 = jnp.max(pm2, -1, keepdims=True)
    i2 = jnp.min(jnp.where(pm2 == p2, lane, float(LANE)), -1, keepdims=True)
    denom = p1 + p2
    gate1 = g_w * p1 / denom
    gate2 = g_w * p2 / denom

    oh1 = (lane == i1).astype(BF16)
    oh2 = (lane == i2).astype(BF16)
    incl1 = _dot(tri_ref[...], oh1)
    incl2 = _dot(tri_ref[...], oh2)
    tot1 = jnp.sum(oh1.astype(F32), 0, keepdims=True)
    tot2 = jnp.sum(oh2.astype(F32), 0, keepdims=True)
    base = carry[...]
    rank1 = jnp.sum(jnp.where(lane == i1, base + incl1, 0.0), -1, keepdims=True) - 1.0
    rank2 = jnp.sum(jnp.where(lane == i2, base + tot1 + incl2, 0.0), -1, keepdims=True) - 1.0
    carry[...] = base + tot1 + tot2
    cnt_ref[...] = carry[...]

    cols = (i1 - ROUTE_OFF, i2 - ROUTE_OFF, gate1, gate2, rank1, rank2)
    info = jnp.zeros(logits.shape, F32)
    for c, val in enumerate(cols):
        info = jnp.where(lane == float(c), val, info)
    info_ref[...] = info


def _router(xb, w_gr, b_gr, tri, tm):
    t_len = xb.shape[0]
    return pl.pallas_call(
        _router_kernel,
        grid=(t_len // tm,),
        in_specs=[pl.BlockSpec((tm, D_MODEL), lambda i: (i, 0)),
                  pl.BlockSpec(w_gr.shape, lambda i: (0, 0)),
                  pl.BlockSpec((1, LANE), lambda i: (0, 0)),
                  pl.BlockSpec((tm, tm), lambda i: (0, 0))],
        out_specs=[pl.BlockSpec((tm, LANE), lambda i: (i, 0)),
                   pl.BlockSpec((1, LANE), lambda i: (0, 0))],
        out_shape=[jax.ShapeDtypeStruct((t_len, LANE), F32), jax.ShapeDtypeStruct((1, LANE), F32)],
        scratch_shapes=[pltpu.VMEM((1, LANE), F32)],
        compiler_params=_cparams(("arbitrary",)),
        name="moe_router",
    )(xb, w_gr, b_gr, tri)


def _row_copy(src, src_row, dst, dst_row, sem):
    return pltpu.make_async_copy(src.at[pl.ds(src_row, 1)], dst.at[pl.ds(dst_row, 1)], sem)


def _dispatch_kernel(pos_ref, x_ref, xe_in_hbm, xe_hbm, sem, *, td):
    del xe_in_hbm
    base = pl.program_id(0) * td

    def issue(t, c):
        for k in range(MOE_TOPK):
            _row_copy(x_ref, t, xe_hbm, pos_ref[MOE_TOPK * (base + t) + k], sem).start()
        return c

    lax.fori_loop(0, td, issue, 0, unroll=8)
    all_rows = xe_hbm.at[pl.ds(0, MOE_TOPK * td)]
    pltpu.make_async_copy(all_rows, all_rows, sem).wait()


def _dispatch(pos_flat, x, rows, td):
    t_len, d = x.shape
    zeros = jnp.zeros((rows, d), x.dtype)
    grid_spec = pltpu.PrefetchScalarGridSpec(
        num_scalar_prefetch=1,
        grid=(t_len // td,),
        in_specs=[pl.BlockSpec((td, d), lambda i, pos: (i, 0)), pl.BlockSpec(memory_space=pl.ANY)],
        out_specs=pl.BlockSpec(memory_space=pl.ANY),
        scratch_shapes=[pltpu.SemaphoreType.DMA(())],
    )
    return pl.pallas_call(
        functools.partial(_dispatch_kernel, td=td),
        grid_spec=grid_spec,
        out_shape=jax.ShapeDtypeStruct((rows, d), x.dtype),
        input_output_aliases={2: 0},
        compiler_params=pltpu.CompilerParams(dimension_semantics=("arbitrary",)),
        name="moe_dispatch",
    )(pos_flat, x, zeros)


def _expert_weight_copies(win_hbm, wout_hbm, layer, expert, win_f32, wout_f32, sem):
    return (pltpu.make_async_copy(win_hbm.at[layer, expert], win_f32, sem.at[0]),
            pltpu.make_async_copy(wout_hbm.at[layer, expert], wout_f32, sem.at[1]))


def _expert_kernel(be_ref, next_ref, nused_ref, xe_ref, win_hbm, wout_hbm, yb_ref, win_f32, wout_f32, win_bf,
                   wout_bf, sem, *, layer):
    b = pl.program_id(0)
    used = b < nused_ref[0]
    new_expert = (b == 0) | (be_ref[b] != be_ref[jnp.maximum(b - 1, 0)])
    copies = functools.partial(_expert_weight_copies, win_hbm, wout_hbm, layer)

    @pl.when(b == 0)
    def _():
        for c in copies(be_ref[0], win_f32, wout_f32, sem):
            c.start()

    @pl.when(used & new_expert)
    def _():
        for c in copies(be_ref[b], win_f32, wout_f32, sem):
            c.wait()
        win_bf[...] = win_f32[...].astype(BF16)
        wout_bf[...] = wout_f32[...].astype(BF16)

        @pl.when(next_ref[b] >= 0)
        def _():
            for c in copies(next_ref[b], win_f32, wout_f32, sem):
                c.start()

    @pl.when(used)
    def _():
        hcat = _dot(xe_ref[...].astype(BF16), win_bf[...])
        a = hcat[:, :EXPERT_FF]
        act = (a * _sigmoid(a) * hcat[:, EXPERT_FF:]).astype(BF16)
        yb_ref[...] = _dot(act, wout_bf[...])

    @pl.when(b >= nused_ref[0])
    def _():
        yb_ref[...] = jnp.zeros(yb_ref.shape, F32)


def _experts(block_expert, next_expert, n_used, xe, w_e_in, w_e_out, layer):
    rows, d = xe.shape
    n_blocks = rows // MOE_BLOCK
    grid_spec = pltpu.PrefetchScalarGridSpec(
        num_scalar_prefetch=3,
        grid=(n_blocks,),
        in_specs=[pl.BlockSpec((MOE_BLOCK, d), lambda b, *_: (b, 0)),
                  pl.BlockSpec(memory_space=pl.ANY),
                  pl.BlockSpec(memory_space=pl.ANY)],
        out_specs=pl.BlockSpec((MOE_BLOCK, d), lambda b, *_: (b, 0)),
        scratch_shapes=[pltpu.VMEM((d, 2 * EXPERT_FF), F32), pltpu.VMEM((EXPERT_FF, d), F32),
                        pltpu.VMEM((d, 2 * EXPERT_FF), BF16), pltpu.VMEM((EXPERT_FF, d), BF16),
                        pltpu.SemaphoreType.DMA((2,))],
    )
    return pl.pallas_call(
        functools.partial(_expert_kernel, layer=layer),
        grid_spec=grid_spec,
        out_shape=jax.ShapeDtypeStruct((rows, d), F32),
        compiler_params=_cparams(("arbitrary",)),
        name="moe_experts",
    )(block_expert, next_expert, n_used, xe, w_e_in, w_e_out)


def _combine_kernel(pos_ref, yb_hbm, x_ref, info_ref, g_ref, b_ref, xo_ref, xob_ref, buf, sem):
    tm = x_ref.shape[0]
    i = pl.program_id(0)
    slot = i & 1

    def gather(tile, slot_):
        def issue(t, c):
            for k in range(MOE_TOPK):
                _row_copy(yb_hbm, pos_ref[MOE_TOPK * (tile * tm + t) + k], buf.at[slot_, k], t,
                          sem.at[slot_]).start()
            return c
        lax.fori_loop(0, tm, issue, 0, unroll=8)

    @pl.when(i == 0)
    def _():
        gather(0, 0)

    @pl.when(i + 1 < pl.num_programs(0))
    def _():
        gather(i + 1, 1 - slot)

    pltpu.make_async_copy(buf.at[slot], buf.at[slot], sem.at[slot]).wait()
    info = info_ref[...]
    y = info[:, 2:3] * buf[slot, 0] + info[:, 3:4] * buf[slot, 1]
    z = _layer_norm(DN_ALPHA * x_ref[...] + y, g_ref[...], b_ref[...])
    xo_ref[...] = z
    xob_ref[...] = z.astype(BF16)


def _combine(pos_flat, yb, x, info, g, b, tm):
    t_len, d = x.shape
    grid_spec = pltpu.PrefetchScalarGridSpec(
        num_scalar_prefetch=1,
        grid=(t_len // tm,),
        in_specs=[pl.BlockSpec(memory_space=pl.ANY),
                  pl.BlockSpec((tm, d), lambda i, pos: (i, 0)),
                  pl.BlockSpec((tm, LANE), lambda i, pos: (i, 0)),
                  pl.BlockSpec((1, d), lambda i, pos: (0, 0)),
                  pl.BlockSpec((1, d), lambda i, pos: (0, 0))],
        out_specs=[pl.BlockSpec((tm, d), lambda i, pos: (i, 0)),
                   pl.BlockSpec((tm, d), lambda i, pos: (i, 0))],
        scratch_shapes=[pltpu.VMEM((2, MOE_TOPK, tm, d), F32), pltpu.SemaphoreType.DMA((2,))],
    )
    return pl.pallas_call(
        _combine_kernel,
        grid_spec=grid_spec,
        out_shape=[jax.ShapeDtypeStruct((t_len, d), F32), jax.ShapeDtypeStruct((t_len, d), BF16)],
        compiler_params=_cparams(("arbitrary",)),
        name="moe_combine_ln",
    )(pos_flat, yb, x, info, g, b)


def _pad_cols(m, width):
    return jnp.pad(m, [(0, 0)] * (m.ndim - 1) + [(0, width - m.shape[-1])])


def _rot_half_cols(m):
    half = m.shape[-1] // 2
    return jnp.concatenate([-m[..., half:], m[..., :half]], -1)


def _layout_in1(m):
    offs = np.cumsum((0,) + IN_SIZES)
    o_q, o_kv, o_g, o_pool, o_cq, o_ckv, o_kr, o_conv = offs[:8]
    z = lambda n: jnp.zeros(m.shape[:-1] + (n,), m.dtype)
    parts = []
    for h in range(NSA_HEADS):
        parts += [m[..., o_q + h * NSA_DH:o_q + (h + 1) * NSA_DH] * (NSA_DH ** -0.5 * LOG2E), z(LANE - NSA_DH)]
    for c in range(6 * NSA_GROUPS):
        parts += [m[..., o_kv + c * NSA_DH:o_kv + (c + 1) * NSA_DH], z(LANE - NSA_DH)]
    parts += [m[..., o_pool:o_pool + POOL_WIDTH], m[..., o_cq:o_cq + MLA_Q_RANK], m[..., o_ckv:o_ckv + MLA_KV_RANK]]
    kr = m[..., o_kr:o_kr + MLA_ROPE]
    parts += [z(MLA_NOPE), kr, z(LANE - MLA_NOPE - MLA_ROPE), z(MLA_NOPE), _rot_half_cols(kr),
              z(LANE - MLA_NOPE - MLA_ROPE)]
    parts += [m[..., o_conv:o_conv + 2 * CONV_CH], _pad_cols(m[..., o_g:o_g + 3 * NSA_HEADS], LANE), z(LANE)]
    return jnp.concatenate(parts, -1)


def _layout_mla_q(w):
    dq = MLA_NOPE + MLA_ROPE
    z = lambda n: jnp.zeros((w.shape[0], n), w.dtype)
    a, b = [], []
    for h in range(MLA_HEADS):
        rope = w[:, h * dq + MLA_NOPE:(h + 1) * dq]
        a += [w[:, h * dq:h * dq + MLA_NOPE], rope, z(LANE - dq)]
        b += [z(MLA_NOPE), _rot_half_cols(rope), z(LANE - dq)]
    return jnp.concatenate(a + b, -1)


def _layout_mla_kv(w):
    dkv = MLA_NOPE + MLA_DV
    z = jnp.zeros((w.shape[0], LANE - MLA_NOPE), w.dtype)
    k, v = [], []
    for h in range(MLA_HEADS):
        k += [w[:, h * dkv:h * dkv + MLA_NOPE], z]
        v += [w[:, h * dkv + MLA_NOPE:(h + 1) * dkv], z]
    return jnp.concatenate(k, -1), jnp.concatenate(v, -1)


def _static_tables(s_len):
    nch = s_len // CMP_STRIDE
    n_cmp = (s_len - CMP_LEN) // CMP_STRIDE + 1
    n_sel = s_len // SEL_LEN
    ratio = SEL_LEN // CMP_STRIDE
    c = np.arange(nch)[:, None]
    j = np.arange(n_sel)[None, :]
    mband = ((c >= ratio * j - 1) & (c <= ratio * j + ratio - 1) & (c < n_cmp)).astype(np.float32)
    per_tile = FLASH_TK // SEL_LEN
    n_tiles = s_len // FLASH_TK
    scat = np.zeros((n_tiles, n_sel, LANE), np.float32)
    for b in range(n_sel):
        scat[b // per_tile, b, NSA_DH + b % per_tile] = 1.0
    eslot = np.zeros((FLASH_TK, LANE), np.float32)
    eslot[np.arange(FLASH_TK), NSA_DH + np.arange(FLASH_TK) // SEL_LEN] = 1.0
    gexp = np.zeros((3, LANE, NSA_HEADS * NSA_DH), np.float32)
    for h in range(NSA_HEADS):
        for jj in range(3):
            gexp[jj, h * 3 + jj, h * NSA_DH:(h + 1) * NSA_DH] = 1.0
    half = MLA_ROPE // 2
    inv = ROPE_BASE ** (-jnp.arange(half, dtype=F32) / half)
    inv_slot = jnp.concatenate([jnp.zeros((MLA_NOPE,), F32), inv, inv,
                                jnp.zeros((LANE - MLA_NOPE - MLA_ROPE,), F32)])[None, :]
    as_bf = lambda a: jnp.asarray(a, BF16)
    return as_bf(mband), as_bf(scat), as_bf(eslot), as_bf(gexp), inv_slot


def _hybrid_mixer(x, xb, pos_col, tabs, w_in, b_in, cmp_pos, cmp_w1, cmp_w2, w_nsa_o, w_pool, pool_scale, w_pool_o,
                  q_norm, w_uq, kv_norm, w_ukv, w_mla_o, conv_w, conv_b, conv_ln_g, conv_ln_b, w_conv_o, w_out,
                  ln_g, ln_b):
    s_len = x.shape[0]
    mband, scat, eslot, gexp, inv_slot = tabs
    row = lambda v: v[None, :]
    o_merge = int(sum(IN_SIZES[:8]))
    tm_proj = min(2048, s_len)
    u1 = _matmul(xb, _layout_in1(w_in[:, :o_merge]).astype(BF16), _layout_in1(row(b_in[:o_merge])),
                 tm_proj, 7 * LANE, BF16)
    um = _matmul(xb, w_in[:, o_merge:].astype(BF16), row(b_in[o_merge:]), tm_proj, 1024, BF16)

    nch = s_len // CMP_STRIDE
    kdim = CMP_STRIDE * LANE
    chunks = u1[:, SLOT_KV * LANE:(SLOT_KV + 4) * LANE].reshape(nch, CMP_STRIDE, 4, LANE)
    chunks = chunks.transpose(2, 0, 1, 3).reshape(4, nch, kdim)
    w1 = _pad_cols(cmp_w1.reshape(2, CMP_LEN, NSA_DH, CMP_HIDDEN).transpose(0, 1, 3, 2), LANE)
    w1 = w1.transpose(0, 1, 3, 2)
    w1cat = jnp.concatenate([w1[:, :CMP_STRIDE].reshape(2, kdim, CMP_HIDDEN),
                             w1[:, CMP_STRIDE:].reshape(2, kdim, CMP_HIDDEN)], -1).astype(BF16)
    posp = _pad_cols(cmp_pos, LANE)
    pos2 = jnp.stack([posp[:, :CMP_STRIDE].reshape(2, kdim), posp[:, CMP_STRIDE:].reshape(2, kdim)], 1)
    pos2 = jnp.pad(pos2, ((0, 0), (0, 6), (0, 0))).astype(BF16)
    kcvc = _nsa_compress(chunks, w1cat, pos2, _pad_cols(cmp_w2, LANE).astype(BF16))
    o_cmp, selneg = _nsa_cmp(u1, kcvc, mband, 256)
    o_sel = _flash(u1, SLOT_Q // NSA_HPG, u1, SLOT_KV + 4, u1, SLOT_KV + 6, n_groups=NSA_GROUPS, hpg=NSA_HPG,
                   shared_kv=True, window=False, out_dtype=F32, dv=NSA_DH, tq=FLASH_TQ, tk=FLASH_TK, sel=selneg,
                   scat=scat, eslot=eslot)
    o_win = _flash(u1, SLOT_Q // NSA_HPG, u1, SLOT_KV + 8, u1, SLOT_KV + 10, n_groups=NSA_GROUPS, hpg=NSA_HPG,
                   shared_kv=True, window=True, out_dtype=F32, dv=NSA_DH, tq=WINDOW, tk=WINDOW)

    wk, wv = _layout_mla_kv(w_ukv)
    q_m, k_m, v_m = _mla_prep(u1, pos_col, inv_slot, row(q_norm), row(kv_norm), _layout_mla_q(w_uq).astype(BF16),
                              wk.astype(BF16), wv.astype(BF16), 512)
    o_mla = _flash(q_m, 0, k_m, 0, v_m, 0, n_groups=MLA_HEADS // 4, hpg=4, shared_kv=False, window=False,
                   out_dtype=BF16, dv=MLA_DV, tq=FLASH_TQ, tk=FLASH_TK)

    pooled = _pool(u1, w_pool.astype(BF16), row(pool_scale), 512)
    conv = _conv(u1, conv_w, row(conv_b), row(conv_ln_g), row(conv_ln_b), 512)
    merged = _merge(o_cmp, o_sel, o_win, u1, gexp, pooled, o_mla, conv, um, w_nsa_o.astype(BF16),
                    w_pool_o.astype(BF16), w_mla_o.astype(BF16), w_conv_o.astype(BF16), 256)
    return _matmul_res_ln(merged, w_out.astype(BF16), x, row(ln_g), row(ln_b), 512)


def _cross_attention(x, xb, mem_b, w_q, w_k, w_v, w_o, ln_g, ln_b):
    row = lambda v: v[None, :]
    kv = _matmul(mem_b, jnp.concatenate([w_k, w_v], 1).astype(BF16), jnp.zeros((1, 2 * X_HEADS * X_DH), F32),
                 mem_b.shape[0], 2 * X_HEADS * X_DH, BF16)
    hw = X_HEADS * X_DH
    return _xattn(xb, x, w_q.astype(BF16), kv[:, :hw], kv[:, hw:], w_o.astype(BF16), row(ln_g), row(ln_b), 512)


def _hier_moe(x, xb, w_group, b_group, w_router, b_router, w_e_in, w_e_out, layer, ln_g, ln_b):
    t_len = x.shape[0]
    row = lambda v: v[None, :]
    tm_r = 512
    w_gr = _pad_cols(jnp.concatenate([w_group, w_router], 1), LANE).astype(BF16)
    b_gr = _pad_cols(row(jnp.concatenate([b_group, b_router])), LANE)
    tri = jnp.asarray(np.tril(np.ones((tm_r, tm_r), np.float32)), BF16)
    info, cnt = _router(xb, w_gr, b_gr, tri, tm_r)

    counts = cnt[0, ROUTE_OFF:ROUTE_OFF + N_EXPERTS].astype(I32)
    padded = (counts + MOE_BLOCK - 1) // MOE_BLOCK * MOE_BLOCK
    pend = jnp.cumsum(padded)
    pstart = pend - padded
    n_blocks = -(-(t_len * MOE_TOPK) // MOE_BLOCK) + N_EXPERTS
    n_used = pend[-1] // MOE_BLOCK
    blk_ids = jnp.minimum(jnp.arange(n_blocks), n_used - 1)
    owner = jnp.sum((pend[None, :] <= (blk_ids * MOE_BLOCK)[:, None]).astype(I32), axis=1)
    block_expert = jnp.minimum(owner, N_EXPERTS - 1).astype(I32)
    e_ids = info[:, 0:MOE_TOPK].astype(I32)
    pos = (pstart[e_ids] + info[:, 4:4 + MOE_TOPK].astype(I32)).reshape(-1)

    xe = _dispatch(pos, x, n_blocks * MOE_BLOCK, 512)
    blk = jnp.arange(n_blocks, dtype=I32)
    change = (blk < n_used) & ((blk == 0) | (block_expert != jnp.roll(block_expert, 1)))
    first_change_from = jnp.flip(lax.cummin(jnp.flip(jnp.where(change, blk, n_blocks))))
    nxt = jnp.concatenate([first_change_from[1:], jnp.full((1,), n_blocks, I32)])
    next_expert = jnp.where(nxt < n_blocks, block_expert[jnp.minimum(nxt, n_blocks - 1)], -1).astype(I32)
    yb = _experts(block_expert, next_expert, n_used.reshape(1).astype(I32), xe, w_e_in, w_e_out, layer)
    return _combine(pos, yb, x, info, row(ln_g), row(ln_b), 256)


def kernel(x, mem, positions, w_in, b_in, nsa_cmp_pos, nsa_cmp_w1, nsa_cmp_w2, w_nsa_o, w_pool, pool_scale, w_pool_o, mla_q_norm, w_mla_uq, mla_kv_norm, w_mla_ukv, w_mla_o, conv_w, conv_b, conv_ln_g, conv_ln_b, w_conv_o, w_out, ln_mix_g, ln_mix_b, w_xq, w_xk, w_xv, w_xo, ln_x_g, ln_x_b, w_group, b_group, w_router, b_router, w_expert_in, w_expert_out, ln_ffn_g, ln_ffn_b):
    batch, s_len, d = x.shape
    assert batch == 1 and d == D_MODEL and s_len % (2 * FLASH_TK) == 0 and s_len % FLASH_TQ == 0
    x = x[0]
    xb = x.astype(BF16)
    mem_b = mem[0].astype(BF16)
    pos_col = positions[0].astype(F32)[:, None]
    tabs = _static_tables(s_len)
    for l in range(w_in.shape[0]):
        x, xb = _hybrid_mixer(x, xb, pos_col, tabs, w_in[l], b_in[l], nsa_cmp_pos[l], nsa_cmp_w1[l], nsa_cmp_w2[l],
                              w_nsa_o[l], w_pool[l], pool_scale[l], w_pool_o[l], mla_q_norm[l], w_mla_uq[l],
                              mla_kv_norm[l], w_mla_ukv[l], w_mla_o[l], conv_w[l], conv_b[l], conv_ln_g[l],
                              conv_ln_b[l], w_conv_o[l], w_out[l], ln_mix_g[l], ln_mix_b[l])
        x, xb = _cross_attention(x, xb, mem_b, w_xq[l], w_xk[l], w_xv[l], w_xo[l], ln_x_g[l], ln_x_b[l])
        x, xb = _hier_moe(x, xb, w_group[l], b_group[l], w_router[l], b_router[l], w_expert_in,
                          w_expert_out, l, ln_ffn_g[l], ln_ffn_b[l])
    return x[None]
```

```python
import functools

import numpy as np
import jax
import jax.numpy as jnp
from jax import lax
from jax.experimental import pallas as pl
from jax.experimental.pallas import tpu as pltpu

F32 = jnp.float32
BF16 = jnp.bfloat16
I32 = jnp.int32

D_MODEL = 2048
NSA_HEADS = 8
NSA_GROUPS = 2
NSA_HPG = NSA_HEADS // NSA_GROUPS
NSA_DH = 64
CMP_LEN = 32
CMP_STRIDE = 16
CMP_HIDDEN = 128
SEL_LEN = 64
SEL_TOPN = 16
WINDOW = 512
FORCE_SCORE = 1.0e4
POOL_GROUPS = 4
POOL_WINDOWS = (2, 4, 8, 16)
POOL_WIDTH = 512
POOL_GW = POOL_WIDTH // POOL_GROUPS
MLA_HEADS = 8
MLA_Q_RANK = 512
MLA_KV_RANK = 256
MLA_NOPE = 64
MLA_ROPE = 32
MLA_DV = 64
ROPE_BASE = 10000.0
CONV_CH = 512
CONV_K = 31
N_BRANCH = 4
X_HEADS = 4
X_DH = 128
MOE_GROUPS = 4
MOE_EPG = 8
N_EXPERTS = MOE_GROUPS * MOE_EPG
MOE_TOPK = 2
EXPERT_FF = 512
MOE_BLOCK = 256
LN_EPS = 1e-5
RMS_EPS = 1e-6
DEPTH = 2
DN_ALPHA = (2 * DEPTH) ** 0.25
IN_SIZES = (NSA_HEADS * NSA_DH, 6 * NSA_GROUPS * NSA_DH, 3 * NSA_HEADS, POOL_WIDTH,
            MLA_Q_RANK, MLA_KV_RANK, MLA_ROPE, 2 * CONV_CH, N_BRANCH * D_MODEL)

LANE = 128
VMEM_LIMIT = 56 * 1024 * 1024
NEG = -1.0e30
FLASH_TQ = 1024
FLASH_TK = 1024
LOG2E = 1.4426950408889634

SLOT_Q = 0
SLOT_KV = 8
SLOT_POOL = 20
SLOT_CQ = 24
SLOT_CKV = 28
SLOT_KR = 30
SLOT_CONV = 32
SLOT_GATE = 40
N_SLOTS1 = 42
N1 = N_SLOTS1 * LANE


def _cparams(sem, vmem=VMEM_LIMIT):
    return pltpu.CompilerParams(dimension_semantics=sem, vmem_limit_bytes=vmem)


def _sigmoid(x):
    return 0.5 * jnp.tanh(0.5 * x) + 0.5


def _layer_norm(z, g, b):
    mu = jnp.mean(z, -1, keepdims=True)
    d = z - mu
    var = jnp.mean(d * d, -1, keepdims=True)
    return d * lax.rsqrt(var + LN_EPS) * g + b


U32 = jnp.uint32


def _pack_rows(y):
    n = y.shape[1] // 2
    bits = lax.bitcast_convert_type(y.astype(BF16).astype(F32), U32)
    return (bits[:, n:] & jnp.uint32(0xFFFF0000)) | (bits[:, :n] >> 16)


def _unpack_rows(w):
    lo = lax.bitcast_convert_type(w << 16, F32)
    hi = lax.bitcast_convert_type(w & jnp.uint32(0xFFFF0000), F32)
    return jnp.concatenate([lo, hi], axis=1)


def _dot(a, b):
    return jnp.dot(a, b, preferred_element_type=F32)


def _dot_nt(a, b):
    return lax.dot_general(a, b, (((1,), (1,)), ((), ())), preferred_element_type=F32)


def _dot3(a, b):
    hi = a.astype(BF16)
    r1 = a - hi.astype(F32)
    mid = r1.astype(BF16)
    lo = (r1 - mid.astype(F32)).astype(BF16)
    return _dot(hi, b) + _dot(mid, b) + _dot(lo, b)


def _mm_kernel(a_ref, b_ref, bias_ref, o_ref):
    o_ref[...] = (_dot(a_ref[...], b_ref[...]) + bias_ref[...]).astype(o_ref.dtype)


def _matmul(a, b, bias, tm, tn, out_dtype):
    m, k = a.shape
    n = b.shape[1]
    return pl.pallas_call(
        _mm_kernel,
        grid=(m // tm, n // tn),
        in_specs=[pl.BlockSpec((tm, k), lambda i, j: (i, 0)),
                  pl.BlockSpec((k, tn), lambda i, j: (0, j)),
                  pl.BlockSpec((1, tn), lambda i, j: (0, j))],
        out_specs=pl.BlockSpec((tm, tn), lambda i, j: (i, j)),
        out_shape=jax.ShapeDtypeStruct((m, n), out_dtype),
        compiler_params=_cparams(("parallel", "arbitrary")),
        name="matmul",
    )(a, b, bias)


def _mm_ln_kernel(a_ref, w_ref, x_ref, g_ref, b_ref, xo_ref, xb_ref):
    h = _dot(a_ref[...], w_ref[...])
    y = _layer_norm(DN_ALPHA * x_ref[...] + h, g_ref[...], b_ref[...])
    xo_ref[...] = y
    xb_ref[...] = y.astype(BF16)


def _matmul_res_ln(a, w, x, g, b, tm):
    m, k = a.shape
    d = w.shape[1]
    return pl.pallas_call(
        _mm_ln_kernel,
        grid=(m // tm,),
        in_specs=[pl.BlockSpec((tm, k), lambda i: (i, 0)),
                  pl.BlockSpec((k, d), lambda i: (0, 0)),
                  pl.BlockSpec((tm, d), lambda i: (i, 0)),
                  pl.BlockSpec((1, d), lambda i: (0, 0)),
                  pl.BlockSpec((1, d), lambda i: (0, 0))],
        out_specs=[pl.BlockSpec((tm, d), lambda i: (i, 0)),
                   pl.BlockSpec((tm, d), lambda i: (i, 0))],
        out_shape=[jax.ShapeDtypeStruct((m, d), F32), jax.ShapeDtypeStruct((m, d), BF16)],
        compiler_params=_cparams(("parallel",)),
        name="matmul_res_ln",
    )(a, w, x, g, b)


def _compress_kernel(a_ref, w1_ref, pos_ref, w2_ref, o_ref):
    nch = a_ref.shape[1]
    hh = _dot(a_ref[0], w1_ref[0])
    pp = _dot(pos_ref[0], w1_ref[0])
    pos_term = pp[0:1, :CMP_HIDDEN] + pp[1:2, CMP_HIDDEN:]
    h2_next = pltpu.roll(hh[:, CMP_HIDDEN:], nch - 1, 0)
    z = hh[:, :CMP_HIDDEN] + h2_next + pos_term
    hid = 0.5 * z * (1.0 + jnp.tanh(0.7978845608028654 * (z + 0.044715 * z * z * z)))
    o_ref[0] = _dot(hid.astype(BF16), w2_ref[0]).astype(o_ref.dtype)


def _nsa_compress(chunks, w1cat, pos2, w2pad):
    n4, nch, kdim = chunks.shape
    return pl.pallas_call(
        _compress_kernel,
        grid=(n4,),
        in_specs=[pl.BlockSpec((1, nch, kdim), lambda c: (c, 0, 0)),
                  pl.BlockSpec((1, kdim, 2 * CMP_HIDDEN), lambda c: (c // NSA_GROUPS, 0, 0)),
                  pl.BlockSpec((1, 8, kdim), lambda c: (c // NSA_GROUPS, 0, 0)),
                  pl.BlockSpec((1, CMP_HIDDEN, LANE), lambda c: (c // NSA_GROUPS, 0, 0))],
        out_specs=pl.BlockSpec((1, nch, LANE), lambda c: (c, 0, 0)),
        out_shape=jax.ShapeDtypeStruct((n4, nch, LANE), BF16),
        compiler_params=_cparams(("arbitrary",)),
        name="nsa_compress",
    )(chunks, w1cat, pos2, w2pad)


CMP_WIDTH_STEPS = 4


def _nsa_cmp_kernel(q_ref, kc_ref, vc_ref, mband_ref, ocmp_ref, selneg_ref, score_sc, *, tq, n_cmp, n_sel, top_n):
    t0 = pl.program_id(0) * tq
    nch = kc_ref.shape[1]
    row = lax.broadcasted_iota(I32, (NSA_HPG * tq, 1), 0)
    t_row = t0 + (row & (tq - 1))

    def softmax_part(width):
        col = lax.broadcasted_iota(I32, (1, width), 1)
        vis = (col * CMP_STRIDE + (CMP_LEN - 1) <= t_row) & (col < n_cmp)
        for g in range(NSA_GROUPS):
            qs = jnp.concatenate(
                [q_ref[:, (g * NSA_HPG + h) * LANE:(g * NSA_HPG + h + 1) * LANE] for h in range(NSA_HPG)], axis=0)
            s = _dot_nt(qs, kc_ref[g, 0:width, :])
            s = jnp.where(vis, s, NEG)
            m = jnp.max(s, -1, keepdims=True)
            p = jnp.where(vis, jnp.exp2(s - m), 0.0)
            l = jnp.sum(p, -1, keepdims=True)
            p = p * (1.0 / jnp.maximum(l, 1e-30))
            o = _dot(p.astype(BF16), vc_ref[g, 0:width, :])
            for h in range(NSA_HPG):
                hh = g * NSA_HPG + h
                ocmp_ref[:, hh * NSA_DH:(hh + 1) * NSA_DH] = o[h * tq:(h + 1) * tq, :NSA_DH]
            imp = p[0:tq]
            for h in range(1, NSA_HPG):
                imp = imp + p[h * tq:(h + 1) * tq]
            score_sc[g * tq:(g + 1) * tq, :] = _dot3(imp, mband_ref[0:width, :])

    last_vis = (t0 + tq - CMP_LEN) // CMP_STRIDE
    step_w = nch // CMP_WIDTH_STEPS
    variant = jnp.minimum(last_vis // step_w, CMP_WIDTH_STEPS - 1)
    for v in range(CMP_WIDTH_STEPS):
        pl.when(variant == v)(functools.partial(softmax_part, (v + 1) * step_w))

    blk = lax.broadcasted_iota(I32, (NSA_GROUPS * tq, n_sel), 1).astype(F32)
    row2 = lax.broadcasted_iota(I32, (NSA_GROUPS * tq, 1), 0)
    cur = ((t0 + (row2 & (tq - 1))) // SEL_LEN).astype(F32)
    valid = blk <= cur
    forced = (blk == 0.0) | (blk == cur) | (blk == cur - 1.0)
    assert FORCE_SCORE > NSA_HPG * (SEL_LEN // CMP_STRIDE + CMP_LEN // CMP_STRIDE - 1) and top_n > 3
    score = jnp.where(forced, -2.0, score_sc[...])
    score = jnp.where(valid, score, -1.0)

    def pick_one(_, sc):
        mx = jnp.max(sc, -1, keepdims=True)
        first = jnp.min(jnp.where(sc == mx, blk, float(n_sel)), -1, keepdims=True)
        return jnp.where(blk == first, -2.0, sc)

    sc = lax.fori_loop(0, top_n - 3, pick_one, score)
    selneg = jnp.where(valid & (sc == -2.0), 0.0, -1.0).astype(selneg_ref.dtype)
    for g in range(NSA_GROUPS):
        selneg_ref[:, g * n_sel:(g + 1) * n_sel] = selneg[g * tq:(g + 1) * tq]


def _nsa_cmp(u1, kcvc, mband, tq):
    s_len = u1.shape[0]
    nch = kcvc.shape[1]
    n_sel = s_len // SEL_LEN
    n_cmp = (s_len - CMP_LEN) // CMP_STRIDE + 1
    kern = functools.partial(_nsa_cmp_kernel, tq=tq, n_cmp=n_cmp, n_sel=n_sel, top_n=min(SEL_TOPN, n_sel))
    return pl.pallas_call(
        kern,
        grid=(s_len // tq,),
        in_specs=[pl.BlockSpec((tq, NSA_HEADS * LANE), lambda i: (i, SLOT_Q // NSA_HEADS)),
                  pl.BlockSpec((NSA_GROUPS, nch, LANE), lambda i: (0, 0, 0)),
                  pl.BlockSpec((NSA_GROUPS, nch, LANE), lambda i: (1, 0, 0)),
                  pl.BlockSpec((nch, n_sel), lambda i: (0, 0))],
        out_specs=[pl.BlockSpec((tq, NSA_HEADS * NSA_DH), lambda i: (i, 0)),
                   pl.BlockSpec((tq, NSA_GROUPS * n_sel), lambda i: (i, 0))],
        out_shape=[jax.ShapeDtypeStruct((s_len, NSA_HEADS * NSA_DH), F32),
                   jax.ShapeDtypeStruct((s_len, NSA_GROUPS * n_sel), BF16)],
        scratch_shapes=[pltpu.VMEM((NSA_GROUPS * tq, n_sel), F32)],
        compiler_params=_cparams(("parallel",)),
        name="nsa_cmp_topk",
    )(u1, kcvc, kcvc, mband)


def _flash_kernel(qi_ref, kj_ref, first_ref, last_ref, mode_ref, off_ref, *refs, hpg, shared_kv, select, modes,
                  dv):
    if select:
        q_ref, k_ref, v_ref, sel_ref, scat_ref, eslot_ref, o_ref, m_sc, acc_sc = refs
    else:
        q_ref, k_ref, v_ref, o_ref, m_sc, acc_sc = refs
    p_idx = pl.program_id(1)
    tq = q_ref.shape[0]
    tk = k_ref.shape[0]

    @pl.when(first_ref[p_idx] == 1)
    def _():
        m_sc[...] = jnp.full(m_sc.shape, NEG, F32)
        acc_sc[...] = jnp.zeros(acc_sc.shape, F32)

    def step(mask_mode):
        k_all = k_ref[...]
        lane = lax.broadcasted_iota(I32, (1, v_ref.shape[1]), 1)
        v_all = v_ref[...] + ((lane & (LANE - 1)) == dv).astype(BF16)
        if select:
            k_all = k_all + eslot_ref[...]
            bias = (_dot(sel_ref[...], scat_ref[kj_ref[p_idx]]) * (-NEG)).astype(BF16)
        if mask_mode:
            r = lax.broadcasted_iota(I32, (tq, tk), 0)
            c = lax.broadcasted_iota(I32, (tq, tk), 1)
            keep = (c <= r + off_ref[p_idx]) if mask_mode == 1 else (c > r)
        for h in range(hpg):
            hk = 0 if shared_kv else h
            q = q_ref[:, h * LANE:(h + 1) * LANE]
            if select:
                q = q + bias
            s = _dot_nt(q, k_all[:, hk * LANE:(hk + 1) * LANE])
            if mask_mode:
                s = jnp.where(keep, s, NEG)
            m_prev = m_sc[h]
            m_new = jnp.maximum(m_prev, jnp.max(s, -1, keepdims=True))
            p = jnp.exp2(s - jnp.concatenate([m_new] * (tk // LANE), axis=1))
            acc_sc[h] = (jnp.exp2(m_prev - m_new) * acc_sc[h]
                         + _dot(p.astype(BF16), v_all[:, hk * LANE:(hk + 1) * LANE]))
            m_sc[h] = m_new

    for mm in modes:
        pl.when(mode_ref[p_idx] == mm)(functools.partial(step, mm))

    @pl.when(last_ref[p_idx] == 1)
    def _():
        for h in range(hpg):
            acc = acc_sc[h]
            o = acc[:, :dv] * (1.0 / acc[:, dv:dv + 1])
            o_ref[:, h * dv:(h + 1) * dv] = o.astype(o_ref.dtype)


def _pair_tables(s_len, tq, tk, window):
    rows = []
    for i in range(s_len // tq):
        if window:
            js = ([i - 1] if i > 0 else []) + [i]
        else:
            js = list(range((i * tq + tq - 1) // tk + 1))
        for j in js:
            if window:
                mode = 1 if j == i else 2
            else:
                mode = 0 if (j + 1) * tk - 1 <= i * tq else 1
            rows.append((i, j, int(j == js[0]), int(j == js[-1]), mode, i * tq - j * tk))
    tab = np.asarray(rows, np.int32)
    return [jnp.asarray(tab[:, c]) for c in range(tab.shape[1])]


def _flash(q_arr, q_blk0, k_arr, k_blk0, v_arr, v_blk0, *, n_groups, hpg, shared_kv, window,
           out_dtype, dv, tq, tk, sel=None, scat=None, eslot=None):
    s_len = q_arr.shape[0]
    assert not window or tk == tq == WINDOW
    nk = 1 if shared_kv else hpg
    select = sel is not None
    tables = _pair_tables(s_len, tq, tk, window)
    n_pairs = int(tables[0].shape[0])
    modes = (1, 2) if window else (0, 1)

    def qmap(g, p, qi, kj, *_):
        return (qi[p], q_blk0 + g)

    def kmap(g, p, qi, kj, *_):
        return (kj[p], k_blk0 + g)

    def vmap_(g, p, qi, kj, *_):
        return (kj[p], v_blk0 + g)

    def omap(g, p, qi, kj, *_):
        return (qi[p], g)

    in_specs = [pl.BlockSpec((tq, hpg * LANE), qmap),
                pl.BlockSpec((tk, nk * LANE), kmap),
                pl.BlockSpec((tk, nk * LANE), vmap_)]
    args = [q_arr, k_arr, v_arr]
    if select:
        n_sel = sel.shape[1] // n_groups
        in_specs += [pl.BlockSpec((tq, n_sel), omap),
                     pl.BlockSpec(scat.shape, lambda g, p, *_: (0, 0, 0)),
                     pl.BlockSpec(eslot.shape, lambda g, p, *_: (0, 0))]
        args += [sel, scat, eslot]
    kern = functools.partial(_flash_kernel, hpg=hpg, shared_kv=shared_kv, select=select, modes=modes, dv=dv)
    grid_spec = pltpu.PrefetchScalarGridSpec(
        num_scalar_prefetch=len(tables),
        grid=(n_groups, n_pairs),
        in_specs=in_specs,
        out_specs=pl.BlockSpec((tq, hpg * dv), omap),
        scratch_shapes=[pltpu.VMEM((hpg, tq, LANE), F32), pltpu.VMEM((hpg, tq, LANE), F32)],
    )
    return pl.pallas_call(
        kern,
        grid_spec=grid_spec,
        out_shape=jax.ShapeDtypeStruct((s_len, n_groups * hpg * dv), out_dtype),
        compiler_params=_cparams(("parallel", "arbitrary")),
        name="flash_sel" if select else ("flash_win" if window else "flash_causal"),
    )(*tables, *args)


def _mla_prep_kernel(cq_ref, ckv_ref, kr_ref, pos_ref, inv_ref, qn_ref, kvn_ref, wuq_ref, wk_ref, wv_ref,
                     q_ref, k_ref, v_ref):
    def rms(x, g):
        return x * lax.rsqrt(jnp.mean(x * x, -1, keepdims=True) + RMS_EPS) * g

    ang = pos_ref[...] * inv_ref[...]
    cos, sin = jnp.cos(ang), jnp.sin(ang)
    hw = MLA_HEADS * LANE
    qh = _dot(rms(cq_ref[...].astype(F32), qn_ref[...]).astype(BF16), wuq_ref[...])
    scale = (MLA_NOPE + MLA_ROPE) ** -0.5 * LOG2E
    cos_t = jnp.concatenate([cos] * MLA_HEADS, axis=1)
    sin_t = jnp.concatenate([sin] * MLA_HEADS, axis=1)
    q_ref[...] = ((qh[:, :hw] * cos_t + qh[:, hw:] * sin_t) * scale).astype(BF16)
    ckv = rms(ckv_ref[...].astype(F32), kvn_ref[...]).astype(BF16)
    kr = kr_ref[...].astype(F32)
    k_rope = kr[:, :LANE] * cos + kr[:, LANE:] * sin
    k_ref[...] = (_dot(ckv, wk_ref[...]) + jnp.concatenate([k_rope] * MLA_HEADS, axis=1)).astype(BF16)
    v_ref[...] = _dot(ckv, wv_ref[...]).astype(BF16)


def _mla_prep(u1, pos_col, inv_slot, qn, kvn, wuq, wk, wv, tm):
    s_len = u1.shape[0]
    hw = MLA_HEADS * LANE
    full = lambda a: pl.BlockSpec(a.shape, lambda i: (0,) * a.ndim)
    out = jax.ShapeDtypeStruct((s_len, hw), BF16)
    return pl.pallas_call(
        _mla_prep_kernel,
        grid=(s_len // tm,),
        in_specs=[pl.BlockSpec((tm, MLA_Q_RANK), lambda i: (i, SLOT_CQ * LANE // MLA_Q_RANK)),
                  pl.BlockSpec((tm, MLA_KV_RANK), lambda i: (i, SLOT_CKV * LANE // MLA_KV_RANK)),
                  pl.BlockSpec((tm, 2 * LANE), lambda i: (i, SLOT_KR // 2)),
                  pl.BlockSpec((tm, 1), lambda i: (i, 0)),
                  full(inv_slot), full(qn), full(kvn), full(wuq), full(wk), full(wv)],
        out_specs=[pl.BlockSpec((tm, hw), lambda i: (i, 0))] * 3,
        out_shape=[out, out, out],
        compiler_params=_cparams(("parallel",)),
        name="mla_prep",
    )(u1, u1, u1, pos_col, inv_slot, qn, kvn, wuq, wk, wv)


POOL_HALO = 16


def _pool_kernel(cur_ref, halo_ref, wp_ref, scale_ref, o_ref):
    i = pl.program_id(0)
    tm = cur_ref.shape[0]
    halo = jnp.where(i > 0, halo_ref[...].astype(F32), 0.0)
    x = jnp.concatenate([halo, cur_ref[...].astype(F32)], axis=0)
    sums = {1: x}
    w = 1
    while w < max(POOL_WINDOWS):
        a = sums[w]
        sums[2 * w] = a[w:] + a[:-w]
        w *= 2
    t = (i * tm + lax.broadcasted_iota(I32, (tm, 1), 0) + 1).astype(F32)
    outs = []
    for gi, w in enumerate(POOL_WINDOWS):
        lo = gi * POOL_GW
        start = POOL_HALO - (w - 1)
        win = sums[w][start:start + tm, lo:lo + POOL_GW]
        mean = win / jnp.minimum(t, float(w))
        pooled = mean - x[POOL_HALO:, lo:lo + POOL_GW]
        outs.append(_dot(pooled.astype(BF16), wp_ref[gi]))
    o_ref[...] = (jnp.concatenate(outs, axis=1) * scale_ref[...]).astype(o_ref.dtype)


def _pool(u1, w_pool, pool_scale, tm):
    s_len = u1.shape[0]
    blk = SLOT_POOL * LANE // POOL_WIDTH
    return pl.pallas_call(
        _pool_kernel,
        grid=(s_len // tm,),
        in_specs=[pl.BlockSpec((tm, POOL_WIDTH), lambda i: (i, blk)),
                  pl.BlockSpec((POOL_HALO, POOL_WIDTH),
                               lambda i: (jnp.maximum(i * (tm // POOL_HALO) - 1, 0), blk)),
                  pl.BlockSpec(w_pool.shape, lambda i: (0, 0, 0)),
                  pl.BlockSpec((1, POOL_WIDTH), lambda i: (0, 0))],
        out_specs=pl.BlockSpec((tm, POOL_WIDTH), lambda i: (i, 0)),
        out_shape=jax.ShapeDtypeStruct((s_len, POOL_WIDTH), BF16),
        compiler_params=_cparams(("parallel",)),
        name="pool",
    )(u1, u1, w_pool, pool_scale)


CONV_HALO = 32


def _conv_kernel(cur_ref, halo_ref, w_ref, b_ref, g_ref, beta_ref, o_ref, hbuf):
    i = pl.program_id(0)
    tm = cur_ref.shape[0]

    def glu(u):
        u = u.astype(F32)
        return u[:, :CONV_CH] * _sigmoid(u[:, CONV_CH:])

    hbuf[0:CONV_HALO, :] = jnp.where(i > 0, glu(halo_ref[...]), 0.0)
    hbuf[CONV_HALO:, :] = glu(cur_ref[...])
    acc = jnp.zeros((tm, CONV_CH), F32) + b_ref[...]
    for k in range(CONV_K):
        off = CONV_HALO - (CONV_K - 1) + k
        acc = acc + hbuf[off:off + tm, :] * w_ref[k:k + 1, :]
    y = _layer_norm(acc, g_ref[...], beta_ref[...])
    o_ref[...] = (y * _sigmoid(y)).astype(o_ref.dtype)


def _conv(u1, conv_w, conv_b, ln_g, ln_b, tm):
    s_len = u1.shape[0]
    blk = SLOT_CONV * LANE // (2 * CONV_CH)
    row = lambda a: pl.BlockSpec(a.shape, lambda i: (0, 0))
    return pl.pallas_call(
        _conv_kernel,
        grid=(s_len // tm,),
        in_specs=[pl.BlockSpec((tm, 2 * CONV_CH), lambda i: (i, blk)),
                  pl.BlockSpec((CONV_HALO, 2 * CONV_CH),
                               lambda i: (jnp.maximum(i * (tm // CONV_HALO) - 1, 0), blk)),
                  row(conv_w), row(conv_b), row(ln_g), row(ln_b)],
        out_specs=pl.BlockSpec((tm, CONV_CH), lambda i: (i, 0)),
        out_shape=jax.ShapeDtypeStruct((s_len, CONV_CH), BF16),
        scratch_shapes=[pltpu.VMEM((tm + CONV_HALO, CONV_CH), F32)],
        compiler_params=_cparams(("parallel",)),
        name="conv_module",
    )(u1, u1, conv_w, conv_b, ln_g, ln_b)


def _merge_kernel(ocmp_ref, osel_ref, owin_ref, gate_ref, gexp_ref, pool_ref, mla_ref, conv_ref, um_ref,
                  wn_ref, wp_ref, wm_ref, wc_ref, o_ref):
    sg = _sigmoid(gate_ref[...].astype(F32))
    nsa = (_dot3(sg, gexp_ref[0]) * ocmp_ref[...] + _dot3(sg, gexp_ref[1]) * osel_ref[...]
           + _dot3(sg, gexp_ref[2]) * owin_ref[...])
    branches = (_dot(nsa.astype(BF16), wn_ref[...]), _dot(pool_ref[...], wp_ref[...]),
                _dot(mla_ref[...], wm_ref[...]), _dot(conv_ref[...], wc_ref[...]))
    merged = None
    for j, br in enumerate(branches):
        term = _sigmoid(um_ref[:, j * D_MODEL:(j + 1) * D_MODEL].astype(F32)) * br
        merged = term if merged is None else merged + term
    o_ref[...] = merged.astype(o_ref.dtype)


def _merge(o_cmp, o_sel, o_win, u1, gexp, pooled, o_mla, conv, um, wn, wp, wm, wc, tm):
    s_len = u1.shape[0]
    tile = lambda w: pl.BlockSpec((tm, w), lambda i: (i, 0))
    full = lambda a: pl.BlockSpec(a.shape, lambda i: (0,) * a.ndim)
    return pl.pallas_call(
        _merge_kernel,
        grid=(s_len // tm,),
        in_specs=[tile(512), tile(512), tile(512),
                  pl.BlockSpec((tm, LANE), lambda i: (i, SLOT_GATE)),
                  full(gexp), tile(512), tile(512), tile(512), tile(N_BRANCH * D_MODEL),
                  full(wn), full(wp), full(wm), full(wc)],
        out_specs=tile(D_MODEL),
        out_shape=jax.ShapeDtypeStruct((s_len, D_MODEL), BF16),
        compiler_params=_cparams(("parallel",)),
        name="branch_merge",
    )(o_cmp, o_sel, o_win, u1, gexp, pooled, o_mla, conv, um, wn, wp, wm, wc)


def _xattn_kernel(xb_ref, x_ref, wq_ref, k_ref, v_ref, wo_ref, g_ref, b_ref, xo_ref, xob_ref, xp_ref):
    q = _dot(xb_ref[...], wq_ref[...]).astype(BF16)
    k = k_ref[...]
    v = v_ref[...]
    outs = []
    for h in range(X_HEADS):
        sl = slice(h * X_DH, (h + 1) * X_DH)
        s = _dot_nt(q[:, sl], k[:, sl]) * (X_DH ** -0.5)
        m = jnp.max(s, -1, keepdims=True)
        p = jnp.exp(s - m)
        p = p * (1.0 / jnp.sum(p, -1, keepdims=True))
        outs.append(_dot(p.astype(BF16), v[:, sl]))
    o = jnp.concatenate(outs, axis=1).astype(BF16)
    y = _layer_norm(DN_ALPHA * x_ref[...] + _dot(o, wo_ref[...]), g_ref[...], b_ref[...])
    xo_ref[...] = y
    xob_ref[...] = y.astype(BF16)
    xp_ref[...] = _pack_rows(y)


def _xattn(xb, x, wq, k, v, wo, g, b, tm):
    s_len = x.shape[0]
    tile = lambda: pl.BlockSpec((tm, D_MODEL), lambda i: (i, 0))
    full = lambda a: pl.BlockSpec(a.shape, lambda i: (0,) * a.ndim)
    return pl.pallas_call(
        _xattn_kernel,
        grid=(s_len // tm,),
        in_specs=[tile(), tile(), full(wq), full(k), full(v), full(wo), full(g), full(b)],
        out_specs=[tile(), tile(), pl.BlockSpec((tm, D_MODEL // 2), lambda i: (i, 0))],
        out_shape=[jax.ShapeDtypeStruct((s_len, D_MODEL), F32), jax.ShapeDtypeStruct((s_len, D_MODEL), BF16),
                   jax.ShapeDtypeStruct((s_len, D_MODEL // 2), U32)],
        compiler_params=_cparams(("parallel",)),
        name="cross_attention_ln",
    )(xb, x, wq, k, v, wo, g, b)


ROUTE_OFF = MOE_GROUPS


def _router_kernel(xb_ref, w_ref, b_ref, tri_ref, info_ref, cnt_ref, carry):
    @pl.when(pl.program_id(0) == 0)
    def _():
        carry[...] = jnp.zeros(carry.shape, F32)

    logits = _dot(xb_ref[...], w_ref[...]) + b_ref[...]
    lane = lax.broadcasted_iota(I32, logits.shape, 1).astype(F32)
    is_g = lane < float(MOE_GROUPS)
    neg_inf = -jnp.inf
    gl = jnp.where(is_g, logits, neg_inf)
    gmax = jnp.max(gl, -1, keepdims=True)
    g_sel = jnp.min(jnp.where(gl == gmax, lane, float(LANE)), -1, keepdims=True)
    g_w = 1.0 / jnp.sum(jnp.where(is_g, jnp.exp(gl - gmax), 0.0), -1, keepdims=True)
    lo = ROUTE_OFF + MOE_EPG * g_sel
    in_g = (lane >= lo) & (lane < lo + MOE_EPG)
    el = jnp.where(in_g, logits, neg_inf)
    emax = jnp.max(el, -1, keepdims=True)
    e = jnp.where(in_g, jnp.exp(el - emax), 0.0)
    p = e / jnp.sum(e, -1, keepdims=True)
    pm = jnp.where(in_g, p, -1.0)
    p1 = jnp.max(pm, -1, keepdims=True)
    i1 = jnp.min(jnp.where(pm == p1, lane, float(LANE)), -1, keepdims=True)
    pm2 = jnp.where(lane == i1, -1.0, pm)
    p2 = jnp.max(pm2, -1, keepdims=True)
    i2 = jnp.min(jnp.where(pm2 == p2, lane, float(LANE)), -1, keepdims=True)
    denom = p1 + p2
    gate1 = g_w * p1 / denom
    gate2 = g_w * p2 / denom

    oh1 = (lane == i1).astype(BF16)
    oh2 = (lane == i2).astype(BF16)
    incl1 = _dot(tri_ref[...], oh1)
    incl2 = _dot(tri_ref[...], oh2)
    tot1 = jnp.sum(oh1.astype(F32), 0, keepdims=True)
    tot2 = jnp.sum(oh2.astype(F32), 0, keepdims=True)
    base = carry[...]
    rank1 = jnp.sum(jnp.where(lane == i1, base + incl1, 0.0), -1, keepdims=True) - 1.0
    rank2 = jnp.sum(jnp.where(lane == i2, base + tot1 + incl2, 0.0), -1, keepdims=True) - 1.0
    carry[...] = base + tot1 + tot2
    cnt_ref[...] = carry[...]

    cols = (i1 - ROUTE_OFF, i2 - ROUTE_OFF, gate1, gate2, rank1, rank2)
    info = jnp.zeros(logits.shape, F32)
    for c, val in enumerate(cols):
        info = jnp.where(lane == float(c), val, info)
    info_ref[...] = info


def _router(xb, w_gr, b_gr, tri, tm):
    t_len = xb.shape[0]
    return pl.pallas_call(
        _router_kernel,
        grid=(t_len // tm,),
        in_specs=[pl.BlockSpec((tm, D_MODEL), lambda i: (i, 0)),
                  pl.BlockSpec(w_gr.shape, lambda i: (0, 0)),
                  pl.BlockSpec((1, LANE), lambda i: (0, 0)),
                  pl.BlockSpec((tm, tm), lambda i: (0, 0))],
        out_specs=[pl.BlockSpec((tm, LANE), lambda i: (i, 0)),
                   pl.BlockSpec((1, LANE), lambda i: (0, 0))],
        out_shape=[jax.ShapeDtypeStruct((t_len, LANE), F32), jax.ShapeDtypeStruct((1, LANE), F32)],
        scratch_shapes=[pltpu.VMEM((1, LANE), F32)],
        compiler_params=_cparams(("arbitrary",)),
        name="moe_router",
    )(xb, w_gr, b_gr, tri)


def _row_copy(src, src_row, dst, dst_row, sem):
    return pltpu.make_async_copy(src.at[pl.ds(src_row, 1)], dst.at[pl.ds(dst_row, 1)], sem)


def _dispatch_kernel(pos_ref, x_ref, xe_in_hbm, xe_hbm, sem, *, td):
    del xe_in_hbm
    base = pl.program_id(0) * td

    def issue(t, c):
        for k in range(MOE_TOPK):
            _row_copy(x_ref, t, xe_hbm, pos_ref[MOE_TOPK * (base + t) + k], sem).start()
        return c

    lax.fori_loop(0, td, issue, 0, unroll=8)
    all_rows = xe_hbm.at[pl.ds(0, MOE_TOPK * td)]
    pltpu.make_async_copy(all_rows, all_rows, sem).wait()


def _dispatch(pos_flat, x, rows, td):
    t_len, d = x.shape
    zeros = jnp.zeros((rows, d), x.dtype)
    grid_spec = pltpu.PrefetchScalarGridSpec(
        num_scalar_prefetch=1,
        grid=(t_len // td,),
        in_specs=[pl.BlockSpec((td, d), lambda i, pos: (i, 0)), pl.BlockSpec(memory_space=pl.ANY)],
        out_specs=pl.BlockSpec(memory_space=pl.ANY),
        scratch_shapes=[pltpu.SemaphoreType.DMA(())],
    )
    return pl.pallas_call(
        functools.partial(_dispatch_kernel, td=td),
        grid_spec=grid_spec,
        out_shape=jax.ShapeDtypeStruct((rows, d), x.dtype),
        input_output_aliases={2: 0},
        compiler_params=pltpu.CompilerParams(dimension_semantics=("arbitrary",)),
        name="moe_dispatch",
    )(pos_flat, x, zeros)


def _expert_weight_copies(win_hbm, wout_hbm, layer, expert, win_f32, wout_f32, sem):
    return (pltpu.make_async_copy(win_hbm.at[layer, expert], win_f32, sem.at[0]),
            pltpu.make_async_copy(wout_hbm.at[layer, expert], wout_f32, sem.at[1]))


def _expert_kernel(be_ref, next_ref, nused_ref, xe_ref, win_hbm, wout_hbm, yb_ref, win_f32, wout_f32, win_bf,
                   wout_bf, sem, *, layer):
    b = pl.program_id(0)
    used = b < nused_ref[0]
    new_expert = (b == 0) | (be_ref[b] != be_ref[jnp.maximum(b - 1, 0)])
    copies = functools.partial(_expert_weight_copies, win_hbm, wout_hbm, layer)

    @pl.when(b == 0)
    def _():
        for c in copies(be_ref[0], win_f32, wout_f32, sem):
            c.start()

    @pl.when(used & new_expert)
    def _():
        for c in copies(be_ref[b], win_f32, wout_f32, sem):
            c.wait()
        win_bf[...] = win_f32[...].astype(BF16)
        wout_bf[...] = wout_f32[...].astype(BF16)

        @pl.when(next_ref[b] >= 0)
        def _():
            for c in copies(next_ref[b], win_f32, wout_f32, sem):
                c.start()

    @pl.when(used)
    def _():
        hcat = _dot(_unpack_rows(xe_ref[...]).astype(BF16), win_bf[...])
        a = hcat[:, :EXPERT_FF]
        act = (a * _sigmoid(a) * hcat[:, EXPERT_FF:]).astype(BF16)
        yb_ref[...] = _pack_rows(_dot(act, wout_bf[...]))

    @pl.when(b >= nused_ref[0])
    def _():
        yb_ref[...] = jnp.zeros(yb_ref.shape, U32)


def _experts(block_expert, next_expert, n_used, xe, w_e_in, w_e_out, layer):
    rows, half = xe.shape
    d = 2 * half
    n_blocks = rows // MOE_BLOCK
    grid_spec = pltpu.PrefetchScalarGridSpec(
        num_scalar_prefetch=3,
        grid=(n_blocks,),
        in_specs=[pl.BlockSpec((MOE_BLOCK, half), lambda b, *_: (b, 0)),
                  pl.BlockSpec(memory_space=pl.ANY),
                  pl.BlockSpec(memory_space=pl.ANY)],
        out_specs=pl.BlockSpec((MOE_BLOCK, half), lambda b, *_: (b, 0)),
        scratch_shapes=[pltpu.VMEM((d, 2 * EXPERT_FF), F32), pltpu.VMEM((EXPERT_FF, d), F32),
                        pltpu.VMEM((d, 2 * EXPERT_FF), BF16), pltpu.VMEM((EXPERT_FF, d), BF16),
                        pltpu.SemaphoreType.DMA((2,))],
    )
    return pl.pallas_call(
        functools.partial(_expert_kernel, layer=layer),
        grid_spec=grid_spec,
        out_shape=jax.ShapeDtypeStruct((rows, half), U32),
        compiler_params=_cparams(("arbitrary",)),
        name="moe_experts",
    )(block_expert, next_expert, n_used, xe, w_e_in, w_e_out)


def _combine_kernel(pos_ref, yb_hbm, x_ref, info_ref, g_ref, b_ref, xo_ref, xob_ref, buf, sem):
    tm = x_ref.shape[0]
    i = pl.program_id(0)
    slot = i & 1

    def gather(tile, slot_):
        def issue(t, c):
            for k in range(MOE_TOPK):
                _row_copy(yb_hbm, pos_ref[MOE_TOPK * (tile * tm + t) + k], buf.at[slot_, k], t,
                          sem.at[slot_]).start()
            return c
        lax.fori_loop(0, tm, issue, 0, unroll=8)

    @pl.when(i == 0)
    def _():
        gather(0, 0)

    @pl.when(i + 1 < pl.num_programs(0))
    def _():
        gather(i + 1, 1 - slot)

    pltpu.make_async_copy(buf.at[slot], buf.at[slot], sem.at[slot]).wait()
    info = info_ref[...]
    y = info[:, 2:3] * _unpack_rows(buf[slot, 0]) + info[:, 3:4] * _unpack_rows(buf[slot, 1])
    z = _layer_norm(DN_ALPHA * x_ref[...] + y, g_ref[...], b_ref[...])
    xo_ref[...] = z
    xob_ref[...] = z.astype(BF16)


def _combine(pos_flat, yb, x, info, g, b, tm):
    t_len, d = x.shape
    grid_spec = pltpu.PrefetchScalarGridSpec(
        num_scalar_prefetch=1,
        grid=(t_len // tm,),
        in_specs=[pl.BlockSpec(memory_space=pl.ANY),
                  pl.BlockSpec((tm, d), lambda i, pos: (i, 0)),
                  pl.BlockSpec((tm, LANE), lambda i, pos: (i, 0)),
                  pl.BlockSpec((1, d), lambda i, pos: (0, 0)),
                  pl.BlockSpec((1, d), lambda i, pos: (0, 0))],
        out_specs=[pl.BlockSpec((tm, d), lambda i, pos: (i, 0)),
                   pl.BlockSpec((tm, d), lambda i, pos: (i, 0))],
        scratch_shapes=[pltpu.VMEM((2, MOE_TOPK, tm, d // 2), U32), pltpu.SemaphoreType.DMA((2,))],
    )
    return pl.pallas_call(
        _combine_kernel,
        grid_spec=grid_spec,
        out_shape=[jax.ShapeDtypeStruct((t_len, d), F32), jax.ShapeDtypeStruct((t_len, d), BF16)],
        compiler_params=_cparams(("arbitrary",)),
        name="moe_combine_ln",
    )(pos_flat, yb, x, info, g, b)


def _pad_cols(m, width):
    return jnp.pad(m, [(0, 0)] * (m.ndim - 1) + [(0, width - m.shape[-1])])


def _rot_half_cols(m):
    half = m.shape[-1] // 2
    return jnp.concatenate([-m[..., half:], m[..., :half]], -1)


def _layout_in1(m):
    offs = np.cumsum((0,) + IN_SIZES)
    o_q, o_kv, o_g, o_pool, o_cq, o_ckv, o_kr, o_conv = offs[:8]
    z = lambda n: jnp.zeros(m.shape[:-1] + (n,), m.dtype)
    parts = []
    for h in range(NSA_HEADS):
        parts += [m[..., o_q + h * NSA_DH:o_q + (h + 1) * NSA_DH] * (NSA_DH ** -0.5 * LOG2E), z(LANE - NSA_DH)]
    for c in range(6 * NSA_GROUPS):
        parts += [m[..., o_kv + c * NSA_DH:o_kv + (c + 1) * NSA_DH], z(LANE - NSA_DH)]
    parts += [m[..., o_pool:o_pool + POOL_WIDTH], m[..., o_cq:o_cq + MLA_Q_RANK], m[..., o_ckv:o_ckv + MLA_KV_RANK]]
    kr = m[..., o_kr:o_kr + MLA_ROPE]
    parts += [z(MLA_NOPE), kr, z(LANE - MLA_NOPE - MLA_ROPE), z(MLA_NOPE), _rot_half_cols(kr),
              z(LANE - MLA_NOPE - MLA_ROPE)]
    parts += [m[..., o_conv:o_conv + 2 * CONV_CH], _pad_cols(m[..., o_g:o_g + 3 * NSA_HEADS], LANE), z(LANE)]
    return jnp.concatenate(parts, -1)


def _layout_mla_q(w):
    dq = MLA_NOPE + MLA_ROPE
    z = lambda n: jnp.zeros((w.shape[0], n), w.dtype)
    a, b = [], []
    for h in range(MLA_HEADS):
        rope = w[:, h * dq + MLA_NOPE:(h + 1) * dq]
        a += [w[:, h * dq:h * dq + MLA_NOPE], rope, z(LANE - dq)]
        b += [z(MLA_NOPE), _rot_half_cols(rope), z(LANE - dq)]
    return jnp.concatenate(a + b, -1)


def _layout_mla_kv(w):
    dkv = MLA_NOPE + MLA_DV
    z = jnp.zeros((w.shape[0], LANE - MLA_NOPE), w.dtype)
    k, v = [], []
    for h in range(MLA_HEADS):
        k += [w[:, h * dkv:h * dkv + MLA_NOPE], z]
        v += [w[:, h * dkv + MLA_NOPE:(h + 1) * dkv], z]
    return jnp.concatenate(k, -1), jnp.concatenate(v, -1)


def _static_tables(s_len):
    nch = s_len // CMP_STRIDE
    n_cmp = (s_len - CMP_LEN) // CMP_STRIDE + 1
    n_sel = s_len // SEL_LEN
    ratio = SEL_LEN // CMP_STRIDE
    c = np.arange(nch)[:, None]
    j = np.arange(n_sel)[None, :]
    mband = ((c >= ratio * j - 1) & (c <= ratio * j + ratio - 1) & (c < n_cmp)).astype(np.float32)
    per_tile = FLASH_TK // SEL_LEN
    n_tiles = s_len // FLASH_TK
    scat = np.zeros((n_tiles, n_sel, LANE), np.float32)
    for b in range(n_sel):
        scat[b // per_tile, b, NSA_DH + b % per_tile] = 1.0
    eslot = np.zeros((FLASH_TK, LANE), np.float32)
    eslot[np.arange(FLASH_TK), NSA_DH + np.arange(FLASH_TK) // SEL_LEN] = 1.0
    gexp = np.zeros((3, LANE, NSA_HEADS * NSA_DH), np.float32)
    for h in range(NSA_HEADS):
        for jj in range(3):
            gexp[jj, h * 3 + jj, h * NSA_DH:(h + 1) * NSA_DH] = 1.0
    half = MLA_ROPE // 2
    inv = ROPE_BASE ** (-jnp.arange(half, dtype=F32) / half)
    inv_slot = jnp.concatenate([jnp.zeros((MLA_NOPE,), F32), inv, inv,
                                jnp.zeros((LANE - MLA_NOPE - MLA_ROPE,), F32)])[None, :]
    as_bf = lambda a: jnp.asarray(a, BF16)
    return as_bf(mband), as_bf(scat), as_bf(eslot), as_bf(gexp), inv_slot


def _hybrid_mixer(x, xb, pos_col, tabs, w_in, b_in, cmp_pos, cmp_w1, cmp_w2, w_nsa_o, w_pool, pool_scale, w_pool_o,
                  q_norm, w_uq, kv_norm, w_ukv, w_mla_o, conv_w, conv_b, conv_ln_g, conv_ln_b, w_conv_o, w_out,
                  ln_g, ln_b):
    s_len = x.shape[0]
    mband, scat, eslot, gexp, inv_slot = tabs
    row = lambda v: v[None, :]
    o_merge = int(sum(IN_SIZES[:8]))
    tm_proj = min(2048, s_len)
    u1 = _matmul(xb, _layout_in1(w_in[:, :o_merge]).astype(BF16), _layout_in1(row(b_in[:o_merge])),
                 tm_proj, 7 * LANE, BF16)
    um = _matmul(xb, w_in[:, o_merge:].astype(BF16), row(b_in[o_merge:]), tm_proj, 1024, BF16)

    nch = s_len // CMP_STRIDE
    kdim = CMP_STRIDE * LANE
    chunks = u1[:, SLOT_KV * LANE:(SLOT_KV + 4) * LANE].reshape(nch, CMP_STRIDE, 4, LANE)
    chunks = chunks.transpose(2, 0, 1, 3).reshape(4, nch, kdim)
    w1 = _pad_cols(cmp_w1.reshape(2, CMP_LEN, NSA_DH, CMP_HIDDEN).transpose(0, 1, 3, 2), LANE)
    w1 = w1.transpose(0, 1, 3, 2)
    w1cat = jnp.concatenate([w1[:, :CMP_STRIDE].reshape(2, kdim, CMP_HIDDEN),
                             w1[:, CMP_STRIDE:].reshape(2, kdim, CMP_HIDDEN)], -1).astype(BF16)
    posp = _pad_cols(cmp_pos, LANE)
    pos2 = jnp.stack([posp[:, :CMP_STRIDE].reshape(2, kdim), posp[:, CMP_STRIDE:].reshape(2, kdim)], 1)
    pos2 = jnp.pad(pos2, ((0, 0), (0, 6), (0, 0))).astype(BF16)
    kcvc = _nsa_compress(chunks, w1cat, pos2, _pad_cols(cmp_w2, LANE).astype(BF16))
    o_cmp, selneg = _nsa_cmp(u1, kcvc, mband, 256)
    o_sel = _flash(u1, SLOT_Q // NSA_HPG, u1, SLOT_KV + 4, u1, SLOT_KV + 6, n_groups=NSA_GROUPS, hpg=NSA_HPG,
                   shared_kv=True, window=False, out_dtype=F32, dv=NSA_DH, tq=FLASH_TQ, tk=FLASH_TK, sel=selneg,
                   scat=scat, eslot=eslot)
    o_win = _flash(u1, SLOT_Q // NSA_HPG, u1, SLOT_KV + 8, u1, SLOT_KV + 10, n_groups=NSA_GROUPS, hpg=NSA_HPG,
                   shared_kv=True, window=True, out_dtype=F32, dv=NSA_DH, tq=WINDOW, tk=WINDOW)

    wk, wv = _layout_mla_kv(w_ukv)
    q_m, k_m, v_m = _mla_prep(u1, pos_col, inv_slot, row(q_norm), row(kv_norm), _layout_mla_q(w_uq).astype(BF16),
                              wk.astype(BF16), wv.astype(BF16), 512)
    o_mla = _flash(q_m, 0, k_m, 0, v_m, 0, n_groups=MLA_HEADS // 4, hpg=4, shared_kv=False, window=False,
                   out_dtype=BF16, dv=MLA_DV, tq=FLASH_TQ, tk=FLASH_TK)

    pooled = _pool(u1, w_pool.astype(BF16), row(pool_scale), 512)
    conv = _conv(u1, conv_w, row(conv_b), row(conv_ln_g), row(conv_ln_b), 512)
    merged = _merge(o_cmp, o_sel, o_win, u1, gexp, pooled, o_mla, conv, um, w_nsa_o.astype(BF16),
                    w_pool_o.astype(BF16), w_mla_o.astype(BF16), w_conv_o.astype(BF16), 256)
    return _matmul_res_ln(merged, w_out.astype(BF16), x, row(ln_g), row(ln_b), 512)


def _cross_attention(x, xb, mem_b, w_q, w_k, w_v, w_o, ln_g, ln_b):
    row = lambda v: v[None, :]
    kv = _matmul(mem_b, jnp.concatenate([w_k, w_v], 1).astype(BF16), jnp.zeros((1, 2 * X_HEADS * X_DH), F32),
                 mem_b.shape[0], 2 * X_HEADS * X_DH, BF16)
    hw = X_HEADS * X_DH
    return _xattn(xb, x, w_q.astype(BF16), kv[:, :hw], kv[:, hw:], w_o.astype(BF16), row(ln_g), row(ln_b), 512)


def _hier_moe(x, xb, xp, w_group, b_group, w_router, b_router, w_e_in, w_e_out, layer, ln_g, ln_b):
    t_len = x.shape[0]
    row = lambda v: v[None, :]
    tm_r = 512
    w_gr = _pad_cols(jnp.concatenate([w_group, w_router], 1), LANE).astype(BF16)
    b_gr = _pad_cols(row(jnp.concatenate([b_group, b_router])), LANE)
    tri = jnp.asarray(np.tril(np.ones((tm_r, tm_r), np.float32)), BF16)
    info, cnt = _router(xb, w_gr, b_gr, tri, tm_r)

    counts = cnt[0, ROUTE_OFF:ROUTE_OFF + N_EXPERTS].astype(I32)
    padded = (counts + MOE_BLOCK - 1) // MOE_BLOCK * MOE_BLOCK
    pend = jnp.cumsum(padded)
    pstart = pend - padded
    n_blocks = -(-(t_len * MOE_TOPK) // MOE_BLOCK) + N_EXPERTS
    n_used = pend[-1] // MOE_BLOCK
    blk_ids = jnp.minimum(jnp.arange(n_blocks), n_used - 1)
    owner = jnp.sum((pend[None, :] <= (blk_ids * MOE_BLOCK)[:, None]).astype(I32), axis=1)
    block_expert = jnp.minimum(owner, N_EXPERTS - 1).astype(I32)
    e_ids = info[:, 0:MOE_TOPK].astype(I32)
    pos = (pstart[e_ids] + info[:, 4:4 + MOE_TOPK].astype(I32)).reshape(-1)

    xe = _dispatch(pos, xp, n_blocks * MOE_BLOCK, 512)
    blk = jnp.arange(n_blocks, dtype=I32)
    change = (blk < n_used) & ((blk == 0) | (block_expert != jnp.roll(block_expert, 1)))
    first_change_from = jnp.flip(lax.cummin(jnp.flip(jnp.where(change, blk, n_blocks))))
    nxt = jnp.concatenate([first_change_from[1:], jnp.full((1,), n_blocks, I32)])
    next_expert = jnp.where(nxt < n_blocks, block_expert[jnp.minimum(nxt, n_blocks - 1)], -1).astype(I32)
    yb = _experts(block_expert, next_expert, n_used.reshape(1).astype(I32), xe, w_e_in, w_e_out, layer)
    return _combine(pos, yb, x, info, row(ln_g), row(ln_b), 256)


def kernel(x, mem, positions, w_in, b_in, nsa_cmp_pos, nsa_cmp_w1, nsa_cmp_w2, w_nsa_o, w_pool, pool_scale, w_pool_o, mla_q_norm, w_mla_uq, mla_kv_norm, w_mla_ukv, w_mla_o, conv_w, conv_b, conv_ln_g, conv_ln_b, w_conv_o, w_out, ln_mix_g, ln_mix_b, w_xq, w_xk, w_xv, w_xo, ln_x_g, ln_x_b, w_group, b_group, w_router, b_router, w_expert_in, w_expert_out, ln_ffn_g, ln_ffn_b):
    batch, s_len, d = x.shape
    assert batch == 1 and d == D_MODEL and s_len % (2 * FLASH_TK) == 0 and s_len % FLASH_TQ == 0
    x = x[0]
    xb = x.astype(BF16)
    mem_b = mem[0].astype(BF16)
    pos_col = positions[0].astype(F32)[:, None]
    tabs = _static_tables(s_len)
    for l in range(w_in.shape[0]):
        x, xb = _hybrid_mixer(x, xb, pos_col, tabs, w_in[l], b_in[l], nsa_cmp_pos[l], nsa_cmp_w1[l], nsa_cmp_w2[l],
                              w_nsa_o[l], w_pool[l], pool_scale[l], w_pool_o[l], mla_q_norm[l], w_mla_uq[l],
                              mla_kv_norm[l], w_mla_ukv[l], w_mla_o[l], conv_w[l], conv_b[l], conv_ln_g[l],
                              conv_ln_b[l], w_conv_o[l], w_out[l], ln_mix_g[l], ln_mix_b[l])
        x, xb, xp = _cross_attention(x, xb, mem_b, w_xq[l], w_xk[l], w_xv[l], w_xo[l], ln_x_g[l], ln_x_b[l])
        x, xb = _hier_moe(x, xb, xp, w_group[l], b_group[l], w_router[l], b_router[l], w_expert_in,
                          w_expert_out, l, ln_ffn_g[l], ln_ffn_b[l])
    return x[None]
```

```python
import functools

import numpy as np
import jax
import jax.numpy as jnp
from jax import lax
from jax.experimental import pallas as pl
from jax.experimental.pallas import tpu as pltpu

F32 = jnp.float32
BF16 = jnp.bfloat16
I32 = jnp.int32

D_MODEL = 2048
NSA_HEADS = 8
NSA_GROUPS = 2
NSA_HPG = NSA_HEADS // NSA_GROUPS
NSA_DH = 64
CMP_LEN = 32
CMP_STRIDE = 16
CMP_HIDDEN = 128
SEL_LEN = 64
SEL_TOPN = 16
WINDOW = 512
FORCE_SCORE = 1.0e4
POOL_GROUPS = 4
POOL_WINDOWS = (2, 4, 8, 16)
POOL_WIDTH = 512
POOL_GW = POOL_WIDTH // POOL_GROUPS
MLA_HEADS = 8
MLA_Q_RANK = 512
MLA_KV_RANK = 256
MLA_NOPE = 64
MLA_ROPE = 32
MLA_DV = 64
ROPE_BASE = 10000.0
CONV_CH = 512
CONV_K = 31
N_BRANCH = 4
X_HEADS = 4
X_DH = 128
MOE_GROUPS = 4
MOE_EPG = 8
N_EXPERTS = MOE_GROUPS * MOE_EPG
MOE_TOPK = 2
EXPERT_FF = 512
MOE_BLOCK = 256
LN_EPS = 1e-5
RMS_EPS = 1e-6
DEPTH = 2
DN_ALPHA = (2 * DEPTH) ** 0.25
IN_SIZES = (NSA_HEADS * NSA_DH, 6 * NSA_GROUPS * NSA_DH, 3 * NSA_HEADS, POOL_WIDTH,
            MLA_Q_RANK, MLA_KV_RANK, MLA_ROPE, 2 * CONV_CH, N_BRANCH * D_MODEL)

LANE = 128
VMEM_LIMIT = 56 * 1024 * 1024
NEG = -1.0e30
FLASH_TQ = 1024
FLASH_TK = 1024
LOG2E = 1.4426950408889634

TM_PROJ = 2048
TN_PROJ = 7 * LANE
TN_MERGE = 1024
TQ_CMP = 256
TM_ROW = 512
TM_MERGE = 256
TM_COMBINE = 256

SLOT_Q = 0
SLOT_KV = 8
SLOT_POOL = 20
SLOT_CQ = 24
SLOT_CKV = 28
SLOT_KR = 30
SLOT_CONV = 32
SLOT_GATE = 40
N_SLOTS1 = 42
N1 = N_SLOTS1 * LANE


def _cparams(sem, vmem=VMEM_LIMIT):
    return pltpu.CompilerParams(dimension_semantics=sem, vmem_limit_bytes=vmem)


def _sigmoid(x):
    return 0.5 * jnp.tanh(0.5 * x) + 0.5


def _layer_norm(z, g, b):
    mu = jnp.mean(z, -1, keepdims=True)
    d = z - mu
    var = jnp.mean(d * d, -1, keepdims=True)
    return d * lax.rsqrt(var + LN_EPS) * g + b


U32 = jnp.uint32


def _pack_rows(y):
    n = y.shape[1] // 2
    bits = lax.bitcast_convert_type(y.astype(BF16).astype(F32), U32)
    return (bits[:, n:] & jnp.uint32(0xFFFF0000)) | (bits[:, :n] >> 16)


def _unpack_rows(w):
    lo = lax.bitcast_convert_type(w << 16, F32)
    hi = lax.bitcast_convert_type(w & jnp.uint32(0xFFFF0000), F32)
    return jnp.concatenate([lo, hi], axis=1)


def _dot(a, b):
    return jnp.dot(a, b, preferred_element_type=F32)


def _dot_nt(a, b):
    return lax.dot_general(a, b, (((1,), (1,)), ((), ())), preferred_element_type=F32)


def _dot3(a, b):
    hi = a.astype(BF16)
    r1 = a - hi.astype(F32)
    mid = r1.astype(BF16)
    lo = (r1 - mid.astype(F32)).astype(BF16)
    return _dot(hi, b) + _dot(mid, b) + _dot(lo, b)


def _mm_kernel(a_ref, b_ref, bias_ref, o_ref):
    o_ref[...] = (_dot(a_ref[...], b_ref[...]) + bias_ref[...]).astype(o_ref.dtype)


def _matmul(a, b, bias, tm, tn, out_dtype):
    m, k = a.shape
    n = b.shape[1]
    return pl.pallas_call(
        _mm_kernel,
        grid=(m // tm, n // tn),
        in_specs=[pl.BlockSpec((tm, k), lambda i, j: (i, 0)),
                  pl.BlockSpec((k, tn), lambda i, j: (0, j)),
                  pl.BlockSpec((1, tn), lambda i, j: (0, j))],
        out_specs=pl.BlockSpec((tm, tn), lambda i, j: (i, j)),
        out_shape=jax.ShapeDtypeStruct((m, n), out_dtype),
        compiler_params=_cparams(("parallel", "arbitrary")),
        name="matmul",
    )(a, b, bias)


def _mm_ln_kernel(a_ref, w_ref, x_ref, g_ref, b_ref, xo_ref, xb_ref):
    h = _dot(a_ref[...], w_ref[...])
    y = _layer_norm(DN_ALPHA * x_ref[...] + h, g_ref[...], b_ref[...])
    xo_ref[...] = y
    xb_ref[...] = y.astype(BF16)


def _matmul_res_ln(a, w, x, g, b, tm):
    m, k = a.shape
    d = w.shape[1]
    return pl.pallas_call(
        _mm_ln_kernel,
        grid=(m // tm,),
        in_specs=[pl.BlockSpec((tm, k), lambda i: (i, 0)),
                  pl.BlockSpec((k, d), lambda i: (0, 0)),
                  pl.BlockSpec((tm, d), lambda i: (i, 0)),
                  pl.BlockSpec((1, d), lambda i: (0, 0)),
                  pl.BlockSpec((1, d), lambda i: (0, 0))],
        out_specs=[pl.BlockSpec((tm, d), lambda i: (i, 0)),
                   pl.BlockSpec((tm, d), lambda i: (i, 0))],
        out_shape=[jax.ShapeDtypeStruct((m, d), F32), jax.ShapeDtypeStruct((m, d), BF16)],
        compiler_params=_cparams(("parallel",)),
        name="matmul_res_ln",
    )(a, w, x, g, b)


def _compress_kernel(a_ref, w1_ref, pos_ref, w2_ref, o_ref):
    nch = a_ref.shape[1]
    hh = _dot(a_ref[0], w1_ref[0])
    pp = _dot(pos_ref[0], w1_ref[0])
    pos_term = pp[0:1, :CMP_HIDDEN] + pp[1:2, CMP_HIDDEN:]
    h2_next = pltpu.roll(hh[:, CMP_HIDDEN:], nch - 1, 0)
    z = hh[:, :CMP_HIDDEN] + h2_next + pos_term
    hid = 0.5 * z * (1.0 + jnp.tanh(0.7978845608028654 * (z + 0.044715 * z * z * z)))
    o_ref[0] = _dot(hid.astype(BF16), w2_ref[0]).astype(o_ref.dtype)


def _nsa_compress(chunks, w1cat, pos2, w2pad):
    n4, nch, kdim = chunks.shape
    return pl.pallas_call(
        _compress_kernel,
        grid=(n4,),
        in_specs=[pl.BlockSpec((1, nch, kdim), lambda c: (c, 0, 0)),
                  pl.BlockSpec((1, kdim, 2 * CMP_HIDDEN), lambda c: (c // NSA_GROUPS, 0, 0)),
                  pl.BlockSpec((1, 8, kdim), lambda c: (c // NSA_GROUPS, 0, 0)),
                  pl.BlockSpec((1, CMP_HIDDEN, LANE), lambda c: (c // NSA_GROUPS, 0, 0))],
        out_specs=pl.BlockSpec((1, nch, LANE), lambda c: (c, 0, 0)),
        out_shape=jax.ShapeDtypeStruct((n4, nch, LANE), BF16),
        compiler_params=_cparams(("arbitrary",)),
        name="nsa_compress",
    )(chunks, w1cat, pos2, w2pad)


CMP_WIDTH_STEPS = 4


def _nsa_cmp_kernel(q_ref, kc_ref, vc_ref, mband_ref, ocmp_ref, selneg_ref, score_sc, *, tq, n_cmp, n_sel, top_n):
    t0 = pl.program_id(0) * tq
    nch = kc_ref.shape[1]
    row = lax.broadcasted_iota(I32, (NSA_HPG * tq, 1), 0)
    t_row = t0 + (row & (tq - 1))

    def softmax_part(width):
        col = lax.broadcasted_iota(I32, (1, width), 1)
        vis = (col * CMP_STRIDE + (CMP_LEN - 1) <= t_row) & (col < n_cmp)
        for g in range(NSA_GROUPS):
            qs = jnp.concatenate(
                [q_ref[:, (g * NSA_HPG + h) * LANE:(g * NSA_HPG + h + 1) * LANE] for h in range(NSA_HPG)], axis=0)
            s = _dot_nt(qs, kc_ref[g, 0:width, :])
            s = jnp.where(vis, s, NEG)
            m = jnp.max(s, -1, keepdims=True)
            p = jnp.where(vis, jnp.exp2(s - m), 0.0)
            l = jnp.sum(p, -1, keepdims=True)
            p = p * (1.0 / jnp.maximum(l, 1e-30))
            o = _dot(p.astype(BF16), vc_ref[g, 0:width, :])
            for h in range(NSA_HPG):
                hh = g * NSA_HPG + h
                ocmp_ref[:, hh * NSA_DH:(hh + 1) * NSA_DH] = o[h * tq:(h + 1) * tq, :NSA_DH]
            imp = p[0:tq]
            for h in range(1, NSA_HPG):
                imp = imp + p[h * tq:(h + 1) * tq]
            score_sc[g * tq:(g + 1) * tq, :] = _dot3(imp, mband_ref[0:width, :])

    last_vis = (t0 + tq - CMP_LEN) // CMP_STRIDE
    step_w = nch // CMP_WIDTH_STEPS
    variant = jnp.minimum(last_vis // step_w, CMP_WIDTH_STEPS - 1)
    for v in range(CMP_WIDTH_STEPS):
        pl.when(variant == v)(functools.partial(softmax_part, (v + 1) * step_w))

    blk = lax.broadcasted_iota(I32, (NSA_GROUPS * tq, n_sel), 1).astype(F32)
    row2 = lax.broadcasted_iota(I32, (NSA_GROUPS * tq, 1), 0)
    cur = ((t0 + (row2 & (tq - 1))) // SEL_LEN).astype(F32)
    valid = blk <= cur
    forced = (blk == 0.0) | (blk == cur) | (blk == cur - 1.0)
    assert FORCE_SCORE > NSA_HPG * (SEL_LEN // CMP_STRIDE + CMP_LEN // CMP_STRIDE - 1) and top_n > 3
    score = jnp.where(forced, -2.0, score_sc[...])
    score = jnp.where(valid, score, -1.0)

    def pick_one(_, sc):
        mx = jnp.max(sc, -1, keepdims=True)
        first = jnp.min(jnp.where(sc == mx, blk, float(n_sel)), -1, keepdims=True)
        return jnp.where(blk == first, -2.0, sc)

    sc = lax.fori_loop(0, top_n - 3, pick_one, score)
    selneg = jnp.where(valid & (sc == -2.0), 0.0, -1.0).astype(selneg_ref.dtype)
    for g in range(NSA_GROUPS):
        selneg_ref[:, g * n_sel:(g + 1) * n_sel] = selneg[g * tq:(g + 1) * tq]


def _nsa_cmp(u1, kcvc, mband, tq):
    s_len = u1.shape[0]
    nch = kcvc.shape[1]
    n_sel = s_len // SEL_LEN
    n_cmp = (s_len - CMP_LEN) // CMP_STRIDE + 1
    kern = functools.partial(_nsa_cmp_kernel, tq=tq, n_cmp=n_cmp, n_sel=n_sel, top_n=min(SEL_TOPN, n_sel))
    return pl.pallas_call(
        kern,
        grid=(s_len // tq,),
        in_specs=[pl.BlockSpec((tq, NSA_HEADS * LANE), lambda i: (i, SLOT_Q // NSA_HEADS)),
                  pl.BlockSpec((NSA_GROUPS, nch, LANE), lambda i: (0, 0, 0)),
                  pl.BlockSpec((NSA_GROUPS, nch, LANE), lambda i: (1, 0, 0)),
                  pl.BlockSpec((nch, n_sel), lambda i: (0, 0))],
        out_specs=[pl.BlockSpec((tq, NSA_HEADS * NSA_DH), lambda i: (i, 0)),
                   pl.BlockSpec((tq, NSA_GROUPS * n_sel), lambda i: (i, 0))],
        out_shape=[jax.ShapeDtypeStruct((s_len, NSA_HEADS * NSA_DH), F32),
                   jax.ShapeDtypeStruct((s_len, NSA_GROUPS * n_sel), BF16)],
        scratch_shapes=[pltpu.VMEM((NSA_GROUPS * tq, n_sel), F32)],
        compiler_params=_cparams(("parallel",)),
        name="nsa_cmp_topk",
    )(u1, kcvc, kcvc, mband)


def _flash_kernel(qi_ref, kj_ref, first_ref, last_ref, mode_ref, *refs, hpg, shared_kv, select, modes, dv,
                  split_masked):
    if select:
        q_ref, k_ref, v_ref, sel_ref, scat_ref, eslot_ref, o_ref, m_sc, acc_sc = refs
    else:
        q_ref, k_ref, v_ref, o_ref, m_sc, acc_sc = refs
    p_idx = pl.program_id(1)
    tq = q_ref.shape[0]
    tk = k_ref.shape[0]

    @pl.when(first_ref[p_idx] == 1)
    def _():
        m_sc[...] = jnp.full(m_sc.shape, NEG, F32)
        acc_sc[...] = jnp.zeros(acc_sc.shape, F32)

    def step(mask_mode):
        k_all = k_ref[...]
        lane = lax.broadcasted_iota(I32, (1, v_ref.shape[1]), 1)
        v_all = v_ref[...] + ((lane & (LANE - 1)) == dv).astype(BF16)
        if select:
            k_all = k_all + eslot_ref[...]
            bias = (_dot(sel_ref[...], scat_ref[kj_ref[p_idx]]) * (-NEG)).astype(BF16)
        hq, hk_ = tq // 2, tk // 2
        if mask_mode == 0 or not split_masked:
            blocks = ((0, tq, 0, tk),)
        else:
            blocks = {1: ((0, hq, 0, hk_), (hq, tq, 0, tk)), 2: ((0, hq, 0, tk), (hq, tq, hk_, tk))}[mask_mode]
        for r0, r1, c0, c1 in blocks:
            if mask_mode:
                r = lax.broadcasted_iota(I32, (r1 - r0, c1 - c0), 0) + r0
                c = lax.broadcasted_iota(I32, (r1 - r0, c1 - c0), 1) + c0
                keep = (c <= r) if mask_mode == 1 else (c > r)
            for h in range(hpg):
                hk = 0 if shared_kv else h
                q = q_ref[r0:r1, h * LANE:(h + 1) * LANE]
                if select:
                    q = q + bias[r0:r1]
                s = _dot_nt(q, k_all[c0:c1, hk * LANE:(hk + 1) * LANE])
                if mask_mode:
                    s = jnp.where(keep, s, NEG)
                m_prev = m_sc[h, r0:r1]
                m_new = jnp.maximum(m_prev, jnp.max(s, -1, keepdims=True))
                p = jnp.exp2(s - jnp.concatenate([m_new] * ((c1 - c0) // LANE), axis=1))
                acc_sc[h, r0:r1] = (jnp.exp2(m_prev - m_new) * acc_sc[h, r0:r1]
                                    + _dot(p.astype(BF16), v_all[c0:c1, hk * LANE:(hk + 1) * LANE]))
                m_sc[h, r0:r1] = m_new

    for mm in modes:
        pl.when(mode_ref[p_idx] == mm)(functools.partial(step, mm))

    @pl.when(last_ref[p_idx] == 1)
    def _():
        for h in range(hpg):
            acc = acc_sc[h]
            o = acc[:, :dv] * (1.0 / acc[:, dv:dv + 1])
            o_ref[:, h * dv:(h + 1) * dv] = o.astype(o_ref.dtype)


def _pair_tables(n_tiles, window):
    rows = []
    for i in range(n_tiles):
        js = ([i - 1] if i > 0 else []) + [i] if window else list(range(i + 1))
        for j in js:
            mode = 1 if j == i else (2 if window else 0)
            rows.append((i, j, int(j == js[0]), int(j == i), mode))
    tab = np.asarray(rows, np.int32)
    return [jnp.asarray(tab[:, c]) for c in range(tab.shape[1])]


def _flash(q_arr, q_blk0, k_arr, k_blk0, v_arr, v_blk0, *, n_groups, hpg, shared_kv, window,
           out_dtype, dv, tq, tk, sel=None, scat=None, eslot=None):
    s_len = q_arr.shape[0]
    assert tq == tk and (not window or tk == WINDOW)
    nk = 1 if shared_kv else hpg
    select = sel is not None
    tables = _pair_tables(s_len // tq, window)
    n_pairs = int(tables[0].shape[0])
    modes = (1, 2) if window else (0, 1)

    def qmap(g, p, qi, kj, *_):
        return (qi[p], q_blk0 + g)

    def kmap(g, p, qi, kj, *_):
        return (kj[p], k_blk0 + g)

    def vmap_(g, p, qi, kj, *_):
        return (kj[p], v_blk0 + g)

    def omap(g, p, qi, kj, *_):
        return (qi[p], g)

    in_specs = [pl.BlockSpec((tq, hpg * LANE), qmap),
                pl.BlockSpec((tk, nk * LANE), kmap),
                pl.BlockSpec((tk, nk * LANE), vmap_)]
    args = [q_arr, k_arr, v_arr]
    if select:
        n_sel = sel.shape[1] // n_groups
        in_specs += [pl.BlockSpec((tq, n_sel), omap),
                     pl.BlockSpec(scat.shape, lambda g, p, *_: (0, 0, 0)),
                     pl.BlockSpec(eslot.shape, lambda g, p, *_: (0, 0))]
        args += [sel, scat, eslot]
    kern = functools.partial(_flash_kernel, hpg=hpg, shared_kv=shared_kv, select=select, modes=modes, dv=dv,
                             split_masked=tq >= 2 * WINDOW)
    grid_spec = pltpu.PrefetchScalarGridSpec(
        num_scalar_prefetch=len(tables),
        grid=(n_groups, n_pairs),
        in_specs=in_specs,
        out_specs=pl.BlockSpec((tq, hpg * dv), omap),
        scratch_shapes=[pltpu.VMEM((hpg, tq, LANE), F32), pltpu.VMEM((hpg, tq, LANE), F32)],
    )
    return pl.pallas_call(
        kern,
        grid_spec=grid_spec,
        out_shape=jax.ShapeDtypeStruct((s_len, n_groups * hpg * dv), out_dtype),
        compiler_params=_cparams(("parallel", "arbitrary")),
        name="flash_sel" if select else ("flash_win" if window else "flash_causal"),
    )(*tables, *args)


def _mla_prep_kernel(cq_ref, ckv_ref, kr_ref, pos_ref, inv_ref, qn_ref, kvn_ref, wuq_ref, wk_ref, wv_ref,
                     q_ref, k_ref, v_ref):
    def rms(x, g):
        return x * lax.rsqrt(jnp.mean(x * x, -1, keepdims=True) + RMS_EPS) * g

    ang = pos_ref[...] * inv_ref[...]
    cos, sin = jnp.cos(ang), jnp.sin(ang)
    hw = MLA_HEADS * LANE
    qh = _dot(rms(cq_ref[...].astype(F32), qn_ref[...]).astype(BF16), wuq_ref[...])
    scale = (MLA_NOPE + MLA_ROPE) ** -0.5 * LOG2E
    cos_t = jnp.concatenate([cos] * MLA_HEADS, axis=1)
    sin_t = jnp.concatenate([sin] * MLA_HEADS, axis=1)
    q_ref[...] = ((qh[:, :hw] * cos_t + qh[:, hw:] * sin_t) * scale).astype(BF16)
    ckv = rms(ckv_ref[...].astype(F32), kvn_ref[...]).astype(BF16)
    kr = kr_ref[...].astype(F32)
    k_rope = kr[:, :LANE] * cos + kr[:, LANE:] * sin
    k_ref[...] = (_dot(ckv, wk_ref[...]) + jnp.concatenate([k_rope] * MLA_HEADS, axis=1)).astype(BF16)
    v_ref[...] = _dot(ckv, wv_ref[...]).astype(BF16)


def _mla_prep(u1, pos_col, inv_slot, qn, kvn, wuq, wk, wv, tm):
    s_len = u1.shape[0]
    hw = MLA_HEADS * LANE
    full = lambda a: pl.BlockSpec(a.shape, lambda i: (0,) * a.ndim)
    out = jax.ShapeDtypeStruct((s_len, hw), BF16)
    return pl.pallas_call(
        _mla_prep_kernel,
        grid=(s_len // tm,),
        in_specs=[pl.BlockSpec((tm, MLA_Q_RANK), lambda i: (i, SLOT_CQ * LANE // MLA_Q_RANK)),
                  pl.BlockSpec((tm, MLA_KV_RANK), lambda i: (i, SLOT_CKV * LANE // MLA_KV_RANK)),
                  pl.BlockSpec((tm, 2 * LANE), lambda i: (i, SLOT_KR // 2)),
                  pl.BlockSpec((tm, 1), lambda i: (i, 0)),
                  full(inv_slot), full(qn), full(kvn), full(wuq), full(wk), full(wv)],
        out_specs=[pl.BlockSpec((tm, hw), lambda i: (i, 0))] * 3,
        out_shape=[out, out, out],
        compiler_params=_cparams(("parallel",)),
        name="mla_prep",
    )(u1, u1, u1, pos_col, inv_slot, qn, kvn, wuq, wk, wv)


POOL_HALO = 16


def _pool_kernel(cur_ref, halo_ref, wp_ref, scale_ref, o_ref):
    i = pl.program_id(0)
    tm = cur_ref.shape[0]
    halo = jnp.where(i > 0, halo_ref[...].astype(F32), 0.0)
    x = jnp.concatenate([halo, cur_ref[...].astype(F32)], axis=0)
    sums = {1: x}
    w = 1
    while w < max(POOL_WINDOWS):
        a = sums[w]
        sums[2 * w] = a[w:] + a[:-w]
        w *= 2
    t = (i * tm + lax.broadcasted_iota(I32, (tm, 1), 0) + 1).astype(F32)
    outs = []
    for gi, w in enumerate(POOL_WINDOWS):
        lo = gi * POOL_GW
        start = POOL_HALO - (w - 1)
        win = sums[w][start:start + tm, lo:lo + POOL_GW]
        mean = win / jnp.minimum(t, float(w))
        pooled = mean - x[POOL_HALO:, lo:lo + POOL_GW]
        outs.append(_dot(pooled.astype(BF16), wp_ref[gi]))
    o_ref[...] = (jnp.concatenate(outs, axis=1) * scale_ref[...]).astype(o_ref.dtype)


def _pool(u1, w_pool, pool_scale, tm):
    s_len = u1.shape[0]
    blk = SLOT_POOL * LANE // POOL_WIDTH
    return pl.pallas_call(
        _pool_kernel,
        grid=(s_len // tm,),
        in_specs=[pl.BlockSpec((tm, POOL_WIDTH), lambda i: (i, blk)),
                  pl.BlockSpec((POOL_HALO, POOL_WIDTH),
                               lambda i: (jnp.maximum(i * (tm // POOL_HALO) - 1, 0), blk)),
                  pl.BlockSpec(w_pool.shape, lambda i: (0, 0, 0)),
                  pl.BlockSpec((1, POOL_WIDTH), lambda i: (0, 0))],
        out_specs=pl.BlockSpec((tm, POOL_WIDTH), lambda i: (i, 0)),
        out_shape=jax.ShapeDtypeStruct((s_len, POOL_WIDTH), BF16),
        compiler_params=_cparams(("parallel",)),
        name="pool",
    )(u1, u1, w_pool, pool_scale)


CONV_HALO = 32


def _conv_kernel(cur_ref, halo_ref, w_ref, b_ref, g_ref, beta_ref, o_ref, hbuf):
    i = pl.program_id(0)
    tm = cur_ref.shape[0]

    def glu(u):
        u = u.astype(F32)
        return u[:, :CONV_CH] * _sigmoid(u[:, CONV_CH:])

    hbuf[0:CONV_HALO, :] = jnp.where(i > 0, glu(halo_ref[...]), 0.0)
    hbuf[CONV_HALO:, :] = glu(cur_ref[...])
    acc = jnp.zeros((tm, CONV_CH), F32) + b_ref[...]
    for k in range(CONV_K):
        off = CONV_HALO - (CONV_K - 1) + k
        acc = acc + hbuf[off:off + tm, :] * w_ref[k:k + 1, :]
    y = _layer_norm(acc, g_ref[...], beta_ref[...])
    o_ref[...] = (y * _sigmoid(y)).astype(o_ref.dtype)


def _conv(u1, conv_w, conv_b, ln_g, ln_b, tm):
    s_len = u1.shape[0]
    blk = SLOT_CONV * LANE // (2 * CONV_CH)
    row = lambda a: pl.BlockSpec(a.shape, lambda i: (0, 0))
    return pl.pallas_call(
        _conv_kernel,
        grid=(s_len // tm,),
        in_specs=[pl.BlockSpec((tm, 2 * CONV_CH), lambda i: (i, blk)),
                  pl.BlockSpec((CONV_HALO, 2 * CONV_CH),
                               lambda i: (jnp.maximum(i * (tm // CONV_HALO) - 1, 0), blk)),
                  row(conv_w), row(conv_b), row(ln_g), row(ln_b)],
        out_specs=pl.BlockSpec((tm, CONV_CH), lambda i: (i, 0)),
        out_shape=jax.ShapeDtypeStruct((s_len, CONV_CH), BF16),
        scratch_shapes=[pltpu.VMEM((tm + CONV_HALO, CONV_CH), F32)],
        compiler_params=_cparams(("parallel",)),
        name="conv_module",
    )(u1, u1, conv_w, conv_b, ln_g, ln_b)


def _merge_kernel(ocmp_ref, osel_ref, owin_ref, gate_ref, gexp_ref, pool_ref, mla_ref, conv_ref, um_ref,
                  wn_ref, wp_ref, wm_ref, wc_ref, o_ref):
    sg = _sigmoid(gate_ref[...].astype(F32))
    nsa = (_dot3(sg, gexp_ref[0]) * ocmp_ref[...] + _dot3(sg, gexp_ref[1]) * osel_ref[...]
           + _dot3(sg, gexp_ref[2]) * owin_ref[...])
    branches = (_dot(nsa.astype(BF16), wn_ref[...]), _dot(pool_ref[...], wp_ref[...]),
                _dot(mla_ref[...], wm_ref[...]), _dot(conv_ref[...], wc_ref[...]))
    merged = None
    for j, br in enumerate(branches):
        term = _sigmoid(um_ref[:, j * D_MODEL:(j + 1) * D_MODEL].astype(F32)) * br
        merged = term if merged is None else merged + term
    o_ref[...] = merged.astype(o_ref.dtype)


def _merge(o_cmp, o_sel, o_win, u1, gexp, pooled, o_mla, conv, um, wn, wp, wm, wc, tm):
    s_len = u1.shape[0]
    tile = lambda w: pl.BlockSpec((tm, w), lambda i: (i, 0))
    full = lambda a: pl.BlockSpec(a.shape, lambda i: (0,) * a.ndim)
    return pl.pallas_call(
        _merge_kernel,
        grid=(s_len // tm,),
        in_specs=[tile(512), tile(512), tile(512),
                  pl.BlockSpec((tm, LANE), lambda i: (i, SLOT_GATE)),
                  full(gexp), tile(512), tile(512), tile(512), tile(N_BRANCH * D_MODEL),
                  full(wn), full(wp), full(wm), full(wc)],
        out_specs=tile(D_MODEL),
        out_shape=jax.ShapeDtypeStruct((s_len, D_MODEL), BF16),
        compiler_params=_cparams(("parallel",)),
        name="branch_merge",
    )(o_cmp, o_sel, o_win, u1, gexp, pooled, o_mla, conv, um, wn, wp, wm, wc)


def _xattn_kernel(xb_ref, x_ref, wq_ref, k_ref, v_ref, wo_ref, g_ref, b_ref, xo_ref, xob_ref, xp_ref):
    q = _dot(xb_ref[...], wq_ref[...]).astype(BF16)
    k = k_ref[...]
    v = v_ref[...]
    outs = []
    for h in range(X_HEADS):
        sl = slice(h * X_DH, (h + 1) * X_DH)
        s = _dot_nt(q[:, sl], k[:, sl]) * (X_DH ** -0.5)
        m = jnp.max(s, -1, keepdims=True)
        p = jnp.exp(s - m)
        p = p * (1.0 / jnp.sum(p, -1, keepdims=True))
        outs.append(_dot(p.astype(BF16), v[:, sl]))
    o = jnp.concatenate(outs, axis=1).astype(BF16)
    y = _layer_norm(DN_ALPHA * x_ref[...] + _dot(o, wo_ref[...]), g_ref[...], b_ref[...])
    xo_ref[...] = y
    xob_ref[...] = y.astype(BF16)
    xp_ref[...] = _pack_rows(y)


def _xattn(xb, x, wq, k, v, wo, g, b, tm):
    s_len = x.shape[0]
    tile = lambda: pl.BlockSpec((tm, D_MODEL), lambda i: (i, 0))
    full = lambda a: pl.BlockSpec(a.shape, lambda i: (0,) * a.ndim)
    return pl.pallas_call(
        _xattn_kernel,
        grid=(s_len // tm,),
        in_specs=[tile(), tile(), full(wq), full(k), full(v), full(wo), full(g), full(b)],
        out_specs=[tile(), tile(), pl.BlockSpec((tm, D_MODEL // 2), lambda i: (i, 0))],
        out_shape=[jax.ShapeDtypeStruct((s_len, D_MODEL), F32), jax.ShapeDtypeStruct((s_len, D_MODEL), BF16),
                   jax.ShapeDtypeStruct((s_len, D_MODEL // 2), U32)],
        compiler_params=_cparams(("parallel",)),
        name="cross_attention_ln",
    )(xb, x, wq, k, v, wo, g, b)


ROUTE_OFF = MOE_GROUPS


def _router_kernel(xb_ref, w_ref, b_ref, tri_ref, info_ref, cnt_ref, carry):
    @pl.when(pl.program_id(0) == 0)
    def _():
        carry[...] = jnp.zeros(carry.shape, F32)

    logits = _dot(xb_ref[...], w_ref[...]) + b_ref[...]
    lane = lax.broadcasted_iota(I32, logits.shape, 1).astype(F32)
    is_g = lane < float(MOE_GROUPS)
    neg_inf = -jnp.inf
    gl = jnp.where(is_g, logits, neg_inf)
    gmax = jnp.max(gl, -1, keepdims=True)
    g_sel = jnp.min(jnp.where(gl == gmax, lane, float(LANE)), -1, keepdims=True)
    g_w = 1.0 / jnp.sum(jnp.where(is_g, jnp.exp(gl - gmax), 0.0), -1, keepdims=True)
    lo = ROUTE_OFF + MOE_EPG * g_sel
    in_g = (lane >= lo) & (lane < lo + MOE_EPG)
    el = jnp.where(in_g, logits, neg_inf)
    emax = jnp.max(el, -1, keepdims=True)
    e = jnp.where(in_g, jnp.exp(el - emax), 0.0)
    p = e / jnp.sum(e, -1, keepdims=True)
    pm = jnp.where(in_g, p, -1.0)
    p1 = jnp.max(pm, -1, keepdims=True)
    i1 = jnp.min(jnp.where(pm == p1, lane, float(LANE)), -1, keepdims=True)
    pm2 = jnp.where(lane == i1, -1.0, pm)
    p2 = jnp.max(pm2, -1, keepdims=True)
    i2 = jnp.min(jnp.where(pm2 == p2, lane, float(LANE)), -1, keepdims=True)
    denom = p1 + p2
    gate1 = g_w * p1 / denom
    gate2 = g_w * p2 / denom

    oh1 = (lane == i1).astype(BF16)
    oh2 = (lane == i2).astype(BF16)
    incl1 = _dot(tri_ref[...], oh1)
    incl2 = _dot(tri_ref[...], oh2)
    tot1 = jnp.sum(oh1.astype(F32), 0, keepdims=True)
    tot2 = jnp.sum(oh2.astype(F32), 0, keepdims=True)
    base = carry[...]
    rank1 = jnp.sum(jnp.where(lane == i1, base + incl1, 0.0), -1, keepdims=True) - 1.0
    rank2 = jnp.sum(jnp.where(lane == i2, base + tot1 + incl2, 0.0), -1, keepdims=True) - 1.0
    carry[...] = base + tot1 + tot2
    cnt_ref[...] = carry[...]

    cols = (i1 - ROUTE_OFF, i2 - ROUTE_OFF, gate1, gate2, rank1, rank2)
    info = jnp.zeros(logits.shape, F32)
    for c, val in enumerate(cols):
        info = jnp.where(lane == float(c), val, info)
    info_ref[...] = info


def _router(xb, w_gr, b_gr, tri, tm):
    t_len = xb.shape[0]
    return pl.pallas_call(
        _router_kernel,
        grid=(t_len // tm,),
        in_specs=[pl.BlockSpec((tm, D_MODEL), lambda i: (i, 0)),
                  pl.BlockSpec(w_gr.shape, lambda i: (0, 0)),
                  pl.BlockSpec((1, LANE), lambda i: (0, 0)),
                  pl.BlockSpec((tm, tm), lambda i: (0, 0))],
        out_specs=[pl.BlockSpec((tm, LANE), lambda i: (i, 0)),
                   pl.BlockSpec((1, LANE), lambda i: (0, 0))],
        out_shape=[jax.ShapeDtypeStruct((t_len, LANE), F32), jax.ShapeDtypeStruct((1, LANE), F32)],
        scratch_shapes=[pltpu.VMEM((1, LANE), F32)],
        compiler_params=_cparams(("arbitrary",)),
        name="moe_router",
    )(xb, w_gr, b_gr, tri)


def _row_copy(src, src_row, dst, dst_row, sem):
    return pltpu.make_async_copy(src.at[pl.ds(src_row, 1)], dst.at[pl.ds(dst_row, 1)], sem)


def _dispatch_kernel(pos_ref, x_ref, xe_in_hbm, xe_hbm, sem, *, td):
    del xe_in_hbm
    base = pl.program_id(0) * td

    def issue(t, c):
        for k in range(MOE_TOPK):
            _row_copy(x_ref, t, xe_hbm, pos_ref[MOE_TOPK * (base + t) + k], sem).start()
        return c

    lax.fori_loop(0, td, issue, 0, unroll=8)
    all_rows = xe_hbm.at[pl.ds(0, MOE_TOPK * td)]
    pltpu.make_async_copy(all_rows, all_rows, sem).wait()


def _dispatch(pos_flat, x, rows, td):
    t_len, d = x.shape
    zeros = jnp.zeros((rows, d), x.dtype)
    grid_spec = pltpu.PrefetchScalarGridSpec(
        num_scalar_prefetch=1,
        grid=(t_len // td,),
        in_specs=[pl.BlockSpec((td, d), lambda i, pos: (i, 0)), pl.BlockSpec(memory_space=pl.ANY)],
        out_specs=pl.BlockSpec(memory_space=pl.ANY),
        scratch_shapes=[pltpu.SemaphoreType.DMA(())],
    )
    return pl.pallas_call(
        functools.partial(_dispatch_kernel, td=td),
        grid_spec=grid_spec,
        out_shape=jax.ShapeDtypeStruct((rows, d), x.dtype),
        input_output_aliases={2: 0},
        compiler_params=pltpu.CompilerParams(dimension_semantics=("arbitrary",)),
        name="moe_dispatch",
    )(pos_flat, x, zeros)


def _expert_weight_copies(win_hbm, wout_hbm, layer, expert, win_f32, wout_f32, sem):
    return (pltpu.make_async_copy(win_hbm.at[layer, expert], win_f32, sem.at[0]),
            pltpu.make_async_copy(wout_hbm.at[layer, expert], wout_f32, sem.at[1]))


def _expert_kernel(be_ref, next_ref, nused_ref, xe_ref, win_hbm, wout_hbm, yb_ref, win_f32, wout_f32, win_bf,
                   wout_bf, sem, *, layer):
    b = pl.program_id(0)
    used = b < nused_ref[0]
    new_expert = (b == 0) | (be_ref[b] != be_ref[jnp.maximum(b - 1, 0)])
    copies = functools.partial(_expert_weight_copies, win_hbm, wout_hbm, layer)

    @pl.when(b == 0)
    def _():
        for c in copies(be_ref[0], win_f32, wout_f32, sem):
            c.start()

    @pl.when(used & new_expert)
    def _():
        for c in copies(be_ref[b], win_f32, wout_f32, sem):
            c.wait()
        win_bf[...] = win_f32[...].astype(BF16)
        wout_bf[...] = wout_f32[...].astype(BF16)

        @pl.when(next_ref[b] >= 0)
        def _():
            for c in copies(next_ref[b], win_f32, wout_f32, sem):
                c.start()

    @pl.when(used)
    def _():
        hcat = _dot(_unpack_rows(xe_ref[...]).astype(BF16), win_bf[...])
        a = hcat[:, :EXPERT_FF]
        act = (a * _sigmoid(a) * hcat[:, EXPERT_FF:]).astype(BF16)
        yb_ref[...] = _pack_rows(_dot(act, wout_bf[...]))

    @pl.when(b >= nused_ref[0])
    def _():
        yb_ref[...] = jnp.zeros(yb_ref.shape, U32)


def _experts(block_expert, next_expert, n_used, xe, w_e_in, w_e_out, layer):
    rows, half = xe.shape
    d = 2 * half
    n_blocks = rows // MOE_BLOCK
    grid_spec = pltpu.PrefetchScalarGridSpec(
        num_scalar_prefetch=3,
        grid=(n_blocks,),
        in_specs=[pl.BlockSpec((MOE_BLOCK, half), lambda b, *_: (b, 0)),
                  pl.BlockSpec(memory_space=pl.ANY),
                  pl.BlockSpec(memory_space=pl.ANY)],
        out_specs=pl.BlockSpec((MOE_BLOCK, half), lambda b, *_: (b, 0)),
        scratch_shapes=[pltpu.VMEM((d, 2 * EXPERT_FF), F32), pltpu.VMEM((EXPERT_FF, d), F32),
                        pltpu.VMEM((d, 2 * EXPERT_FF), BF16), pltpu.VMEM((EXPERT_FF, d), BF16),
                        pltpu.SemaphoreType.DMA((2,))],
    )
    return pl.pallas_call(
        functools.partial(_expert_kernel, layer=layer),
        grid_spec=grid_spec,
        out_shape=jax.ShapeDtypeStruct((rows, half), U32),
        compiler_params=_cparams(("arbitrary",)),
        name="moe_experts",
    )(block_expert, next_expert, n_used, xe, w_e_in, w_e_out)


def _combine_kernel(pos_ref, yb_hbm, x_ref, info_ref, g_ref, b_ref, xo_ref, xob_ref, buf, sem):
    tm = x_ref.shape[0]
    i = pl.program_id(0)
    slot = i & 1

    def gather(tile, slot_):
        def issue(t, c):
            for k in range(MOE_TOPK):
                _row_copy(yb_hbm, pos_ref[MOE_TOPK * (tile * tm + t) + k], buf.at[slot_, k], t,
                          sem.at[slot_]).start()
            return c
        lax.fori_loop(0, tm, issue, 0, unroll=8)

    @pl.when(i == 0)
    def _():
        gather(0, 0)

    @pl.when(i + 1 < pl.num_programs(0))
    def _():
        gather(i + 1, 1 - slot)

    pltpu.make_async_copy(buf.at[slot], buf.at[slot], sem.at[slot]).wait()
    info = info_ref[...]
    y = info[:, 2:3] * _unpack_rows(buf[slot, 0]) + info[:, 3:4] * _unpack_rows(buf[slot, 1])
    z = _layer_norm(DN_ALPHA * x_ref[...] + y, g_ref[...], b_ref[...])
    xo_ref[...] = z
    xob_ref[...] = z.astype(BF16)


def _combine(pos_flat, yb, x, info, g, b, tm):
    t_len, d = x.shape
    grid_spec = pltpu.PrefetchScalarGridSpec(
        num_scalar_prefetch=1,
        grid=(t_len // tm,),
        in_specs=[pl.BlockSpec(memory_space=pl.ANY),
                  pl.BlockSpec((tm, d), lambda i, pos: (i, 0)),
                  pl.BlockSpec((tm, LANE), lambda i, pos: (i, 0)),
                  pl.BlockSpec((1, d), lambda i, pos: (0, 0)),
                  pl.BlockSpec((1, d), lambda i, pos: (0, 0))],
        out_specs=[pl.BlockSpec((tm, d), lambda i, pos: (i, 0)),
                   pl.BlockSpec((tm, d), lambda i, pos: (i, 0))],
        scratch_shapes=[pltpu.VMEM((2, MOE_TOPK, tm, d // 2), U32), pltpu.SemaphoreType.DMA((2,))],
    )
    return pl.pallas_call(
        _combine_kernel,
        grid_spec=grid_spec,
        out_shape=[jax.ShapeDtypeStruct((t_len, d), F32), jax.ShapeDtypeStruct((t_len, d), BF16)],
        compiler_params=_cparams(("arbitrary",)),
        name="moe_combine_ln",
    )(pos_flat, yb, x, info, g, b)


def _pad_cols(m, width):
    return jnp.pad(m, [(0, 0)] * (m.ndim - 1) + [(0, width - m.shape[-1])])


def _rot_half_cols(m):
    half = m.shape[-1] // 2
    return jnp.concatenate([-m[..., half:], m[..., :half]], -1)


def _layout_in1(m):
    offs = np.cumsum((0,) + IN_SIZES)
    o_q, o_kv, o_g, o_pool, o_cq, o_ckv, o_kr, o_conv = offs[:8]
    z = lambda n: jnp.zeros(m.shape[:-1] + (n,), m.dtype)
    parts = []
    for h in range(NSA_HEADS):
        parts += [m[..., o_q + h * NSA_DH:o_q + (h + 1) * NSA_DH] * (NSA_DH ** -0.5 * LOG2E), z(LANE - NSA_DH)]
    for c in range(6 * NSA_GROUPS):
        parts += [m[..., o_kv + c * NSA_DH:o_kv + (c + 1) * NSA_DH], z(LANE - NSA_DH)]
    parts += [m[..., o_pool:o_pool + POOL_WIDTH], m[..., o_cq:o_cq + MLA_Q_RANK], m[..., o_ckv:o_ckv + MLA_KV_RANK]]
    kr = m[..., o_kr:o_kr + MLA_ROPE]
    parts += [z(MLA_NOPE), kr, z(LANE - MLA_NOPE - MLA_ROPE), z(MLA_NOPE), _rot_half_cols(kr),
              z(LANE - MLA_NOPE - MLA_ROPE)]
    parts += [m[..., o_conv:o_conv + 2 * CONV_CH], _pad_cols(m[..., o_g:o_g + 3 * NSA_HEADS], LANE), z(LANE)]
    return jnp.concatenate(parts, -1)


def _layout_mla_q(w):
    dq = MLA_NOPE + MLA_ROPE
    z = lambda n: jnp.zeros((w.shape[0], n), w.dtype)
    a, b = [], []
    for h in range(MLA_HEADS):
        rope = w[:, h * dq + MLA_NOPE:(h + 1) * dq]
        a += [w[:, h * dq:h * dq + MLA_NOPE], rope, z(LANE - dq)]
        b += [z(MLA_NOPE), _rot_half_cols(rope), z(LANE - dq)]
    return jnp.concatenate(a + b, -1)


def _layout_mla_kv(w):
    dkv = MLA_NOPE + MLA_DV
    z = jnp.zeros((w.shape[0], LANE - MLA_NOPE), w.dtype)
    k, v = [], []
    for h in range(MLA_HEADS):
        k += [w[:, h * dkv:h * dkv + MLA_NOPE], z]
        v += [w[:, h * dkv + MLA_NOPE:(h + 1) * dkv], z]
    return jnp.concatenate(k, -1), jnp.concatenate(v, -1)


def _static_tables(s_len):
    nch = s_len // CMP_STRIDE
    n_cmp = (s_len - CMP_LEN) // CMP_STRIDE + 1
    n_sel = s_len // SEL_LEN
    ratio = SEL_LEN // CMP_STRIDE
    c = np.arange(nch)[:, None]
    j = np.arange(n_sel)[None, :]
    mband = ((c >= ratio * j - 1) & (c <= ratio * j + ratio - 1) & (c < n_cmp)).astype(np.float32)
    per_tile = FLASH_TK // SEL_LEN
    n_tiles = s_len // FLASH_TK
    scat = np.zeros((n_tiles, n_sel, LANE), np.float32)
    for b in range(n_sel):
        scat[b // per_tile, b, NSA_DH + b % per_tile] = 1.0
    eslot = np.zeros((FLASH_TK, LANE), np.float32)
    eslot[np.arange(FLASH_TK), NSA_DH + np.arange(FLASH_TK) // SEL_LEN] = 1.0
    gexp = np.zeros((3, LANE, NSA_HEADS * NSA_DH), np.float32)
    for h in range(NSA_HEADS):
        for jj in range(3):
            gexp[jj, h * 3 + jj, h * NSA_DH:(h + 1) * NSA_DH] = 1.0
    half = MLA_ROPE // 2
    inv = ROPE_BASE ** (-jnp.arange(half, dtype=F32) / half)
    inv_slot = jnp.concatenate([jnp.zeros((MLA_NOPE,), F32), inv, inv,
                                jnp.zeros((LANE - MLA_NOPE - MLA_ROPE,), F32)])[None, :]
    as_bf = lambda a: jnp.asarray(a, BF16)
    return as_bf(mband), as_bf(scat), as_bf(eslot), as_bf(gexp), inv_slot


def _hybrid_mixer(x, xb, pos_col, tabs, w_in, b_in, cmp_pos, cmp_w1, cmp_w2, w_nsa_o, w_pool, pool_scale, w_pool_o,
                  q_norm, w_uq, kv_norm, w_ukv, w_mla_o, conv_w, conv_b, conv_ln_g, conv_ln_b, w_conv_o, w_out,
                  ln_g, ln_b):
    s_len = x.shape[0]
    mband, scat, eslot, gexp, inv_slot = tabs
    row = lambda v: v[None, :]
    o_merge = int(sum(IN_SIZES[:8]))
    tm_proj = min(TM_PROJ, s_len)
    u1 = _matmul(xb, _layout_in1(w_in[:, :o_merge]).astype(BF16), _layout_in1(row(b_in[:o_merge])),
                 tm_proj, TN_PROJ, BF16)
    um = _matmul(xb, w_in[:, o_merge:].astype(BF16), row(b_in[o_merge:]), tm_proj, TN_MERGE, BF16)

    nch = s_len // CMP_STRIDE
    kdim = CMP_STRIDE * LANE
    chunks = u1[:, SLOT_KV * LANE:(SLOT_KV + 4) * LANE].reshape(nch, CMP_STRIDE, 4, LANE)
    chunks = chunks.transpose(2, 0, 1, 3).reshape(4, nch, kdim)
    w1 = _pad_cols(cmp_w1.reshape(2, CMP_LEN, NSA_DH, CMP_HIDDEN).transpose(0, 1, 3, 2), LANE)
    w1 = w1.transpose(0, 1, 3, 2)
    w1cat = jnp.concatenate([w1[:, :CMP_STRIDE].reshape(2, kdim, CMP_HIDDEN),
                             w1[:, CMP_STRIDE:].reshape(2, kdim, CMP_HIDDEN)], -1).astype(BF16)
    posp = _pad_cols(cmp_pos, LANE)
    pos2 = jnp.stack([posp[:, :CMP_STRIDE].reshape(2, kdim), posp[:, CMP_STRIDE:].reshape(2, kdim)], 1)
    pos2 = jnp.pad(pos2, ((0, 0), (0, 6), (0, 0))).astype(BF16)
    kcvc = _nsa_compress(chunks, w1cat, pos2, _pad_cols(cmp_w2, LANE).astype(BF16))
    o_cmp, selneg = _nsa_cmp(u1, kcvc, mband, TQ_CMP)
    o_sel = _flash(u1, SLOT_Q // NSA_HPG, u1, SLOT_KV + 4, u1, SLOT_KV + 6, n_groups=NSA_GROUPS, hpg=NSA_HPG,
                   shared_kv=True, window=False, out_dtype=F32, dv=NSA_DH, tq=FLASH_TQ, tk=FLASH_TK, sel=selneg,
                   scat=scat, eslot=eslot)
    o_win = _flash(u1, SLOT_Q // NSA_HPG, u1, SLOT_KV + 8, u1, SLOT_KV + 10, n_groups=NSA_GROUPS, hpg=NSA_HPG,
                   shared_kv=True, window=True, out_dtype=F32, dv=NSA_DH, tq=WINDOW, tk=WINDOW)

    wk, wv = _layout_mla_kv(w_ukv)
    q_m, k_m, v_m = _mla_prep(u1, pos_col, inv_slot, row(q_norm), row(kv_norm), _layout_mla_q(w_uq).astype(BF16),
                              wk.astype(BF16), wv.astype(BF16), TM_ROW)
    o_mla = _flash(q_m, 0, k_m, 0, v_m, 0, n_groups=1, hpg=MLA_HEADS, shared_kv=False, window=False,
                   out_dtype=BF16, dv=MLA_DV, tq=FLASH_TQ, tk=FLASH_TK)

    pooled = _pool(u1, w_pool.astype(BF16), row(pool_scale), TM_ROW)
    conv = _conv(u1, conv_w, row(conv_b), row(conv_ln_g), row(conv_ln_b), TM_ROW)
    merged = _merge(o_cmp, o_sel, o_win, u1, gexp, pooled, o_mla, conv, um, w_nsa_o.astype(BF16),
                    w_pool_o.astype(BF16), w_mla_o.astype(BF16), w_conv_o.astype(BF16), TM_MERGE)
    return _matmul_res_ln(merged, w_out.astype(BF16), x, row(ln_g), row(ln_b), TM_ROW)


def _cross_attention(x, xb, mem_b, w_q, w_k, w_v, w_o, ln_g, ln_b):
    row = lambda v: v[None, :]
    kv = _matmul(mem_b, jnp.concatenate([w_k, w_v], 1).astype(BF16), jnp.zeros((1, 2 * X_HEADS * X_DH), F32),
                 mem_b.shape[0], 2 * X_HEADS * X_DH, BF16)
    hw = X_HEADS * X_DH
    return _xattn(xb, x, w_q.astype(BF16), kv[:, :hw], kv[:, hw:], w_o.astype(BF16), row(ln_g), row(ln_b), TM_ROW)


def _hier_moe(x, xb, xp, w_group, b_group, w_router, b_router, w_e_in, w_e_out, layer, ln_g, ln_b):
    t_len = x.shape[0]
    row = lambda v: v[None, :]
    tm_r = TM_ROW
    w_gr = _pad_cols(jnp.concatenate([w_group, w_router], 1), LANE).astype(BF16)
    b_gr = _pad_cols(row(jnp.concatenate([b_group, b_router])), LANE)
    tri = jnp.asarray(np.tril(np.ones((tm_r, tm_r), np.float32)), BF16)
    info, cnt = _router(xb, w_gr, b_gr, tri, tm_r)

    counts = cnt[0, ROUTE_OFF:ROUTE_OFF + N_EXPERTS].astype(I32)
    padded = (counts + MOE_BLOCK - 1) // MOE_BLOCK * MOE_BLOCK
    pend = jnp.cumsum(padded)
    pstart = pend - padded
    n_blocks = -(-(t_len * MOE_TOPK) // MOE_BLOCK) + N_EXPERTS
    n_used = pend[-1] // MOE_BLOCK
    blk_ids = jnp.minimum(jnp.arange(n_blocks), n_used - 1)
    owner = jnp.sum((pend[None, :] <= (blk_ids * MOE_BLOCK)[:, None]).astype(I32), axis=1)
    block_expert = jnp.minimum(owner, N_EXPERTS - 1).astype(I32)
    e_ids = info[:, 0:MOE_TOPK].astype(I32)
    pos = (pstart[e_ids] + info[:, 4:4 + MOE_TOPK].astype(I32)).reshape(-1)

    xe = _dispatch(pos, xp, n_blocks * MOE_BLOCK, TM_ROW)
    blk = jnp.arange(n_blocks, dtype=I32)
    change = (blk < n_used) & ((blk == 0) | (block_expert != jnp.roll(block_expert, 1)))
    first_change_from = jnp.flip(lax.cummin(jnp.flip(jnp.where(change, blk, n_blocks))))
    nxt = jnp.concatenate([first_change_from[1:], jnp.full((1,), n_blocks, I32)])
    next_expert = jnp.where(nxt < n_blocks, block_expert[jnp.minimum(nxt, n_blocks - 1)], -1).astype(I32)
    yb = _experts(block_expert, next_expert, n_used.reshape(1).astype(I32), xe, w_e_in, w_e_out, layer)
    return _combine(pos, yb, x, info, row(ln_g), row(ln_b), TM_COMBINE)


def kernel(x, mem, positions, w_in, b_in, nsa_cmp_pos, nsa_cmp_w1, nsa_cmp_w2, w_nsa_o, w_pool, pool_scale, w_pool_o, mla_q_norm, w_mla_uq, mla_kv_norm, w_mla_ukv, w_mla_o, conv_w, conv_b, conv_ln_g, conv_ln_b, w_conv_o, w_out, ln_mix_g, ln_mix_b, w_xq, w_xk, w_xv, w_xo, ln_x_g, ln_x_b, w_group, b_group, w_router, b_router, w_expert_in, w_expert_out, ln_ffn_g, ln_ffn_b):
    batch, s_len, d = x.shape
    assert batch == 1 and d == D_MODEL and s_len % (2 * FLASH_TK) == 0 and s_len % FLASH_TQ == 0
    x = x[0]
    xb = x.astype(BF16)
    mem_b = mem[0].astype(BF16)
    pos_col = positions[0].astype(F32)[:, None]
    tabs = _static_tables(s_len)
    for l in range(w_in.shape[0]):
        x, xb = _hybrid_mixer(x, xb, pos_col, tabs, w_in[l], b_in[l], nsa_cmp_pos[l], nsa_cmp_w1[l], nsa_cmp_w2[l],
                              w_nsa_o[l], w_pool[l], pool_scale[l], w_pool_o[l], mla_q_norm[l], w_mla_uq[l],
                              mla_kv_norm[l], w_mla_ukv[l], w_mla_o[l], conv_w[l], conv_b[l], conv_ln_g[l],
                              conv_ln_b[l], w_conv_o[l], w_out[l], ln_mix_g[l], ln_mix_b[l])
        x, xb, xp = _cross_attention(x, xb, mem_b, w_xq[l], w_xk[l], w_xv[l], w_xo[l], ln_x_g[l], ln_x_b[l])
        x, xb = _hier_moe(x, xb, xp, w_group[l], b_group[l], w_router[l], b_router[l], w_expert_in,
                          w_expert_out, l, ln_ffn_g[l], ln_ffn_b[l])
    return x[None]
```

```python
import functools

import numpy as np
import jax
import jax.numpy as jnp
from jax import lax
from jax.experimental import pallas as pl
from jax.experimental.pallas import tpu as pltpu

F32 = jnp.float32
BF16 = jnp.bfloat16
I32 = jnp.int32

D_MODEL = 2048
NSA_HEADS = 8
NSA_GROUPS = 2
NSA_HPG = NSA_HEADS // NSA_GROUPS
NSA_DH = 64
CMP_LEN = 32
CMP_STRIDE = 16
CMP_HIDDEN = 128
SEL_LEN = 64
SEL_TOPN = 16
WINDOW = 512
FORCE_SCORE = 1.0e4
POOL_GROUPS = 4
POOL_WINDOWS = (2, 4, 8, 16)
POOL_WIDTH = 512
POOL_GW = POOL_WIDTH // POOL_GROUPS
MLA_HEADS = 8
MLA_Q_RANK = 512
MLA_KV_RANK = 256
MLA_NOPE = 64
MLA_ROPE = 32
MLA_DV = 64
ROPE_BASE = 10000.0
CONV_CH = 512
CONV_K = 31
N_BRANCH = 4
X_HEADS = 4
X_DH = 128
MOE_GROUPS = 4
MOE_EPG = 8
N_EXPERTS = MOE_GROUPS * MOE_EPG
MOE_TOPK = 2
EXPERT_FF = 512
MOE_BLOCK = 256
LN_EPS = 1e-5
RMS_EPS = 1e-6
DEPTH = 2
DN_ALPHA = (2 * DEPTH) ** 0.25
IN_SIZES = (NSA_HEADS * NSA_DH, 6 * NSA_GROUPS * NSA_DH, 3 * NSA_HEADS, POOL_WIDTH,
            MLA_Q_RANK, MLA_KV_RANK, MLA_ROPE, 2 * CONV_CH, N_BRANCH * D_MODEL)

LANE = 128
VMEM_LIMIT = 56 * 1024 * 1024
NEG = -1.0e30
FLASH_TQ = 1024
FLASH_TK = 1024
LOG2E = 1.4426950408889634

TM_PROJ = 2048
TN_PROJ = 7 * LANE
TN_MERGE = 1024
TQ_CMP = 256
TM_ROW = 512
TM_MERGE = 256
TM_COMBINE = 256

SLOT_Q = 0
SLOT_KV = 8
SLOT_POOL = 20
SLOT_CQ = 24
SLOT_CKV = 28
SLOT_KR = 30
SLOT_CONV = 32
SLOT_GATE = 40
N_SLOTS1 = 42
N1 = N_SLOTS1 * LANE


def _cparams(sem, vmem=VMEM_LIMIT):
    return pltpu.CompilerParams(dimension_semantics=sem, vmem_limit_bytes=vmem)


def _sigmoid(x):
    return 0.5 * jnp.tanh(0.5 * x) + 0.5


def _layer_norm(z, g, b):
    mu = jnp.mean(z, -1, keepdims=True)
    d = z - mu
    var = jnp.mean(d * d, -1, keepdims=True)
    return d * lax.rsqrt(var + LN_EPS) * g + b


U32 = jnp.uint32


def _pack_rows(y):
    n = y.shape[1] // 2
    bits = lax.bitcast_convert_type(y.astype(BF16).astype(F32), U32)
    return (bits[:, n:] & jnp.uint32(0xFFFF0000)) | (bits[:, :n] >> 16)


def _unpack_rows(w):
    lo = lax.bitcast_convert_type(w << 16, F32)
    hi = lax.bitcast_convert_type(w & jnp.uint32(0xFFFF0000), F32)
    return jnp.concatenate([lo, hi], axis=1)


def _dot(a, b):
    return jnp.dot(a, b, preferred_element_type=F32)


def _dot_nt(a, b):
    return lax.dot_general(a, b, (((1,), (1,)), ((), ())), preferred_element_type=F32)


def _dot3(a, b):
    hi = a.astype(BF16)
    r1 = a - hi.astype(F32)
    mid = r1.astype(BF16)
    lo = (r1 - mid.astype(F32)).astype(BF16)
    return _dot(hi, b) + _dot(mid, b) + _dot(lo, b)


def _mm_kernel(a_ref, b_ref, bias_ref, o_ref):
    o_ref[...] = (_dot(a_ref[...], b_ref[...]) + bias_ref[...]).astype(o_ref.dtype)


def _matmul(a, b, bias, tm, tn, out_dtype):
    m, k = a.shape
    n = b.shape[1]
    return pl.pallas_call(
        _mm_kernel,
        grid=(m // tm, n // tn),
        in_specs=[pl.BlockSpec((tm, k), lambda i, j: (i, 0)),
                  pl.BlockSpec((k, tn), lambda i, j: (0, j)),
                  pl.BlockSpec((1, tn), lambda i, j: (0, j))],
        out_specs=pl.BlockSpec((tm, tn), lambda i, j: (i, j)),
        out_shape=jax.ShapeDtypeStruct((m, n), out_dtype),
        compiler_params=_cparams(("parallel", "arbitrary")),
        name="matmul",
    )(a, b, bias)


def _mm_ln_kernel(a_ref, w_ref, x_ref, g_ref, b_ref, xo_ref, xb_ref):
    h = _dot(a_ref[...], w_ref[...])
    y = _layer_norm(DN_ALPHA * x_ref[...] + h, g_ref[...], b_ref[...])
    xo_ref[...] = y
    xb_ref[...] = y.astype(BF16)


def _matmul_res_ln(a, w, x, g, b, tm):
    m, k = a.shape
    d = w.shape[1]
    return pl.pallas_call(
        _mm_ln_kernel,
        grid=(m // tm,),
        in_specs=[pl.BlockSpec((tm, k), lambda i: (i, 0)),
                  pl.BlockSpec((k, d), lambda i: (0, 0)),
                  pl.BlockSpec((tm, d), lambda i: (i, 0)),
                  pl.BlockSpec((1, d), lambda i: (0, 0)),
                  pl.BlockSpec((1, d), lambda i: (0, 0))],
        out_specs=[pl.BlockSpec((tm, d), lambda i: (i, 0)),
                   pl.BlockSpec((tm, d), lambda i: (i, 0))],
        out_shape=[jax.ShapeDtypeStruct((m, d), F32), jax.ShapeDtypeStruct((m, d), BF16)],
        compiler_params=_cparams(("parallel",)),
        name="matmul_res_ln",
    )(a, w, x, g, b)


def _compress_kernel(a_ref, w1_ref, pos_ref, w2_ref, o_ref):
    nch = a_ref.shape[1]
    hh = _dot(a_ref[0], w1_ref[0])
    pp = _dot(pos_ref[0], w1_ref[0])
    pos_term = pp[0:1, :CMP_HIDDEN] + pp[1:2, CMP_HIDDEN:]
    h2_next = pltpu.roll(hh[:, CMP_HIDDEN:], nch - 1, 0)
    z = hh[:, :CMP_HIDDEN] + h2_next + pos_term
    hid = 0.5 * z * (1.0 + jnp.tanh(0.7978845608028654 * (z + 0.044715 * z * z * z)))
    o_ref[0] = _dot(hid.astype(BF16), w2_ref[0]).astype(o_ref.dtype)


def _nsa_compress(chunks, w1cat, pos2, w2pad):
    n4, nch, kdim = chunks.shape
    return pl.pallas_call(
        _compress_kernel,
        grid=(n4,),
        in_specs=[pl.BlockSpec((1, nch, kdim), lambda c: (c, 0, 0)),
                  pl.BlockSpec((1, kdim, 2 * CMP_HIDDEN), lambda c: (c // NSA_GROUPS, 0, 0)),
                  pl.BlockSpec((1, 8, kdim), lambda c: (c // NSA_GROUPS, 0, 0)),
                  pl.BlockSpec((1, CMP_HIDDEN, LANE), lambda c: (c // NSA_GROUPS, 0, 0))],
        out_specs=pl.BlockSpec((1, nch, LANE), lambda c: (c, 0, 0)),
        out_shape=jax.ShapeDtypeStruct((n4, nch, LANE), BF16),
        compiler_params=_cparams(("arbitrary",)),
        name="nsa_compress",
    )(chunks, w1cat, pos2, w2pad)


CMP_WIDTH_STEPS = 4


def _nsa_cmp_kernel(q_ref, kc_ref, vc_ref, mband_ref, ocmp_ref, selneg_ref, score_sc, *, tq, n_cmp, n_sel, top_n):
    t0 = pl.program_id(0) * tq
    nch = kc_ref.shape[1]
    row = lax.broadcasted_iota(I32, (NSA_HPG * tq, 1), 0)
    t_row = t0 + (row & (tq - 1))

    def softmax_part(width):
        col = lax.broadcasted_iota(I32, (1, width), 1)
        vis = (col * CMP_STRIDE + (CMP_LEN - 1) <= t_row) & (col < n_cmp)
        for g in range(NSA_GROUPS):
            qs = jnp.concatenate(
                [q_ref[:, (g * NSA_HPG + h) * LANE:(g * NSA_HPG + h + 1) * LANE] for h in range(NSA_HPG)], axis=0)
            s = _dot_nt(qs, kc_ref[g, 0:width, :])
            s = jnp.where(vis, s, NEG)
            m = jnp.max(s, -1, keepdims=True)
            p = jnp.where(vis, jnp.exp2(s - m), 0.0)
            l = jnp.sum(p, -1, keepdims=True)
            p = p * (1.0 / jnp.maximum(l, 1e-30))
            o = _dot(p.astype(BF16), vc_ref[g, 0:width, :])
            for h in range(NSA_HPG):
                hh = g * NSA_HPG + h
                ocmp_ref[:, hh * NSA_DH:(hh + 1) * NSA_DH] = o[h * tq:(h + 1) * tq, :NSA_DH]
            imp = p[0:tq]
            for h in range(1, NSA_HPG):
                imp = imp + p[h * tq:(h + 1) * tq]
            score_sc[g * tq:(g + 1) * tq, :] = _dot3(imp, mband_ref[0:width, :])

    last_vis = (t0 + tq - CMP_LEN) // CMP_STRIDE
    step_w = nch // CMP_WIDTH_STEPS
    variant = jnp.minimum(last_vis // step_w, CMP_WIDTH_STEPS - 1)
    for v in range(CMP_WIDTH_STEPS):
        pl.when(variant == v)(functools.partial(softmax_part, (v + 1) * step_w))

    blk = lax.broadcasted_iota(I32, (NSA_GROUPS * tq, n_sel), 1).astype(F32)
    row2 = lax.broadcasted_iota(I32, (NSA_GROUPS * tq, 1), 0)
    cur = ((t0 + (row2 & (tq - 1))) // SEL_LEN).astype(F32)
    valid = blk <= cur
    forced = (blk == 0.0) | (blk == cur) | (blk == cur - 1.0)
    assert FORCE_SCORE > NSA_HPG * (SEL_LEN // CMP_STRIDE + CMP_LEN // CMP_STRIDE - 1) and top_n > 3
    score = jnp.where(forced, -2.0, score_sc[...])
    score = jnp.where(valid, score, -1.0)

    def pick_one(_, sc):
        mx = jnp.max(sc, -1, keepdims=True)
        first = jnp.min(jnp.where(sc == mx, blk, float(n_sel)), -1, keepdims=True)
        return jnp.where(blk == first, -2.0, sc)

    sc = lax.fori_loop(0, top_n - 3, pick_one, score)
    selneg = jnp.where(valid & (sc == -2.0), 0.0, -1.0).astype(selneg_ref.dtype)
    for g in range(NSA_GROUPS):
        selneg_ref[:, g * n_sel:(g + 1) * n_sel] = selneg[g * tq:(g + 1) * tq]


def _nsa_cmp(u1, kcvc, mband, tq):
    s_len = u1.shape[0]
    nch = kcvc.shape[1]
    n_sel = s_len // SEL_LEN
    n_cmp = (s_len - CMP_LEN) // CMP_STRIDE + 1
    kern = functools.partial(_nsa_cmp_kernel, tq=tq, n_cmp=n_cmp, n_sel=n_sel, top_n=min(SEL_TOPN, n_sel))
    return pl.pallas_call(
        kern,
        grid=(s_len // tq,),
        in_specs=[pl.BlockSpec((tq, NSA_HEADS * LANE), lambda i: (i, SLOT_Q // NSA_HEADS)),
                  pl.BlockSpec((NSA_GROUPS, nch, LANE), lambda i: (0, 0, 0)),
                  pl.BlockSpec((NSA_GROUPS, nch, LANE), lambda i: (1, 0, 0)),
                  pl.BlockSpec((nch, n_sel), lambda i: (0, 0))],
        out_specs=[pl.BlockSpec((tq, NSA_HEADS * NSA_DH), lambda i: (i, 0)),
                   pl.BlockSpec((tq, NSA_GROUPS * n_sel), lambda i: (i, 0))],
        out_shape=[jax.ShapeDtypeStruct((s_len, NSA_HEADS * NSA_DH), F32),
                   jax.ShapeDtypeStruct((s_len, NSA_GROUPS * n_sel), BF16)],
        scratch_shapes=[pltpu.VMEM((NSA_GROUPS * tq, n_sel), F32)],
        compiler_params=_cparams(("parallel",)),
        name="nsa_cmp_topk",
    )(u1, kcvc, kcvc, mband)


def _flash_kernel(qi_ref, kj_ref, first_ref, last_ref, mode_ref, *refs, hpg, hpk, select, modes, dv,
                  split_masked):
    if select:
        q_ref, k_ref, v_ref, sel_ref, scat_ref, eslot_ref, o_ref, m_sc, acc_sc = refs
    else:
        q_ref, k_ref, v_ref, o_ref, m_sc, acc_sc = refs
    p_idx = pl.program_id(1)
    tq = q_ref.shape[0]
    tk = k_ref.shape[0]

    @pl.when(first_ref[p_idx] == 1)
    def _():
        m_sc[...] = jnp.full(m_sc.shape, NEG, F32)
        acc_sc[...] = jnp.zeros(acc_sc.shape, F32)

    def step(mask_mode):
        k_all = k_ref[...]
        lane = lax.broadcasted_iota(I32, (1, v_ref.shape[1]), 1)
        v_all = v_ref[...] + ((lane & (LANE - 1)) == dv).astype(BF16)
        if select:
            nk = hpg // hpk
            n_sel = sel_ref.shape[1] // nk
            k_all = k_all + jnp.concatenate([eslot_ref[...]] * nk, axis=1)
            scat = scat_ref[kj_ref[p_idx]]
            biases = [(_dot(sel_ref[:, g * n_sel:(g + 1) * n_sel], scat) * (-NEG)).astype(BF16) for g in range(nk)]
        hq, hk_ = tq // 2, tk // 2
        if mask_mode == 0 or not split_masked:
            blocks = ((0, tq, 0, tk),)
        else:
            blocks = {1: ((0, hq, 0, hk_), (hq, tq, 0, tk)), 2: ((0, hq, 0, tk), (hq, tq, hk_, tk))}[mask_mode]
        for r0, r1, c0, c1 in blocks:
            if mask_mode:
                r = lax.broadcasted_iota(I32, (r1 - r0, c1 - c0), 0) + r0
                c = lax.broadcasted_iota(I32, (r1 - r0, c1 - c0), 1) + c0
                keep = (c <= r) if mask_mode == 1 else (c > r)
            for h in range(hpg):
                hk = h // hpk
                q = q_ref[r0:r1, h * LANE:(h + 1) * LANE]
                if select:
                    q = q + biases[hk][r0:r1]
                s = _dot_nt(q, k_all[c0:c1, hk * LANE:(hk + 1) * LANE])
                if mask_mode:
                    s = jnp.where(keep, s, NEG)
                m_prev = m_sc[h, r0:r1]
                m_new = jnp.maximum(m_prev, jnp.max(s, -1, keepdims=True))
                p = jnp.exp2(s - jnp.concatenate([m_new] * ((c1 - c0) // LANE), axis=1))
                acc_sc[h, r0:r1] = (jnp.exp2(m_prev - m_new) * acc_sc[h, r0:r1]
                                    + _dot(p.astype(BF16), v_all[c0:c1, hk * LANE:(hk + 1) * LANE]))
                m_sc[h, r0:r1] = m_new

    for mm in modes:
        pl.when(mode_ref[p_idx] == mm)(functools.partial(step, mm))

    @pl.when(last_ref[p_idx] == 1)
    def _():
        for h in range(hpg):
            acc = acc_sc[h]
            o = acc[:, :dv] * (1.0 / acc[:, dv:dv + 1])
            o_ref[:, h * dv:(h + 1) * dv] = o.astype(o_ref.dtype)


def _pair_tables(n_tiles, window):
    rows = []
    for i in range(n_tiles):
        js = ([i - 1] if i > 0 else []) + [i] if window else list(range(i + 1))
        for j in js:
            mode = 1 if j == i else (2 if window else 0)
            rows.append((i, j, int(j == js[0]), int(j == i), mode))
    tab = np.asarray(rows, np.int32)
    return [jnp.asarray(tab[:, c]) for c in range(tab.shape[1])]


def _flash(q_arr, q_blk0, k_arr, k_blk0, v_arr, v_blk0, *, n_groups, hpg, hpk, window,
           out_dtype, dv, tq, tk, sel=None, scat=None, eslot=None):
    s_len = q_arr.shape[0]
    assert tq == tk and (not window or tk == WINDOW)
    nk = hpg // hpk
    select = sel is not None
    tables = _pair_tables(s_len // tq, window)
    n_pairs = int(tables[0].shape[0])
    modes = (1, 2) if window else (0, 1)

    def qmap(g, p, qi, kj, *_):
        return (qi[p], q_blk0 + g)

    def kmap(g, p, qi, kj, *_):
        return (kj[p], k_blk0 + g)

    def vmap_(g, p, qi, kj, *_):
        return (kj[p], v_blk0 + g)

    def omap(g, p, qi, kj, *_):
        return (qi[p], g)

    in_specs = [pl.BlockSpec((tq, hpg * LANE), qmap),
                pl.BlockSpec((tk, nk * LANE), kmap),
                pl.BlockSpec((tk, nk * LANE), vmap_)]
    args = [q_arr, k_arr, v_arr]
    if select:
        in_specs += [pl.BlockSpec((tq, sel.shape[1] // n_groups), omap),
                     pl.BlockSpec(scat.shape, lambda g, p, *_: (0, 0, 0)),
                     pl.BlockSpec(eslot.shape, lambda g, p, *_: (0, 0))]
        args += [sel, scat, eslot]
    kern = functools.partial(_flash_kernel, hpg=hpg, hpk=hpk, select=select, modes=modes, dv=dv,
                             split_masked=tq >= 2 * WINDOW)
    grid_spec = pltpu.PrefetchScalarGridSpec(
        num_scalar_prefetch=len(tables),
        grid=(n_groups, n_pairs),
        in_specs=in_specs,
        out_specs=pl.BlockSpec((tq, hpg * dv), omap),
        scratch_shapes=[pltpu.VMEM((hpg, tq, LANE), F32), pltpu.VMEM((hpg, tq, LANE), F32)],
    )
    return pl.pallas_call(
        kern,
        grid_spec=grid_spec,
        out_shape=jax.ShapeDtypeStruct((s_len, n_groups * hpg * dv), out_dtype),
        compiler_params=_cparams(("parallel", "arbitrary")),
        name="flash_sel" if select else ("flash_win" if window else "flash_causal"),
    )(*tables, *args)


def _mla_prep_kernel(cq_ref, ckv_ref, kr_ref, pos_ref, inv_ref, qn_ref, kvn_ref, wuq_ref, wk_ref, wv_ref,
                     q_ref, k_ref, v_ref):
    def rms(x, g):
        return x * lax.rsqrt(jnp.mean(x * x, -1, keepdims=True) + RMS_EPS) * g

    ang = pos_ref[...] * inv_ref[...]
    cos, sin = jnp.cos(ang), jnp.sin(ang)
    hw = MLA_HEADS * LANE
    qh = _dot(rms(cq_ref[...].astype(F32), qn_ref[...]).astype(BF16), wuq_ref[...])
    scale = (MLA_NOPE + MLA_ROPE) ** -0.5 * LOG2E
    cos_t = jnp.concatenate([cos] * MLA_HEADS, axis=1)
    sin_t = jnp.concatenate([sin] * MLA_HEADS, axis=1)
    q_ref[...] = ((qh[:, :hw] * cos_t + qh[:, hw:] * sin_t) * scale).astype(BF16)
    ckv = rms(ckv_ref[...].astype(F32), kvn_ref[...]).astype(BF16)
    kr = kr_ref[...].astype(F32)
    k_rope = kr[:, :LANE] * cos + kr[:, LANE:] * sin
    k_ref[...] = (_dot(ckv, wk_ref[...]) + jnp.concatenate([k_rope] * MLA_HEADS, axis=1)).astype(BF16)
    v_ref[...] = _dot(ckv, wv_ref[...]).astype(BF16)


def _mla_prep(u1, pos_col, inv_slot, qn, kvn, wuq, wk, wv, tm):
    s_len = u1.shape[0]
    hw = MLA_HEADS * LANE
    full = lambda a: pl.BlockSpec(a.shape, lambda i: (0,) * a.ndim)
    out = jax.ShapeDtypeStruct((s_len, hw), BF16)
    return pl.pallas_call(
        _mla_prep_kernel,
        grid=(s_len // tm,),
        in_specs=[pl.BlockSpec((tm, MLA_Q_RANK), lambda i: (i, SLOT_CQ * LANE // MLA_Q_RANK)),
                  pl.BlockSpec((tm, MLA_KV_RANK), lambda i: (i, SLOT_CKV * LANE // MLA_KV_RANK)),
                  pl.BlockSpec((tm, 2 * LANE), lambda i: (i, SLOT_KR // 2)),
                  pl.BlockSpec((tm, 1), lambda i: (i, 0)),
                  full(inv_slot), full(qn), full(kvn), full(wuq), full(wk), full(wv)],
        out_specs=[pl.BlockSpec((tm, hw), lambda i: (i, 0))] * 3,
        out_shape=[out, out, out],
        compiler_params=_cparams(("parallel",)),
        name="mla_prep",
    )(u1, u1, u1, pos_col, inv_slot, qn, kvn, wuq, wk, wv)


POOL_HALO = 16


def _pool_kernel(cur_ref, halo_ref, wp_ref, scale_ref, o_ref):
    i = pl.program_id(0)
    tm = cur_ref.shape[0]
    halo = jnp.where(i > 0, halo_ref[...].astype(F32), 0.0)
    x = jnp.concatenate([halo, cur_ref[...].astype(F32)], axis=0)
    sums = {1: x}
    w = 1
    while w < max(POOL_WINDOWS):
        a = sums[w]
        sums[2 * w] = a[w:] + a[:-w]
        w *= 2
    t = (i * tm + lax.broadcasted_iota(I32, (tm, 1), 0) + 1).astype(F32)
    outs = []
    for gi, w in enumerate(POOL_WINDOWS):
        lo = gi * POOL_GW
        start = POOL_HALO - (w - 1)
        win = sums[w][start:start + tm, lo:lo + POOL_GW]
        mean = win / jnp.minimum(t, float(w))
        pooled = mean - x[POOL_HALO:, lo:lo + POOL_GW]
        outs.append(_dot(pooled.astype(BF16), wp_ref[gi]))
    o_ref[...] = (jnp.concatenate(outs, axis=1) * scale_ref[...]).astype(o_ref.dtype)


def _pool(u1, w_pool, pool_scale, tm):
    s_len = u1.shape[0]
    blk = SLOT_POOL * LANE // POOL_WIDTH
    return pl.pallas_call(
        _pool_kernel,
        grid=(s_len // tm,),
        in_specs=[pl.BlockSpec((tm, POOL_WIDTH), lambda i: (i, blk)),
                  pl.BlockSpec((POOL_HALO, POOL_WIDTH),
                               lambda i: (jnp.maximum(i * (tm // POOL_HALO) - 1, 0), blk)),
                  pl.BlockSpec(w_pool.shape, lambda i: (0, 0, 0)),
                  pl.BlockSpec((1, POOL_WIDTH), lambda i: (0, 0))],
        out_specs=pl.BlockSpec((tm, POOL_WIDTH), lambda i: (i, 0)),
        out_shape=jax.ShapeDtypeStruct((s_len, POOL_WIDTH), BF16),
        compiler_params=_cparams(("parallel",)),
        name="pool",
    )(u1, u1, w_pool, pool_scale)


CONV_HALO = 32


def _conv_kernel(cur_ref, halo_ref, w_ref, b_ref, g_ref, beta_ref, o_ref, hbuf):
    i = pl.program_id(0)
    tm = cur_ref.shape[0]

    def glu(u):
        u = u.astype(F32)
        return u[:, :CONV_CH] * _sigmoid(u[:, CONV_CH:])

    hbuf[0:CONV_HALO, :] = jnp.where(i > 0, glu(halo_ref[...]), 0.0)
    hbuf[CONV_HALO:, :] = glu(cur_ref[...])
    acc = jnp.zeros((tm, CONV_CH), F32) + b_ref[...]
    for k in range(CONV_K):
        off = CONV_HALO - (CONV_K - 1) + k
        acc = acc + hbuf[off:off + tm, :] * w_ref[k:k + 1, :]
    y = _layer_norm(acc, g_ref[...], beta_ref[...])
    o_ref[...] = (y * _sigmoid(y)).astype(o_ref.dtype)


def _conv(u1, conv_w, conv_b, ln_g, ln_b, tm):
    s_len = u1.shape[0]
    blk = SLOT_CONV * LANE // (2 * CONV_CH)
    row = lambda a: pl.BlockSpec(a.shape, lambda i: (0, 0))
    return pl.pallas_call(
        _conv_kernel,
        grid=(s_len // tm,),
        in_specs=[pl.BlockSpec((tm, 2 * CONV_CH), lambda i: (i, blk)),
                  pl.BlockSpec((CONV_HALO, 2 * CONV_CH),
                               lambda i: (jnp.maximum(i * (tm // CONV_HALO) - 1, 0), blk)),
                  row(conv_w), row(conv_b), row(ln_g), row(ln_b)],
        out_specs=pl.BlockSpec((tm, CONV_CH), lambda i: (i, 0)),
        out_shape=jax.ShapeDtypeStruct((s_len, CONV_CH), BF16),
        scratch_shapes=[pltpu.VMEM((tm + CONV_HALO, CONV_CH), F32)],
        compiler_params=_cparams(("parallel",)),
        name="conv_module",
    )(u1, u1, conv_w, conv_b, ln_g, ln_b)


def _merge_kernel(ocmp_ref, osel_ref, owin_ref, gate_ref, gexp_ref, pool_ref, mla_ref, conv_ref, um_ref,
                  wn_ref, wp_ref, wm_ref, wc_ref, o_ref):
    sg = _sigmoid(gate_ref[...].astype(F32))
    nsa = (_dot3(sg, gexp_ref[0]) * ocmp_ref[...] + _dot3(sg, gexp_ref[1]) * osel_ref[...]
           + _dot3(sg, gexp_ref[2]) * owin_ref[...])
    branches = (_dot(nsa.astype(BF16), wn_ref[...]), _dot(pool_ref[...], wp_ref[...]),
                _dot(mla_ref[...], wm_ref[...]), _dot(conv_ref[...], wc_ref[...]))
    merged = None
    for j, br in enumerate(branches):
        term = _sigmoid(um_ref[:, j * D_MODEL:(j + 1) * D_MODEL].astype(F32)) * br
        merged = term if merged is None else merged + term
    o_ref[...] = merged.astype(o_ref.dtype)


def _merge(o_cmp, o_sel, o_win, u1, gexp, pooled, o_mla, conv, um, wn, wp, wm, wc, tm):
    s_len = u1.shape[0]
    tile = lambda w: pl.BlockSpec((tm, w), lambda i: (i, 0))
    full = lambda a: pl.BlockSpec(a.shape, lambda i: (0,) * a.ndim)
    return pl.pallas_call(
        _merge_kernel,
        grid=(s_len // tm,),
        in_specs=[tile(512), tile(512), tile(512),
                  pl.BlockSpec((tm, LANE), lambda i: (i, SLOT_GATE)),
                  full(gexp), tile(512), tile(512), tile(512), tile(N_BRANCH * D_MODEL),
                  full(wn), full(wp), full(wm), full(wc)],
        out_specs=tile(D_MODEL),
        out_shape=jax.ShapeDtypeStruct((s_len, D_MODEL), BF16),
        compiler_params=_cparams(("parallel",)),
        name="branch_merge",
    )(o_cmp, o_sel, o_win, u1, gexp, pooled, o_mla, conv, um, wn, wp, wm, wc)


def _xattn_kernel(xb_ref, x_ref, wq_ref, k_ref, v_ref, wo_ref, g_ref, b_ref, xo_ref, xob_ref, xp_ref):
    q = _dot(xb_ref[...], wq_ref[...]).astype(BF16)
    k = k_ref[...]
    v = v_ref[...]
    outs = []
    for h in range(X_HEADS):
        sl = slice(h * X_DH, (h + 1) * X_DH)
        s = _dot_nt(q[:, sl], k[:, sl]) * (X_DH ** -0.5)
        m = jnp.max(s, -1, keepdims=True)
        p = jnp.exp(s - m)
        p = p * (1.0 / jnp.sum(p, -1, keepdims=True))
        outs.append(_dot(p.astype(BF16), v[:, sl]))
    o = jnp.concatenate(outs, axis=1).astype(BF16)
    y = _layer_norm(DN_ALPHA * x_ref[...] + _dot(o, wo_ref[...]), g_ref[...], b_ref[...])
    xo_ref[...] = y
    xob_ref[...] = y.astype(BF16)
    xp_ref[...] = _pack_rows(y)


def _xattn(xb, x, wq, k, v, wo, g, b, tm):
    s_len = x.shape[0]
    tile = lambda: pl.BlockSpec((tm, D_MODEL), lambda i: (i, 0))
    full = lambda a: pl.BlockSpec(a.shape, lambda i: (0,) * a.ndim)
    return pl.pallas_call(
        _xattn_kernel,
        grid=(s_len // tm,),
        in_specs=[tile(), tile(), full(wq), full(k), full(v), full(wo), full(g), full(b)],
        out_specs=[tile(), tile(), pl.BlockSpec((tm, D_MODEL // 2), lambda i: (i, 0))],
        out_shape=[jax.ShapeDtypeStruct((s_len, D_MODEL), F32), jax.ShapeDtypeStruct((s_len, D_MODEL), BF16),
                   jax.ShapeDtypeStruct((s_len, D_MODEL // 2), U32)],
        compiler_params=_cparams(("parallel",)),
        name="cross_attention_ln",
    )(xb, x, wq, k, v, wo, g, b)


ROUTE_OFF = MOE_GROUPS


def _router_kernel(xb_ref, w_ref, b_ref, tri_ref, info_ref, cnt_ref, carry):
    @pl.when(pl.program_id(0) == 0)
    def _():
        carry[...] = jnp.zeros(carry.shape, F32)

    logits = _dot(xb_ref[...], w_ref[...]) + b_ref[...]
    lane = lax.broadcasted_iota(I32, logits.shape, 1).astype(F32)
    is_g = lane < float(MOE_GROUPS)
    neg_inf = -jnp.inf
    gl = jnp.where(is_g, logits, neg_inf)
    gmax = jnp.max(gl, -1, keepdims=True)
    g_sel = jnp.min(jnp.where(gl == gmax, lane, float(LANE)), -1, keepdims=True)
    g_w = 1.0 / jnp.sum(jnp.where(is_g, jnp.exp(gl - gmax), 0.0), -1, keepdims=True)
    lo = ROUTE_OFF + MOE_EPG * g_sel
    in_g = (lane >= lo) & (lane < lo + MOE_EPG)
    el = jnp.where(in_g, logits, neg_inf)
    emax = jnp.max(el, -1, keepdims=True)
    e = jnp.where(in_g, jnp.exp(el - emax), 0.0)
    p = e / jnp.sum(e, -1, keepdims=True)
    pm = jnp.where(in_g, p, -1.0)
    p1 = jnp.max(pm, -1, keepdims=True)
    i1 = jnp.min(jnp.where(pm == p1, lane, float(LANE)), -1, keepdims=True)
    pm2 = jnp.where(lane == i1, -1.0, pm)
    p2 = jnp.max(pm2, -1, keepdims=True)
    i2 = jnp.min(jnp.where(pm2 == p2, lane, float(LANE)), -1, keepdims=True)
    denom = p1 + p2
    gate1 = g_w * p1 / denom
    gate2 = g_w * p2 / denom

    oh1 = (lane == i1).astype(BF16)
    oh2 = (lane == i2).astype(BF16)
    incl1 = _dot(tri_ref[...], oh1)
    incl2 = _dot(tri_ref[...], oh2)
    tot1 = jnp.sum(oh1.astype(F32), 0, keepdims=True)
    tot2 = jnp.sum(oh2.astype(F32), 0, keepdims=True)
    base = carry[...]
    rank1 = jnp.sum(jnp.where(lane == i1, base + incl1, 0.0), -1, keepdims=True) - 1.0
    rank2 = jnp.sum(jnp.where(lane == i2, base + tot1 + incl2, 0.0), -1, keepdims=True) - 1.0
    carry[...] = base + tot1 + tot2
    cnt_ref[...] = carry[...]

    cols = (i1 - ROUTE_OFF, i2 - ROUTE_OFF, gate1, gate2, rank1, rank2)
    info = jnp.zeros(logits.shape, F32)
    for c, val in enumerate(cols):
        info = jnp.where(lane == float(c), val, info)
    info_ref[...] = info


def _router(xb, w_gr, b_gr, tri, tm):
    t_len = xb.shape[0]
    return pl.pallas_call(
        _router_kernel,
        grid=(t_len // tm,),
        in_specs=[pl.BlockSpec((tm, D_MODEL), lambda i: (i, 0)),
                  pl.BlockSpec(w_gr.shape, lambda i: (0, 0)),
                  pl.BlockSpec((1, LANE), lambda i: (0, 0)),
                  pl.BlockSpec((tm, tm), lambda i: (0, 0))],
        out_specs=[pl.BlockSpec((tm, LANE), lambda i: (i, 0)),
                   pl.BlockSpec((1, LANE), lambda i: (0, 0))],
        out_shape=[jax.ShapeDtypeStruct((t_len, LANE), F32), jax.ShapeDtypeStruct((1, LANE), F32)],
        scratch_shapes=[pltpu.VMEM((1, LANE), F32)],
        compiler_params=_cparams(("arbitrary",)),
        name="moe_router",
    )(xb, w_gr, b_gr, tri)


def _row_copy(src, src_row, dst, dst_row, sem):
    return pltpu.make_async_copy(src.at[pl.ds(src_row, 1)], dst.at[pl.ds(dst_row, 1)], sem)


def _dispatch_kernel(pos_ref, x_ref, xe_in_hbm, xe_hbm, sem, *, td):
    del xe_in_hbm
    base = pl.program_id(0) * td

    def issue(t, c):
        for k in range(MOE_TOPK):
            _row_copy(x_ref, t, xe_hbm, pos_ref[MOE_TOPK * (base + t) + k], sem).start()
        return c

    lax.fori_loop(0, td, issue, 0, unroll=8)
    all_rows = xe_hbm.at[pl.ds(0, MOE_TOPK * td)]
    pltpu.make_async_copy(all_rows, all_rows, sem).wait()


def _dispatch(pos_flat, x, rows, td):
    t_len, d = x.shape
    zeros = jnp.zeros((rows, d), x.dtype)
    grid_spec = pltpu.PrefetchScalarGridSpec(
        num_scalar_prefetch=1,
        grid=(t_len // td,),
        in_specs=[pl.BlockSpec((td, d), lambda i, pos: (i, 0)), pl.BlockSpec(memory_space=pl.ANY)],
        out_specs=pl.BlockSpec(memory_space=pl.ANY),
        scratch_shapes=[pltpu.SemaphoreType.DMA(())],
    )
    return pl.pallas_call(
        functools.partial(_dispatch_kernel, td=td),
        grid_spec=grid_spec,
        out_shape=jax.ShapeDtypeStruct((rows, d), x.dtype),
        input_output_aliases={2: 0},
        compiler_params=pltpu.CompilerParams(dimension_semantics=("arbitrary",)),
        name="moe_dispatch",
    )(pos_flat, x, zeros)


def _expert_weight_copies(win_hbm, wout_hbm, layer, expert, win_f32, wout_f32, sem):
    return (pltpu.make_async_copy(win_hbm.at[layer, expert], win_f32, sem.at[0]),
            pltpu.make_async_copy(wout_hbm.at[layer, expert], wout_f32, sem.at[1]))


def _expert_kernel(be_ref, next_ref, nused_ref, xe_ref, win_hbm, wout_hbm, yb_ref, win_f32, wout_f32, win_bf,
                   wout_bf, sem, *, layer):
    b = pl.program_id(0)
    used = b < nused_ref[0]
    new_expert = (b == 0) | (be_ref[b] != be_ref[jnp.maximum(b - 1, 0)])
    copies = functools.partial(_expert_weight_copies, win_hbm, wout_hbm, layer)

    @pl.when(b == 0)
    def _():
        for c in copies(be_ref[0], win_f32, wout_f32, sem):
            c.start()

    @pl.when(used & new_expert)
    def _():
        for c in copies(be_ref[b], win_f32, wout_f32, sem):
            c.wait()
        win_bf[...] = win_f32[...].astype(BF16)
        wout_bf[...] = wout_f32[...].astype(BF16)

        @pl.when(next_ref[b] >= 0)
        def _():
            for c in copies(next_ref[b], win_f32, wout_f32, sem):
                c.start()

    @pl.when(used)
    def _():
        hcat = _dot(_unpack_rows(xe_ref[...]).astype(BF16), win_bf[...])
        a = hcat[:, :EXPERT_FF]
        act = (a * _sigmoid(a) * hcat[:, EXPERT_FF:]).astype(BF16)
        yb_ref[...] = _pack_rows(_dot(act, wout_bf[...]))

    @pl.when(b >= nused_ref[0])
    def _():
        yb_ref[...] = jnp.zeros(yb_ref.shape, U32)


def _experts(block_expert, next_expert, n_used, xe, w_e_in, w_e_out, layer):
    rows, half = xe.shape
    d = 2 * half
    n_blocks = rows // MOE_BLOCK
    grid_spec = pltpu.PrefetchScalarGridSpec(
        num_scalar_prefetch=3,
        grid=(n_blocks,),
        in_specs=[pl.BlockSpec((MOE_BLOCK, half), lambda b, *_: (b, 0)),
                  pl.BlockSpec(memory_space=pl.ANY),
                  pl.BlockSpec(memory_space=pl.ANY)],
        out_specs=pl.BlockSpec((MOE_BLOCK, half), lambda b, *_: (b, 0)),
        scratch_shapes=[pltpu.VMEM((d, 2 * EXPERT_FF), F32), pltpu.VMEM((EXPERT_FF, d), F32),
                        pltpu.VMEM((d, 2 * EXPERT_FF), BF16), pltpu.VMEM((EXPERT_FF, d), BF16),
                        pltpu.SemaphoreType.DMA((2,))],
    )
    return pl.pallas_call(
        functools.partial(_expert_kernel, layer=layer),
        grid_spec=grid_spec,
        out_shape=jax.ShapeDtypeStruct((rows, half), U32),
        compiler_params=_cparams(("arbitrary",)),
        name="moe_experts",
    )(block_expert, next_expert, n_used, xe, w_e_in, w_e_out)


def _combine_kernel(pos_ref, yb_hbm, x_ref, info_ref, g_ref, b_ref, xo_ref, xob_ref, buf, sem):
    tm = x_ref.shape[0]
    i = pl.program_id(0)
    slot = i & 1

    def gather(tile, slot_):
        def issue(t, c):
            for k in range(MOE_TOPK):
                _row_copy(yb_hbm, pos_ref[MOE_TOPK * (tile * tm + t) + k], buf.at[slot_, k], t,
                          sem.at[slot_]).start()
            return c
        lax.fori_loop(0, tm, issue, 0, unroll=8)

    @pl.when(i == 0)
    def _():
        gather(0, 0)

    @pl.when(i + 1 < pl.num_programs(0))
    def _():
        gather(i + 1, 1 - slot)

    pltpu.make_async_copy(buf.at[slot], buf.at[slot], sem.at[slot]).wait()
    info = info_ref[...]
    y = info[:, 2:3] * _unpack_rows(buf[slot, 0]) + info[:, 3:4] * _unpack_rows(buf[slot, 1])
    z = _layer_norm(DN_ALPHA * x_ref[...] + y, g_ref[...], b_ref[...])
    xo_ref[...] = z
    xob_ref[...] = z.astype(BF16)


def _combine(pos_flat, yb, x, info, g, b, tm):
    t_len, d = x.shape
    grid_spec = pltpu.PrefetchScalarGridSpec(
        num_scalar_prefetch=1,
        grid=(t_len // tm,),
        in_specs=[pl.BlockSpec(memory_space=pl.ANY),
                  pl.BlockSpec((tm, d), lambda i, pos: (i, 0)),
                  pl.BlockSpec((tm, LANE), lambda i, pos: (i, 0)),
                  pl.BlockSpec((1, d), lambda i, pos: (0, 0)),
                  pl.BlockSpec((1, d), lambda i, pos: (0, 0))],
        out_specs=[pl.BlockSpec((tm, d), lambda i, pos: (i, 0)),
                   pl.BlockSpec((tm, d), lambda i, pos: (i, 0))],
        scratch_shapes=[pltpu.VMEM((2, MOE_TOPK, tm, d // 2), U32), pltpu.SemaphoreType.DMA((2,))],
    )
    return pl.pallas_call(
        _combine_kernel,
        grid_spec=grid_spec,
        out_shape=[jax.ShapeDtypeStruct((t_len, d), F32), jax.ShapeDtypeStruct((t_len, d), BF16)],
        compiler_params=_cparams(("arbitrary",)),
        name="moe_combine_ln",
    )(pos_flat, yb, x, info, g, b)


def _pad_cols(m, width):
    return jnp.pad(m, [(0, 0)] * (m.ndim - 1) + [(0, width - m.shape[-1])])


def _rot_half_cols(m):
    half = m.shape[-1] // 2
    return jnp.concatenate([-m[..., half:], m[..., :half]], -1)


def _layout_in1(m):
    offs = np.cumsum((0,) + IN_SIZES)
    o_q, o_kv, o_g, o_pool, o_cq, o_ckv, o_kr, o_conv = offs[:8]
    z = lambda n: jnp.zeros(m.shape[:-1] + (n,), m.dtype)
    parts = []
    for h in range(NSA_HEADS):
        parts += [m[..., o_q + h * NSA_DH:o_q + (h + 1) * NSA_DH] * (NSA_DH ** -0.5 * LOG2E), z(LANE - NSA_DH)]
    for c in range(6 * NSA_GROUPS):
        parts += [m[..., o_kv + c * NSA_DH:o_kv + (c + 1) * NSA_DH], z(LANE - NSA_DH)]
    parts += [m[..., o_pool:o_pool + POOL_WIDTH], m[..., o_cq:o_cq + MLA_Q_RANK], m[..., o_ckv:o_ckv + MLA_KV_RANK]]
    kr = m[..., o_kr:o_kr + MLA_ROPE]
    parts += [z(MLA_NOPE), kr, z(LANE - MLA_NOPE - MLA_ROPE), z(MLA_NOPE), _rot_half_cols(kr),
              z(LANE - MLA_NOPE - MLA_ROPE)]
    parts += [m[..., o_conv:o_conv + 2 * CONV_CH], _pad_cols(m[..., o_g:o_g + 3 * NSA_HEADS], LANE), z(LANE)]
    return jnp.concatenate(parts, -1)


def _layout_mla_q(w):
    dq = MLA_NOPE + MLA_ROPE
    z = lambda n: jnp.zeros((w.shape[0], n), w.dtype)
    a, b = [], []
    for h in range(MLA_HEADS):
        rope = w[:, h * dq + MLA_NOPE:(h + 1) * dq]
        a += [w[:, h * dq:h * dq + MLA_NOPE], rope, z(LANE - dq)]
        b += [z(MLA_NOPE), _rot_half_cols(rope), z(LANE - dq)]
    return jnp.concatenate(a + b, -1)


def _layout_mla_kv(w):
    dkv = MLA_NOPE + MLA_DV
    z = jnp.zeros((w.shape[0], LANE - MLA_NOPE), w.dtype)
    k, v = [], []
    for h in range(MLA_HEADS):
        k += [w[:, h * dkv:h * dkv + MLA_NOPE], z]
        v += [w[:, h * dkv + MLA_NOPE:(h + 1) * dkv], z]
    return jnp.concatenate(k, -1), jnp.concatenate(v, -1)


def _static_tables(s_len):
    nch = s_len // CMP_STRIDE
    n_cmp = (s_len - CMP_LEN) // CMP_STRIDE + 1
    n_sel = s_len // SEL_LEN
    ratio = SEL_LEN // CMP_STRIDE
    c = np.arange(nch)[:, None]
    j = np.arange(n_sel)[None, :]
    mband = ((c >= ratio * j - 1) & (c <= ratio * j + ratio - 1) & (c < n_cmp)).astype(np.float32)
    per_tile = FLASH_TK // SEL_LEN
    n_tiles = s_len // FLASH_TK
    scat = np.zeros((n_tiles, n_sel, LANE), np.float32)
    for b in range(n_sel):
        scat[b // per_tile, b, NSA_DH + b % per_tile] = 1.0
    eslot = np.zeros((FLASH_TK, LANE), np.float32)
    eslot[np.arange(FLASH_TK), NSA_DH + np.arange(FLASH_TK) // SEL_LEN] = 1.0
    gexp = np.zeros((3, LANE, NSA_HEADS * NSA_DH), np.float32)
    for h in range(NSA_HEADS):
        for jj in range(3):
            gexp[jj, h * 3 + jj, h * NSA_DH:(h + 1) * NSA_DH] = 1.0
    half = MLA_ROPE // 2
    inv = ROPE_BASE ** (-jnp.arange(half, dtype=F32) / half)
    inv_slot = jnp.concatenate([jnp.zeros((MLA_NOPE,), F32), inv, inv,
                                jnp.zeros((LANE - MLA_NOPE - MLA_ROPE,), F32)])[None, :]
    as_bf = lambda a: jnp.asarray(a, BF16)
    return as_bf(mband), as_bf(scat), as_bf(eslot), as_bf(gexp), inv_slot


def _hybrid_mixer(x, xb, pos_col, tabs, w_in, b_in, cmp_pos, cmp_w1, cmp_w2, w_nsa_o, w_pool, pool_scale, w_pool_o,
                  q_norm, w_uq, kv_norm, w_ukv, w_mla_o, conv_w, conv_b, conv_ln_g, conv_ln_b, w_conv_o, w_out,
                  ln_g, ln_b):
    s_len = x.shape[0]
    mband, scat, eslot, gexp, inv_slot = tabs
    row = lambda v: v[None, :]
    o_merge = int(sum(IN_SIZES[:8]))
    tm_proj = min(TM_PROJ, s_len)
    u1 = _matmul(xb, _layout_in1(w_in[:, :o_merge]).astype(BF16), _layout_in1(row(b_in[:o_merge])),
                 tm_proj, TN_PROJ, BF16)
    um = _matmul(xb, w_in[:, o_merge:].astype(BF16), row(b_in[o_merge:]), tm_proj, TN_MERGE, BF16)

    nch = s_len // CMP_STRIDE
    kdim = CMP_STRIDE * LANE
    chunks = u1[:, SLOT_KV * LANE:(SLOT_KV + 4) * LANE].reshape(nch, CMP_STRIDE, 4, LANE)
    chunks = chunks.transpose(2, 0, 1, 3).reshape(4, nch, kdim)
    w1 = _pad_cols(cmp_w1.reshape(2, CMP_LEN, NSA_DH, CMP_HIDDEN).transpose(0, 1, 3, 2), LANE)
    w1 = w1.transpose(0, 1, 3, 2)
    w1cat = jnp.concatenate([w1[:, :CMP_STRIDE].reshape(2, kdim, CMP_HIDDEN),
                             w1[:, CMP_STRIDE:].reshape(2, kdim, CMP_HIDDEN)], -1).astype(BF16)
    posp = _pad_cols(cmp_pos, LANE)
    pos2 = jnp.stack([posp[:, :CMP_STRIDE].reshape(2, kdim), posp[:, CMP_STRIDE:].reshape(2, kdim)], 1)
    pos2 = jnp.pad(pos2, ((0, 0), (0, 6), (0, 0))).astype(BF16)
    kcvc = _nsa_compress(chunks, w1cat, pos2, _pad_cols(cmp_w2, LANE).astype(BF16))
    o_cmp, selneg = _nsa_cmp(u1, kcvc, mband, TQ_CMP)
    nsa = dict(n_groups=1, hpg=NSA_HEADS, hpk=NSA_HPG, out_dtype=F32, dv=NSA_DH)
    o_sel = _flash(u1, SLOT_Q // NSA_HEADS, u1, (SLOT_KV + 4) // NSA_GROUPS, u1, (SLOT_KV + 6) // NSA_GROUPS,
                   window=False, tq=FLASH_TQ, tk=FLASH_TK, sel=selneg, scat=scat, eslot=eslot, **nsa)
    o_win = _flash(u1, SLOT_Q // NSA_HEADS, u1, (SLOT_KV + 8) // NSA_GROUPS, u1, (SLOT_KV + 10) // NSA_GROUPS,
                   window=True, tq=WINDOW, tk=WINDOW, **nsa)

    wk, wv = _layout_mla_kv(w_ukv)
    q_m, k_m, v_m = _mla_prep(u1, pos_col, inv_slot, row(q_norm), row(kv_norm), _layout_mla_q(w_uq).astype(BF16),
                              wk.astype(BF16), wv.astype(BF16), TM_ROW)
    o_mla = _flash(q_m, 0, k_m, 0, v_m, 0, n_groups=1, hpg=MLA_HEADS, hpk=1, window=False,
                   out_dtype=BF16, dv=MLA_DV, tq=FLASH_TQ, tk=FLASH_TK)

    pooled = _pool(u1, w_pool.astype(BF16), row(pool_scale), TM_ROW)
    conv = _conv(u1, conv_w, row(conv_b), row(conv_ln_g), row(conv_ln_b), TM_ROW)
    merged = _merge(o_cmp, o_sel, o_win, u1, gexp, pooled, o_mla, conv, um, w_nsa_o.astype(BF16),
                    w_pool_o.astype(BF16), w_mla_o.astype(BF16), w_conv_o.astype(BF16), TM_MERGE)
    return _matmul_res_ln(merged, w_out.astype(BF16), x, row(ln_g), row(ln_b), TM_ROW)


def _cross_attention(x, xb, mem_b, w_q, w_k, w_v, w_o, ln_g, ln_b):
    row = lambda v: v[None, :]
    kv = _matmul(mem_b, jnp.concatenate([w_k, w_v], 1).astype(BF16), jnp.zeros((1, 2 * X_HEADS * X_DH), F32),
                 mem_b.shape[0], 2 * X_HEADS * X_DH, BF16)
    hw = X_HEADS * X_DH
    return _xattn(xb, x, w_q.astype(BF16), kv[:, :hw], kv[:, hw:], w_o.astype(BF16), row(ln_g), row(ln_b), TM_ROW)


def _hier_moe(x, xb, xp, w_group, b_group, w_router, b_router, w_e_in, w_e_out, layer, ln_g, ln_b):
    t_len = x.shape[0]
    row = lambda v: v[None, :]
    tm_r = TM_ROW
    w_gr = _pad_cols(jnp.concatenate([w_group, w_router], 1), LANE).astype(BF16)
    b_gr = _pad_cols(row(jnp.concatenate([b_group, b_router])), LANE)
    tri = jnp.asarray(np.tril(np.ones((tm_r, tm_r), np.float32)), BF16)
    info, cnt = _router(xb, w_gr, b_gr, tri, tm_r)

    counts = cnt[0, ROUTE_OFF:ROUTE_OFF + N_EXPERTS].astype(I32)
    padded = (counts + MOE_BLOCK - 1) // MOE_BLOCK * MOE_BLOCK
    pend = jnp.cumsum(padded)
    pstart = pend - padded
    n_blocks = -(-(t_len * MOE_TOPK) // MOE_BLOCK) + N_EXPERTS
    n_used = pend[-1] // MOE_BLOCK
    blk_ids = jnp.minimum(jnp.arange(n_blocks), n_used - 1)
    owner = jnp.sum((pend[None, :] <= (blk_ids * MOE_BLOCK)[:, None]).astype(I32), axis=1)
    block_expert = jnp.minimum(owner, N_EXPERTS - 1).astype(I32)
    e_ids = info[:, 0:MOE_TOPK].astype(I32)
    pos = (pstart[e_ids] + info[:, 4:4 + MOE_TOPK].astype(I32)).reshape(-1)

    xe = _dispatch(pos, xp, n_blocks * MOE_BLOCK, TM_ROW)
    blk = jnp.arange(n_blocks, dtype=I32)
    change = (blk < n_used) & ((blk == 0) | (block_expert != jnp.roll(block_expert, 1)))
    first_change_from = jnp.flip(lax.cummin(jnp.flip(jnp.where(change, blk, n_blocks))))
    nxt = jnp.concatenate([first_change_from[1:], jnp.full((1,), n_blocks, I32)])
    next_expert = jnp.where(nxt < n_blocks, block_expert[jnp.minimum(nxt, n_blocks - 1)], -1).astype(I32)
    yb = _experts(block_expert, next_expert, n_used.reshape(1).astype(I32), xe, w_e_in, w_e_out, layer)
    return _combine(pos, yb, x, info, row(ln_g), row(ln_b), TM_COMBINE)


def kernel(x, mem, positions, w_in, b_in, nsa_cmp_pos, nsa_cmp_w1, nsa_cmp_w2, w_nsa_o, w_pool, pool_scale, w_pool_o, mla_q_norm, w_mla_uq, mla_kv_norm, w_mla_ukv, w_mla_o, conv_w, conv_b, conv_ln_g, conv_ln_b, w_conv_o, w_out, ln_mix_g, ln_mix_b, w_xq, w_xk, w_xv, w_xo, ln_x_g, ln_x_b, w_group, b_group, w_router, b_router, w_expert_in, w_expert_out, ln_ffn_g, ln_ffn_b):
    batch, s_len, d = x.shape
    assert batch == 1 and d == D_MODEL and s_len % (2 * FLASH_TK) == 0 and s_len % FLASH_TQ == 0
    x = x[0]
    xb = x.astype(BF16)
    mem_b = mem[0].astype(BF16)
    pos_col = positions[0].astype(F32)[:, None]
    tabs = _static_tables(s_len)
    for l in range(w_in.shape[0]):
        x, xb = _hybrid_mixer(x, xb, pos_col, tabs, w_in[l], b_in[l], nsa_cmp_pos[l], nsa_cmp_w1[l], nsa_cmp_w2[l],
                              w_nsa_o[l], w_pool[l], pool_scale[l], w_pool_o[l], mla_q_norm[l], w_mla_uq[l],
                              mla_kv_norm[l], w_mla_ukv[l], w_mla_o[l], conv_w[l], conv_b[l], conv_ln_g[l],
                              conv_ln_b[l], w_conv_o[l], w_out[l], ln_mix_g[l], ln_mix_b[l])
        x, xb, xp = _cross_attention(x, xb, mem_b, w_xq[l], w_xk[l], w_xv[l], w_xo[l], ln_x_g[l], ln_x_b[l])
        x, xb = _hier_moe(x, xb, xp, w_group[l], b_group[l], w_router[l], b_router[l], w_expert_in,
                          w_expert_out, l, ln_ffn_g[l], ln_ffn_b[l])
    return x[None]
```

```python
import functools

import numpy as np
import jax
import jax.numpy as jnp
from jax import lax
from jax.experimental import pallas as pl
from jax.experimental.pallas import tpu as pltpu

F32 = jnp.float32
BF16 = jnp.bfloat16
I32 = jnp.int32

D_MODEL = 2048
NSA_HEADS = 8
NSA_GROUPS = 2
NSA_HPG = NSA_HEADS // NSA_GROUPS
NSA_DH = 64
CMP_LEN = 32
CMP_STRIDE = 16
CMP_HIDDEN = 128
SEL_LEN = 64
SEL_TOPN = 16
WINDOW = 512
FORCE_SCORE = 1.0e4
POOL_GROUPS = 4
POOL_WINDOWS = (2, 4, 8, 16)
POOL_WIDTH = 512
POOL_GW = POOL_WIDTH // POOL_GROUPS
MLA_HEADS = 8
MLA_Q_RANK = 512
MLA_KV_RANK = 256
MLA_NOPE = 64
MLA_ROPE = 32
MLA_DV = 64
ROPE_BASE = 10000.0
CONV_CH = 512
CONV_K = 31
N_BRANCH = 4
X_HEADS = 4
X_DH = 128
MOE_GROUPS = 4
MOE_EPG = 8
N_EXPERTS = MOE_GROUPS * MOE_EPG
MOE_TOPK = 2
EXPERT_FF = 512
MOE_BLOCK = 256
LN_EPS = 1e-5
RMS_EPS = 1e-6
DEPTH = 2
DN_ALPHA = (2 * DEPTH) ** 0.25
IN_SIZES = (NSA_HEADS * NSA_DH, 6 * NSA_GROUPS * NSA_DH, 3 * NSA_HEADS, POOL_WIDTH,
            MLA_Q_RANK, MLA_KV_RANK, MLA_ROPE, 2 * CONV_CH, N_BRANCH * D_MODEL)

LANE = 128
VMEM_LIMIT = 56 * 1024 * 1024
NEG = -1.0e30
FLASH_TQ = 1024
FLASH_TK = 1024
LOG2E = 1.4426950408889634

TM_PROJ = 2048
TN_PROJ = 7 * LANE
TN_MERGE = 1024
TQ_CMP = 256
TM_ROW = 512
TM_MERGE = 256
TM_COMBINE = 512

SLOT_Q = 0
SLOT_KV = 8
SLOT_POOL = 20
SLOT_CQ = 24
SLOT_CKV = 28
SLOT_KR = 30
SLOT_CONV = 32
SLOT_GATE = 40
N_SLOTS1 = 42
N1 = N_SLOTS1 * LANE


def _cparams(sem, vmem=VMEM_LIMIT):
    return pltpu.CompilerParams(dimension_semantics=sem, vmem_limit_bytes=vmem)


def _sigmoid(x):
    return 0.5 * jnp.tanh(0.5 * x) + 0.5


def _layer_norm(z, g, b):
    mu = jnp.mean(z, -1, keepdims=True)
    d = z - mu
    var = jnp.mean(d * d, -1, keepdims=True)
    return d * lax.rsqrt(var + LN_EPS) * g + b


U32 = jnp.uint32


def _pack_rows(y):
    n = y.shape[1] // 2
    bits = lax.bitcast_convert_type(y.astype(BF16).astype(F32), U32)
    return (bits[:, n:] & jnp.uint32(0xFFFF0000)) | (bits[:, :n] >> 16)


def _unpack_rows(w):
    lo = lax.bitcast_convert_type(w << 16, F32)
    hi = lax.bitcast_convert_type(w & jnp.uint32(0xFFFF0000), F32)
    return jnp.concatenate([lo, hi], axis=1)


def _dot(a, b):
    return jnp.dot(a, b, preferred_element_type=F32)


def _dot_nt(a, b):
    return lax.dot_general(a, b, (((1,), (1,)), ((), ())), preferred_element_type=F32)


def _dot3(a, b):
    hi = a.astype(BF16)
    r1 = a - hi.astype(F32)
    mid = r1.astype(BF16)
    lo = (r1 - mid.astype(F32)).astype(BF16)
    return _dot(hi, b) + _dot(mid, b) + _dot(lo, b)


def _mm_kernel(a_ref, b_ref, bias_ref, o_ref):
    o_ref[...] = (_dot(a_ref[...], b_ref[...]) + bias_ref[...]).astype(o_ref.dtype)


def _matmul(a, b, bias, tm, tn, out_dtype):
    m, k = a.shape
    n = b.shape[1]
    return pl.pallas_call(
        _mm_kernel,
        grid=(m // tm, n // tn),
        in_specs=[pl.BlockSpec((tm, k), lambda i, j: (i, 0)),
                  pl.BlockSpec((k, tn), lambda i, j: (0, j)),
                  pl.BlockSpec((1, tn), lambda i, j: (0, j))],
        out_specs=pl.BlockSpec((tm, tn), lambda i, j: (i, j)),
        out_shape=jax.ShapeDtypeStruct((m, n), out_dtype),
        compiler_params=_cparams(("parallel", "arbitrary")),
        name="matmul",
    )(a, b, bias)


def _mm_ln_kernel(a_ref, w_ref, x_ref, g_ref, b_ref, xo_ref, xb_ref):
    h = _dot(a_ref[...], w_ref[...])
    y = _layer_norm(DN_ALPHA * x_ref[...] + h, g_ref[...], b_ref[...])
    xo_ref[...] = y
    xb_ref[...] = y.astype(BF16)


def _matmul_res_ln(a, w, x, g, b, tm):
    m, k = a.shape
    d = w.shape[1]
    return pl.pallas_call(
        _mm_ln_kernel,
        grid=(m // tm,),
        in_specs=[pl.BlockSpec((tm, k), lambda i: (i, 0)),
                  pl.BlockSpec((k, d), lambda i: (0, 0)),
                  pl.BlockSpec((tm, d), lambda i: (i, 0)),
                  pl.BlockSpec((1, d), lambda i: (0, 0)),
                  pl.BlockSpec((1, d), lambda i: (0, 0))],
        out_specs=[pl.BlockSpec((tm, d), lambda i: (i, 0)),
                   pl.BlockSpec((tm, d), lambda i: (i, 0))],
        out_shape=[jax.ShapeDtypeStruct((m, d), F32), jax.ShapeDtypeStruct((m, d), BF16)],
        compiler_params=_cparams(("parallel",)),
        name="matmul_res_ln",
    )(a, w, x, g, b)


def _compress_kernel(a_ref, w1_ref, pos_ref, w2_ref, o_ref):
    nch = a_ref.shape[1]
    hh = _dot(a_ref[0], w1_ref[0])
    pp = _dot(pos_ref[0], w1_ref[0])
    pos_term = pp[0:1, :CMP_HIDDEN] + pp[1:2, CMP_HIDDEN:]
    h2_next = pltpu.roll(hh[:, CMP_HIDDEN:], nch - 1, 0)
    z = hh[:, :CMP_HIDDEN] + h2_next + pos_term
    hid = 0.5 * z * (1.0 + jnp.tanh(0.7978845608028654 * (z + 0.044715 * z * z * z)))
    o_ref[0] = _dot(hid.astype(BF16), w2_ref[0]).astype(o_ref.dtype)


def _nsa_compress(chunks, w1cat, pos2, w2pad):
    n4, nch, kdim = chunks.shape
    return pl.pallas_call(
        _compress_kernel,
        grid=(n4,),
        in_specs=[pl.BlockSpec((1, nch, kdim), lambda c: (c, 0, 0)),
                  pl.BlockSpec((1, kdim, 2 * CMP_HIDDEN), lambda c: (c // NSA_GROUPS, 0, 0)),
                  pl.BlockSpec((1, 8, kdim), lambda c: (c // NSA_GROUPS, 0, 0)),
                  pl.BlockSpec((1, CMP_HIDDEN, LANE), lambda c: (c // NSA_GROUPS, 0, 0))],
        out_specs=pl.BlockSpec((1, nch, LANE), lambda c: (c, 0, 0)),
        out_shape=jax.ShapeDtypeStruct((n4, nch, LANE), BF16),
        compiler_params=_cparams(("arbitrary",)),
        name="nsa_compress",
    )(chunks, w1cat, pos2, w2pad)


CMP_WIDTH_STEPS = 4


def _nsa_cmp_kernel(q_ref, kc_ref, vc_ref, mband_ref, ocmp_ref, selneg_ref, score_sc, *, tq, n_cmp, n_sel, top_n):
    t0 = pl.program_id(0) * tq
    nch = kc_ref.shape[1]
    row = lax.broadcasted_iota(I32, (NSA_HPG * tq, 1), 0)
    t_row = t0 + (row & (tq - 1))

    def softmax_part(width):
        col = lax.broadcasted_iota(I32, (1, width), 1)
        vis = (col * CMP_STRIDE + (CMP_LEN - 1) <= t_row) & (col < n_cmp)
        for g in range(NSA_GROUPS):
            qs = jnp.concatenate(
                [q_ref[:, (g * NSA_HPG + h) * LANE:(g * NSA_HPG + h + 1) * LANE] for h in range(NSA_HPG)], axis=0)
            s = _dot_nt(qs, kc_ref[g, 0:width, :])
            s = jnp.where(vis, s, NEG)
            m = jnp.max(s, -1, keepdims=True)
            p = jnp.where(vis, jnp.exp2(s - m), 0.0)
            l = jnp.sum(p, -1, keepdims=True)
            p = p * (1.0 / jnp.maximum(l, 1e-30))
            o = _dot(p.astype(BF16), vc_ref[g, 0:width, :])
            for h in range(NSA_HPG):
                hh = g * NSA_HPG + h
                ocmp_ref[:, hh * NSA_DH:(hh + 1) * NSA_DH] = o[h * tq:(h + 1) * tq, :NSA_DH]
            imp = p[0:tq]
            for h in range(1, NSA_HPG):
                imp = imp + p[h * tq:(h + 1) * tq]
            score_sc[g * tq:(g + 1) * tq, :] = _dot3(imp, mband_ref[0:width, :])

    last_vis = (t0 + tq - CMP_LEN) // CMP_STRIDE
    step_w = nch // CMP_WIDTH_STEPS
    variant = jnp.minimum(last_vis // step_w, CMP_WIDTH_STEPS - 1)
    for v in range(CMP_WIDTH_STEPS):
        pl.when(variant == v)(functools.partial(softmax_part, (v + 1) * step_w))

    blk = lax.broadcasted_iota(I32, (NSA_GROUPS * tq, n_sel), 1).astype(F32)
    row2 = lax.broadcasted_iota(I32, (NSA_GROUPS * tq, 1), 0)
    cur = ((t0 + (row2 & (tq - 1))) // SEL_LEN).astype(F32)
    valid = blk <= cur
    forced = (blk == 0.0) | (blk == cur) | (blk == cur - 1.0)
    assert FORCE_SCORE > NSA_HPG * (SEL_LEN // CMP_STRIDE + CMP_LEN // CMP_STRIDE - 1) and top_n > 3
    score = jnp.where(forced, -2.0, score_sc[...])
    score = jnp.where(valid, score, -1.0)

    def pick_one(_, sc):
        mx = jnp.max(sc, -1, keepdims=True)
        first = jnp.min(jnp.where(sc == mx, blk, float(n_sel)), -1, keepdims=True)
        return jnp.where(blk == first, -2.0, sc)

    sc = lax.fori_loop(0, top_n - 3, pick_one, score)
    selneg = jnp.where(valid & (sc == -2.0), 0.0, -1.0).astype(selneg_ref.dtype)
    for g in range(NSA_GROUPS):
        selneg_ref[:, g * n_sel:(g + 1) * n_sel] = selneg[g * tq:(g + 1) * tq]


def _nsa_cmp(u1, kcvc, mband, tq):
    s_len = u1.shape[0]
    nch = kcvc.shape[1]
    n_sel = s_len // SEL_LEN
    n_cmp = (s_len - CMP_LEN) // CMP_STRIDE + 1
    kern = functools.partial(_nsa_cmp_kernel, tq=tq, n_cmp=n_cmp, n_sel=n_sel, top_n=min(SEL_TOPN, n_sel))
    return pl.pallas_call(
        kern,
        grid=(s_len // tq,),
        in_specs=[pl.BlockSpec((tq, NSA_HEADS * LANE), lambda i: (i, SLOT_Q // NSA_HEADS)),
                  pl.BlockSpec((NSA_GROUPS, nch, LANE), lambda i: (0, 0, 0)),
                  pl.BlockSpec((NSA_GROUPS, nch, LANE), lambda i: (1, 0, 0)),
                  pl.BlockSpec((nch, n_sel), lambda i: (0, 0))],
        out_specs=[pl.BlockSpec((tq, NSA_HEADS * NSA_DH), lambda i: (i, 0)),
                   pl.BlockSpec((tq, NSA_GROUPS * n_sel), lambda i: (i, 0))],
        out_shape=[jax.ShapeDtypeStruct((s_len, NSA_HEADS * NSA_DH), F32),
                   jax.ShapeDtypeStruct((s_len, NSA_GROUPS * n_sel), BF16)],
        scratch_shapes=[pltpu.VMEM((NSA_GROUPS * tq, n_sel), F32)],
        compiler_params=_cparams(("parallel",)),
        name="nsa_cmp_topk",
    )(u1, kcvc, kcvc, mband)


def _flash_kernel(qi_ref, kj_ref, first_ref, last_ref, mode_ref, *refs, hpg, hpk, select, modes, dv,
                  split_masked):
    if select:
        q_ref, k_ref, v_ref, sel_ref, scat_ref, eslot_ref, o_ref, m_sc, acc_sc = refs
    else:
        q_ref, k_ref, v_ref, o_ref, m_sc, acc_sc = refs
    p_idx = pl.program_id(1)
    tq = q_ref.shape[0]
    tk = k_ref.shape[0]

    @pl.when(first_ref[p_idx] == 1)
    def _():
        m_sc[...] = jnp.full(m_sc.shape, NEG, F32)
        acc_sc[...] = jnp.zeros(acc_sc.shape, F32)

    def step(mask_mode):
        k_all = k_ref[...]
        lane = lax.broadcasted_iota(I32, (1, v_ref.shape[1]), 1)
        v_all = v_ref[...] + ((lane & (LANE - 1)) == dv).astype(BF16)
        if select:
            nk = hpg // hpk
            n_sel = sel_ref.shape[1] // nk
            k_all = k_all + jnp.concatenate([eslot_ref[...]] * nk, axis=1)
            scat = scat_ref[kj_ref[p_idx]]
            biases = [(_dot(sel_ref[:, g * n_sel:(g + 1) * n_sel], scat) * (-NEG)).astype(BF16) for g in range(nk)]
        hq, hk_ = tq // 2, tk // 2
        if mask_mode == 0 or not split_masked:
            blocks = ((0, tq, 0, tk),)
        else:
            blocks = {1: ((0, hq, 0, hk_), (hq, tq, 0, tk)), 2: ((0, hq, 0, tk), (hq, tq, hk_, tk))}[mask_mode]
        for r0, r1, c0, c1 in blocks:
            if mask_mode:
                r = lax.broadcasted_iota(I32, (r1 - r0, c1 - c0), 0) + r0
                c = lax.broadcasted_iota(I32, (r1 - r0, c1 - c0), 1) + c0
                keep = (c <= r) if mask_mode == 1 else (c > r)
            for h in range(hpg):
                hk = h // hpk
                q = q_ref[r0:r1, h * LANE:(h + 1) * LANE]
                if select:
                    q = q + biases[hk][r0:r1]
                s = _dot_nt(q, k_all[c0:c1, hk * LANE:(hk + 1) * LANE])
                if mask_mode:
                    s = jnp.where(keep, s, NEG)
                m_prev = m_sc[h, r0:r1]
                m_new = jnp.maximum(m_prev, jnp.max(s, -1, keepdims=True))
                p = jnp.exp2(s - jnp.concatenate([m_new] * ((c1 - c0) // LANE), axis=1))
                acc_sc[h, r0:r1] = (jnp.exp2(m_prev - m_new) * acc_sc[h, r0:r1]
                                    + _dot(p.astype(BF16), v_all[c0:c1, hk * LANE:(hk + 1) * LANE]))
                m_sc[h, r0:r1] = m_new

    for mm in modes:
        pl.when(mode_ref[p_idx] == mm)(functools.partial(step, mm))

    @pl.when(last_ref[p_idx] == 1)
    def _():
        for h in range(hpg):
            acc = acc_sc[h]
            o = acc[:, :dv] * (1.0 / acc[:, dv:dv + 1])
            o_ref[:, h * dv:(h + 1) * dv] = o.astype(o_ref.dtype)


def _pair_tables(n_tiles, window):
    rows = []
    for i in range(n_tiles):
        js = ([i - 1] if i > 0 else []) + [i] if window else list(range(i + 1))
        for j in js:
            mode = 1 if j == i else (2 if window else 0)
            rows.append((i, j, int(j == js[0]), int(j == i), mode))
    tab = np.asarray(rows, np.int32)
    return [jnp.asarray(tab[:, c]) for c in range(tab.shape[1])]


def _flash(q_arr, q_blk0, k_arr, k_blk0, v_arr, v_blk0, *, n_groups, hpg, hpk, window,
           out_dtype, dv, tq, tk, sel=None, scat=None, eslot=None):
    s_len = q_arr.shape[0]
    assert tq == tk and (not window or tk == WINDOW)
    nk = hpg // hpk
    select = sel is not None
    tables = _pair_tables(s_len // tq, window)
    n_pairs = int(tables[0].shape[0])
    modes = (1, 2) if window else (0, 1)

    def qmap(g, p, qi, kj, *_):
        return (qi[p], q_blk0 + g)

    def kmap(g, p, qi, kj, *_):
        return (kj[p], k_blk0 + g)

    def vmap_(g, p, qi, kj, *_):
        return (kj[p], v_blk0 + g)

    def omap(g, p, qi, kj, *_):
        return (qi[p], g)

    in_specs = [pl.BlockSpec((tq, hpg * LANE), qmap),
                pl.BlockSpec((tk, nk * LANE), kmap),
                pl.BlockSpec((tk, nk * LANE), vmap_)]
    args = [q_arr, k_arr, v_arr]
    if select:
        in_specs += [pl.BlockSpec((tq, sel.shape[1] // n_groups), omap),
                     pl.BlockSpec(scat.shape, lambda g, p, *_: (0, 0, 0)),
                     pl.BlockSpec(eslot.shape, lambda g, p, *_: (0, 0))]
        args += [sel, scat, eslot]
    kern = functools.partial(_flash_kernel, hpg=hpg, hpk=hpk, select=select, modes=modes, dv=dv,
                             split_masked=tq >= 2 * WINDOW)
    grid_spec = pltpu.PrefetchScalarGridSpec(
        num_scalar_prefetch=len(tables),
        grid=(n_groups, n_pairs),
        in_specs=in_specs,
        out_specs=pl.BlockSpec((tq, hpg * dv), omap),
        scratch_shapes=[pltpu.VMEM((hpg, tq, LANE), F32), pltpu.VMEM((hpg, tq, LANE), F32)],
    )
    return pl.pallas_call(
        kern,
        grid_spec=grid_spec,
        out_shape=jax.ShapeDtypeStruct((s_len, n_groups * hpg * dv), out_dtype),
        compiler_params=_cparams(("parallel", "arbitrary")),
        name="flash_sel" if select else ("flash_win" if window else "flash_causal"),
    )(*tables, *args)


def _mla_prep_kernel(cq_ref, ckv_ref, kr_ref, pos_ref, inv_ref, qn_ref, kvn_ref, wuq_ref, wk_ref, wv_ref,
                     q_ref, k_ref, v_ref):
    def rms(x, g):
        return x * lax.rsqrt(jnp.mean(x * x, -1, keepdims=True) + RMS_EPS) * g

    ang = pos_ref[...] * inv_ref[...]
    cos, sin = jnp.cos(ang), jnp.sin(ang)
    hw = MLA_HEADS * LANE
    qh = _dot(rms(cq_ref[...].astype(F32), qn_ref[...]).astype(BF16), wuq_ref[...])
    scale = (MLA_NOPE + MLA_ROPE) ** -0.5 * LOG2E
    cos_t = jnp.concatenate([cos] * MLA_HEADS, axis=1)
    sin_t = jnp.concatenate([sin] * MLA_HEADS, axis=1)
    q_ref[...] = ((qh[:, :hw] * cos_t + qh[:, hw:] * sin_t) * scale).astype(BF16)
    ckv = rms(ckv_ref[...].astype(F32), kvn_ref[...]).astype(BF16)
    kr = kr_ref[...].astype(F32)
    k_rope = kr[:, :LANE] * cos + kr[:, LANE:] * sin
    k_ref[...] = (_dot(ckv, wk_ref[...]) + jnp.concatenate([k_rope] * MLA_HEADS, axis=1)).astype(BF16)
    v_ref[...] = _dot(ckv, wv_ref[...]).astype(BF16)


def _mla_prep(u1, pos_col, inv_slot, qn, kvn, wuq, wk, wv, tm):
    s_len = u1.shape[0]
    hw = MLA_HEADS * LANE
    full = lambda a: pl.BlockSpec(a.shape, lambda i: (0,) * a.ndim)
    out = jax.ShapeDtypeStruct((s_len, hw), BF16)
    return pl.pallas_call(
        _mla_prep_kernel,
        grid=(s_len // tm,),
        in_specs=[pl.BlockSpec((tm, MLA_Q_RANK), lambda i: (i, SLOT_CQ * LANE // MLA_Q_RANK)),
                  pl.BlockSpec((tm, MLA_KV_RANK), lambda i: (i, SLOT_CKV * LANE // MLA_KV_RANK)),
                  pl.BlockSpec((tm, 2 * LANE), lambda i: (i, SLOT_KR // 2)),
                  pl.BlockSpec((tm, 1), lambda i: (i, 0)),
                  full(inv_slot), full(qn), full(kvn), full(wuq), full(wk), full(wv)],
        out_specs=[pl.BlockSpec((tm, hw), lambda i: (i, 0))] * 3,
        out_shape=[out, out, out],
        compiler_params=_cparams(("parallel",)),
        name="mla_prep",
    )(u1, u1, u1, pos_col, inv_slot, qn, kvn, wuq, wk, wv)


POOL_HALO = 16


def _pool_kernel(cur_ref, halo_ref, wp_ref, scale_ref, o_ref):
    i = pl.program_id(0)
    tm = cur_ref.shape[0]
    halo = jnp.where(i > 0, halo_ref[...].astype(F32), 0.0)
    x = jnp.concatenate([halo, cur_ref[...].astype(F32)], axis=0)
    sums = {1: x}
    w = 1
    while w < max(POOL_WINDOWS):
        a = sums[w]
        sums[2 * w] = a[w:] + a[:-w]
        w *= 2
    t = (i * tm + lax.broadcasted_iota(I32, (tm, 1), 0) + 1).astype(F32)
    outs = []
    for gi, w in enumerate(POOL_WINDOWS):
        lo = gi * POOL_GW
        start = POOL_HALO - (w - 1)
        win = sums[w][start:start + tm, lo:lo + POOL_GW]
        mean = win / jnp.minimum(t, float(w))
        pooled = mean - x[POOL_HALO:, lo:lo + POOL_GW]
        outs.append(_dot(pooled.astype(BF16), wp_ref[gi]))
    o_ref[...] = (jnp.concatenate(outs, axis=1) * scale_ref[...]).astype(o_ref.dtype)


def _pool(u1, w_pool, pool_scale, tm):
    s_len = u1.shape[0]
    blk = SLOT_POOL * LANE // POOL_WIDTH
    return pl.pallas_call(
        _pool_kernel,
        grid=(s_len // tm,),
        in_specs=[pl.BlockSpec((tm, POOL_WIDTH), lambda i: (i, blk)),
                  pl.BlockSpec((POOL_HALO, POOL_WIDTH),
                               lambda i: (jnp.maximum(i * (tm // POOL_HALO) - 1, 0), blk)),
                  pl.BlockSpec(w_pool.shape, lambda i: (0, 0, 0)),
                  pl.BlockSpec((1, POOL_WIDTH), lambda i: (0, 0))],
        out_specs=pl.BlockSpec((tm, POOL_WIDTH), lambda i: (i, 0)),
        out_shape=jax.ShapeDtypeStruct((s_len, POOL_WIDTH), BF16),
        compiler_params=_cparams(("parallel",)),
        name="pool",
    )(u1, u1, w_pool, pool_scale)


CONV_HALO = 32


def _conv_kernel(cur_ref, halo_ref, w_ref, b_ref, g_ref, beta_ref, o_ref, hbuf):
    i = pl.program_id(0)
    tm = cur_ref.shape[0]

    def glu(u):
        u = u.astype(F32)
        return u[:, :CONV_CH] * _sigmoid(u[:, CONV_CH:])

    hbuf[0:CONV_HALO, :] = jnp.where(i > 0, glu(halo_ref[...]), 0.0)
    hbuf[CONV_HALO:, :] = glu(cur_ref[...])
    acc = jnp.zeros((tm, CONV_CH), F32) + b_ref[...]
    for k in range(CONV_K):
        off = CONV_HALO - (CONV_K - 1) + k
        acc = acc + hbuf[off:off + tm, :] * w_ref[k:k + 1, :]
    y = _layer_norm(acc, g_ref[...], beta_ref[...])
    o_ref[...] = (y * _sigmoid(y)).astype(o_ref.dtype)


def _conv(u1, conv_w, conv_b, ln_g, ln_b, tm):
    s_len = u1.shape[0]
    blk = SLOT_CONV * LANE // (2 * CONV_CH)
    row = lambda a: pl.BlockSpec(a.shape, lambda i: (0, 0))
    return pl.pallas_call(
        _conv_kernel,
        grid=(s_len // tm,),
        in_specs=[pl.BlockSpec((tm, 2 * CONV_CH), lambda i: (i, blk)),
                  pl.BlockSpec((CONV_HALO, 2 * CONV_CH),
                               lambda i: (jnp.maximum(i * (tm // CONV_HALO) - 1, 0), blk)),
                  row(conv_w), row(conv_b), row(ln_g), row(ln_b)],
        out_specs=pl.BlockSpec((tm, CONV_CH), lambda i: (i, 0)),
        out_shape=jax.ShapeDtypeStruct((s_len, CONV_CH), BF16),
        scratch_shapes=[pltpu.VMEM((tm + CONV_HALO, CONV_CH), F32)],
        compiler_params=_cparams(("parallel",)),
        name="conv_module",
    )(u1, u1, conv_w, conv_b, ln_g, ln_b)


def _merge_kernel(ocmp_ref, osel_ref, owin_ref, gate_ref, gexp_ref, pool_ref, mla_ref, conv_ref, um_ref,
                  wn_ref, wp_ref, wm_ref, wc_ref, o_ref):
    sg = _sigmoid(gate_ref[...].astype(F32))
    nsa = (_dot3(sg, gexp_ref[0]) * ocmp_ref[...] + _dot3(sg, gexp_ref[1]) * osel_ref[...]
           + _dot3(sg, gexp_ref[2]) * owin_ref[...])
    branches = (_dot(nsa.astype(BF16), wn_ref[...]), _dot(pool_ref[...], wp_ref[...]),
                _dot(mla_ref[...], wm_ref[...]), _dot(conv_ref[...], wc_ref[...]))
    merged = None
    for j, br in enumerate(branches):
        term = _sigmoid(um_ref[:, j * D_MODEL:(j + 1) * D_MODEL].astype(F32)) * br
        merged = term if merged is None else merged + term
    o_ref[...] = merged.astype(o_ref.dtype)


def _merge(o_cmp, o_sel, o_win, u1, gexp, pooled, o_mla, conv, um, wn, wp, wm, wc, tm):
    s_len = u1.shape[0]
    tile = lambda w: pl.BlockSpec((tm, w), lambda i: (i, 0))
    full = lambda a: pl.BlockSpec(a.shape, lambda i: (0,) * a.ndim)
    return pl.pallas_call(
        _merge_kernel,
        grid=(s_len // tm,),
        in_specs=[tile(512), tile(512), tile(512),
                  pl.BlockSpec((tm, LANE), lambda i: (i, SLOT_GATE)),
                  full(gexp), tile(512), tile(512), tile(512), tile(N_BRANCH * D_MODEL),
                  full(wn), full(wp), full(wm), full(wc)],
        out_specs=tile(D_MODEL),
        out_shape=jax.ShapeDtypeStruct((s_len, D_MODEL), BF16),
        compiler_params=_cparams(("parallel",)),
        name="branch_merge",
    )(o_cmp, o_sel, o_win, u1, gexp, pooled, o_mla, conv, um, wn, wp, wm, wc)


def _xattn_kernel(xb_ref, x_ref, wq_ref, k_ref, v_ref, wo_ref, g_ref, b_ref, xo_ref, xob_ref, xp_ref):
    q = _dot(xb_ref[...], wq_ref[...]).astype(BF16)
    k = k_ref[...]
    v = v_ref[...]
    outs = []
    for h in range(X_HEADS):
        sl = slice(h * X_DH, (h + 1) * X_DH)
        s = _dot_nt(q[:, sl], k[:, sl]) * (X_DH ** -0.5)
        m = jnp.max(s, -1, keepdims=True)
        p = jnp.exp(s - m)
        p = p * (1.0 / jnp.sum(p, -1, keepdims=True))
        outs.append(_dot(p.astype(BF16), v[:, sl]))
    o = jnp.concatenate(outs, axis=1).astype(BF16)
    y = _layer_norm(DN_ALPHA * x_ref[...] + _dot(o, wo_ref[...]), g_ref[...], b_ref[...])
    xo_ref[...] = y
    xob_ref[...] = y.astype(BF16)
    xp_ref[...] = _pack_rows(y)


def _xattn(xb, x, wq, k, v, wo, g, b, tm):
    s_len = x.shape[0]
    tile = lambda: pl.BlockSpec((tm, D_MODEL), lambda i: (i, 0))
    full = lambda a: pl.BlockSpec(a.shape, lambda i: (0,) * a.ndim)
    return pl.pallas_call(
        _xattn_kernel,
        grid=(s_len // tm,),
        in_specs=[tile(), tile(), full(wq), full(k), full(v), full(wo), full(g), full(b)],
        out_specs=[tile(), tile(), pl.BlockSpec((tm, D_MODEL // 2), lambda i: (i, 0))],
        out_shape=[jax.ShapeDtypeStruct((s_len, D_MODEL), F32), jax.ShapeDtypeStruct((s_len, D_MODEL), BF16),
                   jax.ShapeDtypeStruct((s_len, D_MODEL // 2), U32)],
        compiler_params=_cparams(("parallel",)),
        name="cross_attention_ln",
    )(xb, x, wq, k, v, wo, g, b)


ROUTE_OFF = MOE_GROUPS


def _router_kernel(xb_ref, w_ref, b_ref, tri_ref, info_ref, cnt_ref, carry):
    @pl.when(pl.program_id(0) == 0)
    def _():
        carry[...] = jnp.zeros(carry.shape, F32)

    logits = _dot(xb_ref[...], w_ref[...]) + b_ref[...]
    lane = lax.broadcasted_iota(I32, logits.shape, 1).astype(F32)
    is_g = lane < float(MOE_GROUPS)
    neg_inf = -jnp.inf
    gl = jnp.where(is_g, logits, neg_inf)
    gmax = jnp.max(gl, -1, keepdims=True)
    g_sel = jnp.min(jnp.where(gl == gmax, lane, float(LANE)), -1, keepdims=True)
    g_w = 1.0 / jnp.sum(jnp.where(is_g, jnp.exp(gl - gmax), 0.0), -1, keepdims=True)
    lo = ROUTE_OFF + MOE_EPG * g_sel
    in_g = (lane >= lo) & (lane < lo + MOE_EPG)
    el = jnp.where(in_g, logits, neg_inf)
    emax = jnp.max(el, -1, keepdims=True)
    e = jnp.where(in_g, jnp.exp(el - emax), 0.0)
    p = e / jnp.sum(e, -1, keepdims=True)
    pm = jnp.where(in_g, p, -1.0)
    p1 = jnp.max(pm, -1, keepdims=True)
    i1 = jnp.min(jnp.where(pm == p1, lane, float(LANE)), -1, keepdims=True)
    pm2 = jnp.where(lane == i1, -1.0, pm)
    p2 = jnp.max(pm2, -1, keepdims=True)
    i2 = jnp.min(jnp.where(pm2 == p2, lane, float(LANE)), -1, keepdims=True)
    denom = p1 + p2
    gate1 = g_w * p1 / denom
    gate2 = g_w * p2 / denom

    oh1 = (lane == i1).astype(BF16)
    oh2 = (lane == i2).astype(BF16)
    incl1 = _dot(tri_ref[...], oh1)
    incl2 = _dot(tri_ref[...], oh2)
    tot1 = jnp.sum(oh1.astype(F32), 0, keepdims=True)
    tot2 = jnp.sum(oh2.astype(F32), 0, keepdims=True)
    base = carry[...]
    rank1 = jnp.sum(jnp.where(lane == i1, base + incl1, 0.0), -1, keepdims=True) - 1.0
    rank2 = jnp.sum(jnp.where(lane == i2, base + tot1 + incl2, 0.0), -1, keepdims=True) - 1.0
    carry[...] = base + tot1 + tot2
    cnt_ref[...] = carry[...]

    cols = (i1 - ROUTE_OFF, i2 - ROUTE_OFF, gate1, gate2, rank1, rank2)
    info = jnp.zeros(logits.shape, F32)
    for c, val in enumerate(cols):
        info = jnp.where(lane == float(c), val, info)
    info_ref[...] = info


def _router(xb, w_gr, b_gr, tri, tm):
    t_len = xb.shape[0]
    return pl.pallas_call(
        _router_kernel,
        grid=(t_len // tm,),
        in_specs=[pl.BlockSpec((tm, D_MODEL), lambda i: (i, 0)),
                  pl.BlockSpec(w_gr.shape, lambda i: (0, 0)),
                  pl.BlockSpec((1, LANE), lambda i: (0, 0)),
                  pl.BlockSpec((tm, tm), lambda i: (0, 0))],
        out_specs=[pl.BlockSpec((tm, LANE), lambda i: (i, 0)),
                   pl.BlockSpec((1, LANE), lambda i: (0, 0))],
        out_shape=[jax.ShapeDtypeStruct((t_len, LANE), F32), jax.ShapeDtypeStruct((1, LANE), F32)],
        scratch_shapes=[pltpu.VMEM((1, LANE), F32)],
        compiler_params=_cparams(("arbitrary",)),
        name="moe_router",
    )(xb, w_gr, b_gr, tri)


def _row_copy(src, src_row, dst, dst_row, sem):
    return pltpu.make_async_copy(src.at[pl.ds(src_row, 1)], dst.at[pl.ds(dst_row, 1)], sem)


def _dispatch_kernel(pos_ref, x_ref, xe_in_hbm, xe_hbm, sem, *, td):
    del xe_in_hbm
    base = pl.program_id(0) * td

    def issue(t, c):
        for k in range(MOE_TOPK):
            _row_copy(x_ref, t, xe_hbm, pos_ref[MOE_TOPK * (base + t) + k], sem).start()
        return c

    lax.fori_loop(0, td, issue, 0, unroll=8)
    all_rows = xe_hbm.at[pl.ds(0, MOE_TOPK * td)]
    pltpu.make_async_copy(all_rows, all_rows, sem).wait()


def _dispatch(pos_flat, x, rows, td):
    t_len, d = x.shape
    zeros = jnp.zeros((rows, d), x.dtype)
    grid_spec = pltpu.PrefetchScalarGridSpec(
        num_scalar_prefetch=1,
        grid=(t_len // td,),
        in_specs=[pl.BlockSpec((td, d), lambda i, pos: (i, 0)), pl.BlockSpec(memory_space=pl.ANY)],
        out_specs=pl.BlockSpec(memory_space=pl.ANY),
        scratch_shapes=[pltpu.SemaphoreType.DMA(())],
    )
    return pl.pallas_call(
        functools.partial(_dispatch_kernel, td=td),
        grid_spec=grid_spec,
        out_shape=jax.ShapeDtypeStruct((rows, d), x.dtype),
        input_output_aliases={2: 0},
        compiler_params=pltpu.CompilerParams(dimension_semantics=("arbitrary",)),
        name="moe_dispatch",
    )(pos_flat, x, zeros)


def _expert_weight_copies(win_hbm, wout_hbm, layer, expert, win_f32, wout_f32, sem):
    return (pltpu.make_async_copy(win_hbm.at[layer, expert], win_f32, sem.at[0]),
            pltpu.make_async_copy(wout_hbm.at[layer, expert], wout_f32, sem.at[1]))


def _expert_kernel(be_ref, next_ref, nused_ref, xe_ref, win_hbm, wout_hbm, yb_ref, win_f32, wout_f32, win_bf,
                   wout_bf, sem, *, layer):
    b = pl.program_id(0)
    used = b < nused_ref[0]
    new_expert = (b == 0) | (be_ref[b] != be_ref[jnp.maximum(b - 1, 0)])
    copies = functools.partial(_expert_weight_copies, win_hbm, wout_hbm, layer)

    @pl.when(b == 0)
    def _():
        for c in copies(be_ref[0], win_f32, wout_f32, sem):
            c.start()

    @pl.when(used & new_expert)
    def _():
        for c in copies(be_ref[b], win_f32, wout_f32, sem):
            c.wait()
        win_bf[...] = win_f32[...].astype(BF16)
        wout_bf[...] = wout_f32[...].astype(BF16)

        @pl.when(next_ref[b] >= 0)
        def _():
            for c in copies(next_ref[b], win_f32, wout_f32, sem):
                c.start()

    @pl.when(used)
    def _():
        hcat = _dot(_unpack_rows(xe_ref[...]).astype(BF16), win_bf[...])
        a = hcat[:, :EXPERT_FF]
        act = (a * _sigmoid(a) * hcat[:, EXPERT_FF:]).astype(BF16)
        yb_ref[...] = _pack_rows(_dot(act, wout_bf[...]))

    @pl.when(b >= nused_ref[0])
    def _():
        yb_ref[...] = jnp.zeros(yb_ref.shape, U32)


def _experts(block_expert, next_expert, n_used, xe, w_e_in, w_e_out, layer):
    rows, half = xe.shape
    d = 2 * half
    n_blocks = rows // MOE_BLOCK
    grid_spec = pltpu.PrefetchScalarGridSpec(
        num_scalar_prefetch=3,
        grid=(n_blocks,),
        in_specs=[pl.BlockSpec((MOE_BLOCK, half), lambda b, *_: (b, 0)),
                  pl.BlockSpec(memory_space=pl.ANY),
                  pl.BlockSpec(memory_space=pl.ANY)],
        out_specs=pl.BlockSpec((MOE_BLOCK, half), lambda b, *_: (b, 0)),
        scratch_shapes=[pltpu.VMEM((d, 2 * EXPERT_FF), F32), pltpu.VMEM((EXPERT_FF, d), F32),
                        pltpu.VMEM((d, 2 * EXPERT_FF), BF16), pltpu.VMEM((EXPERT_FF, d), BF16),
                        pltpu.SemaphoreType.DMA((2,))],
    )
    return pl.pallas_call(
        functools.partial(_expert_kernel, layer=layer),
        grid_spec=grid_spec,
        out_shape=jax.ShapeDtypeStruct((rows, half), U32),
        compiler_params=_cparams(("arbitrary",)),
        name="moe_experts",
    )(block_expert, next_expert, n_used, xe, w_e_in, w_e_out)


def _combine_kernel(pos_ref, yb_hbm, x_ref, info_ref, g_ref, b_ref, xo_ref, xob_ref, buf, sem):
    tm = x_ref.shape[0]
    i = pl.program_id(0)
    slot = i & 1

    def gather(tile, slot_):
        def issue(t, c):
            for k in range(MOE_TOPK):
                _row_copy(yb_hbm, pos_ref[MOE_TOPK * (tile * tm + t) + k], buf.at[slot_, k], t,
                          sem.at[slot_]).start()
            return c
        lax.fori_loop(0, tm, issue, 0, unroll=8)

    @pl.when(i == 0)
    def _():
        gather(0, 0)

    @pl.when(i + 1 < pl.num_programs(0))
    def _():
        gather(i + 1, 1 - slot)

    pltpu.make_async_copy(buf.at[slot], buf.at[slot], sem.at[slot]).wait()
    info = info_ref[...]
    y = info[:, 2:3] * _unpack_rows(buf[slot, 0]) + info[:, 3:4] * _unpack_rows(buf[slot, 1])
    z = _layer_norm(DN_ALPHA * x_ref[...] + y, g_ref[...], b_ref[...])
    xo_ref[...] = z
    xob_ref[...] = z.astype(BF16)


def _combine(pos_flat, yb, x, info, g, b, tm):
    t_len, d = x.shape
    grid_spec = pltpu.PrefetchScalarGridSpec(
        num_scalar_prefetch=1,
        grid=(t_len // tm,),
        in_specs=[pl.BlockSpec(memory_space=pl.ANY),
                  pl.BlockSpec((tm, d), lambda i, pos: (i, 0)),
                  pl.BlockSpec((tm, LANE), lambda i, pos: (i, 0)),
                  pl.BlockSpec((1, d), lambda i, pos: (0, 0)),
                  pl.BlockSpec((1, d), lambda i, pos: (0, 0))],
        out_specs=[pl.BlockSpec((tm, d), lambda i, pos: (i, 0)),
                   pl.BlockSpec((tm, d), lambda i, pos: (i, 0))],
        scratch_shapes=[pltpu.VMEM((2, MOE_TOPK, tm, d // 2), U32), pltpu.SemaphoreType.DMA((2,))],
    )
    return pl.pallas_call(
        _combine_kernel,
        grid_spec=grid_spec,
        out_shape=[jax.ShapeDtypeStruct((t_len, d), F32), jax.ShapeDtypeStruct((t_len, d), BF16)],
        compiler_params=_cparams(("arbitrary",)),
        name="moe_combine_ln",
    )(pos_flat, yb, x, info, g, b)


def _pad_cols(m, width):
    return jnp.pad(m, [(0, 0)] * (m.ndim - 1) + [(0, width - m.shape[-1])])


def _rot_half_cols(m):
    half = m.shape[-1] // 2
    return jnp.concatenate([-m[..., half:], m[..., :half]], -1)


def _layout_in1(m):
    offs = np.cumsum((0,) + IN_SIZES)
    o_q, o_kv, o_g, o_pool, o_cq, o_ckv, o_kr, o_conv = offs[:8]
    z = lambda n: jnp.zeros(m.shape[:-1] + (n,), m.dtype)
    parts = []
    for h in range(NSA_HEADS):
        parts += [m[..., o_q + h * NSA_DH:o_q + (h + 1) * NSA_DH] * (NSA_DH ** -0.5 * LOG2E), z(LANE - NSA_DH)]
    for c in range(6 * NSA_GROUPS):
        parts += [m[..., o_kv + c * NSA_DH:o_kv + (c + 1) * NSA_DH], z(LANE - NSA_DH)]
    parts += [m[..., o_pool:o_pool + POOL_WIDTH], m[..., o_cq:o_cq + MLA_Q_RANK], m[..., o_ckv:o_ckv + MLA_KV_RANK]]
    kr = m[..., o_kr:o_kr + MLA_ROPE]
    parts += [z(MLA_NOPE), kr, z(LANE - MLA_NOPE - MLA_ROPE), z(MLA_NOPE), _rot_half_cols(kr),
              z(LANE - MLA_NOPE - MLA_ROPE)]
    parts += [m[..., o_conv:o_conv + 2 * CONV_CH], _pad_cols(m[..., o_g:o_g + 3 * NSA_HEADS], LANE), z(LANE)]
    return jnp.concatenate(parts, -1)


def _layout_mla_q(w):
    dq = MLA_NOPE + MLA_ROPE
    z = lambda n: jnp.zeros((w.shape[0], n), w.dtype)
    a, b = [], []
    for h in range(MLA_HEADS):
        rope = w[:, h * dq + MLA_NOPE:(h + 1) * dq]
        a += [w[:, h * dq:h * dq + MLA_NOPE], rope, z(LANE - dq)]
        b += [z(MLA_NOPE), _rot_half_cols(rope), z(LANE - dq)]
    return jnp.concatenate(a + b, -1)


def _layout_mla_kv(w):
    dkv = MLA_NOPE + MLA_DV
    z = jnp.zeros((w.shape[0], LANE - MLA_NOPE), w.dtype)
    k, v = [], []
    for h in range(MLA_HEADS):
        k += [w[:, h * dkv:h * dkv + MLA_NOPE], z]
        v += [w[:, h * dkv + MLA_NOPE:(h + 1) * dkv], z]
    return jnp.concatenate(k, -1), jnp.concatenate(v, -1)


def _static_tables(s_len):
    nch = s_len // CMP_STRIDE
    n_cmp = (s_len - CMP_LEN) // CMP_STRIDE + 1
    n_sel = s_len // SEL_LEN
    ratio = SEL_LEN // CMP_STRIDE
    c = np.arange(nch)[:, None]
    j = np.arange(n_sel)[None, :]
    mband = ((c >= ratio * j - 1) & (c <= ratio * j + ratio - 1) & (c < n_cmp)).astype(np.float32)
    per_tile = FLASH_TK // SEL_LEN
    n_tiles = s_len // FLASH_TK
    scat = np.zeros((n_tiles, n_sel, LANE), np.float32)
    for b in range(n_sel):
        scat[b // per_tile, b, NSA_DH + b % per_tile] = 1.0
    eslot = np.zeros((FLASH_TK, LANE), np.float32)
    eslot[np.arange(FLASH_TK), NSA_DH + np.arange(FLASH_TK) // SEL_LEN] = 1.0
    gexp = np.zeros((3, LANE, NSA_HEADS * NSA_DH), np.float32)
    for h in range(NSA_HEADS):
        for jj in range(3):
            gexp[jj, h * 3 + jj, h * NSA_DH:(h + 1) * NSA_DH] = 1.0
    half = MLA_ROPE // 2
    inv = ROPE_BASE ** (-jnp.arange(half, dtype=F32) / half)
    inv_slot = jnp.concatenate([jnp.zeros((MLA_NOPE,), F32), inv, inv,
                                jnp.zeros((LANE - MLA_NOPE - MLA_ROPE,), F32)])[None, :]
    as_bf = lambda a: jnp.asarray(a, BF16)
    return as_bf(mband), as_bf(scat), as_bf(eslot), as_bf(gexp), inv_slot


def _layout_projection(w_in, b_in):
    o_merge = int(sum(IN_SIZES[:8]))
    b_rows = b_in[:, None, :]
    return (_layout_in1(w_in[..., :o_merge]).astype(BF16), _layout_in1(b_rows[..., :o_merge]),
            w_in[..., o_merge:].astype(BF16), b_rows[..., o_merge:])


def _hybrid_mixer(x, xb, pos_col, tabs, proj, cmp_pos, cmp_w1, cmp_w2, w_nsa_o, w_pool, pool_scale, w_pool_o,
                  q_norm, w_uq, kv_norm, w_ukv, w_mla_o, conv_w, conv_b, conv_ln_g, conv_ln_b, w_conv_o, w_out,
                  ln_g, ln_b):
    s_len = x.shape[0]
    mband, scat, eslot, gexp, inv_slot = tabs
    row = lambda v: v[None, :]
    w1, b1, wm, bm = proj
    tm_proj = min(TM_PROJ, s_len)
    u1 = _matmul(xb, w1, b1, tm_proj, TN_PROJ, BF16)
    um = _matmul(xb, wm, bm, tm_proj, TN_MERGE, BF16)

    nch = s_len // CMP_STRIDE
    kdim = CMP_STRIDE * LANE
    chunks = u1[:, SLOT_KV * LANE:(SLOT_KV + 4) * LANE].reshape(nch, CMP_STRIDE, 4, LANE)
    chunks = chunks.transpose(2, 0, 1, 3).reshape(4, nch, kdim)
    w1 = _pad_cols(cmp_w1.reshape(2, CMP_LEN, NSA_DH, CMP_HIDDEN).transpose(0, 1, 3, 2), LANE)
    w1 = w1.transpose(0, 1, 3, 2)
    w1cat = jnp.concatenate([w1[:, :CMP_STRIDE].reshape(2, kdim, CMP_HIDDEN),
                             w1[:, CMP_STRIDE:].reshape(2, kdim, CMP_HIDDEN)], -1).astype(BF16)
    posp = _pad_cols(cmp_pos, LANE)
    pos2 = jnp.stack([posp[:, :CMP_STRIDE].reshape(2, kdim), posp[:, CMP_STRIDE:].reshape(2, kdim)], 1)
    pos2 = jnp.pad(pos2, ((0, 0), (0, 6), (0, 0))).astype(BF16)
    kcvc = _nsa_compress(chunks, w1cat, pos2, _pad_cols(cmp_w2, LANE).astype(BF16))
    o_cmp, selneg = _nsa_cmp(u1, kcvc, mband, TQ_CMP)
    nsa = dict(n_groups=1, hpg=NSA_HEADS, hpk=NSA_HPG, out_dtype=F32, dv=NSA_DH)
    o_sel = _flash(u1, SLOT_Q // NSA_HEADS, u1, (SLOT_KV + 4) // NSA_GROUPS, u1, (SLOT_KV + 6) // NSA_GROUPS,
                   window=False, tq=FLASH_TQ, tk=FLASH_TK, sel=selneg, scat=scat, eslot=eslot, **nsa)
    o_win = _flash(u1, SLOT_Q // NSA_HEADS, u1, (SLOT_KV + 8) // NSA_GROUPS, u1, (SLOT_KV + 10) // NSA_GROUPS,
                   window=True, tq=WINDOW, tk=WINDOW, **nsa)

    wk, wv = _layout_mla_kv(w_ukv)
    q_m, k_m, v_m = _mla_prep(u1, pos_col, inv_slot, row(q_norm), row(kv_norm), _layout_mla_q(w_uq).astype(BF16),
                              wk.astype(BF16), wv.astype(BF16), TM_ROW)
    o_mla = _flash(q_m, 0, k_m, 0, v_m, 0, n_groups=1, hpg=MLA_HEADS, hpk=1, window=False,
                   out_dtype=BF16, dv=MLA_DV, tq=FLASH_TQ, tk=FLASH_TK)

    pooled = _pool(u1, w_pool.astype(BF16), row(pool_scale), TM_ROW)
    conv = _conv(u1, conv_w, row(conv_b), row(conv_ln_g), row(conv_ln_b), TM_ROW)
    merged = _merge(o_cmp, o_sel, o_win, u1, gexp, pooled, o_mla, conv, um, w_nsa_o.astype(BF16),
                    w_pool_o.astype(BF16), w_mla_o.astype(BF16), w_conv_o.astype(BF16), TM_MERGE)
    return _matmul_res_ln(merged, w_out.astype(BF16), x, row(ln_g), row(ln_b), TM_ROW)


def _cross_attention(x, xb, mem_b, w_q, w_k, w_v, w_o, ln_g, ln_b):
    row = lambda v: v[None, :]
    kv = _matmul(mem_b, jnp.concatenate([w_k, w_v], 1).astype(BF16), jnp.zeros((1, 2 * X_HEADS * X_DH), F32),
                 mem_b.shape[0], 2 * X_HEADS * X_DH, BF16)
    hw = X_HEADS * X_DH
    return _xattn(xb, x, w_q.astype(BF16), kv[:, :hw], kv[:, hw:], w_o.astype(BF16), row(ln_g), row(ln_b), TM_ROW)


def _hier_moe(x, xb, xp, w_group, b_group, w_router, b_router, w_e_in, w_e_out, layer, ln_g, ln_b):
    t_len = x.shape[0]
    row = lambda v: v[None, :]
    tm_r = TM_ROW
    w_gr = _pad_cols(jnp.concatenate([w_group, w_router], 1), LANE).astype(BF16)
    b_gr = _pad_cols(row(jnp.concatenate([b_group, b_router])), LANE)
    tri = jnp.asarray(np.tril(np.ones((tm_r, tm_r), np.float32)), BF16)
    info, cnt = _router(xb, w_gr, b_gr, tri, tm_r)

    counts = cnt[0, ROUTE_OFF:ROUTE_OFF + N_EXPERTS].astype(I32)
    padded = (counts + MOE_BLOCK - 1) // MOE_BLOCK * MOE_BLOCK
    pend = jnp.cumsum(padded)
    pstart = pend - padded
    n_blocks = -(-(t_len * MOE_TOPK) // MOE_BLOCK) + N_EXPERTS
    n_used = pend[-1] // MOE_BLOCK
    blk_ids = jnp.minimum(jnp.arange(n_blocks), n_used - 1)
    owner = jnp.sum((pend[None, :] <= (blk_ids * MOE_BLOCK)[:, None]).astype(I32), axis=1)
    block_expert = jnp.minimum(owner, N_EXPERTS - 1).astype(I32)
    e_ids = info[:, 0:MOE_TOPK].astype(I32)
    pos = (pstart[e_ids] + info[:, 4:4 + MOE_TOPK].astype(I32)).reshape(-1)

    xe = _dispatch(pos, xp, n_blocks * MOE_BLOCK, TM_ROW)
    blk = jnp.arange(n_blocks, dtype=I32)
    change = (blk < n_used) & ((blk == 0) | (block_expert != jnp.roll(block_expert, 1)))
    first_change_from = jnp.flip(lax.cummin(jnp.flip(jnp.where(change, blk, n_blocks))))
    nxt = jnp.concatenate([first_change_from[1:], jnp.full((1,), n_blocks, I32)])
    next_expert = jnp.where(nxt < n_blocks, block_expert[jnp.minimum(nxt, n_blocks - 1)], -1).astype(I32)
    yb = _experts(block_expert, next_expert, n_used.reshape(1).astype(I32), xe, w_e_in, w_e_out, layer)
    return _combine(pos, yb, x, info, row(ln_g), row(ln_b), TM_COMBINE)


def kernel(x, mem, positions, w_in, b_in, nsa_cmp_pos, nsa_cmp_w1, nsa_cmp_w2, w_nsa_o, w_pool, pool_scale, w_pool_o, mla_q_norm, w_mla_uq, mla_kv_norm, w_mla_ukv, w_mla_o, conv_w, conv_b, conv_ln_g, conv_ln_b, w_conv_o, w_out, ln_mix_g, ln_mix_b, w_xq, w_xk, w_xv, w_xo, ln_x_g, ln_x_b, w_group, b_group, w_router, b_router, w_expert_in, w_expert_out, ln_ffn_g, ln_ffn_b):
    batch, s_len, d = x.shape
    assert batch == 1 and d == D_MODEL and s_len % (2 * FLASH_TK) == 0 and s_len % FLASH_TQ == 0
    x = x[0]
    xb = x.astype(BF16)
    mem_b = mem[0].astype(BF16)
    pos_col = positions[0].astype(F32)[:, None]
    tabs = _static_tables(s_len)
    proj = _layout_projection(w_in, b_in)
    for l in range(w_in.shape[0]):
        x, xb = _hybrid_mixer(x, xb, pos_col, tabs, [a[l] for a in proj], nsa_cmp_pos[l], nsa_cmp_w1[l], nsa_cmp_w2[l],
                              w_nsa_o[l], w_pool[l], pool_scale[l], w_pool_o[l], mla_q_norm[l], w_mla_uq[l],
                              mla_kv_norm[l], w_mla_ukv[l], w_mla_o[l], conv_w[l], conv_b[l], conv_ln_g[l],
                              conv_ln_b[l], w_conv_o[l], w_out[l], ln_mix_g[l], ln_mix_b[l])
        x, xb, xp = _cross_attention(x, xb, mem_b, w_xq[l], w_xk[l], w_xv[l], w_xo[l], ln_x_g[l], ln_x_b[l])
        x, xb = _hier_moe(x, xb, xp, w_group[l], b_group[l], w_router[l], b_router[l], w_expert_in,
                          w_expert_out, l, ln_ffn_g[l], ln_ffn_b[l])
    return x[None]
```

```python
import functools

import numpy as np
import jax
import jax.numpy as jnp
from jax import lax
from jax.experimental import pallas as pl
from jax.experimental.pallas import tpu as pltpu

F32 = jnp.float32
BF16 = jnp.bfloat16
I32 = jnp.int32

D_MODEL = 2048
NSA_HEADS = 8
NSA_GROUPS = 2
NSA_HPG = NSA_HEADS // NSA_GROUPS
NSA_DH = 64
CMP_LEN = 32
CMP_STRIDE = 16
CMP_HIDDEN = 128
SEL_LEN = 64
SEL_TOPN = 16
WINDOW = 512
FORCE_SCORE = 1.0e4
POOL_GROUPS = 4
POOL_WINDOWS = (2, 4, 8, 16)
POOL_WIDTH = 512
POOL_GW = POOL_WIDTH // POOL_GROUPS
MLA_HEADS = 8
MLA_Q_RANK = 512
MLA_KV_RANK = 256
MLA_NOPE = 64
MLA_ROPE = 32
MLA_DV = 64
ROPE_BASE = 10000.0
CONV_CH = 512
CONV_K = 31
N_BRANCH = 4
X_HEADS = 4
X_DH = 128
MOE_GROUPS = 4
MOE_EPG = 8
N_EXPERTS = MOE_GROUPS * MOE_EPG
MOE_TOPK = 2
EXPERT_FF = 512
MOE_BLOCK = 256
LN_EPS = 1e-5
RMS_EPS = 1e-6
DEPTH = 2
DN_ALPHA = (2 * DEPTH) ** 0.25
IN_SIZES = (NSA_HEADS * NSA_DH, 6 * NSA_GROUPS * NSA_DH, 3 * NSA_HEADS, POOL_WIDTH,
            MLA_Q_RANK, MLA_KV_RANK, MLA_ROPE, 2 * CONV_CH, N_BRANCH * D_MODEL)

LANE = 128
VMEM_LIMIT = 56 * 1024 * 1024
NEG = -1.0e30
FLASH_TQ = 1024
FLASH_TK = 1024
LOG2E = 1.4426950408889634

TM_PROJ = 2048
TN_PROJ = 7 * LANE
TN_MERGE = 1024
TQ_CMP = 256
TM_ROW = 512
TM_MERGE = 256
TM_COMBINE = 512

SLOT_Q = 0
SLOT_KV = 8
SLOT_POOL = 20
SLOT_CQ = 24
SLOT_CKV = 28
SLOT_KR = 30
SLOT_CONV = 32
SLOT_GATE = 40
N_SLOTS1 = 42
N1 = N_SLOTS1 * LANE


def _cparams(sem, vmem=VMEM_LIMIT):
    return pltpu.CompilerParams(dimension_semantics=sem, vmem_limit_bytes=vmem)


def _sigmoid(x):
    return 0.5 * jnp.tanh(0.5 * x) + 0.5


def _layer_norm(z, g, b):
    mu = jnp.mean(z, -1, keepdims=True)
    d = z - mu
    var = jnp.mean(d * d, -1, keepdims=True)
    return d * lax.rsqrt(var + LN_EPS) * g + b


U32 = jnp.uint32


def _pack_rows(y):
    n = y.shape[1] // 2
    bits = lax.bitcast_convert_type(y.astype(BF16).astype(F32), U32)
    return (bits[:, n:] & jnp.uint32(0xFFFF0000)) | (bits[:, :n] >> 16)


def _unpack_rows(w):
    lo = lax.bitcast_convert_type(w << 16, F32)
    hi = lax.bitcast_convert_type(w & jnp.uint32(0xFFFF0000), F32)
    return jnp.concatenate([lo, hi], axis=1)


def _dot(a, b):
    return jnp.dot(a, b, preferred_element_type=F32)


def _dot_nt(a, b):
    return lax.dot_general(a, b, (((1,), (1,)), ((), ())), preferred_element_type=F32)


def _dot3(a, b):
    hi = a.astype(BF16)
    r1 = a - hi.astype(F32)
    mid = r1.astype(BF16)
    lo = (r1 - mid.astype(F32)).astype(BF16)
    return _dot(hi, b) + _dot(mid, b) + _dot(lo, b)


def _mm_kernel(a_ref, b_ref, bias_ref, o_ref):
    o_ref[...] = (_dot(a_ref[...], b_ref[...]) + bias_ref[...]).astype(o_ref.dtype)


def _matmul(a, b, bias, tm, tn, out_dtype):
    m, k = a.shape
    n = b.shape[1]
    return pl.pallas_call(
        _mm_kernel,
        grid=(m // tm, n // tn),
        in_specs=[pl.BlockSpec((tm, k), lambda i, j: (i, 0)),
                  pl.BlockSpec((k, tn), lambda i, j: (0, j)),
                  pl.BlockSpec((1, tn), lambda i, j: (0, j))],
        out_specs=pl.BlockSpec((tm, tn), lambda i, j: (i, j)),
        out_shape=jax.ShapeDtypeStruct((m, n), out_dtype),
        compiler_params=_cparams(("parallel", "arbitrary")),
        name="matmul",
    )(a, b, bias)


def _mm_ln_kernel(a_ref, w_ref, x_ref, g_ref, b_ref, xo_ref, xb_ref):
    h = _dot(a_ref[...], w_ref[...])
    y = _layer_norm(DN_ALPHA * x_ref[...] + h, g_ref[...], b_ref[...])
    xo_ref[...] = y
    xb_ref[...] = y.astype(BF16)


def _matmul_res_ln(a, w, x, g, b, tm):
    m, k = a.shape
    d = w.shape[1]
    return pl.pallas_call(
        _mm_ln_kernel,
        grid=(m // tm,),
        in_specs=[pl.BlockSpec((tm, k), lambda i: (i, 0)),
                  pl.BlockSpec((k, d), lambda i: (0, 0)),
                  pl.BlockSpec((tm, d), lambda i: (i, 0)),
                  pl.BlockSpec((1, d), lambda i: (0, 0)),
                  pl.BlockSpec((1, d), lambda i: (0, 0))],
        out_specs=[pl.BlockSpec((tm, d), lambda i: (i, 0)),
                   pl.BlockSpec((tm, d), lambda i: (i, 0))],
        out_shape=[jax.ShapeDtypeStruct((m, d), F32), jax.ShapeDtypeStruct((m, d), BF16)],
        compiler_params=_cparams(("parallel",)),
        name="matmul_res_ln",
    )(a, w, x, g, b)


def _compress_kernel(a_ref, w1_ref, pos_ref, w2_ref, o_ref):
    nch = a_ref.shape[1]
    hh = _dot(a_ref[0], w1_ref[0])
    pp = _dot(pos_ref[0], w1_ref[0])
    pos_term = pp[0:1, :CMP_HIDDEN] + pp[1:2, CMP_HIDDEN:]
    h2_next = pltpu.roll(hh[:, CMP_HIDDEN:], nch - 1, 0)
    z = hh[:, :CMP_HIDDEN] + h2_next + pos_term
    hid = 0.5 * z * (1.0 + jnp.tanh(0.7978845608028654 * (z + 0.044715 * z * z * z)))
    o_ref[0] = _dot(hid.astype(BF16), w2_ref[0]).astype(o_ref.dtype)


def _nsa_compress(chunks, w1cat, pos2, w2pad):
    n4, nch, kdim = chunks.shape
    return pl.pallas_call(
        _compress_kernel,
        grid=(n4,),
        in_specs=[pl.BlockSpec((1, nch, kdim), lambda c: (c, 0, 0)),
                  pl.BlockSpec((1, kdim, 2 * CMP_HIDDEN), lambda c: (c // NSA_GROUPS, 0, 0)),
                  pl.BlockSpec((1, 8, kdim), lambda c: (c // NSA_GROUPS, 0, 0)),
                  pl.BlockSpec((1, CMP_HIDDEN, LANE), lambda c: (c // NSA_GROUPS, 0, 0))],
        out_specs=pl.BlockSpec((1, nch, LANE), lambda c: (c, 0, 0)),
        out_shape=jax.ShapeDtypeStruct((n4, nch, LANE), BF16),
        compiler_params=_cparams(("arbitrary",)),
        name="nsa_compress",
    )(chunks, w1cat, pos2, w2pad)


CMP_WIDTH_STEPS = 4


def _nsa_cmp_kernel(q_ref, kc_ref, vc_ref, mband_ref, ocmp_ref, selneg_ref, score_sc, *, tq, n_cmp, n_sel, top_n):
    t0 = pl.program_id(0) * tq
    nch = kc_ref.shape[1]
    row = lax.broadcasted_iota(I32, (NSA_HPG * tq, 1), 0)
    t_row = t0 + (row & (tq - 1))

    def softmax_part(width):
        col = lax.broadcasted_iota(I32, (1, width), 1)
        vis = (col * CMP_STRIDE + (CMP_LEN - 1) <= t_row) & (col < n_cmp)
        for g in range(NSA_GROUPS):
            qs = jnp.concatenate(
                [q_ref[:, (g * NSA_HPG + h) * LANE:(g * NSA_HPG + h + 1) * LANE] for h in range(NSA_HPG)], axis=0)
            s = _dot_nt(qs, kc_ref[g, 0:width, :])
            s = jnp.where(vis, s, NEG)
            m = jnp.max(s, -1, keepdims=True)
            p = jnp.where(vis, jnp.exp2(s - m), 0.0)
            l = jnp.sum(p, -1, keepdims=True)
            p = p * (1.0 / jnp.maximum(l, 1e-30))
            o = _dot(p.astype(BF16), vc_ref[g, 0:width, :])
            for h in range(NSA_HPG):
                hh = g * NSA_HPG + h
                ocmp_ref[:, hh * NSA_DH:(hh + 1) * NSA_DH] = o[h * tq:(h + 1) * tq, :NSA_DH]
            imp = p[0:tq]
            for h in range(1, NSA_HPG):
                imp = imp + p[h * tq:(h + 1) * tq]
            score_sc[g * tq:(g + 1) * tq, :] = _dot3(imp, mband_ref[0:width, :])

    last_vis = (t0 + tq - CMP_LEN) // CMP_STRIDE
    step_w = nch // CMP_WIDTH_STEPS
    variant = jnp.minimum(last_vis // step_w, CMP_WIDTH_STEPS - 1)
    for v in range(CMP_WIDTH_STEPS):
        pl.when(variant == v)(functools.partial(softmax_part, (v + 1) * step_w))

    blk = lax.broadcasted_iota(I32, (NSA_GROUPS * tq, n_sel), 1).astype(F32)
    row2 = lax.broadcasted_iota(I32, (NSA_GROUPS * tq, 1), 0)
    cur = ((t0 + (row2 & (tq - 1))) // SEL_LEN).astype(F32)
    valid = blk <= cur
    forced = (blk == 0.0) | (blk == cur) | (blk == cur - 1.0)
    assert FORCE_SCORE > NSA_HPG * (SEL_LEN // CMP_STRIDE + CMP_LEN // CMP_STRIDE - 1) and top_n > 3
    score = jnp.where(forced, -2.0, score_sc[...])
    score = jnp.where(valid, score, -1.0)

    def pick_one(_, sc):
        mx = jnp.max(sc, -1, keepdims=True)
        first = jnp.min(jnp.where(sc == mx, blk, float(n_sel)), -1, keepdims=True)
        return jnp.where(blk == first, -2.0, sc)

    sc = lax.fori_loop(0, top_n - 3, pick_one, score)
    selneg = jnp.where(valid & (sc == -2.0), 0.0, -1.0).astype(selneg_ref.dtype)
    for g in range(NSA_GROUPS):
        selneg_ref[:, g * n_sel:(g + 1) * n_sel] = selneg[g * tq:(g + 1) * tq]


def _nsa_cmp(u1, kcvc, mband, tq):
    s_len = u1.shape[0]
    nch = kcvc.shape[1]
    n_sel = s_len // SEL_LEN
    n_cmp = (s_len - CMP_LEN) // CMP_STRIDE + 1
    kern = functools.partial(_nsa_cmp_kernel, tq=tq, n_cmp=n_cmp, n_sel=n_sel, top_n=min(SEL_TOPN, n_sel))
    return pl.pallas_call(
        kern,
        grid=(s_len // tq,),
        in_specs=[pl.BlockSpec((tq, NSA_HEADS * LANE), lambda i: (i, SLOT_Q // NSA_HEADS)),
                  pl.BlockSpec((NSA_GROUPS, nch, LANE), lambda i: (0, 0, 0)),
                  pl.BlockSpec((NSA_GROUPS, nch, LANE), lambda i: (1, 0, 0)),
                  pl.BlockSpec((nch, n_sel), lambda i: (0, 0))],
        out_specs=[pl.BlockSpec((tq, NSA_HEADS * NSA_DH), lambda i: (i, 0)),
                   pl.BlockSpec((tq, NSA_GROUPS * n_sel), lambda i: (i, 0))],
        out_shape=[jax.ShapeDtypeStruct((s_len, NSA_HEADS * NSA_DH), F32),
                   jax.ShapeDtypeStruct((s_len, NSA_GROUPS * n_sel), BF16)],
        scratch_shapes=[pltpu.VMEM((NSA_GROUPS * tq, n_sel), F32)],
        compiler_params=_cparams(("parallel",)),
        name="nsa_cmp_topk",
    )(u1, kcvc, kcvc, mband)


def _flash_kernel(qi_ref, kj_ref, first_ref, last_ref, mode_ref, *refs, hpg, hpk, select, modes, dv,
                  split_masked):
    if select:
        q_ref, k_ref, v_ref, sel_ref, scat_ref, eslot_ref, o_ref, m_sc, acc_sc = refs
    else:
        q_ref, k_ref, v_ref, o_ref, m_sc, acc_sc = refs
    p_idx = pl.program_id(1)
    tq = q_ref.shape[0]
    tk = k_ref.shape[0]

    @pl.when(first_ref[p_idx] == 1)
    def _():
        m_sc[...] = jnp.full(m_sc.shape, NEG, F32)
        acc_sc[...] = jnp.zeros(acc_sc.shape, F32)

    def step(mask_mode):
        k_all = k_ref[...]
        lane = lax.broadcasted_iota(I32, (1, v_ref.shape[1]), 1)
        v_all = v_ref[...] + ((lane & (LANE - 1)) == dv).astype(BF16)
        if select:
            nk = hpg // hpk
            n_sel = sel_ref.shape[1] // nk
            k_all = k_all + jnp.concatenate([eslot_ref[...]] * nk, axis=1)
            scat = scat_ref[kj_ref[p_idx]]
            biases = [(_dot(sel_ref[:, g * n_sel:(g + 1) * n_sel], scat) * (-NEG)).astype(BF16) for g in range(nk)]
        hq, hk_ = tq // 2, tk // 2
        if mask_mode == 0 or not split_masked:
            blocks = ((0, tq, 0, tk),)
        else:
            blocks = {1: ((0, hq, 0, hk_), (hq, tq, 0, tk)), 2: ((0, hq, 0, tk), (hq, tq, hk_, tk))}[mask_mode]
        for r0, r1, c0, c1 in blocks:
            if mask_mode:
                r = lax.broadcasted_iota(I32, (r1 - r0, c1 - c0), 0) + r0
                c = lax.broadcasted_iota(I32, (r1 - r0, c1 - c0), 1) + c0
                keep = (c <= r) if mask_mode == 1 else (c > r)
            for h in range(hpg):
                hk = h // hpk
                q = q_ref[r0:r1, h * LANE:(h + 1) * LANE]
                if select:
                    q = q + biases[hk][r0:r1]
                s = _dot_nt(q, k_all[c0:c1, hk * LANE:(hk + 1) * LANE])
                if mask_mode:
                    s = jnp.where(keep, s, NEG)
                m_prev = m_sc[h, r0:r1]
                m_new = jnp.maximum(m_prev, jnp.max(s, -1, keepdims=True))
                p = jnp.exp2(s - jnp.concatenate([m_new] * ((c1 - c0) // LANE), axis=1))
                acc_sc[h, r0:r1] = (jnp.exp2(m_prev - m_new) * acc_sc[h, r0:r1]
                                    + _dot(p.astype(BF16), v_all[c0:c1, hk * LANE:(hk + 1) * LANE]))
                m_sc[h, r0:r1] = m_new

    for mm in modes:
        pl.when(mode_ref[p_idx] == mm)(functools.partial(step, mm))

    @pl.when(last_ref[p_idx] == 1)
    def _():
        for h in range(hpg):
            acc = acc_sc[h]
            o = acc[:, :dv] * (1.0 / acc[:, dv:dv + 1])
            o_ref[:, h * dv:(h + 1) * dv] = o.astype(o_ref.dtype)


def _pair_tables(n_tiles, window):
    rows = []
    for i in range(n_tiles):
        js = ([i - 1] if i > 0 else []) + [i] if window else list(range(i + 1))
        for j in js:
            mode = 1 if j == i else (2 if window else 0)
            rows.append((i, j, int(j == js[0]), int(j == i), mode))
    tab = np.asarray(rows, np.int32)
    return [jnp.asarray(tab[:, c]) for c in range(tab.shape[1])]


def _flash(q_arr, q_blk0, k_arr, k_blk0, v_arr, v_blk0, *, n_groups, hpg, hpk, window,
           out_dtype, dv, tq, tk, sel=None, scat=None, eslot=None):
    s_len = q_arr.shape[0]
    assert tq == tk and (not window or tk == WINDOW)
    nk = hpg // hpk
    select = sel is not None
    tables = _pair_tables(s_len // tq, window)
    n_pairs = int(tables[0].shape[0])
    modes = (1, 2) if window else (0, 1)

    def qmap(g, p, qi, kj, *_):
        return (qi[p], q_blk0 + g)

    def kmap(g, p, qi, kj, *_):
        return (kj[p], k_blk0 + g)

    def vmap_(g, p, qi, kj, *_):
        return (kj[p], v_blk0 + g)

    def omap(g, p, qi, kj, *_):
        return (qi[p], g)

    in_specs = [pl.BlockSpec((tq, hpg * LANE), qmap),
                pl.BlockSpec((tk, nk * LANE), kmap),
                pl.BlockSpec((tk, nk * LANE), vmap_)]
    args = [q_arr, k_arr, v_arr]
    if select:
        in_specs += [pl.BlockSpec((tq, sel.shape[1] // n_groups), omap),
                     pl.BlockSpec(scat.shape, lambda g, p, *_: (0, 0, 0)),
                     pl.BlockSpec(eslot.shape, lambda g, p, *_: (0, 0))]
        args += [sel, scat, eslot]
    kern = functools.partial(_flash_kernel, hpg=hpg, hpk=hpk, select=select, modes=modes, dv=dv,
                             split_masked=tq >= 2 * WINDOW)
    grid_spec = pltpu.PrefetchScalarGridSpec(
        num_scalar_prefetch=len(tables),
        grid=(n_groups, n_pairs),
        in_specs=in_specs,
        out_specs=pl.BlockSpec((tq, hpg * dv), omap),
        scratch_shapes=[pltpu.VMEM((hpg, tq, LANE), F32), pltpu.VMEM((hpg, tq, LANE), F32)],
    )
    return pl.pallas_call(
        kern,
        grid_spec=grid_spec,
        out_shape=jax.ShapeDtypeStruct((s_len, n_groups * hpg * dv), out_dtype),
        compiler_params=_cparams(("parallel", "arbitrary")),
        name="flash_sel" if select else ("flash_win" if window else "flash_causal"),
    )(*tables, *args)


def _mla_prep_kernel(cq_ref, ckv_ref, kr_ref, pos_ref, inv_ref, qn_ref, kvn_ref, wuq_ref, wk_ref, wv_ref,
                     q_ref, k_ref, v_ref):
    def rms(x, g):
        return x * lax.rsqrt(jnp.mean(x * x, -1, keepdims=True) + RMS_EPS) * g

    ang = pos_ref[...] * inv_ref[...]
    cos, sin = jnp.cos(ang), jnp.sin(ang)
    hw = MLA_HEADS * LANE
    qh = _dot(rms(cq_ref[...].astype(F32), qn_ref[...]).astype(BF16), wuq_ref[...])
    scale = (MLA_NOPE + MLA_ROPE) ** -0.5 * LOG2E
    cos_t = jnp.concatenate([cos] * MLA_HEADS, axis=1)
    sin_t = jnp.concatenate([sin] * MLA_HEADS, axis=1)
    q_ref[...] = ((qh[:, :hw] * cos_t + qh[:, hw:] * sin_t) * scale).astype(BF16)
    ckv = rms(ckv_ref[...].astype(F32), kvn_ref[...]).astype(BF16)
    kr = kr_ref[...].astype(F32)
    k_rope = kr[:, :LANE] * cos + kr[:, LANE:] * sin
    k_ref[...] = (_dot(ckv, wk_ref[...]) + jnp.concatenate([k_rope] * MLA_HEADS, axis=1)).astype(BF16)
    v_ref[...] = _dot(ckv, wv_ref[...]).astype(BF16)


def _mla_prep(u1, pos_col, inv_slot, qn, kvn, wuq, wk, wv, tm):
    s_len = u1.shape[0]
    hw = MLA_HEADS * LANE
    full = lambda a: pl.BlockSpec(a.shape, lambda i: (0,) * a.ndim)
    out = jax.ShapeDtypeStruct((s_len, hw), BF16)
    return pl.pallas_call(
        _mla_prep_kernel,
        grid=(s_len // tm,),
        in_specs=[pl.BlockSpec((tm, MLA_Q_RANK), lambda i: (i, SLOT_CQ * LANE // MLA_Q_RANK)),
                  pl.BlockSpec((tm, MLA_KV_RANK), lambda i: (i, SLOT_CKV * LANE // MLA_KV_RANK)),
                  pl.BlockSpec((tm, 2 * LANE), lambda i: (i, SLOT_KR // 2)),
                  pl.BlockSpec((tm, 1), lambda i: (i, 0)),
                  full(inv_slot), full(qn), full(kvn), full(wuq), full(wk), full(wv)],
        out_specs=[pl.BlockSpec((tm, hw), lambda i: (i, 0))] * 3,
        out_shape=[out, out, out],
        compiler_params=_cparams(("parallel",)),
        name="mla_prep",
    )(u1, u1, u1, pos_col, inv_slot, qn, kvn, wuq, wk, wv)


POOL_HALO = 16


def _pool_kernel(cur_ref, halo_ref, wp_ref, scale_ref, o_ref):
    i = pl.program_id(0)
    tm = cur_ref.shape[0]
    halo = jnp.where(i > 0, halo_ref[...].astype(F32), 0.0)
    x = jnp.concatenate([halo, cur_ref[...].astype(F32)], axis=0)
    sums = {1: x}
    w = 1
    while w < max(POOL_WINDOWS):
        a = sums[w]
        sums[2 * w] = a[w:] + a[:-w]
        w *= 2
    t = (i * tm + lax.broadcasted_iota(I32, (tm, 1), 0) + 1).astype(F32)
    outs = []
    for gi, w in enumerate(POOL_WINDOWS):
        lo = gi * POOL_GW
        start = POOL_HALO - (w - 1)
        win = sums[w][start:start + tm, lo:lo + POOL_GW]
        mean = win / jnp.minimum(t, float(w))
        pooled = mean - x[POOL_HALO:, lo:lo + POOL_GW]
        outs.append(_dot(pooled.astype(BF16), wp_ref[gi]))
    o_ref[...] = (jnp.concatenate(outs, axis=1) * scale_ref[...]).astype(o_ref.dtype)


def _pool(u1, w_pool, pool_scale, tm):
    s_len = u1.shape[0]
    blk = SLOT_POOL * LANE // POOL_WIDTH
    return pl.pallas_call(
        _pool_kernel,
        grid=(s_len // tm,),
        in_specs=[pl.BlockSpec((tm, POOL_WIDTH), lambda i: (i, blk)),
                  pl.BlockSpec((POOL_HALO, POOL_WIDTH),
                               lambda i: (jnp.maximum(i * (tm // POOL_HALO) - 1, 0), blk)),
                  pl.BlockSpec(w_pool.shape, lambda i: (0, 0, 0)),
                  pl.BlockSpec((1, POOL_WIDTH), lambda i: (0, 0))],
        out_specs=pl.BlockSpec((tm, POOL_WIDTH), lambda i: (i, 0)),
        out_shape=jax.ShapeDtypeStruct((s_len, POOL_WIDTH), BF16),
        compiler_params=_cparams(("parallel",)),
        name="pool",
    )(u1, u1, w_pool, pool_scale)


CONV_HALO = 32


def _conv_kernel(cur_ref, halo_ref, w_ref, b_ref, g_ref, beta_ref, o_ref, hbuf):
    i = pl.program_id(0)
    tm = cur_ref.shape[0]

    def glu(u):
        u = u.astype(F32)
        return u[:, :CONV_CH] * _sigmoid(u[:, CONV_CH:])

    hbuf[0:CONV_HALO, :] = jnp.where(i > 0, glu(halo_ref[...]), 0.0)
    hbuf[CONV_HALO:, :] = glu(cur_ref[...])
    acc = jnp.zeros((tm, CONV_CH), F32) + b_ref[...]
    for k in range(CONV_K):
        off = CONV_HALO - (CONV_K - 1) + k
        acc = acc + hbuf[off:off + tm, :] * w_ref[k:k + 1, :]
    y = _layer_norm(acc, g_ref[...], beta_ref[...])
    o_ref[...] = (y * _sigmoid(y)).astype(o_ref.dtype)


def _conv(u1, conv_w, conv_b, ln_g, ln_b, tm):
    s_len = u1.shape[0]
    blk = SLOT_CONV * LANE // (2 * CONV_CH)
    row = lambda a: pl.BlockSpec(a.shape, lambda i: (0, 0))
    return pl.pallas_call(
        _conv_kernel,
        grid=(s_len // tm,),
        in_specs=[pl.BlockSpec((tm, 2 * CONV_CH), lambda i: (i, blk)),
                  pl.BlockSpec((CONV_HALO, 2 * CONV_CH),
                               lambda i: (jnp.maximum(i * (tm // CONV_HALO) - 1, 0), blk)),
                  row(conv_w), row(conv_b), row(ln_g), row(ln_b)],
        out_specs=pl.BlockSpec((tm, CONV_CH), lambda i: (i, 0)),
        out_shape=jax.ShapeDtypeStruct((s_len, CONV_CH), BF16),
        scratch_shapes=[pltpu.VMEM((tm + CONV_HALO, CONV_CH), F32)],
        compiler_params=_cparams(("parallel",)),
        name="conv_module",
    )(u1, u1, conv_w, conv_b, ln_g, ln_b)


def _merge_kernel(ocmp_ref, osel_ref, owin_ref, gate_ref, gexp_ref, pool_ref, mla_ref, conv_ref, um_ref,
                  wn_ref, wp_ref, wm_ref, wc_ref, o_ref):
    sg = _sigmoid(gate_ref[...].astype(F32))
    nsa = (_dot3(sg, gexp_ref[0]) * ocmp_ref[...] + _dot3(sg, gexp_ref[1]) * osel_ref[...]
           + _dot3(sg, gexp_ref[2]) * owin_ref[...])
    branches = (_dot(nsa.astype(BF16), wn_ref[...]), _dot(pool_ref[...], wp_ref[...]),
                _dot(mla_ref[...], wm_ref[...]), _dot(conv_ref[...], wc_ref[...]))
    merged = None
    for j, br in enumerate(branches):
        term = _sigmoid(um_ref[:, j * D_MODEL:(j + 1) * D_MODEL].astype(F32)) * br
        merged = term if merged is None else merged + term
    o_ref[...] = merged.astype(o_ref.dtype)


def _merge(o_cmp, o_sel, o_win, u1, gexp, pooled, o_mla, conv, um, wn, wp, wm, wc, tm):
    s_len = u1.shape[0]
    tile = lambda w: pl.BlockSpec((tm, w), lambda i: (i, 0))
    full = lambda a: pl.BlockSpec(a.shape, lambda i: (0,) * a.ndim)
    return pl.pallas_call(
        _merge_kernel,
        grid=(s_len // tm,),
        in_specs=[tile(512), tile(512), tile(512),
                  pl.BlockSpec((tm, LANE), lambda i: (i, SLOT_GATE)),
                  full(gexp), tile(512), tile(512), tile(512), tile(N_BRANCH * D_MODEL),
                  full(wn), full(wp), full(wm), full(wc)],
        out_specs=tile(D_MODEL),
        out_shape=jax.ShapeDtypeStruct((s_len, D_MODEL), BF16),
        compiler_params=_cparams(("parallel",)),
        name="branch_merge",
    )(o_cmp, o_sel, o_win, u1, gexp, pooled, o_mla, conv, um, wn, wp, wm, wc)


def _xattn_kernel(xb_ref, x_ref, wq_ref, k_ref, v_ref, wo_ref, g_ref, b_ref, xo_ref, xob_ref, xp_ref):
    q = _dot(xb_ref[...], wq_ref[...]).astype(BF16)
    k = k_ref[...]
    v = v_ref[...]
    outs = []
    for h in range(X_HEADS):
        sl = slice(h * X_DH, (h + 1) * X_DH)
        s = _dot_nt(q[:, sl], k[:, sl]) * (X_DH ** -0.5)
        m = jnp.max(s, -1, keepdims=True)
        p = jnp.exp(s - m)
        p = p * (1.0 / jnp.sum(p, -1, keepdims=True))
        outs.append(_dot(p.astype(BF16), v[:, sl]))
    o = jnp.concatenate(outs, axis=1).astype(BF16)
    y = _layer_norm(DN_ALPHA * x_ref[...] + _dot(o, wo_ref[...]), g_ref[...], b_ref[...])
    xo_ref[...] = y
    xob_ref[...] = y.astype(BF16)
    xp_ref[...] = _pack_rows(y)


def _xattn(xb, x, wq, k, v, wo, g, b, tm):
    s_len = x.shape[0]
    tile = lambda: pl.BlockSpec((tm, D_MODEL), lambda i: (i, 0))
    full = lambda a: pl.BlockSpec(a.shape, lambda i: (0,) * a.ndim)
    return pl.pallas_call(
        _xattn_kernel,
        grid=(s_len // tm,),
        in_specs=[tile(), tile(), full(wq), full(k), full(v), full(wo), full(g), full(b)],
        out_specs=[tile(), tile(), pl.BlockSpec((tm, D_MODEL // 2), lambda i: (i, 0))],
        out_shape=[jax.ShapeDtypeStruct((s_len, D_MODEL), F32), jax.ShapeDtypeStruct((s_len, D_MODEL), BF16),
                   jax.ShapeDtypeStruct((s_len, D_MODEL // 2), U32)],
        compiler_params=_cparams(("parallel",)),
        name="cross_attention_ln",
    )(xb, x, wq, k, v, wo, g, b)


ROUTE_OFF = MOE_GROUPS


def _router_kernel(xb_ref, w_ref, b_ref, tri_ref, info_ref, cnt_ref, carry):
    @pl.when(pl.program_id(0) == 0)
    def _():
        carry[...] = jnp.zeros(carry.shape, F32)

    logits = _dot(xb_ref[...], w_ref[...]) + b_ref[...]
    lane = lax.broadcasted_iota(I32, logits.shape, 1).astype(F32)
    is_g = lane < float(MOE_GROUPS)
    neg_inf = -jnp.inf
    gl = jnp.where(is_g, logits, neg_inf)
    gmax = jnp.max(gl, -1, keepdims=True)
    g_sel = jnp.min(jnp.where(gl == gmax, lane, float(LANE)), -1, keepdims=True)
    g_w = 1.0 / jnp.sum(jnp.where(is_g, jnp.exp(gl - gmax), 0.0), -1, keepdims=True)
    lo = ROUTE_OFF + MOE_EPG * g_sel
    in_g = (lane >= lo) & (lane < lo + MOE_EPG)
    el = jnp.where(in_g, logits, neg_inf)
    emax = jnp.max(el, -1, keepdims=True)
    e = jnp.where(in_g, jnp.exp(el - emax), 0.0)
    p = e / jnp.sum(e, -1, keepdims=True)
    pm = jnp.where(in_g, p, -1.0)
    p1 = jnp.max(pm, -1, keepdims=True)
    i1 = jnp.min(jnp.where(pm == p1, lane, float(LANE)), -1, keepdims=True)
    pm2 = jnp.where(lane == i1, -1.0, pm)
    p2 = jnp.max(pm2, -1, keepdims=True)
    i2 = jnp.min(jnp.where(pm2 == p2, lane, float(LANE)), -1, keepdims=True)
    denom = p1 + p2
    gate1 = g_w * p1 / denom
    gate2 = g_w * p2 / denom

    oh1 = (lane == i1).astype(BF16)
    oh2 = (lane == i2).astype(BF16)
    incl1 = _dot(tri_ref[...], oh1)
    incl2 = _dot(tri_ref[...], oh2)
    tot1 = jnp.sum(oh1.astype(F32), 0, keepdims=True)
    tot2 = jnp.sum(oh2.astype(F32), 0, keepdims=True)
    base = carry[...]
    rank1 = jnp.sum(jnp.where(lane == i1, base + incl1, 0.0), -1, keepdims=True) - 1.0
    rank2 = jnp.sum(jnp.where(lane == i2, base + tot1 + incl2, 0.0), -1, keepdims=True) - 1.0
    carry[...] = base + tot1 + tot2
    cnt_ref[...] = carry[...]

    cols = (i1 - ROUTE_OFF, i2 - ROUTE_OFF, gate1, gate2, rank1, rank2)
    info = jnp.zeros(logits.shape, F32)
    for c, val in enumerate(cols):
        info = jnp.where(lane == float(c), val, info)
    info_ref[...] = info


def _router(xb, w_gr, b_gr, tri, tm):
    t_len = xb.shape[0]
    return pl.pallas_call(
        _router_kernel,
        grid=(t_len // tm,),
        in_specs=[pl.BlockSpec((tm, D_MODEL), lambda i: (i, 0)),
                  pl.BlockSpec(w_gr.shape, lambda i: (0, 0)),
                  pl.BlockSpec((1, LANE), lambda i: (0, 0)),
                  pl.BlockSpec((tm, tm), lambda i: (0, 0))],
        out_specs=[pl.BlockSpec((tm, LANE), lambda i: (i, 0)),
                   pl.BlockSpec((1, LANE), lambda i: (0, 0))],
        out_shape=[jax.ShapeDtypeStruct((t_len, LANE), F32), jax.ShapeDtypeStruct((1, LANE), F32)],
        scratch_shapes=[pltpu.VMEM((1, LANE), F32)],
        compiler_params=_cparams(("arbitrary",)),
        name="moe_router",
    )(xb, w_gr, b_gr, tri)


def _row_copy(src, src_row, dst, dst_row, sem):
    return pltpu.make_async_copy(src.at[pl.ds(src_row, 1)], dst.at[pl.ds(dst_row, 1)], sem)


def _dispatch_kernel(pos_ref, x_ref, xe_in_hbm, xe_hbm, sem, *, td, n_tok):
    del xe_in_hbm
    base = pl.program_id(0) * td

    def issue(t, c):
        for k in range(MOE_TOPK):
            _row_copy(x_ref, t, xe_hbm, pos_ref[k * n_tok + base + t], sem).start()
        return c

    lax.fori_loop(0, td, issue, 0, unroll=8)
    all_rows = xe_hbm.at[pl.ds(0, MOE_TOPK * td)]
    pltpu.make_async_copy(all_rows, all_rows, sem).wait()


def _dispatch(pos_flat, x, rows, td):
    t_len, d = x.shape
    zeros = jnp.zeros((rows, d), x.dtype)
    grid_spec = pltpu.PrefetchScalarGridSpec(
        num_scalar_prefetch=1,
        grid=(t_len // td,),
        in_specs=[pl.BlockSpec((td, d), lambda i, pos: (i, 0)), pl.BlockSpec(memory_space=pl.ANY)],
        out_specs=pl.BlockSpec(memory_space=pl.ANY),
        scratch_shapes=[pltpu.SemaphoreType.DMA(())],
    )
    return pl.pallas_call(
        functools.partial(_dispatch_kernel, td=td, n_tok=t_len),
        grid_spec=grid_spec,
        out_shape=jax.ShapeDtypeStruct((rows, d), x.dtype),
        input_output_aliases={2: 0},
        compiler_params=pltpu.CompilerParams(dimension_semantics=("arbitrary",)),
        name="moe_dispatch",
    )(pos_flat, x, zeros)


def _expert_weight_copies(win_hbm, wout_hbm, layer, expert, win_f32, wout_f32, sem):
    return (pltpu.make_async_copy(win_hbm.at[layer, expert], win_f32, sem.at[0]),
            pltpu.make_async_copy(wout_hbm.at[layer, expert], wout_f32, sem.at[1]))


def _expert_kernel(be_ref, next_ref, nused_ref, xe_ref, win_hbm, wout_hbm, yb_ref, win_f32, wout_f32, win_bf,
                   wout_bf, sem, *, layer):
    b = pl.program_id(0)
    used = b < nused_ref[0]
    new_expert = (b == 0) | (be_ref[b] != be_ref[jnp.maximum(b - 1, 0)])
    copies = functools.partial(_expert_weight_copies, win_hbm, wout_hbm, layer)

    @pl.when(b == 0)
    def _():
        for c in copies(be_ref[0], win_f32, wout_f32, sem):
            c.start()

    @pl.when(used & new_expert)
    def _():
        for c in copies(be_ref[b], win_f32, wout_f32, sem):
            c.wait()
        win_bf[...] = win_f32[...].astype(BF16)
        wout_bf[...] = wout_f32[...].astype(BF16)

        @pl.when(next_ref[b] >= 0)
        def _():
            for c in copies(next_ref[b], win_f32, wout_f32, sem):
                c.start()

    @pl.when(used)
    def _():
        hcat = _dot(_unpack_rows(xe_ref[...]).astype(BF16), win_bf[...])
        a = hcat[:, :EXPERT_FF]
        act = (a * _sigmoid(a) * hcat[:, EXPERT_FF:]).astype(BF16)
        yb_ref[...] = _pack_rows(_dot(act, wout_bf[...]))

    @pl.when(b >= nused_ref[0])
    def _():
        yb_ref[...] = jnp.zeros(yb_ref.shape, U32)


def _experts(block_expert, next_expert, n_used, xe, w_e_in, w_e_out, layer):
    rows, half = xe.shape
    d = 2 * half
    n_blocks = rows // MOE_BLOCK
    grid_spec = pltpu.PrefetchScalarGridSpec(
        num_scalar_prefetch=3,
        grid=(n_blocks,),
        in_specs=[pl.BlockSpec((MOE_BLOCK, half), lambda b, *_: (b, 0)),
                  pl.BlockSpec(memory_space=pl.ANY),
                  pl.BlockSpec(memory_space=pl.ANY)],
        out_specs=pl.BlockSpec((MOE_BLOCK, half), lambda b, *_: (b, 0)),
        scratch_shapes=[pltpu.VMEM((d, 2 * EXPERT_FF), F32), pltpu.VMEM((EXPERT_FF, d), F32),
                        pltpu.VMEM((d, 2 * EXPERT_FF), BF16), pltpu.VMEM((EXPERT_FF, d), BF16),
                        pltpu.SemaphoreType.DMA((2,))],
    )
    return pl.pallas_call(
        functools.partial(_expert_kernel, layer=layer),
        grid_spec=grid_spec,
        out_shape=jax.ShapeDtypeStruct((rows, half), U32),
        compiler_params=_cparams(("arbitrary",)),
        name="moe_experts",
    )(block_expert, next_expert, n_used, xe, w_e_in, w_e_out)


def _combine_kernel(pos_ref, yb_hbm, x_ref, info_ref, g_ref, b_ref, xo_ref, xob_ref, buf, sem):
    tm = x_ref.shape[0]
    i = pl.program_id(0)
    slot = i & 1
    n_tok = tm * pl.num_programs(0)

    def gather(tile, slot_):
        def issue(t, c):
            for k in range(MOE_TOPK):
                _row_copy(yb_hbm, pos_ref[k * n_tok + tile * tm + t], buf.at[slot_, k], t,
                          sem.at[slot_]).start()
            return c
        lax.fori_loop(0, tm, issue, 0, unroll=8)

    @pl.when(i == 0)
    def _():
        gather(0, 0)

    @pl.when(i + 1 < pl.num_programs(0))
    def _():
        gather(i + 1, 1 - slot)

    pltpu.make_async_copy(buf.at[slot], buf.at[slot], sem.at[slot]).wait()
    info = info_ref[...]
    y = info[:, 2:3] * _unpack_rows(buf[slot, 0]) + info[:, 3:4] * _unpack_rows(buf[slot, 1])
    z = _layer_norm(DN_ALPHA * x_ref[...] + y, g_ref[...], b_ref[...])
    xo_ref[...] = z
    xob_ref[...] = z.astype(BF16)


def _combine(pos_flat, yb, x, info, g, b, tm):
    t_len, d = x.shape
    grid_spec = pltpu.PrefetchScalarGridSpec(
        num_scalar_prefetch=1,
        grid=(t_len // tm,),
        in_specs=[pl.BlockSpec(memory_space=pl.ANY),
                  pl.BlockSpec((tm, d), lambda i, pos: (i, 0)),
                  pl.BlockSpec((tm, LANE), lambda i, pos: (i, 0)),
                  pl.BlockSpec((1, d), lambda i, pos: (0, 0)),
                  pl.BlockSpec((1, d), lambda i, pos: (0, 0))],
        out_specs=[pl.BlockSpec((tm, d), lambda i, pos: (i, 0)),
                   pl.BlockSpec((tm, d), lambda i, pos: (i, 0))],
        scratch_shapes=[pltpu.VMEM((2, MOE_TOPK, tm, d // 2), U32), pltpu.SemaphoreType.DMA((2,))],
    )
    return pl.pallas_call(
        _combine_kernel,
        grid_spec=grid_spec,
        out_shape=[jax.ShapeDtypeStruct((t_len, d), F32), jax.ShapeDtypeStruct((t_len, d), BF16)],
        compiler_params=_cparams(("arbitrary",)),
        name="moe_combine_ln",
    )(pos_flat, yb, x, info, g, b)


def _pad_cols(m, width):
    return jnp.pad(m, [(0, 0)] * (m.ndim - 1) + [(0, width - m.shape[-1])])


def _rot_half_cols(m):
    half = m.shape[-1] // 2
    return jnp.concatenate([-m[..., half:], m[..., :half]], -1)


def _layout_in1(m):
    offs = np.cumsum((0,) + IN_SIZES)
    o_q, o_kv, o_g, o_pool, o_cq, o_ckv, o_kr, o_conv = offs[:8]
    z = lambda n: jnp.zeros(m.shape[:-1] + (n,), m.dtype)
    parts = []
    for h in range(NSA_HEADS):
        parts += [m[..., o_q + h * NSA_DH:o_q + (h + 1) * NSA_DH] * (NSA_DH ** -0.5 * LOG2E), z(LANE - NSA_DH)]
    for c in range(6 * NSA_GROUPS):
        parts += [m[..., o_kv + c * NSA_DH:o_kv + (c + 1) * NSA_DH], z(LANE - NSA_DH)]
    parts += [m[..., o_pool:o_pool + POOL_WIDTH], m[..., o_cq:o_cq + MLA_Q_RANK], m[..., o_ckv:o_ckv + MLA_KV_RANK]]
    kr = m[..., o_kr:o_kr + MLA_ROPE]
    parts += [z(MLA_NOPE), kr, z(LANE - MLA_NOPE - MLA_ROPE), z(MLA_NOPE), _rot_half_cols(kr),
              z(LANE - MLA_NOPE - MLA_ROPE)]
    parts += [m[..., o_conv:o_conv + 2 * CONV_CH], _pad_cols(m[..., o_g:o_g + 3 * NSA_HEADS], LANE), z(LANE)]
    return jnp.concatenate(parts, -1)


def _layout_mla_q(w):
    dq = MLA_NOPE + MLA_ROPE
    z = lambda n: jnp.zeros((w.shape[0], n), w.dtype)
    a, b = [], []
    for h in range(MLA_HEADS):
        rope = w[:, h * dq + MLA_NOPE:(h + 1) * dq]
        a += [w[:, h * dq:h * dq + MLA_NOPE], rope, z(LANE - dq)]
        b += [z(MLA_NOPE), _rot_half_cols(rope), z(LANE - dq)]
    return jnp.concatenate(a + b, -1)


def _layout_mla_kv(w):
    dkv = MLA_NOPE + MLA_DV
    z = jnp.zeros((w.shape[0], LANE - MLA_NOPE), w.dtype)
    k, v = [], []
    for h in range(MLA_HEADS):
        k += [w[:, h * dkv:h * dkv + MLA_NOPE], z]
        v += [w[:, h * dkv + MLA_NOPE:(h + 1) * dkv], z]
    return jnp.concatenate(k, -1), jnp.concatenate(v, -1)


def _static_tables(s_len):
    nch = s_len // CMP_STRIDE
    n_cmp = (s_len - CMP_LEN) // CMP_STRIDE + 1
    n_sel = s_len // SEL_LEN
    ratio = SEL_LEN // CMP_STRIDE
    c = np.arange(nch)[:, None]
    j = np.arange(n_sel)[None, :]
    mband = ((c >= ratio * j - 1) & (c <= ratio * j + ratio - 1) & (c < n_cmp)).astype(np.float32)
    per_tile = FLASH_TK // SEL_LEN
    n_tiles = s_len // FLASH_TK
    scat = np.zeros((n_tiles, n_sel, LANE), np.float32)
    for b in range(n_sel):
        scat[b // per_tile, b, NSA_DH + b % per_tile] = 1.0
    eslot = np.zeros((FLASH_TK, LANE), np.float32)
    eslot[np.arange(FLASH_TK), NSA_DH + np.arange(FLASH_TK) // SEL_LEN] = 1.0
    gexp = np.zeros((3, LANE, NSA_HEADS * NSA_DH), np.float32)
    for h in range(NSA_HEADS):
        for jj in range(3):
            gexp[jj, h * 3 + jj, h * NSA_DH:(h + 1) * NSA_DH] = 1.0
    half = MLA_ROPE // 2
    inv = ROPE_BASE ** (-jnp.arange(half, dtype=F32) / half)
    inv_slot = jnp.concatenate([jnp.zeros((MLA_NOPE,), F32), inv, inv,
                                jnp.zeros((LANE - MLA_NOPE - MLA_ROPE,), F32)])[None, :]
    as_bf = lambda a: jnp.asarray(a, BF16)
    return as_bf(mband), as_bf(scat), as_bf(eslot), as_bf(gexp), inv_slot


def _layout_projection(w_in, b_in):
    o_merge = int(sum(IN_SIZES[:8]))
    b_rows = b_in[:, None, :]
    return (_layout_in1(w_in[..., :o_merge]).astype(BF16), _layout_in1(b_rows[..., :o_merge]),
            w_in[..., o_merge:].astype(BF16), b_rows[..., o_merge:])


def _hybrid_mixer(x, xb, pos_col, tabs, proj, cmp_pos, cmp_w1, cmp_w2, w_nsa_o, w_pool, pool_scale, w_pool_o,
                  q_norm, w_uq, kv_norm, w_ukv, w_mla_o, conv_w, conv_b, conv_ln_g, conv_ln_b, w_conv_o, w_out,
                  ln_g, ln_b):
    s_len = x.shape[0]
    mband, scat, eslot, gexp, inv_slot = tabs
    row = lambda v: v[None, :]
    w1, b1, wm, bm = proj
    tm_proj = min(TM_PROJ, s_len)
    u1 = _matmul(xb, w1, b1, tm_proj, TN_PROJ, BF16)
    um = _matmul(xb, wm, bm, tm_proj, TN_MERGE, BF16)

    nch = s_len // CMP_STRIDE
    kdim = CMP_STRIDE * LANE
    chunks = u1[:, SLOT_KV * LANE:(SLOT_KV + 4) * LANE].reshape(nch, CMP_STRIDE, 4, LANE)
    chunks = chunks.transpose(2, 0, 1, 3).reshape(4, nch, kdim)
    w1 = _pad_cols(cmp_w1.reshape(2, CMP_LEN, NSA_DH, CMP_HIDDEN).transpose(0, 1, 3, 2), LANE)
    w1 = w1.transpose(0, 1, 3, 2)
    w1cat = jnp.concatenate([w1[:, :CMP_STRIDE].reshape(2, kdim, CMP_HIDDEN),
                             w1[:, CMP_STRIDE:].reshape(2, kdim, CMP_HIDDEN)], -1).astype(BF16)
    posp = _pad_cols(cmp_pos, LANE)
    pos2 = jnp.stack([posp[:, :CMP_STRIDE].reshape(2, kdim), posp[:, CMP_STRIDE:].reshape(2, kdim)], 1)
    pos2 = jnp.pad(pos2, ((0, 0), (0, 6), (0, 0))).astype(BF16)
    kcvc = _nsa_compress(chunks, w1cat, pos2, _pad_cols(cmp_w2, LANE).astype(BF16))
    o_cmp, selneg = _nsa_cmp(u1, kcvc, mband, TQ_CMP)
    nsa = dict(n_groups=1, hpg=NSA_HEADS, hpk=NSA_HPG, out_dtype=F32, dv=NSA_DH)
    o_sel = _flash(u1, SLOT_Q // NSA_HEADS, u1, (SLOT_KV + 4) // NSA_GROUPS, u1, (SLOT_KV + 6) // NSA_GROUPS,
                   window=False, tq=FLASH_TQ, tk=FLASH_TK, sel=selneg, scat=scat, eslot=eslot, **nsa)
    o_win = _flash(u1, SLOT_Q // NSA_HEADS, u1, (SLOT_KV + 8) // NSA_GROUPS, u1, (SLOT_KV + 10) // NSA_GROUPS,
                   window=True, tq=WINDOW, tk=WINDOW, **nsa)

    wk, wv = _layout_mla_kv(w_ukv)
    q_m, k_m, v_m = _mla_prep(u1, pos_col, inv_slot, row(q_norm), row(kv_norm), _layout_mla_q(w_uq).astype(BF16),
                              wk.astype(BF16), wv.astype(BF16), TM_ROW)
    o_mla = _flash(q_m, 0, k_m, 0, v_m, 0, n_groups=1, hpg=MLA_HEADS, hpk=1, window=False,
                   out_dtype=BF16, dv=MLA_DV, tq=FLASH_TQ, tk=FLASH_TK)

    pooled = _pool(u1, w_pool.astype(BF16), row(pool_scale), TM_ROW)
    conv = _conv(u1, conv_w, row(conv_b), row(conv_ln_g), row(conv_ln_b), TM_ROW)
    merged = _merge(o_cmp, o_sel, o_win, u1, gexp, pooled, o_mla, conv, um, w_nsa_o.astype(BF16),
                    w_pool_o.astype(BF16), w_mla_o.astype(BF16), w_conv_o.astype(BF16), TM_MERGE)
    return _matmul_res_ln(merged, w_out.astype(BF16), x, row(ln_g), row(ln_b), TM_ROW)


def _cross_attention(x, xb, mem_b, w_q, w_k, w_v, w_o, ln_g, ln_b):
    row = lambda v: v[None, :]
    kv = _matmul(mem_b, jnp.concatenate([w_k, w_v], 1).astype(BF16), jnp.zeros((1, 2 * X_HEADS * X_DH), F32),
                 mem_b.shape[0], 2 * X_HEADS * X_DH, BF16)
    hw = X_HEADS * X_DH
    return _xattn(xb, x, w_q.astype(BF16), kv[:, :hw], kv[:, hw:], w_o.astype(BF16), row(ln_g), row(ln_b), TM_ROW)


def _hier_moe(x, xb, xp, w_group, b_group, w_router, b_router, w_e_in, w_e_out, layer, ln_g, ln_b):
    t_len = x.shape[0]
    row = lambda v: v[None, :]
    tm_r = TM_ROW
    w_gr = _pad_cols(jnp.concatenate([w_group, w_router], 1), LANE).astype(BF16)
    b_gr = _pad_cols(row(jnp.concatenate([b_group, b_router])), LANE)
    tri = jnp.asarray(np.tril(np.ones((tm_r, tm_r), np.float32)), BF16)
    info, cnt = _router(xb, w_gr, b_gr, tri, tm_r)

    counts = cnt[0, ROUTE_OFF:ROUTE_OFF + N_EXPERTS].astype(I32)
    padded = (counts + MOE_BLOCK - 1) // MOE_BLOCK * MOE_BLOCK
    pend = jnp.cumsum(padded)
    pstart = pend - padded
    n_blocks = -(-(t_len * MOE_TOPK) // MOE_BLOCK) + N_EXPERTS
    n_used = pend[-1] // MOE_BLOCK
    blk_ids = jnp.minimum(jnp.arange(n_blocks), n_used - 1)
    owner = jnp.sum((pend[None, :] <= (blk_ids * MOE_BLOCK)[:, None]).astype(I32), axis=1)
    block_expert = jnp.minimum(owner, N_EXPERTS - 1).astype(I32)
    e_ids = info[:, 0:MOE_TOPK].T.astype(I32)
    pos = (pstart[e_ids] + info[:, 4:4 + MOE_TOPK].T.astype(I32)).reshape(-1)

    xe = _dispatch(pos, xp, n_blocks * MOE_BLOCK, TM_ROW)
    blk = jnp.arange(n_blocks, dtype=I32)
    change = (blk < n_used) & ((blk == 0) | (block_expert != jnp.roll(block_expert, 1)))
    first_change_from = jnp.flip(lax.cummin(jnp.flip(jnp.where(change, blk, n_blocks))))
    nxt = jnp.concatenate([first_change_from[1:], jnp.full((1,), n_blocks, I32)])
    next_expert = jnp.where(nxt < n_blocks, block_expert[jnp.minimum(nxt, n_blocks - 1)], -1).astype(I32)
    yb = _experts(block_expert, next_expert, n_used.reshape(1).astype(I32), xe, w_e_in, w_e_out, layer)
    return _combine(pos, yb, x, info, row(ln_g), row(ln_b), TM_COMBINE)


def kernel(x, mem, positions, w_in, b_in, nsa_cmp_pos, nsa_cmp_w1, nsa_cmp_w2, w_nsa_o, w_pool, pool_scale, w_pool_o, mla_q_norm, w_mla_uq, mla_kv_norm, w_mla_ukv, w_mla_o, conv_w, conv_b, conv_ln_g, conv_ln_b, w_conv_o, w_out, ln_mix_g, ln_mix_b, w_xq, w_xk, w_xv, w_xo, ln_x_g, ln_x_b, w_group, b_group, w_router, b_router, w_expert_in, w_expert_out, ln_ffn_g, ln_ffn_b):
    batch, s_len, d = x.shape
    assert batch == 1 and d == D_MODEL and s_len % (2 * FLASH_TK) == 0 and s_len % FLASH_TQ == 0
    x = x[0]
    xb = x.astype(BF16)
    mem_b = mem[0].astype(BF16)
    pos_col = positions[0].astype(F32)[:, None]
    tabs = _static_tables(s_len)
    proj = _layout_projection(w_in, b_in)
    for l in range(w_in.shape[0]):
        x, xb = _hybrid_mixer(x, xb, pos_col, tabs, [a[l] for a in proj], nsa_cmp_pos[l], nsa_cmp_w1[l], nsa_cmp_w2[l],
                              w_nsa_o[l], w_pool[l], pool_scale[l], w_pool_o[l], mla_q_norm[l], w_mla_uq[l],
                              mla_kv_norm[l], w_mla_ukv[l], w_mla_o[l], conv_w[l], conv_b[l], conv_ln_g[l],
                              conv_ln_b[l], w_conv_o[l], w_out[l], ln_mix_g[l], ln_mix_b[l])
        x, xb, xp = _cross_attention(x, xb, mem_b, w_xq[l], w_xk[l], w_xv[l], w_xo[l], ln_x_g[l], ln_x_b[l])
        x, xb = _hier_moe(x, xb, xp, w_group[l], b_group[l], w_router[l], b_router[l], w_expert_in,
                          w_expert_out, l, ln_ffn_g[l], ln_ffn_b[l])
    return x[None]
```

```python
import functools

import numpy as np
import jax
import jax.numpy as jnp
from jax import lax
from jax.experimental import pallas as pl
from jax.experimental.pallas import tpu as pltpu

F32 = jnp.float32
BF16 = jnp.bfloat16
I32 = jnp.int32

D_MODEL = 2048
NSA_HEADS = 8
NSA_GROUPS = 2
NSA_HPG = NSA_HEADS // NSA_GROUPS
NSA_DH = 64
CMP_LEN = 32
CMP_STRIDE = 16
CMP_HIDDEN = 128
SEL_LEN = 64
SEL_TOPN = 16
WINDOW = 512
FORCE_SCORE = 1.0e4
POOL_GROUPS = 4
POOL_WINDOWS = (2, 4, 8, 16)
POOL_WIDTH = 512
POOL_GW = POOL_WIDTH // POOL_GROUPS
MLA_HEADS = 8
MLA_Q_RANK = 512
MLA_KV_RANK = 256
MLA_NOPE = 64
MLA_ROPE = 32
MLA_DV = 64
ROPE_BASE = 10000.0
CONV_CH = 512
CONV_K = 31
N_BRANCH = 4
X_HEADS = 4
X_DH = 128
MOE_GROUPS = 4
MOE_EPG = 8
N_EXPERTS = MOE_GROUPS * MOE_EPG
MOE_TOPK = 2
EXPERT_FF = 512
MOE_BLOCK = 256
LN_EPS = 1e-5
RMS_EPS = 1e-6
DEPTH = 2
DN_ALPHA = (2 * DEPTH) ** 0.25
IN_SIZES = (NSA_HEADS * NSA_DH, 6 * NSA_GROUPS * NSA_DH, 3 * NSA_HEADS, POOL_WIDTH,
            MLA_Q_RANK, MLA_KV_RANK, MLA_ROPE, 2 * CONV_CH, N_BRANCH * D_MODEL)

LANE = 128
VMEM_LIMIT = 56 * 1024 * 1024
NEG = -1.0e30
FLASH_TQ = 1024
FLASH_TK = 1024
LOG2E = 1.4426950408889634

TM_PROJ = 2048
TN_PROJ = 7 * LANE
TN_MERGE = 1024
TQ_CMP = 256
TM_ROW = 512
TM_MERGE = 256
TM_COMBINE = 512

SLOT_Q = 0
SLOT_KV = 8
SLOT_POOL = 20
SLOT_CQ = 24
SLOT_CKV = 28
SLOT_KR = 30
SLOT_CONV = 32
SLOT_GATE = 40
N_SLOTS1 = 42
N1 = N_SLOTS1 * LANE


def _cparams(sem, vmem=VMEM_LIMIT):
    return pltpu.CompilerParams(dimension_semantics=sem, vmem_limit_bytes=vmem)


def _sigmoid(x):
    return 0.5 * jnp.tanh(0.5 * x) + 0.5


def _layer_norm(z, g, b):
    mu = jnp.mean(z, -1, keepdims=True)
    d = z - mu
    var = jnp.mean(d * d, -1, keepdims=True)
    return d * lax.rsqrt(var + LN_EPS) * g + b


U32 = jnp.uint32


def _pack_rows(y):
    n = y.shape[1] // 2
    bits = lax.bitcast_convert_type(y.astype(BF16).astype(F32), U32)
    return (bits[:, n:] & jnp.uint32(0xFFFF0000)) | (bits[:, :n] >> 16)


def _unpack_rows(w):
    lo = lax.bitcast_convert_type(w << 16, F32)
    hi = lax.bitcast_convert_type(w & jnp.uint32(0xFFFF0000), F32)
    return jnp.concatenate([lo, hi], axis=1)


def _dot(a, b):
    return jnp.dot(a, b, preferred_element_type=F32)


def _dot_nt(a, b):
    return lax.dot_general(a, b, (((1,), (1,)), ((), ())), preferred_element_type=F32)


def _dot3(a, b):
    hi = a.astype(BF16)
    r1 = a - hi.astype(F32)
    mid = r1.astype(BF16)
    lo = (r1 - mid.astype(F32)).astype(BF16)
    return _dot(hi, b) + _dot(mid, b) + _dot(lo, b)


def _mm_kernel(a_ref, b_ref, bias_ref, o_ref):
    o_ref[...] = (_dot(a_ref[...], b_ref[...]) + bias_ref[...]).astype(o_ref.dtype)


def _matmul(a, b, bias, tm, tn, out_dtype):
    m, k = a.shape
    n = b.shape[1]
    return pl.pallas_call(
        _mm_kernel,
        grid=(m // tm, n // tn),
        in_specs=[pl.BlockSpec((tm, k), lambda i, j: (i, 0)),
                  pl.BlockSpec((k, tn), lambda i, j: (0, j)),
                  pl.BlockSpec((1, tn), lambda i, j: (0, j))],
        out_specs=pl.BlockSpec((tm, tn), lambda i, j: (i, j)),
        out_shape=jax.ShapeDtypeStruct((m, n), out_dtype),
        compiler_params=_cparams(("parallel", "arbitrary")),
        name="matmul",
    )(a, b, bias)


def _mm_ln_kernel(a_ref, w_ref, x_ref, g_ref, b_ref, xo_ref, xb_ref):
    h = _dot(a_ref[...], w_ref[...])
    y = _layer_norm(DN_ALPHA * x_ref[...] + h, g_ref[...], b_ref[...])
    xo_ref[...] = y
    xb_ref[...] = y.astype(BF16)


def _matmul_res_ln(a, w, x, g, b, tm):
    m, k = a.shape
    d = w.shape[1]
    return pl.pallas_call(
        _mm_ln_kernel,
        grid=(m // tm,),
        in_specs=[pl.BlockSpec((tm, k), lambda i: (i, 0)),
                  pl.BlockSpec((k, d), lambda i: (0, 0)),
                  pl.BlockSpec((tm, d), lambda i: (i, 0)),
                  pl.BlockSpec((1, d), lambda i: (0, 0)),
                  pl.BlockSpec((1, d), lambda i: (0, 0))],
        out_specs=[pl.BlockSpec((tm, d), lambda i: (i, 0)),
                   pl.BlockSpec((tm, d), lambda i: (i, 0))],
        out_shape=[jax.ShapeDtypeStruct((m, d), F32), jax.ShapeDtypeStruct((m, d), BF16)],
        compiler_params=_cparams(("parallel",)),
        name="matmul_res_ln",
    )(a, w, x, g, b)


def _compress_kernel(a_ref, w1_ref, pos_ref, w2_ref, o_ref):
    nch = a_ref.shape[1]
    hh = _dot(a_ref[0], w1_ref[0])
    pp = _dot(pos_ref[0], w1_ref[0])
    pos_term = pp[0:1, :CMP_HIDDEN] + pp[1:2, CMP_HIDDEN:]
    h2_next = pltpu.roll(hh[:, CMP_HIDDEN:], nch - 1, 0)
    z = hh[:, :CMP_HIDDEN] + h2_next + pos_term
    hid = 0.5 * z * (1.0 + jnp.tanh(0.7978845608028654 * (z + 0.044715 * z * z * z)))
    o_ref[0] = _dot(hid.astype(BF16), w2_ref[0]).astype(o_ref.dtype)


def _nsa_compress(chunks, w1cat, pos2, w2pad):
    n4, nch, kdim = chunks.shape
    return pl.pallas_call(
        _compress_kernel,
        grid=(n4,),
        in_specs=[pl.BlockSpec((1, nch, kdim), lambda c: (c, 0, 0)),
                  pl.BlockSpec((1, kdim, 2 * CMP_HIDDEN), lambda c: (c // NSA_GROUPS, 0, 0)),
                  pl.BlockSpec((1, 8, kdim), lambda c: (c // NSA_GROUPS, 0, 0)),
                  pl.BlockSpec((1, CMP_HIDDEN, LANE), lambda c: (c // NSA_GROUPS, 0, 0))],
        out_specs=pl.BlockSpec((1, nch, LANE), lambda c: (c, 0, 0)),
        out_shape=jax.ShapeDtypeStruct((n4, nch, LANE), BF16),
        compiler_params=_cparams(("arbitrary",)),
        name="nsa_compress",
    )(chunks, w1cat, pos2, w2pad)


CMP_WIDTH_STEPS = 4


def _nsa_cmp_kernel(q_ref, kc_ref, vc_ref, mband_ref, ocmp_ref, selneg_ref, score_sc, *, tq, n_cmp, n_sel, top_n):
    t0 = pl.program_id(0) * tq
    nch = kc_ref.shape[1]
    row = lax.broadcasted_iota(I32, (NSA_HPG * tq, 1), 0)
    t_row = t0 + (row & (tq - 1))

    def softmax_part(width):
        col = lax.broadcasted_iota(I32, (1, width), 1)
        vis = (col * CMP_STRIDE + (CMP_LEN - 1) <= t_row) & (col < n_cmp)
        for g in range(NSA_GROUPS):
            qs = jnp.concatenate(
                [q_ref[:, (g * NSA_HPG + h) * LANE:(g * NSA_HPG + h + 1) * LANE] for h in range(NSA_HPG)], axis=0)
            s = _dot_nt(qs, kc_ref[g, 0:width, :])
            s = jnp.where(vis, s, NEG)
            m = jnp.max(s, -1, keepdims=True)
            p = jnp.where(vis, jnp.exp2(s - m), 0.0)
            l = jnp.sum(p, -1, keepdims=True)
            p = p * (1.0 / jnp.maximum(l, 1e-30))
            o = _dot(p.astype(BF16), vc_ref[g, 0:width, :])
            for h in range(NSA_HPG):
                hh = g * NSA_HPG + h
                ocmp_ref[:, hh * NSA_DH:(hh + 1) * NSA_DH] = o[h * tq:(h + 1) * tq, :NSA_DH]
            imp = p[0:tq]
            for h in range(1, NSA_HPG):
                imp = imp + p[h * tq:(h + 1) * tq]
            score_sc[g * tq:(g + 1) * tq, :] = _dot3(imp, mband_ref[0:width, :])

    last_vis = (t0 + tq - CMP_LEN) // CMP_STRIDE
    step_w = nch // CMP_WIDTH_STEPS
    variant = jnp.minimum(last_vis // step_w, CMP_WIDTH_STEPS - 1)
    for v in range(CMP_WIDTH_STEPS):
        pl.when(variant == v)(functools.partial(softmax_part, (v + 1) * step_w))

    blk = lax.broadcasted_iota(I32, (NSA_GROUPS * tq, n_sel), 1).astype(F32)
    row2 = lax.broadcasted_iota(I32, (NSA_GROUPS * tq, 1), 0)
    cur = ((t0 + (row2 & (tq - 1))) // SEL_LEN).astype(F32)
    valid = blk <= cur
    forced = (blk == 0.0) | (blk == cur) | (blk == cur - 1.0)
    assert FORCE_SCORE > NSA_HPG * (SEL_LEN // CMP_STRIDE + CMP_LEN // CMP_STRIDE - 1) and top_n > 3
    score = jnp.where(forced, -2.0, score_sc[...])
    score = jnp.where(valid, score, -1.0)

    def pick_one(_, sc):
        mx = jnp.max(sc, -1, keepdims=True)
        first = jnp.min(jnp.where(sc == mx, blk, float(n_sel)), -1, keepdims=True)
        return jnp.where(blk == first, -2.0, sc)

    sc = lax.fori_loop(0, top_n - 3, pick_one, score)
    selneg = jnp.where(valid & (sc == -2.0), 0.0, -1.0).astype(selneg_ref.dtype)
    for g in range(NSA_GROUPS):
        selneg_ref[:, g * n_sel:(g + 1) * n_sel] = selneg[g * tq:(g + 1) * tq]


def _nsa_cmp(u1, kcvc, mband, tq):
    s_len = u1.shape[0]
    nch = kcvc.shape[1]
    n_sel = s_len // SEL_LEN
    n_cmp = (s_len - CMP_LEN) // CMP_STRIDE + 1
    kern = functools.partial(_nsa_cmp_kernel, tq=tq, n_cmp=n_cmp, n_sel=n_sel, top_n=min(SEL_TOPN, n_sel))
    return pl.pallas_call(
        kern,
        grid=(s_len // tq,),
        in_specs=[pl.BlockSpec((tq, NSA_HEADS * LANE), lambda i: (i, SLOT_Q // NSA_HEADS)),
                  pl.BlockSpec((NSA_GROUPS, nch, LANE), lambda i: (0, 0, 0)),
                  pl.BlockSpec((NSA_GROUPS, nch, LANE), lambda i: (1, 0, 0)),
                  pl.BlockSpec((nch, n_sel), lambda i: (0, 0))],
        out_specs=[pl.BlockSpec((tq, NSA_HEADS * NSA_DH), lambda i: (i, 0)),
                   pl.BlockSpec((tq, NSA_GROUPS * n_sel), lambda i: (i, 0))],
        out_shape=[jax.ShapeDtypeStruct((s_len, NSA_HEADS * NSA_DH), F32),
                   jax.ShapeDtypeStruct((s_len, NSA_GROUPS * n_sel), BF16)],
        scratch_shapes=[pltpu.VMEM((NSA_GROUPS * tq, n_sel), F32)],
        compiler_params=_cparams(("parallel",)),
        name="nsa_cmp_topk",
    )(u1, kcvc, kcvc, mband)


def _flash_kernel(qi_ref, kj_ref, first_ref, last_ref, mode_ref, *refs, hpg, hpk, select, modes, dv,
                  split_masked):
    if select:
        q_ref, k_ref, v_ref, sel_ref, scat_ref, eslot_ref, o_ref, m_sc, acc_sc = refs
    else:
        q_ref, k_ref, v_ref, o_ref, m_sc, acc_sc = refs
    p_idx = pl.program_id(1)
    tq = q_ref.shape[0]
    tk = k_ref.shape[0]

    @pl.when(first_ref[p_idx] == 1)
    def _():
        m_sc[...] = jnp.full(m_sc.shape, NEG, F32)
        acc_sc[...] = jnp.zeros(acc_sc.shape, F32)

    def step(mask_mode):
        k_all = k_ref[...]
        lane = lax.broadcasted_iota(I32, (1, v_ref.shape[1]), 1)
        v_all = v_ref[...] + ((lane & (LANE - 1)) == dv).astype(BF16)
        if select:
            nk = hpg // hpk
            n_sel = sel_ref.shape[1] // nk
            k_all = k_all + jnp.concatenate([eslot_ref[...]] * nk, axis=1)
            scat = scat_ref[kj_ref[p_idx]]
            biases = [(_dot(sel_ref[:, g * n_sel:(g + 1) * n_sel], scat) * (-NEG)).astype(BF16) for g in range(nk)]
        hq, hk_ = tq // 2, tk // 2
        if mask_mode == 0 or not split_masked:
            blocks = ((0, tq, 0, tk),)
        else:
            blocks = {1: ((0, hq, 0, hk_), (hq, tq, 0, tk)), 2: ((0, hq, 0, tk), (hq, tq, hk_, tk))}[mask_mode]
        for r0, r1, c0, c1 in blocks:
            if mask_mode:
                r = lax.broadcasted_iota(I32, (r1 - r0, c1 - c0), 0) + r0
                c = lax.broadcasted_iota(I32, (r1 - r0, c1 - c0), 1) + c0
                keep = (c <= r) if mask_mode == 1 else (c > r)
            for h in range(hpg):
                hk = h // hpk
                q = q_ref[r0:r1, h * LANE:(h + 1) * LANE]
                if select:
                    q = q + biases[hk][r0:r1]
                s = _dot_nt(q, k_all[c0:c1, hk * LANE:(hk + 1) * LANE])
                if mask_mode:
                    s = jnp.where(keep, s, NEG)
                m_prev = m_sc[h, r0:r1]
                m_new = jnp.maximum(m_prev, jnp.max(s, -1, keepdims=True))
                p = jnp.exp2(s - jnp.concatenate([m_new] * ((c1 - c0) // LANE), axis=1))
                acc_sc[h, r0:r1] = (jnp.exp2(m_prev - m_new) * acc_sc[h, r0:r1]
                                    + _dot(p.astype(BF16), v_all[c0:c1, hk * LANE:(hk + 1) * LANE]))
                m_sc[h, r0:r1] = m_new

    for mm in modes:
        pl.when(mode_ref[p_idx] == mm)(functools.partial(step, mm))

    @pl.when(last_ref[p_idx] == 1)
    def _():
        for h in range(hpg):
            acc = acc_sc[h]
            o = acc[:, :dv] * (1.0 / acc[:, dv:dv + 1])
            o_ref[:, h * dv:(h + 1) * dv] = o.astype(o_ref.dtype)


def _pair_tables(n_tiles, window):
    rows = []
    for i in range(n_tiles):
        js = ([i - 1] if i > 0 else []) + [i] if window else list(range(i + 1))
        for j in js:
            mode = 1 if j == i else (2 if window else 0)
            rows.append((i, j, int(j == js[0]), int(j == i), mode))
    tab = np.asarray(rows, np.int32)
    return [jnp.asarray(tab[:, c]) for c in range(tab.shape[1])]


def _flash(q_arr, q_blk0, k_arr, k_blk0, v_arr, v_blk0, *, n_groups, hpg, hpk, window,
           out_dtype, dv, tq, tk, sel=None, scat=None, eslot=None):
    s_len = q_arr.shape[0]
    assert tq == tk and (not window or tk == WINDOW)
    nk = hpg // hpk
    select = sel is not None
    tables = _pair_tables(s_len // tq, window)
    n_pairs = int(tables[0].shape[0])
    modes = (1, 2) if window else (0, 1)

    def qmap(g, p, qi, kj, *_):
        return (qi[p], q_blk0 + g)

    def kmap(g, p, qi, kj, *_):
        return (kj[p], k_blk0 + g)

    def vmap_(g, p, qi, kj, *_):
        return (kj[p], v_blk0 + g)

    def omap(g, p, qi, kj, *_):
        return (qi[p], g)

    in_specs = [pl.BlockSpec((tq, hpg * LANE), qmap),
                pl.BlockSpec((tk, nk * LANE), kmap),
                pl.BlockSpec((tk, nk * LANE), vmap_)]
    args = [q_arr, k_arr, v_arr]
    if select:
        in_specs += [pl.BlockSpec((tq, sel.shape[1] // n_groups), omap),
                     pl.BlockSpec(scat.shape, lambda g, p, *_: (0, 0, 0)),
                     pl.BlockSpec(eslot.shape, lambda g, p, *_: (0, 0))]
        args += [sel, scat, eslot]
    kern = functools.partial(_flash_kernel, hpg=hpg, hpk=hpk, select=select, modes=modes, dv=dv,
                             split_masked=tq >= 2 * WINDOW)
    grid_spec = pltpu.PrefetchScalarGridSpec(
        num_scalar_prefetch=len(tables),
        grid=(n_groups, n_pairs),
        in_specs=in_specs,
        out_specs=pl.BlockSpec((tq, hpg * dv), omap),
        scratch_shapes=[pltpu.VMEM((hpg, tq, LANE), F32), pltpu.VMEM((hpg, tq, LANE), F32)],
    )
    return pl.pallas_call(
        kern,
        grid_spec=grid_spec,
        out_shape=jax.ShapeDtypeStruct((s_len, n_groups * hpg * dv), out_dtype),
        compiler_params=_cparams(("parallel", "arbitrary")),
        name="flash_sel" if select else ("flash_win" if window else "flash_causal"),
    )(*tables, *args)


def _mla_prep_kernel(cq_ref, ckv_ref, kr_ref, pos_ref, inv_ref, qn_ref, kvn_ref, wuq_ref, wk_ref, wv_ref,
                     q_ref, k_ref, v_ref):
    def rms(x, g):
        return x * lax.rsqrt(jnp.mean(x * x, -1, keepdims=True) + RMS_EPS) * g

    ang = pos_ref[...] * inv_ref[...]
    cos, sin = jnp.cos(ang), jnp.sin(ang)
    hw = MLA_HEADS * LANE
    qh = _dot(rms(cq_ref[...].astype(F32), qn_ref[...]).astype(BF16), wuq_ref[...])
    scale = (MLA_NOPE + MLA_ROPE) ** -0.5 * LOG2E
    cos_t = jnp.concatenate([cos] * MLA_HEADS, axis=1)
    sin_t = jnp.concatenate([sin] * MLA_HEADS, axis=1)
    q_ref[...] = ((qh[:, :hw] * cos_t + qh[:, hw:] * sin_t) * scale).astype(BF16)
    ckv = rms(ckv_ref[...].astype(F32), kvn_ref[...]).astype(BF16)
    kr = kr_ref[...].astype(F32)
    k_rope = kr[:, :LANE] * cos + kr[:, LANE:] * sin
    k_ref[...] = (_dot(ckv, wk_ref[...]) + jnp.concatenate([k_rope] * MLA_HEADS, axis=1)).astype(BF16)
    v_ref[...] = _dot(ckv, wv_ref[...]).astype(BF16)


def _mla_prep(u1, pos_col, inv_slot, qn, kvn, wuq, wk, wv, tm):
    s_len = u1.shape[0]
    hw = MLA_HEADS * LANE
    full = lambda a: pl.BlockSpec(a.shape, lambda i: (0,) * a.ndim)
    out = jax.ShapeDtypeStruct((s_len, hw), BF16)
    return pl.pallas_call(
        _mla_prep_kernel,
        grid=(s_len // tm,),
        in_specs=[pl.BlockSpec((tm, MLA_Q_RANK), lambda i: (i, SLOT_CQ * LANE // MLA_Q_RANK)),
                  pl.BlockSpec((tm, MLA_KV_RANK), lambda i: (i, SLOT_CKV * LANE // MLA_KV_RANK)),
                  pl.BlockSpec((tm, 2 * LANE), lambda i: (i, SLOT_KR // 2)),
                  pl.BlockSpec((tm, 1), lambda i: (i, 0)),
                  full(inv_slot), full(qn), full(kvn), full(wuq), full(wk), full(wv)],
        out_specs=[pl.BlockSpec((tm, hw), lambda i: (i, 0))] * 3,
        out_shape=[out, out, out],
        compiler_params=_cparams(("parallel",)),
        name="mla_prep",
    )(u1, u1, u1, pos_col, inv_slot, qn, kvn, wuq, wk, wv)


POOL_HALO = 16


def _pool_kernel(cur_ref, halo_ref, wp_ref, scale_ref, o_ref):
    i = pl.program_id(0)
    tm = cur_ref.shape[0]
    halo = jnp.where(i > 0, halo_ref[...].astype(F32), 0.0)
    x = jnp.concatenate([halo, cur_ref[...].astype(F32)], axis=0)
    sums = {1: x}
    w = 1
    while w < max(POOL_WINDOWS):
        a = sums[w]
        sums[2 * w] = a[w:] + a[:-w]
        w *= 2
    t = (i * tm + lax.broadcasted_iota(I32, (tm, 1), 0) + 1).astype(F32)
    outs = []
    for gi, w in enumerate(POOL_WINDOWS):
        lo = gi * POOL_GW
        start = POOL_HALO - (w - 1)
        win = sums[w][start:start + tm, lo:lo + POOL_GW]
        mean = win / jnp.minimum(t, float(w))
        pooled = mean - x[POOL_HALO:, lo:lo + POOL_GW]
        outs.append(_dot(pooled.astype(BF16), wp_ref[gi]))
    o_ref[...] = (jnp.concatenate(outs, axis=1) * scale_ref[...]).astype(o_ref.dtype)


def _pool(u1, w_pool, pool_scale, tm):
    s_len = u1.shape[0]
    blk = SLOT_POOL * LANE // POOL_WIDTH
    return pl.pallas_call(
        _pool_kernel,
        grid=(s_len // tm,),
        in_specs=[pl.BlockSpec((tm, POOL_WIDTH), lambda i: (i, blk)),
                  pl.BlockSpec((POOL_HALO, POOL_WIDTH),
                               lambda i: (jnp.maximum(i * (tm // POOL_HALO) - 1, 0), blk)),
                  pl.BlockSpec(w_pool.shape, lambda i: (0, 0, 0)),
                  pl.BlockSpec((1, POOL_WIDTH), lambda i: (0, 0))],
        out_specs=pl.BlockSpec((tm, POOL_WIDTH), lambda i: (i, 0)),
        out_shape=jax.ShapeDtypeStruct((s_len, POOL_WIDTH), BF16),
        compiler_params=_cparams(("parallel",)),
        name="pool",
    )(u1, u1, w_pool, pool_scale)


CONV_HALO = 32


def _conv_kernel(cur_ref, halo_ref, w_ref, b_ref, g_ref, beta_ref, o_ref, hbuf):
    i = pl.program_id(0)
    tm = cur_ref.shape[0]

    def glu(u):
        u = u.astype(F32)
        return u[:, :CONV_CH] * _sigmoid(u[:, CONV_CH:])

    hbuf[0:CONV_HALO, :] = jnp.where(i > 0, glu(halo_ref[...]), 0.0)
    hbuf[CONV_HALO:, :] = glu(cur_ref[...])
    acc = jnp.zeros((tm, CONV_CH), F32) + b_ref[...]
    for k in range(CONV_K):
        off = CONV_HALO - (CONV_K - 1) + k
        acc = acc + hbuf[off:off + tm, :] * w_ref[k:k + 1, :]
    y = _layer_norm(acc, g_ref[...], beta_ref[...])
    o_ref[...] = (y * _sigmoid(y)).astype(o_ref.dtype)


def _conv(u1, conv_w, conv_b, ln_g, ln_b, tm):
    s_len = u1.shape[0]
    blk = SLOT_CONV * LANE // (2 * CONV_CH)
    row = lambda a: pl.BlockSpec(a.shape, lambda i: (0, 0))
    return pl.pallas_call(
        _conv_kernel,
        grid=(s_len // tm,),
        in_specs=[pl.BlockSpec((tm, 2 * CONV_CH), lambda i: (i, blk)),
                  pl.BlockSpec((CONV_HALO, 2 * CONV_CH),
                               lambda i: (jnp.maximum(i * (tm // CONV_HALO) - 1, 0), blk)),
                  row(conv_w), row(conv_b), row(ln_g), row(ln_b)],
        out_specs=pl.BlockSpec((tm, CONV_CH), lambda i: (i, 0)),
        out_shape=jax.ShapeDtypeStruct((s_len, CONV_CH), BF16),
        scratch_shapes=[pltpu.VMEM((tm + CONV_HALO, CONV_CH), F32)],
        compiler_params=_cparams(("parallel",)),
        name="conv_module",
    )(u1, u1, conv_w, conv_b, ln_g, ln_b)


def _merge_kernel(ocmp_ref, osel_ref, owin_ref, gate_ref, gexp_ref, pool_ref, mla_ref, conv_ref, um_ref,
                  wn_ref, wp_ref, wm_ref, wc_ref, o_ref):
    sg = _sigmoid(gate_ref[...].astype(F32))
    nsa = (_dot3(sg, gexp_ref[0]) * ocmp_ref[...] + _dot3(sg, gexp_ref[1]) * osel_ref[...]
           + _dot3(sg, gexp_ref[2]) * owin_ref[...])
    branches = (_dot(nsa.astype(BF16), wn_ref[...]), _dot(pool_ref[...], wp_ref[...]),
                _dot(mla_ref[...], wm_ref[...]), _dot(conv_ref[...], wc_ref[...]))
    merged = None
    for j, br in enumerate(branches):
        term = _sigmoid(um_ref[:, j * D_MODEL:(j + 1) * D_MODEL].astype(F32)) * br
        merged = term if merged is None else merged + term
    o_ref[...] = merged.astype(o_ref.dtype)


def _merge(o_cmp, o_sel, o_win, u1, gexp, pooled, o_mla, conv, um, wn, wp, wm, wc, tm):
    s_len = u1.shape[0]
    tile = lambda w: pl.BlockSpec((tm, w), lambda i: (i, 0))
    full = lambda a: pl.BlockSpec(a.shape, lambda i: (0,) * a.ndim)
    return pl.pallas_call(
        _merge_kernel,
        grid=(s_len // tm,),
        in_specs=[tile(512), tile(512), tile(512),
                  pl.BlockSpec((tm, LANE), lambda i: (i, SLOT_GATE)),
                  full(gexp), tile(512), tile(512), tile(512), tile(N_BRANCH * D_MODEL),
                  full(wn), full(wp), full(wm), full(wc)],
        out_specs=tile(D_MODEL),
        out_shape=jax.ShapeDtypeStruct((s_len, D_MODEL), BF16),
        compiler_params=_cparams(("parallel",)),
        name="branch_merge",
    )(o_cmp, o_sel, o_win, u1, gexp, pooled, o_mla, conv, um, wn, wp, wm, wc)


def _xattn_kernel(xb_ref, x_ref, wq_ref, k_ref, v_ref, wo_ref, g_ref, b_ref, xo_ref, xob_ref, xp_ref):
    q = _dot(xb_ref[...], wq_ref[...]).astype(BF16)
    k = k_ref[...]
    v = v_ref[...]
    outs = []
    for h in range(X_HEADS):
        sl = slice(h * X_DH, (h + 1) * X_DH)
        s = _dot_nt(q[:, sl], k[:, sl]) * (X_DH ** -0.5)
        m = jnp.max(s, -1, keepdims=True)
        p = jnp.exp(s - m)
        p = p * (1.0 / jnp.sum(p, -1, keepdims=True))
        outs.append(_dot(p.astype(BF16), v[:, sl]))
    o = jnp.concatenate(outs, axis=1).astype(BF16)
    y = _layer_norm(DN_ALPHA * x_ref[...] + _dot(o, wo_ref[...]), g_ref[...], b_ref[...])
    xo_ref[...] = y
    xob_ref[...] = y.astype(BF16)
    xp_ref[...] = _pack_rows(y)


def _xattn(xb, x, wq, k, v, wo, g, b, tm):
    s_len = x.shape[0]
    tile = lambda: pl.BlockSpec((tm, D_MODEL), lambda i: (i, 0))
    full = lambda a: pl.BlockSpec(a.shape, lambda i: (0,) * a.ndim)
    return pl.pallas_call(
        _xattn_kernel,
        grid=(s_len // tm,),
        in_specs=[tile(), tile(), full(wq), full(k), full(v), full(wo), full(g), full(b)],
        out_specs=[tile(), tile(), pl.BlockSpec((tm, D_MODEL // 2), lambda i: (i, 0))],
        out_shape=[jax.ShapeDtypeStruct((s_len, D_MODEL), F32), jax.ShapeDtypeStruct((s_len, D_MODEL), BF16),
                   jax.ShapeDtypeStruct((s_len, D_MODEL // 2), U32)],
        compiler_params=_cparams(("parallel",)),
        name="cross_attention_ln",
    )(xb, x, wq, k, v, wo, g, b)


ROUTE_OFF = MOE_GROUPS


def _router_kernel(xb_ref, w_ref, b_ref, tri_ref, info_ref, cnt_ref, carry):
    @pl.when(pl.program_id(0) == 0)
    def _():
        carry[...] = jnp.zeros(carry.shape, F32)

    logits = _dot(xb_ref[...], w_ref[...]) + b_ref[...]
    lane = lax.broadcasted_iota(I32, logits.shape, 1).astype(F32)
    is_g = lane < float(MOE_GROUPS)
    neg_inf = -jnp.inf
    gl = jnp.where(is_g, logits, neg_inf)
    gmax = jnp.max(gl, -1, keepdims=True)
    g_sel = jnp.min(jnp.where(gl == gmax, lane, float(LANE)), -1, keepdims=True)
    g_w = 1.0 / jnp.sum(jnp.where(is_g, jnp.exp(gl - gmax), 0.0), -1, keepdims=True)
    lo = ROUTE_OFF + MOE_EPG * g_sel
    in_g = (lane >= lo) & (lane < lo + MOE_EPG)
    el = jnp.where(in_g, logits, neg_inf)
    emax = jnp.max(el, -1, keepdims=True)
    e = jnp.where(in_g, jnp.exp(el - emax), 0.0)
    p = e / jnp.sum(e, -1, keepdims=True)
    pm = jnp.where(in_g, p, -1.0)
    p1 = jnp.max(pm, -1, keepdims=True)
    i1 = jnp.min(jnp.where(pm == p1, lane, float(LANE)), -1, keepdims=True)
    pm2 = jnp.where(lane == i1, -1.0, pm)
    p2 = jnp.max(pm2, -1, keepdims=True)
    i2 = jnp.min(jnp.where(pm2 == p2, lane, float(LANE)), -1, keepdims=True)
    denom = p1 + p2
    gate1 = g_w * p1 / denom
    gate2 = g_w * p2 / denom

    oh1 = (lane == i1).astype(BF16)
    oh2 = (lane == i2).astype(BF16)
    incl1 = _dot(tri_ref[...], oh1)
    incl2 = _dot(tri_ref[...], oh2)
    tot1 = jnp.sum(oh1.astype(F32), 0, keepdims=True)
    tot2 = jnp.sum(oh2.astype(F32), 0, keepdims=True)
    base = carry[...]
    rank1 = jnp.sum(jnp.where(lane == i1, base + incl1, 0.0), -1, keepdims=True) - 1.0
    rank2 = jnp.sum(jnp.where(lane == i2, base + tot1 + incl2, 0.0), -1, keepdims=True) - 1.0
    carry[...] = base + tot1 + tot2
    cnt_ref[...] = carry[...]

    cols = (i1 - ROUTE_OFF, i2 - ROUTE_OFF, gate1, gate2, rank1, rank2)
    info = jnp.zeros(logits.shape, F32)
    for c, val in enumerate(cols):
        info = jnp.where(lane == float(c), val, info)
    info_ref[...] = info


def _router(xb, w_gr, b_gr, tri, tm):
    t_len = xb.shape[0]
    return pl.pallas_call(
        _router_kernel,
        grid=(t_len // tm,),
        in_specs=[pl.BlockSpec((tm, D_MODEL), lambda i: (i, 0)),
                  pl.BlockSpec(w_gr.shape, lambda i: (0, 0)),
                  pl.BlockSpec((1, LANE), lambda i: (0, 0)),
                  pl.BlockSpec((tm, tm), lambda i: (0, 0))],
        out_specs=[pl.BlockSpec((tm, LANE), lambda i: (i, 0)),
                   pl.BlockSpec((1, LANE), lambda i: (0, 0))],
        out_shape=[jax.ShapeDtypeStruct((t_len, LANE), F32), jax.ShapeDtypeStruct((1, LANE), F32)],
        scratch_shapes=[pltpu.VMEM((1, LANE), F32)],
        compiler_params=_cparams(("arbitrary",)),
        name="moe_router",
    )(xb, w_gr, b_gr, tri)


def _row_copy(src, src_row, dst, dst_row, sem):
    return pltpu.make_async_copy(src.at[pl.ds(src_row, 1)], dst.at[pl.ds(dst_row, 1)], sem)


def _dispatch_kernel(pos_ref, x_ref, xe_in_hbm, xe_hbm, sem, *, td):
    del xe_in_hbm
    base = pl.program_id(0) * td

    def issue(t, c):
        for k in range(MOE_TOPK):
            _row_copy(x_ref, t, xe_hbm, pos_ref[MOE_TOPK * (base + t) + k], sem).start(priority=k)
        return c

    lax.fori_loop(0, td, issue, 0, unroll=8)
    all_rows = xe_hbm.at[pl.ds(0, MOE_TOPK * td)]
    pltpu.make_async_copy(all_rows, all_rows, sem).wait()


def _dispatch(pos_flat, x, rows, td):
    t_len, d = x.shape
    zeros = jnp.zeros((rows, d), x.dtype)
    grid_spec = pltpu.PrefetchScalarGridSpec(
        num_scalar_prefetch=1,
        grid=(t_len // td,),
        in_specs=[pl.BlockSpec((td, d), lambda i, pos: (i, 0)), pl.BlockSpec(memory_space=pl.ANY)],
        out_specs=pl.BlockSpec(memory_space=pl.ANY),
        scratch_shapes=[pltpu.SemaphoreType.DMA(())],
    )
    return pl.pallas_call(
        functools.partial(_dispatch_kernel, td=td),
        grid_spec=grid_spec,
        out_shape=jax.ShapeDtypeStruct((rows, d), x.dtype),
        input_output_aliases={2: 0},
        compiler_params=pltpu.CompilerParams(dimension_semantics=("arbitrary",)),
        name="moe_dispatch",
    )(pos_flat, x, zeros)


def _expert_weight_copies(win_hbm, wout_hbm, layer, expert, win_f32, wout_f32, sem):
    return (pltpu.make_async_copy(win_hbm.at[layer, expert], win_f32, sem.at[0]),
            pltpu.make_async_copy(wout_hbm.at[layer, expert], wout_f32, sem.at[1]))


def _expert_kernel(be_ref, next_ref, nused_ref, xe_ref, win_hbm, wout_hbm, yb_ref, win_f32, wout_f32, win_bf,
                   wout_bf, sem, *, layer):
    b = pl.program_id(0)
    used = b < nused_ref[0]
    new_expert = (b == 0) | (be_ref[b] != be_ref[jnp.maximum(b - 1, 0)])
    copies = functools.partial(_expert_weight_copies, win_hbm, wout_hbm, layer)

    @pl.when(b == 0)
    def _():
        for c in copies(be_ref[0], win_f32, wout_f32, sem):
            c.start()

    @pl.when(used & new_expert)
    def _():
        for c in copies(be_ref[b], win_f32, wout_f32, sem):
            c.wait()
        win_bf[...] = win_f32[...].astype(BF16)
        wout_bf[...] = wout_f32[...].astype(BF16)

        @pl.when(next_ref[b] >= 0)
        def _():
            for c in copies(next_ref[b], win_f32, wout_f32, sem):
                c.start()

    @pl.when(used)
    def _():
        hcat = _dot(_unpack_rows(xe_ref[...]).astype(BF16), win_bf[...])
        a = hcat[:, :EXPERT_FF]
        act = (a * _sigmoid(a) * hcat[:, EXPERT_FF:]).astype(BF16)
        yb_ref[...] = _pack_rows(_dot(act, wout_bf[...]))

    @pl.when(b >= nused_ref[0])
    def _():
        yb_ref[...] = jnp.zeros(yb_ref.shape, U32)


def _experts(block_expert, next_expert, n_used, xe, w_e_in, w_e_out, layer):
    rows, half = xe.shape
    d = 2 * half
    n_blocks = rows // MOE_BLOCK
    grid_spec = pltpu.PrefetchScalarGridSpec(
        num_scalar_prefetch=3,
        grid=(n_blocks,),
        in_specs=[pl.BlockSpec((MOE_BLOCK, half), lambda b, *_: (b, 0)),
                  pl.BlockSpec(memory_space=pl.ANY),
                  pl.BlockSpec(memory_space=pl.ANY)],
        out_specs=pl.BlockSpec((MOE_BLOCK, half), lambda b, *_: (b, 0)),
        scratch_shapes=[pltpu.VMEM((d, 2 * EXPERT_FF), F32), pltpu.VMEM((EXPERT_FF, d), F32),
                        pltpu.VMEM((d, 2 * EXPERT_FF), BF16), pltpu.VMEM((EXPERT_FF, d), BF16),
                        pltpu.SemaphoreType.DMA((2,))],
    )
    return pl.pallas_call(
        functools.partial(_expert_kernel, layer=layer),
        grid_spec=grid_spec,
        out_shape=jax.ShapeDtypeStruct((rows, half), U32),
        compiler_params=_cparams(("arbitrary",)),
        name="moe_experts",
    )(block_expert, next_expert, n_used, xe, w_e_in, w_e_out)


def _combine_kernel(pos_ref, yb_hbm, x_ref, info_ref, g_ref, b_ref, xo_ref, xob_ref, buf, sem):
    tm = x_ref.shape[0]
    i = pl.program_id(0)
    slot = i & 1

    def gather(tile, slot_):
        def issue(t, c):
            for k in range(MOE_TOPK):
                _row_copy(yb_hbm, pos_ref[MOE_TOPK * (tile * tm + t) + k], buf.at[slot_, k], t,
                          sem.at[slot_]).start(priority=k)
            return c
        lax.fori_loop(0, tm, issue, 0, unroll=8)

    @pl.when(i == 0)
    def _():
        gather(0, 0)

    @pl.when(i + 1 < pl.num_programs(0))
    def _():
        gather(i + 1, 1 - slot)

    pltpu.make_async_copy(buf.at[slot], buf.at[slot], sem.at[slot]).wait()
    info = info_ref[...]
    y = info[:, 2:3] * _unpack_rows(buf[slot, 0]) + info[:, 3:4] * _unpack_rows(buf[slot, 1])
    z = _layer_norm(DN_ALPHA * x_ref[...] + y, g_ref[...], b_ref[...])
    xo_ref[...] = z
    xob_ref[...] = z.astype(BF16)


def _combine(pos_flat, yb, x, info, g, b, tm):
    t_len, d = x.shape
    grid_spec = pltpu.PrefetchScalarGridSpec(
        num_scalar_prefetch=1,
        grid=(t_len // tm,),
        in_specs=[pl.BlockSpec(memory_space=pl.ANY),
                  pl.BlockSpec((tm, d), lambda i, pos: (i, 0)),
                  pl.BlockSpec((tm, LANE), lambda i, pos: (i, 0)),
                  pl.BlockSpec((1, d), lambda i, pos: (0, 0)),
                  pl.BlockSpec((1, d), lambda i, pos: (0, 0))],
        out_specs=[pl.BlockSpec((tm, d), lambda i, pos: (i, 0)),
                   pl.BlockSpec((tm, d), lambda i, pos: (i, 0))],
        scratch_shapes=[pltpu.VMEM((2, MOE_TOPK, tm, d // 2), U32), pltpu.SemaphoreType.DMA((2,))],
    )
    return pl.pallas_call(
        _combine_kernel,
        grid_spec=grid_spec,
        out_shape=[jax.ShapeDtypeStruct((t_len, d), F32), jax.ShapeDtypeStruct((t_len, d), BF16)],
        compiler_params=_cparams(("arbitrary",)),
        name="moe_combine_ln",
    )(pos_flat, yb, x, info, g, b)


def _pad_cols(m, width):
    return jnp.pad(m, [(0, 0)] * (m.ndim - 1) + [(0, width - m.shape[-1])])


def _rot_half_cols(m):
    half = m.shape[-1] // 2
    return jnp.concatenate([-m[..., half:], m[..., :half]], -1)


def _layout_in1(m):
    offs = np.cumsum((0,) + IN_SIZES)
    o_q, o_kv, o_g, o_pool, o_cq, o_ckv, o_kr, o_conv = offs[:8]
    z = lambda n: jnp.zeros(m.shape[:-1] + (n,), m.dtype)
    parts = []
    for h in range(NSA_HEADS):
        parts += [m[..., o_q + h * NSA_DH:o_q + (h + 1) * NSA_DH] * (NSA_DH ** -0.5 * LOG2E), z(LANE - NSA_DH)]
    for c in range(6 * NSA_GROUPS):
        parts += [m[..., o_kv + c * NSA_DH:o_kv + (c + 1) * NSA_DH], z(LANE - NSA_DH)]
    parts += [m[..., o_pool:o_pool + POOL_WIDTH], m[..., o_cq:o_cq + MLA_Q_RANK], m[..., o_ckv:o_ckv + MLA_KV_RANK]]
    kr = m[..., o_kr:o_kr + MLA_ROPE]
    parts += [z(MLA_NOPE), kr, z(LANE - MLA_NOPE - MLA_ROPE), z(MLA_NOPE), _rot_half_cols(kr),
              z(LANE - MLA_NOPE - MLA_ROPE)]
    parts += [m[..., o_conv:o_conv + 2 * CONV_CH], _pad_cols(m[..., o_g:o_g + 3 * NSA_HEADS], LANE), z(LANE)]
    return jnp.concatenate(parts, -1)


def _layout_mla_q(w):
    dq = MLA_NOPE + MLA_ROPE
    z = lambda n: jnp.zeros((w.shape[0], n), w.dtype)
    a, b = [], []
    for h in range(MLA_HEADS):
        rope = w[:, h * dq + MLA_NOPE:(h + 1) * dq]
        a += [w[:, h * dq:h * dq + MLA_NOPE], rope, z(LANE - dq)]
        b += [z(MLA_NOPE), _rot_half_cols(rope), z(LANE - dq)]
    return jnp.concatenate(a + b, -1)


def _layout_mla_kv(w):
    dkv = MLA_NOPE + MLA_DV
    z = jnp.zeros((w.shape[0], LANE - MLA_NOPE), w.dtype)
    k, v = [], []
    for h in range(MLA_HEADS):
        k += [w[:, h * dkv:h * dkv + MLA_NOPE], z]
        v += [w[:, h * dkv + MLA_NOPE:(h + 1) * dkv], z]
    return jnp.concatenate(k, -1), jnp.concatenate(v, -1)


def _static_tables(s_len):
    nch = s_len // CMP_STRIDE
    n_cmp = (s_len - CMP_LEN) // CMP_STRIDE + 1
    n_sel = s_len // SEL_LEN
    ratio = SEL_LEN // CMP_STRIDE
    c = np.arange(nch)[:, None]
    j = np.arange(n_sel)[None, :]
    mband = ((c >= ratio * j - 1) & (c <= ratio * j + ratio - 1) & (c < n_cmp)).astype(np.float32)
    per_tile = FLASH_TK // SEL_LEN
    n_tiles = s_len // FLASH_TK
    scat = np.zeros((n_tiles, n_sel, LANE), np.float32)
    for b in range(n_sel):
        scat[b // per_tile, b, NSA_DH + b % per_tile] = 1.0
    eslot = np.zeros((FLASH_TK, LANE), np.float32)
    eslot[np.arange(FLASH_TK), NSA_DH + np.arange(FLASH_TK) // SEL_LEN] = 1.0
    gexp = np.zeros((3, LANE, NSA_HEADS * NSA_DH), np.float32)
    for h in range(NSA_HEADS):
        for jj in range(3):
            gexp[jj, h * 3 + jj, h * NSA_DH:(h + 1) * NSA_DH] = 1.0
    half = MLA_ROPE // 2
    inv = ROPE_BASE ** (-jnp.arange(half, dtype=F32) / half)
    inv_slot = jnp.concatenate([jnp.zeros((MLA_NOPE,), F32), inv, inv,
                                jnp.zeros((LANE - MLA_NOPE - MLA_ROPE,), F32)])[None, :]
    as_bf = lambda a: jnp.asarray(a, BF16)
    return as_bf(mband), as_bf(scat), as_bf(eslot), as_bf(gexp), inv_slot


def _layout_projection(w_in, b_in):
    o_merge = int(sum(IN_SIZES[:8]))
    b_rows = b_in[:, None, :]
    return (_layout_in1(w_in[..., :o_merge]).astype(BF16), _layout_in1(b_rows[..., :o_merge]),
            w_in[..., o_merge:].astype(BF16), b_rows[..., o_merge:])


def _hybrid_mixer(x, xb, pos_col, tabs, proj, cmp_pos, cmp_w1, cmp_w2, w_nsa_o, w_pool, pool_scale, w_pool_o,
                  q_norm, w_uq, kv_norm, w_ukv, w_mla_o, conv_w, conv_b, conv_ln_g, conv_ln_b, w_conv_o, w_out,
                  ln_g, ln_b):
    s_len = x.shape[0]
    mband, scat, eslot, gexp, inv_slot = tabs
    row = lambda v: v[None, :]
    w1, b1, wm, bm = proj
    tm_proj = min(TM_PROJ, s_len)
    u1 = _matmul(xb, w1, b1, tm_proj, TN_PROJ, BF16)
    um = _matmul(xb, wm, bm, tm_proj, TN_MERGE, BF16)

    nch = s_len // CMP_STRIDE
    kdim = CMP_STRIDE * LANE
    chunks = u1[:, SLOT_KV * LANE:(SLOT_KV + 4) * LANE].reshape(nch, CMP_STRIDE, 4, LANE)
    chunks = chunks.transpose(2, 0, 1, 3).reshape(4, nch, kdim)
    w1 = _pad_cols(cmp_w1.reshape(2, CMP_LEN, NSA_DH, CMP_HIDDEN).transpose(0, 1, 3, 2), LANE)
    w1 = w1.transpose(0, 1, 3, 2)
    w1cat = jnp.concatenate([w1[:, :CMP_STRIDE].reshape(2, kdim, CMP_HIDDEN),
                             w1[:, CMP_STRIDE:].reshape(2, kdim, CMP_HIDDEN)], -1).astype(BF16)
    posp = _pad_cols(cmp_pos, LANE)
    pos2 = jnp.stack([posp[:, :CMP_STRIDE].reshape(2, kdim), posp[:, CMP_STRIDE:].reshape(2, kdim)], 1)
    pos2 = jnp.pad(pos2, ((0, 0), (0, 6), (0, 0))).astype(BF16)
    kcvc = _nsa_compress(chunks, w1cat, pos2, _pad_cols(cmp_w2, LANE).astype(BF16))
    o_cmp, selneg = _nsa_cmp(u1, kcvc, mband, TQ_CMP)
    nsa = dict(n_groups=1, hpg=NSA_HEADS, hpk=NSA_HPG, out_dtype=F32, dv=NSA_DH)
    o_sel = _flash(u1, SLOT_Q // NSA_HEADS, u1, (SLOT_KV + 4) // NSA_GROUPS, u1, (SLOT_KV + 6) // NSA_GROUPS,
                   window=False, tq=FLASH_TQ, tk=FLASH_TK, sel=selneg, scat=scat, eslot=eslot, **nsa)
    o_win = _flash(u1, SLOT_Q // NSA_HEADS, u1, (SLOT_KV + 8) // NSA_GROUPS, u1, (SLOT_KV + 10) // NSA_GROUPS,
                   window=True, tq=WINDOW, tk=WINDOW, **nsa)

    wk, wv = _layout_mla_kv(w_ukv)
    q_m, k_m, v_m = _mla_prep(u1, pos_col, inv_slot, row(q_norm), row(kv_norm), _layout_mla_q(w_uq).astype(BF16),
                              wk.astype(BF16), wv.astype(BF16), TM_ROW)
    o_mla = _flash(q_m, 0, k_m, 0, v_m, 0, n_groups=1, hpg=MLA_HEADS, hpk=1, window=False,
                   out_dtype=BF16, dv=MLA_DV, tq=FLASH_TQ, tk=FLASH_TK)

    pooled = _pool(u1, w_pool.astype(BF16), row(pool_scale), TM_ROW)
    conv = _conv(u1, conv_w, row(conv_b), row(conv_ln_g), row(conv_ln_b), TM_ROW)
    merged = _merge(o_cmp, o_sel, o_win, u1, gexp, pooled, o_mla, conv, um, w_nsa_o.astype(BF16),
                    w_pool_o.astype(BF16), w_mla_o.astype(BF16), w_conv_o.astype(BF16), TM_MERGE)
    return _matmul_res_ln(merged, w_out.astype(BF16), x, row(ln_g), row(ln_b), TM_ROW)


def _cross_attention(x, xb, mem_b, w_q, w_k, w_v, w_o, ln_g, ln_b):
    row = lambda v: v[None, :]
    kv = _matmul(mem_b, jnp.concatenate([w_k, w_v], 1).astype(BF16), jnp.zeros((1, 2 * X_HEADS * X_DH), F32),
                 mem_b.shape[0], 2 * X_HEADS * X_DH, BF16)
    hw = X_HEADS * X_DH
    return _xattn(xb, x, w_q.astype(BF16), kv[:, :hw], kv[:, hw:], w_o.astype(BF16), row(ln_g), row(ln_b), TM_ROW)


def _hier_moe(x, xb, xp, w_group, b_group, w_router, b_router, w_e_in, w_e_out, layer, ln_g, ln_b):
    t_len = x.shape[0]
    row = lambda v: v[None, :]
    tm_r = TM_ROW
    w_gr = _pad_cols(jnp.concatenate([w_group, w_router], 1), LANE).astype(BF16)
    b_gr = _pad_cols(row(jnp.concatenate([b_group, b_router])), LANE)
    tri = jnp.asarray(np.tril(np.ones((tm_r, tm_r), np.float32)), BF16)
    info, cnt = _router(xb, w_gr, b_gr, tri, tm_r)

    counts = cnt[0, ROUTE_OFF:ROUTE_OFF + N_EXPERTS].astype(I32)
    padded = (counts + MOE_BLOCK - 1) // MOE_BLOCK * MOE_BLOCK
    pend = jnp.cumsum(padded)
    pstart = pend - padded
    n_blocks = -(-(t_len * MOE_TOPK) // MOE_BLOCK) + N_EXPERTS
    n_used = pend[-1] // MOE_BLOCK
    blk_ids = jnp.minimum(jnp.arange(n_blocks), n_used - 1)
    owner = jnp.sum((pend[None, :] <= (blk_ids * MOE_BLOCK)[:, None]).astype(I32), axis=1)
    block_expert = jnp.minimum(owner, N_EXPERTS - 1).astype(I32)
    e_ids = info[:, 0:MOE_TOPK].astype(I32)
    pos = (pstart[e_ids] + info[:, 4:4 + MOE_TOPK].astype(I32)).reshape(-1)

    xe = _dispatch(pos, xp, n_blocks * MOE_BLOCK, TM_ROW)
    blk = jnp.arange(n_blocks, dtype=I32)
    change = (blk < n_used) & ((blk == 0) | (block_expert != jnp.roll(block_expert, 1)))
    first_change_from = jnp.flip(lax.cummin(jnp.flip(jnp.where(change, blk, n_blocks))))
    nxt = jnp.concatenate([first_change_from[1:], jnp.full((1,), n_blocks, I32)])
    next_expert = jnp.where(nxt < n_blocks, block_expert[jnp.minimum(nxt, n_blocks - 1)], -1).astype(I32)
    yb = _experts(block_expert, next_expert, n_used.reshape(1).astype(I32), xe, w_e_in, w_e_out, layer)
    return _combine(pos, yb, x, info, row(ln_g), row(ln_b), TM_COMBINE)


def kernel(x, mem, positions, w_in, b_in, nsa_cmp_pos, nsa_cmp_w1, nsa_cmp_w2, w_nsa_o, w_pool, pool_scale, w_pool_o, mla_q_norm, w_mla_uq, mla_kv_norm, w_mla_ukv, w_mla_o, conv_w, conv_b, conv_ln_g, conv_ln_b, w_conv_o, w_out, ln_mix_g, ln_mix_b, w_xq, w_xk, w_xv, w_xo, ln_x_g, ln_x_b, w_group, b_group, w_router, b_router, w_expert_in, w_expert_out, ln_ffn_g, ln_ffn_b):
    batch, s_len, d = x.shape
    assert batch == 1 and d == D_MODEL and s_len % (2 * FLASH_TK) == 0 and s_len % FLASH_TQ == 0
    x = x[0]
    xb = x.astype(BF16)
    mem_b = mem[0].astype(BF16)
    pos_col = positions[0].astype(F32)[:, None]
    tabs = _static_tables(s_len)
    proj = _layout_projection(w_in, b_in)
    for l in range(w_in.shape[0]):
        x, xb = _hybrid_mixer(x, xb, pos_col, tabs, [a[l] for a in proj], nsa_cmp_pos[l], nsa_cmp_w1[l], nsa_cmp_w2[l],
                              w_nsa_o[l], w_pool[l], pool_scale[l], w_pool_o[l], mla_q_norm[l], w_mla_uq[l],
                              mla_kv_norm[l], w_mla_ukv[l], w_mla_o[l], conv_w[l], conv_b[l], conv_ln_g[l],
                              conv_ln_b[l], w_conv_o[l], w_out[l], ln_mix_g[l], ln_mix_b[l])
        x, xb, xp = _cross_attention(x, xb, mem_b, w_xq[l], w_xk[l], w_xv[l], w_xo[l], ln_x_g[l], ln_x_b[l])
        x, xb = _hier_moe(x, xb, xp, w_group[l], b_group[l], w_router[l], b_router[l], w_expert_in,
                          w_expert_out, l, ln_ffn_g[l], ln_ffn_b[l])
    return x[None]
```
